```python
import jax
import jax.numpy as jnp
from jax import lax
import numpy as np


D_MODEL = 1024
BATCH = 8
SEQ = 8192
DEPTH = 1

HEAD_DIM = 64
DIL_PATTERNS = ((128, 1), (512, 4), (2048, 16))
DIL_HEADS_PER_GROUP = 4
N_DIL_GROUPS = len(DIL_PATTERNS)
DIL_HEADS = N_DIL_GROUPS * DIL_HEADS_PER_GROUP
DIL_WIDTH = DIL_HEADS * HEAD_DIM
DIL_OUT_WIDTH = DIL_HEADS_PER_GROUP * HEAD_DIM
SB_HEADS = 4
SB_WIDTH = SB_HEADS * HEAD_DIM
QKV_WIDTH = 3 * DIL_WIDTH + 3 * SB_WIDTH
QBLOCK = 128
ROPE_THETA = 500000.0
ROPE_DIM = HEAD_DIM // 4
N_GROUPS = 4
EXPERTS_PER_GROUP = 8
N_EXPERTS = N_GROUPS * EXPERTS_PER_GROUP
TOP_K_INNER = 2
D_EXPERT = 512
EXPERT_CHUNK = 256
RMS_EPS = 1e-6

kernel_name = 'hybrid_dilated_stickbreak_hmoe_layer'


def rms_norm(x, g):
    xf = x.astype(jnp.float32)
    y = xf * lax.rsqrt(jnp.mean(xf * xf, axis=-1, keepdims=True) + RMS_EPS)
    return (y * g.astype(jnp.float32)).astype(x.dtype)


def partial_rope(x, cos, sin):
    half = ROPE_DIM // 2
    xr = x[..., :ROPE_DIM].astype(jnp.float32)
    x1, x2 = xr[..., :half], xr[..., half:]
    rot = jnp.concatenate([x1 * cos - x2 * sin, x2 * cos + x1 * sin], axis=-1)
    return jnp.concatenate([rot.astype(x.dtype), x[..., ROPE_DIM:]], axis=-1)


def dilated_attention(q, k, v):
    b, s = q.shape[0], q.shape[1]
    n_blocks = s // QBLOCK
    scale = HEAD_DIM ** -0.5
    hpg = DIL_HEADS_PER_GROUP
    kf = k.astype(jnp.float32)
    vf = v.astype(jnp.float32)
    k_groups = [kf[:, :, g * hpg:(g + 1) * hpg] for g in range(N_DIL_GROUPS)]
    v_groups = [vf[:, :, g * hpg:(g + 1) * hpg] for g in range(N_DIL_GROUPS)]

    def block(i):
        start = i * QBLOCK
        t = start + jnp.arange(QBLOCK, dtype=jnp.int32)
        qb = lax.dynamic_slice_in_dim(q, start, QBLOCK, axis=1).astype(jnp.float32)
        outs, lses = [], []
        for g, (window, dil) in enumerate(DIL_PATTERNS):
            n_keys = window // dil + 1
            idx = t[:, None] - dil * jnp.arange(n_keys, dtype=jnp.int32)[None, :]
            valid = idx >= 0
            idx = jnp.maximum(idx, 0)
            kg = jnp.take(k_groups[g], idx, axis=1)
            vg = jnp.take(v_groups[g], idx, axis=1)
            sc = jnp.einsum('bqhd,bqjhd->bhqj', qb[:, :, g * hpg:(g + 1) * hpg], kg) * scale
            sc = jnp.where(valid[None, None], sc, -jnp.inf)
            lse = jax.nn.logsumexp(sc, axis=-1)
            p = jnp.exp(sc - lse[..., None])
            outs.append(jnp.einsum('bhqj,bqjhd->bqhd', p, vg))
            lses.append(lse)
        w = jax.nn.softmax(jnp.stack(lses, axis=0), axis=0)
        w = jnp.transpose(w, (0, 1, 3, 2))[..., None]
        return jnp.sum(w * jnp.stack(outs, axis=0), axis=0)

    out = lax.map(block, jnp.arange(n_blocks, dtype=jnp.int32))
    out = jnp.transpose(out, (1, 0, 2, 3, 4)).reshape(b, s, DIL_OUT_WIDTH)
    return out.astype(q.dtype)


def stick_breaking_attention(q, k, v):
    b, s = q.shape[0], q.shape[1]
    n_blocks = s // QBLOCK
    scale = HEAD_DIM ** -0.5
    kf = k.astype(jnp.float32)
    vf = v.astype(jnp.float32)
    key_pos = jnp.arange(s, dtype=jnp.int32)

    def block(i):
        start = i * QBLOCK
        t = start + jnp.arange(QBLOCK, dtype=jnp.int32)
        qb = lax.dynamic_slice_in_dim(q, start, QBLOCK, axis=1).astype(jnp.float32)
        z = jnp.einsum('bqhd,bshd->bhqs', qb, kf) * scale
        mask = (key_pos[None, :] < t[:, None])[None, None]
        log_1m = jnp.where(mask, jax.nn.log_sigmoid(-z), 0.0)
        suffix = lax.cumsum(log_1m, axis=3, reverse=True)
        between = jnp.concatenate([suffix[..., 1:], jnp.zeros_like(suffix[..., :1])], axis=-1)
        a = jnp.where(mask, jnp.exp(jax.nn.log_sigmoid(z) + between), 0.0)
        return jnp.einsum('bhqs,bshd->bqhd', a, vf)

    out = lax.map(block, jnp.arange(n_blocks, dtype=jnp.int32))
    out = jnp.transpose(out, (1, 0, 2, 3, 4)).reshape(b, s, SB_WIDTH)
    return out.astype(q.dtype)


def hierarchical_moe(u, w_rg, b_rg, w_re, b_re, w_g, w_u, w_d):
    n_tok, d = u.shape
    uf = u.astype(jnp.float32)
    p_group = jax.nn.softmax(uf @ w_rg.astype(jnp.float32) + b_rg.astype(jnp.float32), axis=-1)
    pg_top, g_sel = lax.top_k(p_group, 1)
    logits_e = (uf @ w_re.astype(jnp.float32) + b_re.astype(jnp.float32)).reshape(n_tok, N_GROUPS, EXPERTS_PER_GROUP)
    logits_sel = jnp.take_along_axis(logits_e, g_sel[:, :, None], axis=1)[:, 0]
    pi_top, e_in = lax.top_k(jax.nn.softmax(logits_sel, axis=-1), TOP_K_INNER)
    weights = pg_top * pi_top / jnp.sum(pi_top, axis=-1, keepdims=True)
    expert_id = g_sel * EXPERTS_PER_GROUP + e_in

    n_assign = n_tok * TOP_K_INNER
    flat_e = expert_id.reshape(-1).astype(jnp.int32)
    flat_w = weights.reshape(-1)
    flat_tok = jnp.repeat(jnp.arange(n_tok, dtype=jnp.int32), TOP_K_INNER)
    order = jnp.argsort(flat_e)
    sorted_e = flat_e[order]
    counts = jnp.bincount(flat_e, length=N_EXPERTS).astype(jnp.int32)
    starts = jnp.cumsum(counts) - counts
    padded = (counts + EXPERT_CHUNK - 1) // EXPERT_CHUNK * EXPERT_CHUNK
    padded_ends = jnp.cumsum(padded)
    padded_starts = padded_ends - padded
    dest = padded_starts[sorted_e] + jnp.arange(n_assign, dtype=jnp.int32) - starts[sorted_e]
    n_chunks = -(-n_assign // EXPERT_CHUNK) + N_EXPERTS
    n_slots = n_chunks * EXPERT_CHUNK
    slot_tok = jnp.full((n_slots,), n_tok, jnp.int32).at[dest].set(flat_tok[order])
    slot_w = jnp.zeros((n_slots,), jnp.float32).at[dest].set(flat_w[order])
    chunk_e = jnp.minimum(
        jnp.searchsorted(padded_ends, jnp.arange(n_chunks, dtype=jnp.int32) * EXPERT_CHUNK, side='right'),
        N_EXPERTS - 1).astype(jnp.int32)
    u_pad = jnp.concatenate([u, jnp.zeros((1, d), u.dtype)], axis=0)
    chunk_x = u_pad[slot_tok].reshape(n_chunks, EXPERT_CHUNK, d)

    def expert_chunk(args):
        xc, e = args
        h = jax.nn.silu(xc @ w_g[e]) * (xc @ w_u[e])
        return h @ w_d[e]

    y = lax.map(expert_chunk, (chunk_x, chunk_e)).reshape(n_slots, d)
    y = y * slot_w[:, None].astype(y.dtype)
    return jnp.zeros((n_tok + 1, d), y.dtype).at[slot_tok].add(y)[:n_tok]


def setup_inputs(seed: int = 0) -> dict:
    key = jax.random.key(seed)
    ks = jax.random.split(key, 24)
    L = DEPTH
    D = D_MODEL

    def nrm(k, shape, fan_in):
        return jax.random.normal(k, shape, jnp.float32) * fan_in ** -0.5

    def gain(k, shape):
        return 1.0 + 0.01 * jax.random.normal(k, shape, jnp.float32)

    x = jax.random.normal(ks[0], (BATCH, SEQ, D), jnp.float32)
    c = jax.random.normal(ks[1], (BATCH, D), jnp.float32)
    positions = (jax.random.randint(ks[2], (BATCH, 1), 0, 4096, dtype=jnp.int32)
                 + jnp.arange(SEQ, dtype=jnp.int32)[None, :])
    return {
        'x': x,
        'c': c,
        'positions': positions,
        'w_ada': nrm(ks[3], (L, D, 6 * D), D) * 0.5,
        'b_ada': 0.01 * jax.random.normal(ks[4], (L, 6 * D), jnp.float32),
        'g_norm_mix': gain(ks[5], (L, D)),
        'g_norm_ffn': gain(ks[6], (L, D)),
        'w_in': nrm(ks[7], (L, D, QKV_WIDTH), D),
        'w_branch_gate': nrm(ks[8], (L, D, 2 * D), D),
        'q_norm_g': gain(ks[9], (L, HEAD_DIM)),
        'k_norm_g': gain(ks[10], (L, HEAD_DIM)),
        'w_branch_dil': nrm(ks[11], (L, DIL_OUT_WIDTH, D), DIL_OUT_WIDTH),
        'w_branch_sb': nrm(ks[12], (L, SB_WIDTH, D), SB_WIDTH),
        'w_out': nrm(ks[13], (L, D, D), D),
        'w_router_group': nrm(ks[14], (L, D, N_GROUPS), D),
        'b_router_group': 0.01 * jax.random.normal(ks[15], (L, N_GROUPS), jnp.float32),
        'w_router_expert': nrm(ks[16], (L, D, N_EXPERTS), D),
        'b_router_expert': 0.01 * jax.random.normal(ks[17], (L, N_EXPERTS), jnp.float32),
        'w_expert_gate': nrm(ks[18], (L, N_EXPERTS, D, D_EXPERT), D),
        'w_expert_up': nrm(ks[19], (L, N_EXPERTS, D, D_EXPERT), D),
        'w_expert_down': nrm(ks[20], (L, N_EXPERTS, D_EXPERT, D), D_EXPERT),
    }


def reference(x, c, positions, w_ada, b_ada, g_norm_mix, g_norm_ffn, w_in, w_branch_gate,
              q_norm_g, k_norm_g, w_branch_dil, w_branch_sb, w_out, w_router_group,
              b_router_group, w_router_expert, b_router_expert, w_expert_gate, w_expert_up,
              w_expert_down):
    b, s, d = x.shape
    inv_freq = ROPE_THETA ** (-jnp.arange(0, ROPE_DIM, 2, dtype=jnp.float32) / ROPE_DIM)
    ang = positions.astype(jnp.float32)[..., None] * inv_freq
    cos = jnp.cos(ang)[:, :, None, :]
    sin = jnp.sin(ang)[:, :, None, :]
    c_act = jax.nn.silu(c.astype(jnp.float32))

    for l in range(DEPTH):
        mod = (c_act @ w_ada[l].astype(jnp.float32) + b_ada[l].astype(jnp.float32)).astype(x.dtype)
        shift1, scale1, gate1, shift2, scale2, gate2 = jnp.split(mod[:, None, :], 6, axis=-1)

        h = rms_norm(x, g_norm_mix[l]) * (1 + scale1) + shift1
        proj = h @ w_in[l]
        cuts = np.cumsum([DIL_WIDTH, DIL_WIDTH, DIL_WIDTH, SB_WIDTH, SB_WIDTH]).tolist()
        qa, ka, va, qb, kb, vb = jnp.split(proj, cuts, axis=-1)
        qa = qa.reshape(b, s, DIL_HEADS, HEAD_DIM)
        ka = ka.reshape(b, s, DIL_HEADS, HEAD_DIM)
        va = va.reshape(b, s, DIL_HEADS, HEAD_DIM)
        qa = partial_rope(rms_norm(qa, q_norm_g[l]), cos, sin)
        ka = partial_rope(rms_norm(ka, k_norm_g[l]), cos, sin)
        o_dil = dilated_attention(qa, ka, va)
        o_sb = stick_breaking_attention(qb.reshape(b, s, SB_HEADS, HEAD_DIM),
                                        kb.reshape(b, s, SB_HEADS, HEAD_DIM),
                                        vb.reshape(b, s, SB_HEADS, HEAD_DIM))
        g_dil, g_sb = jnp.split(jax.nn.sigmoid(h @ w_branch_gate[l]), 2, axis=-1)
        merged = g_dil * (o_dil @ w_branch_dil[l]) + g_sb * (o_sb @ w_branch_sb[l])
        x = x + gate1 * (merged @ w_out[l])

        h2 = rms_norm(x, g_norm_ffn[l]) * (1 + scale2) + shift2
        y = hierarchical_moe(h2.reshape(b * s, d), w_router_group[l], b_router_group[l],
                             w_router_expert[l], b_router_expert[l], w_expert_gate[l],
                             w_expert_up[l], w_expert_down[l])
        x = x + gate2 * y.reshape(b, s, d)
    return x
```

```python
import functools

import jax
import jax.numpy as jnp
from jax import lax
from jax.experimental import pallas as pl
from jax.experimental.pallas import tpu as pltpu

D_MODEL = 1024
HEAD_DIM = 64
DIL_PATTERNS = ((128, 1), (512, 4), (2048, 16))
HEADS_PER_GROUP = 4
GROUP_WIDTH = HEADS_PER_GROUP * HEAD_DIM
N_DIL_GROUPS = len(DIL_PATTERNS)
DIL_WIDTH = N_DIL_GROUPS * GROUP_WIDTH
QKV_WIDTH = 3 * DIL_WIDTH + 3 * GROUP_WIDTH
WINDOW_KEYS = 128
ROPE_THETA = 500000.0
ROPE_DIM = HEAD_DIM // 4
N_GROUPS = 4
EXPERTS_PER_GROUP = 8
N_EXPERTS = N_GROUPS * EXPERTS_PER_GROUP
D_EXPERT = 512
RMS_EPS = 1e-6
ATTN_SCALE = HEAD_DIM ** -0.5

LANES = 128
ROUTER_LANES = LANES
EXPERT_LANE0 = N_GROUPS
LSE_SEG = LANES // HEADS_PER_GROUP
NEG_BIG = -1e30
SB_DEAD_LOG = -120.0

ROW_TILE = 512
QBLK = 128
DIL_QTILE = 512
EXPERT_CHUNK = 256
SCATTER_TILE = 1024
COMBINE_TILE = 512
VMEM_LIMIT = 48 * 1024 * 1024

_BF16 = jnp.bfloat16
_F32 = jnp.float32
_NT = (((1,), (1,)), ((), ()))


def _dot(a, b):
    return jnp.dot(a, b, preferred_element_type=_F32)


def _dot_nt(a, b):
    return lax.dot_general(a, b, _NT, preferred_element_type=_F32)


def _split(a):
    hi = a.astype(_BF16)
    lo = (a - hi.astype(_F32)).astype(_BF16)
    return hi, lo


def _dot3(a, b):
    ah, al = _split(a)
    bh, bl = _split(b)
    return _dot(ah, bh) + (_dot(ah, bl) + _dot(al, bh))


def _rms_mod(x, g, scale, shift):
    y = x * lax.rsqrt(jnp.mean(x * x, axis=-1, keepdims=True) + RMS_EPS)
    return y * g * (1.0 + scale) + shift


def _params(*sem):
    return pltpu.CompilerParams(dimension_semantics=sem, vmem_limit_bytes=VMEM_LIMIT)


def _ada_body(c_ref, w_ref, b_ref, o_ref):
    c = c_ref[...]
    o_ref[...] = _dot3(c * jax.nn.sigmoid(c), w_ref[...]) + b_ref[...]


def _ada(c, w_ada, b_ada):
    b, d = c.shape
    n = w_ada.shape[1]
    rows = -(-b // 16) * 16
    cp = jnp.zeros((rows, d), _F32).at[:b].set(c)
    nt = 1536
    out = pl.pallas_call(
        _ada_body,
        out_shape=jax.ShapeDtypeStruct((rows, n), _F32),
        grid=(n // nt,),
        in_specs=[pl.BlockSpec((rows, d), lambda j: (0, 0)),
                  pl.BlockSpec((d, nt), lambda j: (0, j)),
                  pl.BlockSpec((1, nt), lambda j: (0, j))],
        out_specs=pl.BlockSpec((rows, nt), lambda j: (0, j)),
        compiler_params=_params("arbitrary"),
        name="ada",
    )(cp, w_ada, b_ada.reshape(1, n))
    return out[:b]


def _qkv_body(x_ref, mod_ref, g_ref, w_ref, qg_ref, kg_ref, c_ref, s1_ref, s2_ref, bd_ref,
              o0_ref, o1_ref, o2_ref, osb_ref):
    d = D_MODEL
    h = _rms_mod(x_ref[...], g_ref[...], mod_ref[:, d:2 * d], mod_ref[:, 0:d])
    hb = h.astype(_BF16)
    cc = jnp.concatenate([c_ref[...]] * 2, axis=1)
    s1 = jnp.concatenate([s1_ref[...]] * 2, axis=1)
    s2 = jnp.concatenate([s2_ref[...]] * 2, axis=1)
    bd = bd_ref[...]
    gw = GROUP_WIDTH

    def normed_rotated(col0, gain):
        acc = _dot(hb, w_ref[:, col0:col0 + gw])
        ms = _dot((acc * acc).astype(_BF16), bd)
        y = acc * lax.rsqrt(ms + RMS_EPS) * gain
        y = y * cc + pltpu.roll(y, gw - ROPE_DIM // 2, 1) * s1 + pltpu.roll(y, ROPE_DIM // 2, 1) * s2
        return y.astype(_BF16)

    for g, o_ref in enumerate((o0_ref, o1_ref, o2_ref)):
        o_ref[:, 0:gw] = normed_rotated(g * gw, qg_ref[...])
        o_ref[:, gw:2 * gw] = normed_rotated(DIL_WIDTH + g * gw, kg_ref[...])
        o_ref[:, 2 * gw:3 * gw] = _dot(hb, w_ref[:, 2 * DIL_WIDTH + g * gw:2 * DIL_WIDTH + (g + 1) * gw]).astype(_BF16)
    for part in range(3):
        col0 = 3 * DIL_WIDTH + part * gw
        osb_ref[:, part * gw:(part + 1) * gw] = _dot(hb, w_ref[:, col0:col0 + gw]).astype(_BF16)


def _qkv(x2, mod3, g_mix, w_in, qg, kg, rope_c, rope_s1, rope_s2, bd, seq):
    t, d = x2.shape
    tm = ROW_TILE
    per_b = seq // tm
    row = lambda i: (i, 0)
    const = lambda i: (0, 0)
    out = jax.ShapeDtypeStruct((t, 3 * GROUP_WIDTH), _BF16)
    return pl.pallas_call(
        _qkv_body,
        out_shape=[out] * 4,
        grid=(t // tm,),
        in_specs=[pl.BlockSpec((tm, d), row),
                  pl.BlockSpec((None, 1, mod3.shape[2]), lambda i: (i // per_b, 0, 0)),
                  pl.BlockSpec((1, d), const),
                  pl.BlockSpec(w_in.shape, const),
                  pl.BlockSpec((1, GROUP_WIDTH), const),
                  pl.BlockSpec((1, GROUP_WIDTH), const),
                  pl.BlockSpec((tm, LANES), row),
                  pl.BlockSpec((tm, LANES), row),
                  pl.BlockSpec((tm, LANES), row),
                  pl.BlockSpec(bd.shape, const)],
        out_specs=[pl.BlockSpec((tm, 3 * GROUP_WIDTH), row)] * 4,
        compiler_params=_params("arbitrary"),
        name="qkv",
    )(x2, mod3, g_mix, w_in, qg, kg, rope_c, rope_s1, rope_s2, bd)


def _dil_body(q_ref, kp_ref, kc_ref, vp_ref, vc_ref, o_ref, lse_ref, kf_ref, vf_ref):
    tq = q_ref.shape[0]
    first = pl.program_id(2) == 0
    kf_ref[0:QBLK, :] = kp_ref[...]
    kf_ref[QBLK:, :] = kc_ref[...]
    vf_ref[0:QBLK, :] = vp_ref[...]
    vf_ref[QBLK:, :] = vc_ref[...]
    row = lax.broadcasted_iota(jnp.int32, (QBLK, 2 * QBLK), 0)
    col = lax.broadcasted_iota(jnp.int32, (QBLK, 2 * QBLK), 1)
    band = (col >= row) & (col <= row + WINDOW_KEYS)
    lane = lax.broadcasted_iota(jnp.int32, (1, GROUP_WIDTH), 1)
    slane = lax.broadcasted_iota(jnp.int32, (1, LANES), 1)
    for j in range(tq // QBLK):
        qj = q_ref[j * QBLK:(j + 1) * QBLK, :]
        kcat = kf_ref[j * QBLK:(j + 2) * QBLK, :]
        vcat = vf_ref[j * QBLK:(j + 2) * QBLK, :]
        valid = band & ((col >= QBLK) | jnp.logical_not(first)) if j == 0 else band
        o_acc = jnp.zeros((QBLK, GROUP_WIDTH), _F32)
        lse_t = jnp.zeros((QBLK, LANES), _F32)
        for h in range(HEADS_PER_GROUP):
            hm = (lane >= h * HEAD_DIM) & (lane < (h + 1) * HEAD_DIM)
            qh = jnp.where(hm, qj, jnp.zeros_like(qj))
            s = jnp.where(valid, _dot_nt(qh, kcat) * ATTN_SCALE, NEG_BIG)
            m = jnp.max(s, axis=1, keepdims=True)
            p = jnp.exp(s - m)
            l = jnp.sum(p, axis=1, keepdims=True)
            oh = _dot(p.astype(_BF16), vcat)
            o_acc = jnp.where(hm, oh / l, o_acc)
            sm = (slane >= h * LSE_SEG) & (slane < (h + 1) * LSE_SEG)
            lse_t = jnp.where(sm, m + jnp.log(l), lse_t)
        o_ref[j * QBLK:(j + 1) * QBLK, :] = o_acc.astype(_BF16)
        lse_ref[j * QBLK:(j + 1) * QBLK, :] = lse_t


def _dilated_group(arr, dil, batch, seq):
    sd = seq // dil
    tq = min(DIL_QTILE, sd)
    per = tq // QBLK
    gw = GROUP_WIDTH
    view = arr.reshape(batch, sd, dil * 3 * gw)
    cur = lambda part: pl.BlockSpec((None, tq, gw), lambda b, r, i: (b, i, 3 * r + part))
    prev = lambda part: pl.BlockSpec((None, QBLK, gw),
                                     lambda b, r, i: (b, jnp.maximum(i * per - 1, 0), 3 * r + part))
    o, lse = pl.pallas_call(
        _dil_body,
        out_shape=[jax.ShapeDtypeStruct((batch, sd, dil * gw), _BF16),
                   jax.ShapeDtypeStruct((batch, sd, dil * LANES), _F32)],
        grid=(batch, dil, sd // tq),
        in_specs=[cur(0), prev(1), cur(1), prev(2), cur(2)],
        out_specs=[pl.BlockSpec((None, tq, gw), lambda b, r, i: (b, i, r)),
                   pl.BlockSpec((None, tq, LANES), lambda b, r, i: (b, i, r))],
        scratch_shapes=[pltpu.VMEM((tq + QBLK, gw), _BF16), pltpu.VMEM((tq + QBLK, gw), _BF16)],
        compiler_params=_params("arbitrary", "arbitrary", "arbitrary"),
        name=f"dil{dil}",
    )(view, view, view, view, view)
    return o.reshape(batch * seq, gw), lse.reshape(batch * seq, LANES)


def _sb_body(q_ref, k_ref, v_ref, o_ref, carry_ref, acc_ref):
    i = pl.program_id(1)
    q = q_ref[...]
    row = lax.broadcasted_iota(jnp.int32, (QBLK, QBLK), 0)
    col = lax.broadcasted_iota(jnp.int32, (QBLK, QBLK), 1)
    strict = col < row
    ur = lax.broadcasted_iota(jnp.int32, (QBLK, 2 * QBLK), 0)
    uc = lax.broadcasted_iota(jnp.int32, (QBLK, 2 * QBLK), 1)
    u = jnp.where((uc >= QBLK) | (ur > uc), 1.0, 0.0).astype(_BF16)
    lane = lax.broadcasted_iota(jnp.int32, (1, GROUP_WIDTH), 1)
    out = jnp.zeros((QBLK, GROUP_WIDTH), _F32)
    for h in range(HEADS_PER_GROUP):
        hm = (lane >= h * HEAD_DIM) & (lane < (h + 1) * HEAD_DIM)
        qh = jnp.where(hm, q, jnp.zeros_like(q))

        def tile(kb, diag, qh=qh):
            start = pl.multiple_of(kb * QBLK, QBLK)
            z = _dot_nt(qh, k_ref[pl.ds(start, QBLK), :]) * ATTN_SCALE
            sp = jnp.maximum(z, 0.0) + jnp.log1p(jnp.exp(-jnp.abs(z)))
            log_1m = -sp
            if diag:
                log_1m = jnp.where(strict, log_1m, 0.0)
            hi, lo = _split(log_1m)
            sums = _dot(hi, u) + _dot(lo, u)
            a = jnp.exp((z - sp) + sums[:, :QBLK] + carry_ref[...])
            if diag:
                a = jnp.where(strict, a, 0.0)
            acc_ref[...] += _dot(a.astype(_BF16), v_ref[pl.ds(start, QBLK), :])
            carry_ref[...] += sums[:, QBLK:]

        carry_ref[...] = jnp.zeros_like(carry_ref)
        acc_ref[...] = jnp.zeros_like(acc_ref)
        tile(i, True)

        def cond(st):
            return (st[0] >= 0) & (st[1] > SB_DEAD_LOG)

        def step(st, tile=tile):
            tile(st[0], False)
            return st[0] - 1, jnp.max(carry_ref[...])

        lax.while_loop(cond, step, (i - 1, jnp.max(carry_ref[...])))
        out = jnp.where(hm, acc_ref[...], out)
    o_ref[...] = out.astype(_BF16)


def _stick_breaking(arr, batch, seq):
    gw = GROUP_WIDTH
    view = arr.reshape(batch, seq, 3 * gw)
    o = pl.pallas_call(
        _sb_body,
        out_shape=jax.ShapeDtypeStruct((batch, seq, gw), _BF16),
        grid=(batch, seq // QBLK),
        in_specs=[pl.BlockSpec((None, QBLK, gw), lambda b, i: (b, i, 0)),
                  pl.BlockSpec((None, seq, gw), lambda b, i: (b, 0, 1)),
                  pl.BlockSpec((None, seq, gw), lambda b, i: (b, 0, 2))],
        out_specs=pl.BlockSpec((None, QBLK, gw), lambda b, i: (b, i, 0)),
        scratch_shapes=[pltpu.VMEM((QBLK, QBLK), _F32), pltpu.VMEM((QBLK, gw), _F32)],
        compiler_params=_params("arbitrary", "arbitrary"),
        name="sb",
    )(view, view, view)
    return o.reshape(batch * seq, gw)


def _merge_body(x_ref, mod_ref, g1_ref, g2_ref, o0_ref, o1_ref, o2_ref, l0_ref, l1_ref, l2_ref, osb_ref,
                wbg_ref, wbd_ref, wbs_ref, wout_ref, wrh_ref, wrl_ref, br_ref, ex_ref, tri_ref,
                x1_ref, h2p_ref, rinfo_ref, cnt_out_ref, cnt_ref):
    d = D_MODEL
    x = x_ref[...]
    hb = _rms_mod(x, g1_ref[...], mod_ref[:, d:2 * d], mod_ref[:, 0:d]).astype(_BF16)
    gates = jax.nn.sigmoid(_dot(hb, wbg_ref[...]))

    l0, l1, l2 = l0_ref[...], l1_ref[...], l2_ref[...]
    lmax = jnp.maximum(jnp.maximum(l0, l1), l2)
    e0, e1, e2 = jnp.exp(l0 - lmax), jnp.exp(l1 - lmax), jnp.exp(l2 - lmax)
    inv = 1.0 / (e0 + e1 + e2)
    ex = ex_ref[...]

    def widen(w):
        hi, lo = _split(w)
        return _dot(hi, ex) + _dot(lo, ex)

    o_dil = (widen(e0 * inv) * o0_ref[...].astype(_F32) + widen(e1 * inv) * o1_ref[...].astype(_F32)
             + widen(e2 * inv) * o2_ref[...].astype(_F32))
    merged = (gates[:, :d] * _dot(o_dil.astype(_BF16), wbd_ref[...])
              + gates[:, d:] * _dot(osb_ref[...], wbs_ref[...]))
    x1 = x + mod_ref[:, 2 * d:3 * d] * _dot(merged.astype(_BF16), wout_ref[...])
    x1_ref[...] = x1

    h2 = _rms_mod(x1, g2_ref[...], mod_ref[:, 4 * d:5 * d], mod_ref[:, 3 * d:4 * d])
    half = d // 2
    lo_bits = lax.bitcast_convert_type(h2[:, :half].astype(_BF16).astype(_F32), jnp.uint32) >> 16
    hi_bits = lax.bitcast_convert_type(h2[:, half:].astype(_BF16).astype(_F32), jnp.uint32) & jnp.uint32(0xFFFF0000)
    h2p_ref[...] = lo_bits | hi_bits

    hh, hl = _split(h2)
    logits = _dot(hh, wrh_ref[...]) + (_dot(hl, wrh_ref[...]) + _dot(hh, wrl_ref[...])) + br_ref[...]
    tm = logits.shape[0]
    lane = lax.broadcasted_iota(jnp.int32, (tm, ROUTER_LANES), 1).astype(_F32)
    far = float(ROUTER_LANES)

    def top(vals):
        m = jnp.max(vals, axis=1, keepdims=True)
        return m, jnp.min(jnp.where(vals == m, lane, far), axis=1, keepdims=True)

    is_group = lane < N_GROUPS
    mg, gsel = top(jnp.where(is_group, logits, NEG_BIG))
    pg_top = 1.0 / jnp.sum(jnp.where(is_group, jnp.exp(logits - mg), 0.0), axis=1, keepdims=True)
    lane0 = EXPERT_LANE0 + EXPERTS_PER_GROUP * gsel
    le = jnp.where((lane >= lane0) & (lane < lane0 + EXPERTS_PER_GROUP), logits, NEG_BIG)
    m1, i1 = top(le)
    m2, i2 = top(jnp.where(lane == i1, NEG_BIG, le))
    t2 = jnp.exp(m2 - m1)
    w0 = pg_top / (1.0 + t2)
    w1 = pg_top * t2 / (1.0 + t2)

    @pl.when(pl.program_id(0) == 0)
    def _():
        cnt_ref[...] = jnp.zeros_like(cnt_ref)

    sel0, sel1 = lane == i1, lane == i2
    onehot = jnp.where(sel0 | sel1, 1.0, 0.0)
    before = _dot(tri_ref[...], onehot.astype(_BF16)) + cnt_ref[0:1, :]
    r0 = jnp.sum(jnp.where(sel0, before, 0.0), axis=1, keepdims=True)
    r1 = jnp.sum(jnp.where(sel1, before, 0.0), axis=1, keepdims=True)
    cnt_ref[...] += jnp.sum(onehot, axis=0, keepdims=True)
    cnt_out_ref[...] = cnt_ref[...]

    cols = (i1 - EXPERT_LANE0, i2 - EXPERT_LANE0, r0, r1, w0, w1)
    rinfo = jnp.zeros((tm, ROUTER_LANES), _F32)
    for c, v in enumerate(cols):
        rinfo = jnp.where(lane == float(c), v, rinfo)
    rinfo_ref[...] = rinfo


def _merge(x2, mod3, g1, g2, outs, lses, osb, wbg, wbd, wbs, wout, wrh, wrl, br, ex, tri, seq):
    t, d = x2.shape
    tm = ROW_TILE
    per_b = seq // tm
    row = lambda i: (i, 0)
    const = lambda i: (0, 0)
    full = lambda a: pl.BlockSpec(a.shape, const)
    gw = GROUP_WIDTH
    return pl.pallas_call(
        _merge_body,
        out_shape=[jax.ShapeDtypeStruct((t, d), _F32),
                   jax.ShapeDtypeStruct((t, d // 2), jnp.uint32),
                   jax.ShapeDtypeStruct((t, ROUTER_LANES), _F32),
                   jax.ShapeDtypeStruct((8, ROUTER_LANES), _F32)],
        grid=(t // tm,),
        in_specs=[pl.BlockSpec((tm, d), row),
                  pl.BlockSpec((None, 1, mod3.shape[2]), lambda i: (i // per_b, 0, 0)),
                  full(g1), full(g2)]
                 + [pl.BlockSpec((tm, gw), row)] * 3
                 + [pl.BlockSpec((tm, LANES), row)] * 3
                 + [pl.BlockSpec((tm, gw), row)]
                 + [full(a) for a in (wbg, wbd, wbs, wout, wrh, wrl, br, ex, tri)],
        out_specs=[pl.BlockSpec((tm, d), row),
                   pl.BlockSpec((tm, d // 2), row),
                   pl.BlockSpec((tm, ROUTER_LANES), row),
                   pl.BlockSpec((8, ROUTER_LANES), const)],
        scratch_shapes=[pltpu.VMEM((8, ROUTER_LANES), _F32)],
        compiler_params=_params("arbitrary"),
        name="merge",
    )(x2, mod3, g1, g2, *outs, *lses, osb, wbg, wbd, wbs, wout, wrh, wrl, br, ex, tri)


def _scatter_body(dest_ref, h_ref, xs_in_ref, xs_ref, sem):
    del xs_in_ref
    ts = h_ref.shape[0]

    def row_copy(r, slot):
        return pltpu.make_async_copy(h_ref.at[pl.ds(r, 1)], xs_ref.at[pl.ds(slot, 1)], sem)

    def issue(r, c):
        row_copy(r, dest_ref[0, 2 * r]).start()
        row_copy(r, dest_ref[0, 2 * r + 1]).start()
        return c

    def drain(r, c):
        row_copy(0, 0).wait()
        row_copy(0, 0).wait()
        return c

    lax.fori_loop(0, ts, issue, 0)
    lax.fori_loop(0, ts, drain, 0)


def _scatter(dest, h2p, n_slots):
    t, w = h2p.shape
    ts = min(SCATTER_TILE, t)
    dest3 = dest.reshape(t // ts, 1, 2 * ts)
    return pl.pallas_call(
        _scatter_body,
        out_shape=jax.ShapeDtypeStruct((n_slots, w), jnp.uint32),
        grid=(t // ts,),
        in_specs=[pl.BlockSpec((None, 1, 2 * ts), lambda i: (i, 0, 0), memory_space=pltpu.SMEM),
                  pl.BlockSpec((ts, w), lambda i: (i, 0)),
                  pl.BlockSpec(memory_space=pl.ANY)],
        out_specs=pl.BlockSpec(memory_space=pl.ANY),
        scratch_shapes=[pltpu.SemaphoreType.DMA(())],
        input_output_aliases={2: 0},
        compiler_params=_params("arbitrary"),
        name="scatter",
    )(dest3, h2p, jnp.zeros((n_slots, w), jnp.uint32))


def _experts_body(ce_ref, nu_ref, xs_ref, wg_ref, wu_ref, wd_ref, ys_ref):
    del ce_ref

    @pl.when(pl.program_id(0) < nu_ref[0])
    def _():
        w = xs_ref[...]
        x = jnp.concatenate(
            [lax.bitcast_convert_type(w << 16, _F32), lax.bitcast_convert_type(w & jnp.uint32(0xFFFF0000), _F32)],
            axis=1).astype(_BF16)
        g = _dot(x, wg_ref[...])
        hmid = (g * jax.nn.sigmoid(g)) * _dot(x, wu_ref[...])
        ys_ref[...] = _dot(hmid.astype(_BF16), wd_ref[...])


def _experts(chunk_e, n_used, xs, wg, wu, wd):
    n_slots, w = xs.shape
    ch = EXPERT_CHUNK
    d, de = wg.shape[1], wg.shape[2]
    slot = lambda c, ce, nu: (jnp.minimum(c, nu[0] - 1), 0)
    return pl.pallas_call(
        _experts_body,
        out_shape=jax.ShapeDtypeStruct((n_slots, d), _F32),
        grid_spec=pltpu.PrefetchScalarGridSpec(
            num_scalar_prefetch=2,
            grid=(n_slots // ch,),
            in_specs=[pl.BlockSpec((ch, w), slot),
                      pl.BlockSpec((None, d, de), lambda c, ce, nu: (ce[c], 0, 0)),
                      pl.BlockSpec((None, d, de), lambda c, ce, nu: (ce[c], 0, 0)),
                      pl.BlockSpec((None, de, d), lambda c, ce, nu: (ce[c], 0, 0))],
            out_specs=pl.BlockSpec((ch, d), slot)),
        compiler_params=_params("arbitrary"),
        name="experts",
    )(chunk_e, n_used, xs, wg, wu, wd)


def _combine_body(dest_ref, x1_ref, rinfo_ref, mod_ref, ys_ref, o_ref, buf_ref, sem):
    tf = x1_ref.shape[0]

    def row_copy(r, k, slot):
        return pltpu.make_async_copy(ys_ref.at[pl.ds(slot, 1)], buf_ref.at[k, pl.ds(r, 1)], sem)

    def issue(r, c):
        row_copy(r, 0, dest_ref[0, 2 * r]).start()
        row_copy(r, 1, dest_ref[0, 2 * r + 1]).start()
        return c

    def drain(r, c):
        row_copy(0, 0, 0).wait()
        row_copy(0, 0, 0).wait()
        return c

    lax.fori_loop(0, tf, issue, 0)
    lax.fori_loop(0, tf, drain, 0)
    w0 = rinfo_ref[:, 4:5]
    w1 = rinfo_ref[:, 5:6]
    y = w0 * buf_ref[0] + w1 * buf_ref[1]
    o_ref[...] = x1_ref[...] + mod_ref[:, 5 * D_MODEL:6 * D_MODEL] * y


def _combine(dest, x1, rinfo, mod3, ys, seq):
    t, d = x1.shape
    tf = min(COMBINE_TILE, seq)
    per_b = seq // tf
    dest3 = dest.reshape(t // tf, 1, 2 * tf)
    return pl.pallas_call(
        _combine_body,
        out_shape=jax.ShapeDtypeStruct((t, d), _F32),
        grid=(t // tf,),
        in_specs=[pl.BlockSpec((None, 1, 2 * tf), lambda i: (i, 0, 0), memory_space=pltpu.SMEM),
                  pl.BlockSpec((tf, d), lambda i: (i, 0)),
                  pl.BlockSpec((tf, ROUTER_LANES), lambda i: (i, 0)),
                  pl.BlockSpec((None, 1, mod3.shape[2]), lambda i: (i // per_b, 0, 0)),
                  pl.BlockSpec(memory_space=pl.ANY)],
        out_specs=pl.BlockSpec((tf, d), lambda i: (i, 0)),
        scratch_shapes=[pltpu.VMEM((2, tf, d), _F32), pltpu.SemaphoreType.DMA(())],
        compiler_params=_params("arbitrary"),
        name="combine",
    )(dest3, x1, rinfo, mod3, ys)


def _rope_tables(positions):
    half = ROPE_DIM // 2
    inv_freq = ROPE_THETA ** (-jnp.arange(0, ROPE_DIM, 2, dtype=_F32) / ROPE_DIM)
    ang = positions.reshape(-1).astype(_F32)[:, None] * inv_freq
    cos, sin = jnp.cos(ang), jnp.sin(ang)
    t = ang.shape[0]
    rest = HEAD_DIM - ROPE_DIM
    c = jnp.concatenate([cos, cos, jnp.ones((t, rest), _F32)], axis=1)
    s1 = jnp.concatenate([-sin, jnp.zeros((t, half + rest), _F32)], axis=1)
    s2 = jnp.concatenate([jnp.zeros((t, half), _F32), sin, jnp.zeros((t, rest), _F32)], axis=1)
    two = LANES // HEAD_DIM
    return tuple(jnp.tile(a, (1, two)) for a in (c, s1, s2))


def _layer(x, mod, positions, g_mix, g_ffn, w_in, w_bg, qg, kg, w_bd, w_bs, w_out, w_rg, b_rg, w_re, b_re,
           w_eg, w_eu, w_ed):
    batch, seq, d = x.shape
    t = batch * seq
    x2 = x.reshape(t, d)
    mod3 = mod.reshape(batch, 1, mod.shape[1])
    gw = GROUP_WIDTH

    lane = jnp.arange(gw)
    bd = jnp.where(lane[:, None] // HEAD_DIM == lane[None, :] // HEAD_DIM, 1.0 / HEAD_DIM, 0.0).astype(_BF16)
    ex = (jnp.arange(LANES)[:, None] == (lane[None, :] // HEAD_DIM) * LSE_SEG).astype(_BF16)
    tri = (jnp.arange(ROW_TILE)[:, None] > jnp.arange(ROW_TILE)[None, :]).astype(_BF16)
    rope = _rope_tables(positions)
    tile4 = lambda g: jnp.tile(g.astype(_F32), HEADS_PER_GROUP).reshape(1, gw)
    wr = jnp.zeros((d, ROUTER_LANES), _F32).at[:, :N_GROUPS].set(w_rg).at[:, N_GROUPS:N_GROUPS + N_EXPERTS].set(w_re)
    wrh = wr.astype(_BF16)
    wrl = (wr - wrh.astype(_F32)).astype(_BF16)
    br = jnp.zeros((1, ROUTER_LANES), _F32).at[0, :N_GROUPS].set(b_rg).at[0, N_GROUPS:N_GROUPS + N_EXPERTS].set(b_re)

    d0, d1, d2, sbp = _qkv(x2, mod3, g_mix.reshape(1, d), w_in.astype(_BF16), tile4(qg), tile4(kg), *rope, bd, seq)
    dil = [_dilated_group(a, dl, batch, seq) for a, (_, dl) in zip((d0, d1, d2), DIL_PATTERNS)]
    osb = _stick_breaking(sbp, batch, seq)

    x1, h2p, rinfo, cnt = _merge(
        x2, mod3, g_mix.reshape(1, d), g_ffn.reshape(1, d), [o for o, _ in dil], [l for _, l in dil], osb,
        w_bg.astype(_BF16), w_bd.astype(_BF16), w_bs.astype(_BF16), w_out.astype(_BF16), wrh, wrl, br, ex, tri, seq)

    ch = EXPERT_CHUNK
    counts = cnt[0, EXPERT_LANE0:EXPERT_LANE0 + N_EXPERTS].astype(jnp.int32)
    padded = (counts + ch - 1) // ch * ch
    pend = jnp.cumsum(padded)
    pstart = pend - padded
    ri = rinfo[:, :4].astype(jnp.int32)
    dest = jnp.stack([pstart[ri[:, 0]] + ri[:, 2], pstart[ri[:, 1]] + ri[:, 3]], axis=1).reshape(-1)
    n_chunks = -(-2 * t // ch) + N_EXPERTS
    chunk_e = jnp.minimum(jnp.searchsorted(pend, jnp.arange(n_chunks, dtype=jnp.int32) * ch, side="right"),
                          N_EXPERTS - 1).astype(jnp.int32)
    n_used = (pend[-1:] // ch).astype(jnp.int32)

    xs = _scatter(dest, h2p, n_chunks * ch)
    ys = _experts(chunk_e, n_used, xs, w_eg.astype(_BF16), w_eu.astype(_BF16), w_ed.astype(_BF16))
    out = _combine(dest, x1, rinfo, mod3, ys, seq)
    return out.reshape(batch, seq, d)


def kernel(x, c, positions, w_ada, b_ada, g_norm_mix, g_norm_ffn, w_in, w_branch_gate, q_norm_g, k_norm_g,
           w_branch_dil, w_branch_sb, w_out, w_router_group, b_router_group, w_router_expert, b_router_expert,
           w_expert_gate, w_expert_up, w_expert_down):
    for l in range(w_ada.shape[0]):
        mod = _ada(c, w_ada[l], b_ada[l])
        x = _layer(x, mod, positions, g_norm_mix[l], g_norm_ffn[l], w_in[l], w_branch_gate[l], q_norm_g[l],
                   k_norm_g[l], w_branch_dil[l], w_branch_sb[l], w_out[l], w_router_group[l], b_router_group[l],
                   w_router_expert[l], b_router_expert[l], w_expert_gate[l], w_expert_up[l], w_expert_down[l])
    return x
```

```python
import functools

import jax
import jax.numpy as jnp
from jax import lax
from jax.experimental import pallas as pl
from jax.experimental.pallas import tpu as pltpu

D_MODEL = 1024
HEAD_DIM = 64
DIL_PATTERNS = ((128, 1), (512, 4), (2048, 16))
HEADS_PER_GROUP = 4
GROUP_WIDTH = HEADS_PER_GROUP * HEAD_DIM
N_DIL_GROUPS = len(DIL_PATTERNS)
DIL_WIDTH = N_DIL_GROUPS * GROUP_WIDTH
QKV_WIDTH = 3 * DIL_WIDTH + 3 * GROUP_WIDTH
WINDOW_KEYS = 128
ROPE_THETA = 500000.0
ROPE_DIM = HEAD_DIM // 4
N_GROUPS = 4
EXPERTS_PER_GROUP = 8
N_EXPERTS = N_GROUPS * EXPERTS_PER_GROUP
D_EXPERT = 512
RMS_EPS = 1e-6
ATTN_SCALE = HEAD_DIM ** -0.5

LANES = 128
ROUTER_LANES = LANES
EXPERT_LANE0 = N_GROUPS
LSE_SEG = LANES // HEADS_PER_GROUP
NEG_BIG = -1e30
SB_DEAD_LOG = -120.0

ROW_TILE = 512
QBLK = 128
DIL_QTILE = 512
EXPERT_CHUNK = 256
SCATTER_TILE = 1024
COMBINE_TILE = 512
DMA_UNROLL = 8
VMEM_LIMIT = 48 * 1024 * 1024

_BF16 = jnp.bfloat16
_F32 = jnp.float32
_NT = (((1,), (1,)), ((), ()))


def _dot(a, b):
    return jnp.dot(a, b, preferred_element_type=_F32)


def _dot_nt(a, b):
    return lax.dot_general(a, b, _NT, preferred_element_type=_F32)


def _split(a):
    hi = a.astype(_BF16)
    lo = (a - hi.astype(_F32)).astype(_BF16)
    return hi, lo


def _dot3(a, b):
    ah, al = _split(a)
    bh, bl = _split(b)
    return _dot(ah, bh) + (_dot(ah, bl) + _dot(al, bh))


def _rms_mod(x, g, scale, shift):
    y = x * lax.rsqrt(jnp.mean(x * x, axis=-1, keepdims=True) + RMS_EPS)
    return y * g * (1.0 + scale) + shift


def _params(*sem):
    return pltpu.CompilerParams(dimension_semantics=sem, vmem_limit_bytes=VMEM_LIMIT)


def _ada_body(c_ref, w_ref, b_ref, o_ref):
    c = c_ref[...]
    o_ref[...] = _dot3(c * jax.nn.sigmoid(c), w_ref[...]) + b_ref[...]


def _ada(c, w_ada, b_ada):
    b, d = c.shape
    n = w_ada.shape[1]
    rows = -(-b // 16) * 16
    cp = jnp.zeros((rows, d), _F32).at[:b].set(c)
    nt = 1536
    out = pl.pallas_call(
        _ada_body,
        out_shape=jax.ShapeDtypeStruct((rows, n), _F32),
        grid=(n // nt,),
        in_specs=[pl.BlockSpec((rows, d), lambda j: (0, 0)),
                  pl.BlockSpec((d, nt), lambda j: (0, j)),
                  pl.BlockSpec((1, nt), lambda j: (0, j))],
        out_specs=pl.BlockSpec((rows, nt), lambda j: (0, j)),
        compiler_params=_params("arbitrary"),
        name="ada",
    )(cp, w_ada, b_ada.reshape(1, n))
    return out[:b]


def _qkv_body(x_ref, mod_ref, g_ref, w_ref, qg_ref, kg_ref, c_ref, s1_ref, s2_ref, bd_ref,
              o0_ref, o1_ref, o2_ref, osb_ref, st_ref):
    d = D_MODEL
    tm = x_ref.shape[0]
    h = _rms_mod(x_ref[...], g_ref[...], mod_ref[:, d:2 * d], mod_ref[:, 0:d])
    hb = h.astype(_BF16)
    cc = jnp.concatenate([c_ref[...]] * 2, axis=1)
    s1 = jnp.concatenate([s1_ref[...]] * 2, axis=1)
    s2 = jnp.concatenate([s2_ref[...]] * 2, axis=1)
    bd = bd_ref[...]
    gw = GROUP_WIDTH

    def normed_rotated(col0, gain):
        acc = _dot(hb, w_ref[:, col0:col0 + gw])
        ms = _dot((acc * acc).astype(_BF16), bd)
        y = acc * lax.rsqrt(ms + RMS_EPS) * gain
        return y * cc + pltpu.roll(y, gw - ROPE_DIM // 2, 1) * s1 + pltpu.roll(y, ROPE_DIM // 2, 1) * s2

    def store(o_ref, dil, part, y):
        if dil == 1:
            o_ref[:, part * gw:(part + 1) * gw] = y.astype(_BF16)
            return
        for s in range(gw // LANES):
            st_ref[s] = y[:, s * LANES:(s + 1) * LANES]
        for r in range(dil):
            for s in range(gw // LANES):
                col0 = (3 * r + part) * gw + s * LANES
                o_ref[:, col0:col0 + LANES] = st_ref[s, pl.ds(r, tm // dil, stride=dil), :].astype(_BF16)

    for g, o_ref in enumerate((o0_ref, o1_ref, o2_ref)):
        dil = DIL_PATTERNS[g][1]
        store(o_ref, dil, 0, normed_rotated(g * gw, qg_ref[...]))
        store(o_ref, dil, 1, normed_rotated(DIL_WIDTH + g * gw, kg_ref[...]))
        store(o_ref, dil, 2, _dot(hb, w_ref[:, 2 * DIL_WIDTH + g * gw:2 * DIL_WIDTH + (g + 1) * gw]))
    for part in range(3):
        col0 = 3 * DIL_WIDTH + part * gw
        osb_ref[:, part * gw:(part + 1) * gw] = _dot(hb, w_ref[:, col0:col0 + gw]).astype(_BF16)


def _qkv(x2, mod3, g_mix, w_in, qg, kg, rope_c, rope_s1, rope_s2, bd, seq):
    t, d = x2.shape
    tm = ROW_TILE
    per_b = seq // tm
    row = lambda i: (i, 0)
    const = lambda i: (0, 0)
    width = 3 * GROUP_WIDTH
    dils = [dl for _, dl in DIL_PATTERNS] + [1]
    return pl.pallas_call(
        _qkv_body,
        out_shape=[jax.ShapeDtypeStruct((t // dl, dl * width), _BF16) for dl in dils],
        grid=(t // tm,),
        in_specs=[pl.BlockSpec((tm, d), row),
                  pl.BlockSpec((None, 1, mod3.shape[2]), lambda i: (i // per_b, 0, 0)),
                  pl.BlockSpec((1, d), const),
                  pl.BlockSpec(w_in.shape, const),
                  pl.BlockSpec((1, GROUP_WIDTH), const),
                  pl.BlockSpec((1, GROUP_WIDTH), const),
                  pl.BlockSpec((tm, LANES), row),
                  pl.BlockSpec((tm, LANES), row),
                  pl.BlockSpec((tm, LANES), row),
                  pl.BlockSpec(bd.shape, const)],
        out_specs=[pl.BlockSpec((tm // dl, dl * width), row) for dl in dils],
        scratch_shapes=[pltpu.VMEM((GROUP_WIDTH // LANES, tm, LANES), _F32)],
        compiler_params=_params("arbitrary"),
        name="qkv",
    )(x2, mod3, g_mix, w_in, qg, kg, rope_c, rope_s1, rope_s2, bd)


def _dil_body(q_ref, kp_ref, kc_ref, vp_ref, vc_ref, o_ref, lse_ref, kf_ref, vf_ref):
    tq = q_ref.shape[0]
    first = pl.program_id(2) == 0
    kf_ref[0:QBLK, :] = kp_ref[...]
    kf_ref[QBLK:, :] = kc_ref[...]
    vf_ref[0:QBLK, :] = vp_ref[...]
    vf_ref[QBLK:, :] = vc_ref[...]
    nh = HEADS_PER_GROUP
    row = lax.broadcasted_iota(jnp.int32, (nh * QBLK, 2 * QBLK), 0) & (QBLK - 1)
    col = lax.broadcasted_iota(jnp.int32, (nh * QBLK, 2 * QBLK), 1)
    band = (col >= row) & (col <= row + WINDOW_KEYS)
    lane = lax.broadcasted_iota(jnp.int32, (1, GROUP_WIDTH), 1)
    slane = lax.broadcasted_iota(jnp.int32, (1, LANES), 1)
    head_masks = [(lane >= h * HEAD_DIM) & (lane < (h + 1) * HEAD_DIM) for h in range(nh)]
    for j in range(tq // QBLK):
        qj = q_ref[j * QBLK:(j + 1) * QBLK, :]
        qs = jnp.concatenate([jnp.where(hm, qj, jnp.zeros_like(qj)) for hm in head_masks], axis=0)
        kcat = kf_ref[j * QBLK:(j + 2) * QBLK, :]
        vcat = vf_ref[j * QBLK:(j + 2) * QBLK, :]
        valid = band & ((col >= QBLK) | jnp.logical_not(first)) if j == 0 else band
        s = jnp.where(valid, _dot_nt(qs, kcat) * ATTN_SCALE, NEG_BIG)
        m = jnp.max(s, axis=1, keepdims=True)
        p = jnp.exp(s - m)
        l = jnp.sum(p, axis=1, keepdims=True)
        o_all = _dot(p.astype(_BF16), vcat) / l
        lse_all = m + jnp.log(l)
        o_acc = jnp.zeros((QBLK, GROUP_WIDTH), _F32)
        lse_t = jnp.zeros((QBLK, LANES), _F32)
        for h, hm in enumerate(head_masks):
            o_acc = jnp.where(hm, o_all[h * QBLK:(h + 1) * QBLK, :], o_acc)
            sm = (slane >= h * LSE_SEG) & (slane < (h + 1) * LSE_SEG)
            lse_t = jnp.where(sm, lse_all[h * QBLK:(h + 1) * QBLK, :], lse_t)
        o_ref[j * QBLK:(j + 1) * QBLK, :] = o_acc.astype(_BF16)
        lse_ref[j * QBLK:(j + 1) * QBLK, :] = lse_t


def _dilated_group(view2, dil, batch, seq):
    sd = seq // dil
    tq = min(DIL_QTILE, sd)
    per = tq // QBLK
    gw = GROUP_WIDTH
    view = view2.reshape(batch, sd, dil * 3 * gw)
    cur = lambda part: pl.BlockSpec((None, tq, gw), lambda b, r, i: (b, i, 3 * r + part))
    prev = lambda part: pl.BlockSpec((None, QBLK, gw),
                                     lambda b, r, i: (b, jnp.maximum(i * per - 1, 0), 3 * r + part))
    o, lse = pl.pallas_call(
        _dil_body,
        out_shape=[jax.ShapeDtypeStruct((batch, sd, dil * gw), _BF16),
                   jax.ShapeDtypeStruct((batch, sd, dil * LANES), _F32)],
        grid=(batch, dil, sd // tq),
        in_specs=[cur(0), prev(1), cur(1), prev(2), cur(2)],
        out_specs=[pl.BlockSpec((None, tq, gw), lambda b, r, i: (b, i, r)),
                   pl.BlockSpec((None, tq, LANES), lambda b, r, i: (b, i, r))],
        scratch_shapes=[pltpu.VMEM((tq + QBLK, gw), _BF16), pltpu.VMEM((tq + QBLK, gw), _BF16)],
        compiler_params=_params("arbitrary", "arbitrary", "arbitrary"),
        name=f"dil{dil}",
    )(view, view, view, view, view)
    return o.reshape(batch * sd, dil * gw), lse.reshape(batch * sd, dil * LANES)


def _sb_body(q_ref, k_ref, v_ref, o_ref, carry_ref, acc_ref, qs_ref):
    i = pl.program_id(1)
    q = q_ref[...]
    nh = HEADS_PER_GROUP
    row = lax.broadcasted_iota(jnp.int32, (nh * QBLK, QBLK), 0) & (QBLK - 1)
    col = lax.broadcasted_iota(jnp.int32, (nh * QBLK, QBLK), 1)
    strict = col < row
    ur = lax.broadcasted_iota(jnp.int32, (QBLK, 2 * QBLK), 0)
    uc = lax.broadcasted_iota(jnp.int32, (QBLK, 2 * QBLK), 1)
    u = jnp.where((uc >= QBLK) | (ur > uc), 1.0, 0.0).astype(_BF16)
    lane = lax.broadcasted_iota(jnp.int32, (1, GROUP_WIDTH), 1)
    head_masks = [(lane >= h * HEAD_DIM) & (lane < (h + 1) * HEAD_DIM) for h in range(nh)]
    qs_ref[...] = jnp.concatenate([jnp.where(hm, q, jnp.zeros_like(q)) for hm in head_masks], axis=0)

    def tile(kb, diag):
        start = pl.multiple_of(kb * QBLK, QBLK)
        z = _dot_nt(qs_ref[...], k_ref[pl.ds(start, QBLK), :]) * ATTN_SCALE
        sp = jnp.maximum(z, 0.0) + jnp.log1p(jnp.exp(-jnp.abs(z)))
        log_1m = -sp
        if diag:
            log_1m = jnp.where(strict, log_1m, 0.0)
        hi, lo = _split(log_1m)
        sums = _dot(hi, u) + _dot(lo, u)
        a = jnp.exp((z - sp) + sums[:, :QBLK] + carry_ref[...])
        if diag:
            a = jnp.where(strict, a, 0.0)
        acc_ref[...] += _dot(a.astype(_BF16), v_ref[pl.ds(start, QBLK), :])
        carry = carry_ref[...] + sums[:, QBLK:]
        carry_ref[...] = carry
        return jnp.max(carry)

    carry_ref[...] = jnp.zeros_like(carry_ref)
    acc_ref[...] = jnp.zeros_like(acc_ref)
    top = tile(i, True)

    def cond(st):
        return (st[0] >= 0) & (st[1] > SB_DEAD_LOG)

    def step(st):
        return st[0] - 1, tile(st[0], False)

    lax.while_loop(cond, step, (i - 1, top))
    out = jnp.zeros((QBLK, GROUP_WIDTH), _F32)
    for h, hm in enumerate(head_masks):
        out = jnp.where(hm, acc_ref[h * QBLK:(h + 1) * QBLK, :], out)
    o_ref[...] = out.astype(_BF16)


def _stick_breaking(arr, batch, seq):
    gw = GROUP_WIDTH
    view = arr.reshape(batch, seq, 3 * gw)
    o = pl.pallas_call(
        _sb_body,
        out_shape=jax.ShapeDtypeStruct((batch, seq, gw), _BF16),
        grid=(batch, seq // QBLK),
        in_specs=[pl.BlockSpec((None, QBLK, gw), lambda b, i: (b, i, 0)),
                  pl.BlockSpec((None, seq, gw), lambda b, i: (b, 0, 1)),
                  pl.BlockSpec((None, seq, gw), lambda b, i: (b, 0, 2))],
        out_specs=pl.BlockSpec((None, QBLK, gw), lambda b, i: (b, i, 0)),
        scratch_shapes=[pltpu.VMEM((HEADS_PER_GROUP * QBLK, QBLK), _F32),
                        pltpu.VMEM((HEADS_PER_GROUP * QBLK, gw), _F32),
                        pltpu.VMEM((HEADS_PER_GROUP * QBLK, gw), _BF16)],
        compiler_params=_params("arbitrary", "arbitrary"),
        name="sb",
    )(view, view, view)
    return o.reshape(batch * seq, gw)


def _merge_body(x_ref, mod_ref, g1_ref, g2_ref, o0_ref, o1_ref, o2_ref, l0_ref, l1_ref, l2_ref, osb_ref,
                wbg_ref, wbd_ref, wbs_ref, wout_ref, wrh_ref, wrl_ref, br_ref, ex_ref, tri_ref,
                x1_ref, h2p_ref, rinfo_ref, cnt_out_ref, cnt_ref, os1_ref, os2_ref, ls1_ref, ls2_ref):
    d = D_MODEL
    tm = x_ref.shape[0]
    x = x_ref[...]
    hb = _rms_mod(x, g1_ref[...], mod_ref[:, d:2 * d], mod_ref[:, 0:d]).astype(_BF16)
    gates = jax.nn.sigmoid(_dot(hb, wbg_ref[...]))

    def natural(ref, st_ref, dil):
        if dil == 1:
            return ref[...].astype(_F32)
        slabs = st_ref.shape[0]
        for r in range(dil):
            for s in range(slabs):
                col0 = (r * slabs + s) * LANES
                st_ref[s, pl.ds(r, tm // dil, stride=dil), :] = ref[:, col0:col0 + LANES].astype(_F32)
        return jnp.concatenate([st_ref[s] for s in range(slabs)], axis=1)

    dils = [dl for _, dl in DIL_PATTERNS]
    o_nat = [natural(r, s, dl) for r, s, dl in zip((o0_ref, o1_ref, o2_ref), (None, os1_ref, os2_ref), dils)]
    l0, l1, l2 = [natural(r, s, dl) for r, s, dl in zip((l0_ref, l1_ref, l2_ref), (None, ls1_ref, ls2_ref), dils)]

    lmax = jnp.maximum(jnp.maximum(l0, l1), l2)
    e0, e1, e2 = jnp.exp(l0 - lmax), jnp.exp(l1 - lmax), jnp.exp(l2 - lmax)
    inv = 1.0 / (e0 + e1 + e2)
    ex = ex_ref[...]

    def widen(w):
        hi, lo = _split(w)
        return _dot(hi, ex) + _dot(lo, ex)

    o_dil = widen(e0 * inv) * o_nat[0] + widen(e1 * inv) * o_nat[1] + widen(e2 * inv) * o_nat[2]
    merged = (gates[:, :d] * _dot(o_dil.astype(_BF16), wbd_ref[...])
              + gates[:, d:] * _dot(osb_ref[...], wbs_ref[...]))
    x1 = x + mod_ref[:, 2 * d:3 * d] * _dot(merged.astype(_BF16), wout_ref[...])
    x1_ref[...] = x1

    h2 = _rms_mod(x1, g2_ref[...], mod_ref[:, 4 * d:5 * d], mod_ref[:, 3 * d:4 * d])
    half = d // 2
    lo_bits = lax.bitcast_convert_type(h2[:, :half].astype(_BF16).astype(_F32), jnp.uint32) >> 16
    hi_bits = lax.bitcast_convert_type(h2[:, half:].astype(_BF16).astype(_F32), jnp.uint32) & jnp.uint32(0xFFFF0000)
    h2p_ref[...] = lo_bits | hi_bits

    hh, hl = _split(h2)
    logits = _dot(hh, wrh_ref[...]) + (_dot(hl, wrh_ref[...]) + _dot(hh, wrl_ref[...])) + br_ref[...]
    lane =lax.broadcasted_iota(jnp.int32, (tm, ROUTER_LANES), 1).astype(_F32)
    far = float(ROUTER_LANES)

    def top(vals):
        m = jnp.max(vals, axis=1, keepdims=True)
        return m, jnp.min(jnp.where(vals == m, lane, far), axis=1, keepdims=True)

    is_group = lane < N_GROUPS
    mg, gsel = top(jnp.where(is_group, logits, NEG_BIG))
    pg_top = 1.0 / jnp.sum(jnp.where(is_group, jnp.exp(logits - mg), 0.0), axis=1, keepdims=True)
    lane0 = EXPERT_LANE0 + EXPERTS_PER_GROUP * gsel
    le = jnp.where((lane >= lane0) & (lane < lane0 + EXPERTS_PER_GROUP), logits, NEG_BIG)
    m1, i1 = top(le)
    m2, i2 = top(jnp.where(lane == i1, NEG_BIG, le))
    t2 = jnp.exp(m2 - m1)
    w0 = pg_top / (1.0 + t2)
    w1 = pg_top * t2 / (1.0 + t2)

    @pl.when(pl.program_id(0) == 0)
    def _():
        cnt_ref[...] = jnp.zeros_like(cnt_ref)

    sel0, sel1 = lane == i1, lane == i2
    onehot = jnp.where(sel0 | sel1, 1.0, 0.0)
    before = _dot(tri_ref[...], onehot.astype(_BF16)) + cnt_ref[0:1, :]
    r0 = jnp.sum(jnp.where(sel0, before, 0.0), axis=1, keepdims=True)
    r1 = jnp.sum(jnp.where(sel1, before, 0.0), axis=1, keepdims=True)
    cnt_ref[...] += jnp.sum(onehot, axis=0, keepdims=True)
    cnt_out_ref[...] = cnt_ref[...]

    cols = (i1 - EXPERT_LANE0, i2 - EXPERT_LANE0, r0, r1, w0, w1)
    rinfo = jnp.zeros((tm, ROUTER_LANES), _F32)
    for c, v in enumerate(cols):
        rinfo = jnp.where(lane == float(c), v, rinfo)
    rinfo_ref[...] = rinfo


def _merge(x2, mod3, g1, g2, outs, lses, osb, wbg, wbd, wbs, wout, wrh, wrl, br, ex, tri, seq):
    t, d = x2.shape
    tm = ROW_TILE
    per_b = seq // tm
    row = lambda i: (i, 0)
    const = lambda i: (0, 0)
    full = lambda a: pl.BlockSpec(a.shape, const)
    gw = GROUP_WIDTH
    dils = [dl for _, dl in DIL_PATTERNS]
    return pl.pallas_call(
        _merge_body,
        out_shape=[jax.ShapeDtypeStruct((t, d), _F32),
                   jax.ShapeDtypeStruct((t, d // 2), jnp.uint32),
                   jax.ShapeDtypeStruct((t, ROUTER_LANES), _F32),
                   jax.ShapeDtypeStruct((8, ROUTER_LANES), _F32)],
        grid=(t // tm,),
        in_specs=[pl.BlockSpec((tm, d), row),
                  pl.BlockSpec((None, 1, mod3.shape[2]), lambda i: (i // per_b, 0, 0)),
                  full(g1), full(g2)]
                 + [pl.BlockSpec((tm // dl, dl * gw), row) for dl in dils]
                 + [pl.BlockSpec((tm // dl, dl * LANES), row) for dl in dils]
                 + [pl.BlockSpec((tm, gw), row)]
                 + [full(a) for a in (wbg, wbd, wbs, wout, wrh, wrl, br, ex, tri)],
        out_specs=[pl.BlockSpec((tm, d), row),
                   pl.BlockSpec((tm, d // 2), row),
                   pl.BlockSpec((tm, ROUTER_LANES), row),
                   pl.BlockSpec((8, ROUTER_LANES), const)],
        scratch_shapes=[pltpu.VMEM((8, ROUTER_LANES), _F32),
                        pltpu.VMEM((gw // LANES, tm, LANES), _F32), pltpu.VMEM((gw // LANES, tm, LANES), _F32),
                        pltpu.VMEM((1, tm, LANES), _F32), pltpu.VMEM((1, tm, LANES), _F32)],
        compiler_params=_params("arbitrary"),
        name="merge",
    )(x2, mod3, g1, g2, *outs, *lses, osb, wbg, wbd, wbs, wout, wrh, wrl, br, ex, tri)


def _scatter_body(ps_ref, pe_ref, er_ref, h_ref, xs_ref, z_ref, sem, zsem):
    ts = h_ref.shape[0]
    ch = z_ref.shape[0]

    @pl.when(pl.program_id(0) == 0)
    def _():
        z_ref[...] = jnp.zeros_like(z_ref)

        def tail(e):
            start = pl.multiple_of(pe_ref[e] - ch, ch)
            return pltpu.make_async_copy(z_ref, xs_ref.at[pl.ds(start, ch)], zsem)

        for e in range(N_EXPERTS):
            @pl.when(pe_ref[e] > ps_ref[e])
            def _():
                tail(e).start()
        for e in range(N_EXPERTS):
            @pl.when(pe_ref[e] > ps_ref[e])
            def _():
                tail(e).wait()

    def row_copy(r, slot):
        return pltpu.make_async_copy(h_ref.at[pl.ds(r, 1)], xs_ref.at[pl.ds(slot, 1)], sem)

    def issue(r, c):
        row_copy(r, ps_ref[er_ref[0, 4 * r]] + er_ref[0, 4 * r + 2]).start(priority=0)
        row_copy(r, ps_ref[er_ref[0, 4 * r + 1]] + er_ref[0, 4 * r + 3]).start(priority=1)
        return c

    def drain(r, c):
        row_copy(0, 0).wait()
        row_copy(0, 0).wait()
        return c

    lax.fori_loop(0, ts, issue, 0, unroll=DMA_UNROLL)
    lax.fori_loop(0, ts, drain, 0, unroll=DMA_UNROLL)


def _scatter(pstart, pend, er, h2p, n_slots):
    t, w = h2p.shape
    ts = min(SCATTER_TILE, t)
    er3 = er.reshape(t // ts, 1, 4 * ts)
    return pl.pallas_call(
        _scatter_body,
        out_shape=jax.ShapeDtypeStruct((n_slots, w), jnp.uint32),
        grid_spec=pltpu.PrefetchScalarGridSpec(
            num_scalar_prefetch=2,
            grid=(t // ts,),
            in_specs=[pl.BlockSpec((None, 1, 4 * ts), lambda i, ps, pe: (i, 0, 0), memory_space=pltpu.SMEM),
                      pl.BlockSpec((ts, w), lambda i, ps, pe: (i, 0))],
            out_specs=pl.BlockSpec(memory_space=pl.ANY),
            scratch_shapes=[pltpu.VMEM((EXPERT_CHUNK, w), jnp.uint32),
                            pltpu.SemaphoreType.DMA(()), pltpu.SemaphoreType.DMA(())]),
        compiler_params=_params("arbitrary"),
        name="scatter",
    )(pstart, pend, er3, h2p)


def _experts_body(ce_ref, nu_ref, xs_ref, wg_ref, wu_ref, wd_ref, ys_ref):
    del ce_ref

    @pl.when(pl.program_id(0) < nu_ref[0])
    def _():
        w = xs_ref[...]
        x = jnp.concatenate(
            [lax.bitcast_convert_type(w << 16, _F32), lax.bitcast_convert_type(w & jnp.uint32(0xFFFF0000), _F32)],
            axis=1).astype(_BF16)
        g = _dot(x, wg_ref[...])
        hmid = (g * jax.nn.sigmoid(g)) * _dot(x, wu_ref[...])
        ys_ref[...] = _dot(hmid.astype(_BF16), wd_ref[...])


def _experts(chunk_e, n_used, xs, wg, wu, wd):
    n_slots, w = xs.shape
    ch = EXPERT_CHUNK
    d, de = wg.shape[1], wg.shape[2]
    slot = lambda c, ce, nu: (jnp.minimum(c, nu[0] - 1), 0)
    return pl.pallas_call(
        _experts_body,
        out_shape=jax.ShapeDtypeStruct((n_slots, d), _F32),
        grid_spec=pltpu.PrefetchScalarGridSpec(
            num_scalar_prefetch=2,
            grid=(n_slots // ch,),
            in_specs=[pl.BlockSpec((ch, w), slot),
                      pl.BlockSpec((None, d, de), lambda c, ce, nu: (ce[c], 0, 0)),
                      pl.BlockSpec((None, d, de), lambda c, ce, nu: (ce[c], 0, 0)),
                      pl.BlockSpec((None, de, d), lambda c, ce, nu: (ce[c], 0, 0))],
            out_specs=pl.BlockSpec((ch, d), slot)),
        compiler_params=_params("arbitrary"),
        name="experts",
    )(chunk_e, n_used, xs, wg, wu, wd)


def _combine_body(ps_ref, er_ref, x1_ref, rinfo_ref, mod_ref, ys_ref, o_ref, buf_ref, sem):
    tf = x1_ref.shape[0]

    def row_copy(r, k, slot):
        return pltpu.make_async_copy(ys_ref.at[pl.ds(slot, 1)], buf_ref.at[k, pl.ds(r, 1)], sem)

    def issue(r, c):
        row_copy(r, 0, ps_ref[er_ref[0, 4 * r]] + er_ref[0, 4 * r + 2]).start(priority=0)
        row_copy(r, 1, ps_ref[er_ref[0, 4 * r + 1]] + er_ref[0, 4 * r + 3]).start(priority=1)
        return c

    def drain(r, c):
        row_copy(0, 0, 0).wait()
        row_copy(0, 0, 0).wait()
        return c

    lax.fori_loop(0, tf, issue, 0, unroll=DMA_UNROLL)
    lax.fori_loop(0, tf, drain, 0, unroll=DMA_UNROLL)
    w0 = rinfo_ref[:, 4:5]
    w1 = rinfo_ref[:, 5:6]
    y = w0 * buf_ref[0] + w1 * buf_ref[1]
    o_ref[...] = x1_ref[...] + mod_ref[:, 5 * D_MODEL:6 * D_MODEL] * y


def _combine(pstart, er, x1, rinfo, mod3, ys, seq):
    t, d = x1.shape
    tf = min(COMBINE_TILE, seq)
    per_b = seq // tf
    er3 = er.reshape(t // tf, 1, 4 * tf)
    return pl.pallas_call(
        _combine_body,
        out_shape=jax.ShapeDtypeStruct((t, d), _F32),
        grid_spec=pltpu.PrefetchScalarGridSpec(
            num_scalar_prefetch=1,
            grid=(t // tf,),
            in_specs=[pl.BlockSpec((None, 1, 4 * tf), lambda i, ps: (i, 0, 0), memory_space=pltpu.SMEM),
                      pl.BlockSpec((tf, d), lambda i, ps: (i, 0)),
                      pl.BlockSpec((tf, ROUTER_LANES), lambda i, ps: (i, 0)),
                      pl.BlockSpec((None, 1, mod3.shape[2]), lambda i, ps: (i // per_b, 0, 0)),
                      pl.BlockSpec(memory_space=pl.ANY)],
            out_specs=pl.BlockSpec((tf, d), lambda i, ps: (i, 0)),
            scratch_shapes=[pltpu.VMEM((2, tf, d), _F32), pltpu.SemaphoreType.DMA(())]),
        compiler_params=_params("arbitrary"),
        name="combine",
    )(pstart, er3, x1, rinfo, mod3, ys)


def _rope_tables(positions):
    half = ROPE_DIM // 2
    inv_freq = ROPE_THETA ** (-jnp.arange(0, ROPE_DIM, 2, dtype=_F32) / ROPE_DIM)
    ang = positions.reshape(-1).astype(_F32)[:, None] * inv_freq
    cos, sin = jnp.cos(ang), jnp.sin(ang)
    t = ang.shape[0]
    rest = HEAD_DIM - ROPE_DIM
    c = jnp.concatenate([cos, cos, jnp.ones((t, rest), _F32)], axis=1)
    s1 = jnp.concatenate([-sin, jnp.zeros((t, half + rest), _F32)], axis=1)
    s2 = jnp.concatenate([jnp.zeros((t, half), _F32), sin, jnp.zeros((t, rest), _F32)], axis=1)
    two = LANES // HEAD_DIM
    return tuple(jnp.tile(a, (1, two)) for a in (c, s1, s2))


def _layer(x, mod, positions, g_mix, g_ffn, w_in, w_bg, qg, kg, w_bd, w_bs, w_out, w_rg, b_rg, w_re, b_re,
           w_eg, w_eu, w_ed):
    batch, seq, d = x.shape
    t = batch * seq
    x2 = x.reshape(t, d)
    mod3 = mod.reshape(batch, 1, mod.shape[1])
    gw = GROUP_WIDTH

    lane = jnp.arange(gw)
    bd = jnp.where(lane[:, None] // HEAD_DIM == lane[None, :] // HEAD_DIM, 1.0 / HEAD_DIM, 0.0).astype(_BF16)
    ex = (jnp.arange(LANES)[:, None] == (lane[None, :] // HEAD_DIM) * LSE_SEG).astype(_BF16)
    tri = (jnp.arange(ROW_TILE)[:, None] > jnp.arange(ROW_TILE)[None, :]).astype(_BF16)
    rope = _rope_tables(positions)
    tile4 = lambda g: jnp.tile(g.astype(_F32), HEADS_PER_GROUP).reshape(1, gw)
    wr = jnp.zeros((d, ROUTER_LANES), _F32).at[:, :N_GROUPS].set(w_rg).at[:, N_GROUPS:N_GROUPS + N_EXPERTS].set(w_re)
    wrh = wr.astype(_BF16)
    wrl = (wr - wrh.astype(_F32)).astype(_BF16)
    br = jnp.zeros((1, ROUTER_LANES), _F32).at[0, :N_GROUPS].set(b_rg).at[0, N_GROUPS:N_GROUPS + N_EXPERTS].set(b_re)

    d0, d1, d2, sbp = _qkv(x2, mod3, g_mix.reshape(1, d), w_in.astype(_BF16), tile4(qg), tile4(kg), *rope, bd, seq)
    dil = [_dilated_group(a, dl, batch, seq) for a, (_, dl) in zip((d0, d1, d2), DIL_PATTERNS)]
    osb = _stick_breaking(sbp, batch, seq)

    x1, h2p, rinfo, cnt = _merge(
        x2, mod3, g_mix.reshape(1, d), g_ffn.reshape(1, d), [o for o, _ in dil], [l for _, l in dil], osb,
        w_bg.astype(_BF16), w_bd.astype(_BF16), w_bs.astype(_BF16), w_out.astype(_BF16), wrh, wrl, br, ex, tri, seq)

    ch = EXPERT_CHUNK
    counts = cnt[0, EXPERT_LANE0:EXPERT_LANE0 + N_EXPERTS].astype(jnp.int32)
    padded = (counts + ch - 1) // ch * ch
    pend = jnp.cumsum(padded)
    pstart = pend - padded
    er = rinfo[:, :4].astype(jnp.int32)
    n_chunks = -(-2 * t // ch) + N_EXPERTS
    chunk_start = jnp.arange(n_chunks, dtype=jnp.int32) * ch
    chunk_e = jnp.minimum(jnp.sum((pend[None, :] <= chunk_start[:, None]).astype(jnp.int32), axis=1), N_EXPERTS - 1)
    n_used = (pend[-1:] // ch).astype(jnp.int32)

    xs = _scatter(pstart, pend, er, h2p, n_chunks * ch)
    ys = _experts(chunk_e, n_used, xs, w_eg.astype(_BF16), w_eu.astype(_BF16), w_ed.astype(_BF16))
    out = _combine(pstart, er, x1, rinfo, mod3, ys, seq)
    return out.reshape(batch, seq, d)


def kernel(x, c, positions, w_ada, b_ada, g_norm_mix, g_norm_ffn, w_in, w_branch_gate, q_norm_g, k_norm_g,
           w_branch_dil, w_branch_sb, w_out, w_router_group, b_router_group, w_router_expert, b_router_expert,
           w_expert_gate, w_expert_up, w_expert_down):
    for l in range(w_ada.shape[0]):
        mod = _ada(c, w_ada[l], b_ada[l])
        x = _layer(x, mod, positions, g_norm_mix[l], g_norm_ffn[l], w_in[l], w_branch_gate[l], q_norm_g[l],
                   k_norm_g[l], w_branch_dil[l], w_branch_sb[l], w_out[l], w_router_group[l], b_router_group[l],
                   w_router_expert[l], b_router_expert[l], w_expert_gate[l], w_expert_up[l], w_expert_down[l])
    return x
```

```python
import functools

import jax
import jax.numpy as jnp
from jax import lax
from jax.experimental import pallas as pl
from jax.experimental.pallas import tpu as pltpu

D_MODEL = 1024
HEAD_DIM = 64
DIL_PATTERNS = ((128, 1), (512, 4), (2048, 16))
HEADS_PER_GROUP = 4
GROUP_WIDTH = HEADS_PER_GROUP * HEAD_DIM
N_DIL_GROUPS = len(DIL_PATTERNS)
DIL_WIDTH = N_DIL_GROUPS * GROUP_WIDTH
QKV_WIDTH = 3 * DIL_WIDTH + 3 * GROUP_WIDTH
WINDOW_KEYS = 128
ROPE_THETA = 500000.0
ROPE_DIM = HEAD_DIM // 4
N_GROUPS = 4
EXPERTS_PER_GROUP = 8
N_EXPERTS = N_GROUPS * EXPERTS_PER_GROUP
D_EXPERT = 512
RMS_EPS = 1e-6
ATTN_SCALE = HEAD_DIM ** -0.5

LANES = 128
ROUTER_LANES = LANES
EXPERT_LANE0 = N_GROUPS
LSE_SEG = LANES // HEADS_PER_GROUP
NEG_BIG = -1e30
SB_DEAD_LOG = -120.0
SB_HEAD_BLOCKS = 3

ROW_TILE = 512
QBLK = 128
DIL_QTILE = 512
EXPERT_CHUNK = 256
SCATTER_TILE = 1024
COMBINE_TILE = 512
DMA_UNROLL = 8
VMEM_LIMIT = 48 * 1024 * 1024

_BF16 = jnp.bfloat16
_F32 = jnp.float32
_NT = (((1,), (1,)), ((), ()))


def _dot(a, b):
    return jnp.dot(a, b, preferred_element_type=_F32)


def _dot_nt(a, b):
    return lax.dot_general(a, b, _NT, preferred_element_type=_F32)


def _split(a):
    hi = a.astype(_BF16)
    lo = (a - hi.astype(_F32)).astype(_BF16)
    return hi, lo


def _dot3(a, b):
    ah, al = _split(a)
    bh, bl = _split(b)
    return _dot(ah, bh) + (_dot(ah, bl) + _dot(al, bh))


def _rms_mod(x, g, scale, shift):
    y = x * lax.rsqrt(jnp.mean(x * x, axis=-1, keepdims=True) + RMS_EPS)
    return y * g * (1.0 + scale) + shift


def _params(*sem):
    return pltpu.CompilerParams(dimension_semantics=sem, vmem_limit_bytes=VMEM_LIMIT)


def _ada_body(c_ref, w_ref, b_ref, o_ref):
    c = c_ref[...]
    o_ref[...] = _dot3(c * jax.nn.sigmoid(c), w_ref[...]) + b_ref[...]


def _ada(c, w_ada, b_ada):
    b, d = c.shape
    n = w_ada.shape[1]
    rows = -(-b // 16) * 16
    cp = jnp.zeros((rows, d), _F32).at[:b].set(c)
    nt = 1536
    out = pl.pallas_call(
        _ada_body,
        out_shape=jax.ShapeDtypeStruct((rows, n), _F32),
        grid=(n // nt,),
        in_specs=[pl.BlockSpec((rows, d), lambda j: (0, 0)),
                  pl.BlockSpec((d, nt), lambda j: (0, j)),
                  pl.BlockSpec((1, nt), lambda j: (0, j))],
        out_specs=pl.BlockSpec((rows, nt), lambda j: (0, j)),
        compiler_params=_params("arbitrary"),
        name="ada",
    )(cp, w_ada, b_ada.reshape(1, n))
    return out[:b]


def _qkv_body(x_ref, mod_ref, g_ref, w_ref, qg_ref, kg_ref, c_ref, s1_ref, s2_ref, bd_ref,
              o0_ref, o1_ref, o2_ref, osb_ref, st_ref):
    d = D_MODEL
    tm = x_ref.shape[0]
    h = _rms_mod(x_ref[...], g_ref[...], mod_ref[:, d:2 * d], mod_ref[:, 0:d])
    hb = h.astype(_BF16)
    cc = jnp.concatenate([c_ref[...]] * 2, axis=1)
    s1 = jnp.concatenate([s1_ref[...]] * 2, axis=1)
    s2 = jnp.concatenate([s2_ref[...]] * 2, axis=1)
    bd = bd_ref[...]
    gw = GROUP_WIDTH

    def normed_rotated(col0, gain):
        acc = _dot(hb, w_ref[:, col0:col0 + gw])
        ms = _dot((acc * acc).astype(_BF16), bd)
        y = acc * lax.rsqrt(ms + RMS_EPS) * gain
        return y * cc + pltpu.roll(y, gw - ROPE_DIM // 2, 1) * s1 + pltpu.roll(y, ROPE_DIM // 2, 1) * s2

    def store(o_ref, dil, part, y):
        if dil == 1:
            o_ref[:, part * gw:(part + 1) * gw] = y.astype(_BF16)
            return
        for s in range(gw // LANES):
            st_ref[s] = y[:, s * LANES:(s + 1) * LANES]
        for r in range(dil):
            for s in range(gw // LANES):
                col0 = (3 * r + part) * gw + s * LANES
                o_ref[:, col0:col0 + LANES] = st_ref[s, pl.ds(r, tm // dil, stride=dil), :].astype(_BF16)

    for g, o_ref in enumerate((o0_ref, o1_ref, o2_ref)):
        dil = DIL_PATTERNS[g][1]
        store(o_ref, dil, 0, normed_rotated(g * gw, qg_ref[...]))
        store(o_ref, dil, 1, normed_rotated(DIL_WIDTH + g * gw, kg_ref[...]))
        store(o_ref, dil, 2, _dot(hb, w_ref[:, 2 * DIL_WIDTH + g * gw:2 * DIL_WIDTH + (g + 1) * gw]))
    for part in range(3):
        col0 = 3 * DIL_WIDTH + part * gw
        osb_ref[:, part * gw:(part + 1) * gw] = _dot(hb, w_ref[:, col0:col0 + gw]).astype(_BF16)


def _qkv(x2, mod3, g_mix, w_in, qg, kg, rope_c, rope_s1, rope_s2, bd, seq):
    t, d = x2.shape
    tm = ROW_TILE
    per_b = seq // tm
    row = lambda i: (i, 0)
    const = lambda i: (0, 0)
    width = 3 * GROUP_WIDTH
    dils = [dl for _, dl in DIL_PATTERNS] + [1]
    return pl.pallas_call(
        _qkv_body,
        out_shape=[jax.ShapeDtypeStruct((t // dl, dl * width), _BF16) for dl in dils],
        grid=(t // tm,),
        in_specs=[pl.BlockSpec((tm, d), row),
                  pl.BlockSpec((None, 1, mod3.shape[2]), lambda i: (i // per_b, 0, 0)),
                  pl.BlockSpec((1, d), const),
                  pl.BlockSpec(w_in.shape, const),
                  pl.BlockSpec((1, GROUP_WIDTH), const),
                  pl.BlockSpec((1, GROUP_WIDTH), const),
                  pl.BlockSpec((tm, LANES), row),
                  pl.BlockSpec((tm, LANES), row),
                  pl.BlockSpec((tm, LANES), row),
                  pl.BlockSpec(bd.shape, const)],
        out_specs=[pl.BlockSpec((tm // dl, dl * width), row) for dl in dils],
        scratch_shapes=[pltpu.VMEM((GROUP_WIDTH // LANES, tm, LANES), _F32)],
        compiler_params=_params("arbitrary"),
        name="qkv",
    )(x2, mod3, g_mix, w_in, qg, kg, rope_c, rope_s1, rope_s2, bd)


def _dil_body(q_ref, kp_ref, kc_ref, vp_ref, vc_ref, o_ref, lse_ref, kf_ref, vf_ref):
    tq = q_ref.shape[0]
    first = pl.program_id(2) == 0
    kf_ref[0:QBLK, :] = kp_ref[...]
    kf_ref[QBLK:, :] = kc_ref[...]
    vf_ref[0:QBLK, :] = vp_ref[...]
    vf_ref[QBLK:, :] = vc_ref[...]
    nh = HEADS_PER_GROUP
    row = lax.broadcasted_iota(jnp.int32, (nh * QBLK, 2 * QBLK), 0) & (QBLK - 1)
    col = lax.broadcasted_iota(jnp.int32, (nh * QBLK, 2 * QBLK), 1)
    band = (col >= row) & (col <= row + WINDOW_KEYS)
    lane = lax.broadcasted_iota(jnp.int32, (1, GROUP_WIDTH), 1)
    slane = lax.broadcasted_iota(jnp.int32, (1, LANES), 1)
    head_masks = [(lane >= h * HEAD_DIM) & (lane < (h + 1) * HEAD_DIM) for h in range(nh)]
    for j in range(tq // QBLK):
        qj = q_ref[j * QBLK:(j + 1) * QBLK, :]
        qs = jnp.concatenate([jnp.where(hm, qj, jnp.zeros_like(qj)) for hm in head_masks], axis=0)
        kcat = kf_ref[j * QBLK:(j + 2) * QBLK, :]
        vcat = vf_ref[j * QBLK:(j + 2) * QBLK, :]
        valid = band & ((col >= QBLK) | jnp.logical_not(first)) if j == 0 else band
        s = jnp.where(valid, _dot_nt(qs, kcat) * ATTN_SCALE, NEG_BIG)
        m = jnp.max(s, axis=1, keepdims=True)
        p = jnp.exp(s - m)
        l = jnp.sum(p, axis=1, keepdims=True)
        o_all = _dot(p.astype(_BF16), vcat) / l
        lse_all = m + jnp.log(l)
        o_acc = jnp.zeros((QBLK, GROUP_WIDTH), _F32)
        lse_t = jnp.zeros((QBLK, LANES), _F32)
        for h, hm in enumerate(head_masks):
            o_acc = jnp.where(hm, o_all[h * QBLK:(h + 1) * QBLK, :], o_acc)
            sm = (slane >= h * LSE_SEG) & (slane < (h + 1) * LSE_SEG)
            lse_t = jnp.where(sm, lse_all[h * QBLK:(h + 1) * QBLK, :], lse_t)
        o_ref[j * QBLK:(j + 1) * QBLK, :] = o_acc.astype(_BF16)
        lse_ref[j * QBLK:(j + 1) * QBLK, :] = lse_t


def _dilated_group(view2, dil, batch, seq):
    sd = seq // dil
    tq = min(DIL_QTILE, sd)
    per = tq // QBLK
    gw = GROUP_WIDTH
    view = view2.reshape(batch, sd, dil * 3 * gw)
    cur = lambda part: pl.BlockSpec((None, tq, gw), lambda b, r, i: (b, i, 3 * r + part))
    prev = lambda part: pl.BlockSpec((None, QBLK, gw),
                                     lambda b, r, i: (b, jnp.maximum(i * per - 1, 0), 3 * r + part))
    o, lse = pl.pallas_call(
        _dil_body,
        out_shape=[jax.ShapeDtypeStruct((batch, sd, dil * gw), _BF16),
                   jax.ShapeDtypeStruct((batch, sd, dil * LANES), _F32)],
        grid=(batch, dil, sd // tq),
        in_specs=[cur(0), prev(1), cur(1), prev(2), cur(2)],
        out_specs=[pl.BlockSpec((None, tq, gw), lambda b, r, i: (b, i, r)),
                   pl.BlockSpec((None, tq, LANES), lambda b, r, i: (b, i, r))],
        scratch_shapes=[pltpu.VMEM((tq + QBLK, gw), _BF16), pltpu.VMEM((tq + QBLK, gw), _BF16)],
        compiler_params=_params("arbitrary", "arbitrary", "arbitrary"),
        name=f"dil{dil}",
    )(view, view, view, view, view)
    return o.reshape(batch * sd, dil * gw), lse.reshape(batch * sd, dil * LANES)


def _sb_body(q_ref, k_ref, v_ref, o_ref, carry_ref, acc_ref, qs_ref):
    i = pl.program_id(1)
    q = q_ref[...]
    nh = HEADS_PER_GROUP
    row = lax.broadcasted_iota(jnp.int32, (nh * QBLK, QBLK), 0) & (QBLK - 1)
    col = lax.broadcasted_iota(jnp.int32, (nh * QBLK, QBLK), 1)
    strict = col < row
    ur = lax.broadcasted_iota(jnp.int32, (2 * QBLK, QBLK), 0) & (QBLK - 1)
    uc = lax.broadcasted_iota(jnp.int32, (2 * QBLK, QBLK), 1)
    u = jnp.where(ur > uc, 1.0, 0.0).astype(_BF16)
    lane = lax.broadcasted_iota(jnp.int32, (1, GROUP_WIDTH), 1)
    head_masks = [(lane >= h * HEAD_DIM) & (lane < (h + 1) * HEAD_DIM) for h in range(nh)]
    qs_ref[...] = jnp.concatenate([jnp.where(hm, q, jnp.zeros_like(q)) for hm in head_masks], axis=0)

    def softplus(z):
        return jnp.maximum(z, 0.0) + jnp.log(1.0 + jnp.exp(-jnp.abs(z)))

    def later_keys(log_1m):
        hi, lo = _split(log_1m)
        return _dot(jnp.concatenate([hi, lo], axis=1), u)

    nb = SB_HEAD_BLOCKS
    kbs = [i - (nb - 1) + j for j in range(nb)]
    starts = [pl.multiple_of(jnp.maximum(kb, 0) * QBLK, QBLK) for kb in kbs]
    keep = [strict if j == nb - 1 else (kbs[j] >= 0) for j in range(nb)]
    z = _dot_nt(qs_ref[...], jnp.concatenate([k_ref[pl.ds(s, QBLK), :] for s in starts], axis=0)) * ATTN_SCALE
    sp = softplus(z)
    log_1m = [jnp.where(keep[j], -sp[:, j * QBLK:(j + 1) * QBLK], 0.0) for j in range(nb)]
    totals = [jnp.sum(l, axis=1, keepdims=True) for l in log_1m]
    a_blocks = []
    after = jnp.zeros_like(totals[0])
    for j in reversed(range(nb)):
        cols = slice(j * QBLK, (j + 1) * QBLK)
        a = jnp.exp((z[:, cols] - sp[:, cols]) + later_keys(log_1m[j]) + after)
        a_blocks.insert(0, jnp.where(keep[j], a, 0.0).astype(_BF16))
        after = after + totals[j]
    acc_ref[...] = _dot(jnp.concatenate(a_blocks, axis=1),
                        jnp.concatenate([v_ref[pl.ds(s, QBLK), :] for s in starts], axis=0))
    carry_ref[...] = after

    def tile(kb):
        start = pl.multiple_of(kb * QBLK, QBLK)
        z = _dot_nt(qs_ref[...], k_ref[pl.ds(start, QBLK), :]) * ATTN_SCALE
        sp = softplus(z)
        log_1m = -sp
        a = jnp.exp((z - sp) + later_keys(log_1m) + carry_ref[...])
        acc_ref[...] += _dot(a.astype(_BF16), v_ref[pl.ds(start, QBLK), :])
        carry = carry_ref[...] + jnp.sum(log_1m, axis=1, keepdims=True)
        carry_ref[...] = carry
        return jnp.max(carry)

    def cond(st):
        return (st[0] >= 0) & (st[1] > SB_DEAD_LOG)

    def step(st):
        return st[0] - 1, tile(st[0])

    lax.while_loop(cond, step, (i - nb, jnp.max(after)))
    out = jnp.zeros((QBLK, GROUP_WIDTH), _F32)
    for h, hm in enumerate(head_masks):
        out = jnp.where(hm, acc_ref[h * QBLK:(h + 1) * QBLK, :], out)
    o_ref[...] = out.astype(_BF16)


def _stick_breaking(arr, batch, seq):
    gw = GROUP_WIDTH
    view = arr.reshape(batch, seq, 3 * gw)
    o = pl.pallas_call(
        _sb_body,
        out_shape=jax.ShapeDtypeStruct((batch, seq, gw), _BF16),
        grid=(batch, seq // QBLK),
        in_specs=[pl.BlockSpec((None, QBLK, gw), lambda b, i: (b, i, 0)),
                  pl.BlockSpec((None, seq, gw), lambda b, i: (b, 0, 1)),
                  pl.BlockSpec((None, seq, gw), lambda b, i: (b, 0, 2))],
        out_specs=pl.BlockSpec((None, QBLK, gw), lambda b, i: (b, i, 0)),
        scratch_shapes=[pltpu.VMEM((HEADS_PER_GROUP * QBLK, 1), _F32),
                        pltpu.VMEM((HEADS_PER_GROUP * QBLK, gw), _F32),
                        pltpu.VMEM((HEADS_PER_GROUP * QBLK, gw), _BF16)],
        compiler_params=_params("arbitrary", "arbitrary"),
        name="sb",
    )(view, view, view)
    return o.reshape(batch * seq, gw)


def _merge_body(x_ref, mod_ref, g1_ref, g2_ref, o0_ref, o1_ref, o2_ref, l0_ref, l1_ref, l2_ref, osb_ref,
                wbg_ref, wbd_ref, wbs_ref, wout_ref, wrh_ref, wrl_ref, br_ref, ex_ref, tri_ref,
                x1_ref, h2p_ref, rinfo_ref, cnt_out_ref, cnt_ref, os1_ref, os2_ref, ls1_ref, ls2_ref):
    d = D_MODEL
    tm = x_ref.shape[0]
    x = x_ref[...]
    hb = _rms_mod(x, g1_ref[...], mod_ref[:, d:2 * d], mod_ref[:, 0:d]).astype(_BF16)
    gates = jax.nn.sigmoid(_dot(hb, wbg_ref[...]))

    def natural(ref, st_ref, dil):
        if dil == 1:
            return ref[...].astype(_F32)
        slabs = st_ref.shape[0]
        for r in range(dil):
            for s in range(slabs):
                col0 = (r * slabs + s) * LANES
                st_ref[s, pl.ds(r, tm // dil, stride=dil), :] = ref[:, col0:col0 + LANES].astype(_F32)
        return jnp.concatenate([st_ref[s] for s in range(slabs)], axis=1)

    dils = [dl for _, dl in DIL_PATTERNS]
    o_nat = [natural(r, s, dl) for r, s, dl in zip((o0_ref, o1_ref, o2_ref), (None, os1_ref, os2_ref), dils)]
    l0, l1, l2 = [natural(r, s, dl) for r, s, dl in zip((l0_ref, l1_ref, l2_ref), (None, ls1_ref, ls2_ref), dils)]

    lmax = jnp.maximum(jnp.maximum(l0, l1), l2)
    e0, e1, e2 = jnp.exp(l0 - lmax), jnp.exp(l1 - lmax), jnp.exp(l2 - lmax)
    inv = 1.0 / (e0 + e1 + e2)
    ex = ex_ref[...]

    def widen(w):
        hi, lo = _split(w)
        return _dot(hi, ex) + _dot(lo, ex)

    o_dil = widen(e0 * inv) * o_nat[0] + widen(e1 * inv) * o_nat[1] + widen(e2 * inv) * o_nat[2]
    merged = (gates[:, :d] * _dot(o_dil.astype(_BF16), wbd_ref[...])
              + gates[:, d:] * _dot(osb_ref[...], wbs_ref[...]))
    x1 = x + mod_ref[:, 2 * d:3 * d] * _dot(merged.astype(_BF16), wout_ref[...])
    x1_ref[...] = x1

    h2 = _rms_mod(x1, g2_ref[...], mod_ref[:, 4 * d:5 * d], mod_ref[:, 3 * d:4 * d])
    half = d // 2
    lo_bits = lax.bitcast_convert_type(h2[:, :half].astype(_BF16).astype(_F32), jnp.uint32) >> 16
    hi_bits = lax.bitcast_convert_type(h2[:, half:].astype(_BF16).astype(_F32), jnp.uint32) & jnp.uint32(0xFFFF0000)
    h2p_ref[...] = lo_bits | hi_bits

    hh, hl = _split(h2)
    logits = _dot(hh, wrh_ref[...]) + (_dot(hl, wrh_ref[...]) + _dot(hh, wrl_ref[...])) + br_ref[...]
    lane =lax.broadcasted_iota(jnp.int32, (tm, ROUTER_LANES), 1).astype(_F32)
    far = float(ROUTER_LANES)

    def top(vals):
        m = jnp.max(vals, axis=1, keepdims=True)
        return m, jnp.min(jnp.where(vals == m, lane, far), axis=1, keepdims=True)

    is_group = lane < N_GROUPS
    mg, gsel = top(jnp.where(is_group, logits, NEG_BIG))
    pg_top = 1.0 / jnp.sum(jnp.where(is_group, jnp.exp(logits - mg), 0.0), axis=1, keepdims=True)
    lane0 = EXPERT_LANE0 + EXPERTS_PER_GROUP * gsel
    le = jnp.where((lane >= lane0) & (lane < lane0 + EXPERTS_PER_GROUP), logits, NEG_BIG)
    m1, i1 = top(le)
    m2, i2 = top(jnp.where(lane == i1, NEG_BIG, le))
    t2 = jnp.exp(m2 - m1)
    w0 = pg_top / (1.0 + t2)
    w1 = pg_top * t2 / (1.0 + t2)

    @pl.when(pl.program_id(0) == 0)
    def _():
        cnt_ref[...] = jnp.zeros_like(cnt_ref)

    sel0, sel1 = lane == i1, lane == i2
    onehot = jnp.where(sel0 | sel1, 1.0, 0.0)
    before = _dot(tri_ref[...], onehot.astype(_BF16)) + cnt_ref[0:1, :]
    r0 = jnp.sum(jnp.where(sel0, before, 0.0), axis=1, keepdims=True)
    r1 = jnp.sum(jnp.where(sel1, before, 0.0), axis=1, keepdims=True)
    cnt_ref[...] += jnp.sum(onehot, axis=0, keepdims=True)
    cnt_out_ref[...] = cnt_ref[...]

    cols = (i1 - EXPERT_LANE0, i2 - EXPERT_LANE0, r0, r1, w0, w1)
    rinfo = jnp.zeros((tm, ROUTER_LANES), _F32)
    for c, v in enumerate(cols):
        rinfo = jnp.where(lane == float(c), v, rinfo)
    rinfo_ref[...] = rinfo


def _merge(x2, mod3, g1, g2, outs, lses, osb, wbg, wbd, wbs, wout, wrh, wrl, br, ex, tri, seq):
    t, d = x2.shape
    tm = ROW_TILE
    per_b = seq // tm
    row = lambda i: (i, 0)
    const = lambda i: (0, 0)
    full = lambda a: pl.BlockSpec(a.shape, const)
    gw = GROUP_WIDTH
    dils = [dl for _, dl in DIL_PATTERNS]
    return pl.pallas_call(
        _merge_body,
        out_shape=[jax.ShapeDtypeStruct((t, d), _F32),
                   jax.ShapeDtypeStruct((t, d // 2), jnp.uint32),
                   jax.ShapeDtypeStruct((t, ROUTER_LANES), _F32),
                   jax.ShapeDtypeStruct((8, ROUTER_LANES), _F32)],
        grid=(t // tm,),
        in_specs=[pl.BlockSpec((tm, d), row),
                  pl.BlockSpec((None, 1, mod3.shape[2]), lambda i: (i // per_b, 0, 0)),
                  full(g1), full(g2)]
                 + [pl.BlockSpec((tm // dl, dl * gw), row) for dl in dils]
                 + [pl.BlockSpec((tm // dl, dl * LANES), row) for dl in dils]
                 + [pl.BlockSpec((tm, gw), row)]
                 + [full(a) for a in (wbg, wbd, wbs, wout, wrh, wrl, br, ex, tri)],
        out_specs=[pl.BlockSpec((tm, d), row),
                   pl.BlockSpec((tm, d // 2), row),
                   pl.BlockSpec((tm, ROUTER_LANES), row),
                   pl.BlockSpec((8, ROUTER_LANES), const)],
        scratch_shapes=[pltpu.VMEM((8, ROUTER_LANES), _F32),
                        pltpu.VMEM((gw // LANES, tm, LANES), _F32), pltpu.VMEM((gw // LANES, tm, LANES), _F32),
                        pltpu.VMEM((1, tm, LANES), _F32), pltpu.VMEM((1, tm, LANES), _F32)],
        compiler_params=_params("arbitrary"),
        name="merge",
    )(x2, mod3, g1, g2, *outs, *lses, osb, wbg, wbd, wbs, wout, wrh, wrl, br, ex, tri)


def _scatter_body(ps_ref, pe_ref, er_ref, h_ref, xs_ref, z_ref, sem, zsem):
    ts = h_ref.shape[0]
    ch = z_ref.shape[0]

    @pl.when(pl.program_id(0) == 0)
    def _():
        z_ref[...] = jnp.zeros_like(z_ref)

        def tail(e):
            start = pl.multiple_of(pe_ref[e] - ch, ch)
            return pltpu.make_async_copy(z_ref, xs_ref.at[pl.ds(start, ch)], zsem)

        for e in range(N_EXPERTS):
            @pl.when(pe_ref[e] > ps_ref[e])
            def _():
                tail(e).start()
        for e in range(N_EXPERTS):
            @pl.when(pe_ref[e] > ps_ref[e])
            def _():
                tail(e).wait()

    def row_copy(r, slot):
        return pltpu.make_async_copy(h_ref.at[pl.ds(r, 1)], xs_ref.at[pl.ds(slot, 1)], sem)

    def issue(r, c):
        row_copy(r, ps_ref[er_ref[0, 4 * r]] + er_ref[0, 4 * r + 2]).start(priority=0)
        row_copy(r, ps_ref[er_ref[0, 4 * r + 1]] + er_ref[0, 4 * r + 3]).start(priority=1)
        return c

    def drain(r, c):
        row_copy(0, 0).wait()
        row_copy(0, 0).wait()
        return c

    lax.fori_loop(0, ts, issue, 0, unroll=DMA_UNROLL)
    lax.fori_loop(0, ts, drain, 0, unroll=DMA_UNROLL)


def _scatter(pstart, pend, er, h2p, n_slots):
    t, w = h2p.shape
    ts = min(SCATTER_TILE, t)
    er3 = er.reshape(t // ts, 1, 4 * ts)
    return pl.pallas_call(
        _scatter_body,
        out_shape=jax.ShapeDtypeStruct((n_slots, w), jnp.uint32),
        grid_spec=pltpu.PrefetchScalarGridSpec(
            num_scalar_prefetch=2,
            grid=(t // ts,),
            in_specs=[pl.BlockSpec((None, 1, 4 * ts), lambda i, ps, pe: (i, 0, 0), memory_space=pltpu.SMEM),
                      pl.BlockSpec((ts, w), lambda i, ps, pe: (i, 0))],
            out_specs=pl.BlockSpec(memory_space=pl.ANY),
            scratch_shapes=[pltpu.VMEM((EXPERT_CHUNK, w), jnp.uint32),
                            pltpu.SemaphoreType.DMA(()), pltpu.SemaphoreType.DMA(())]),
        compiler_params=_params("arbitrary"),
        name="scatter",
    )(pstart, pend, er3, h2p)


def _experts_body(ce_ref, nu_ref, xs_ref, wg_ref, wu_ref, wd_ref, ys_ref):
    del ce_ref

    @pl.when(pl.program_id(0) < nu_ref[0])
    def _():
        w = xs_ref[...]
        x = jnp.concatenate(
            [lax.bitcast_convert_type(w << 16, _F32), lax.bitcast_convert_type(w & jnp.uint32(0xFFFF0000), _F32)],
            axis=1).astype(_BF16)
        g = _dot(x, wg_ref[...])
        hmid = (g * jax.nn.sigmoid(g)) * _dot(x, wu_ref[...])
        ys_ref[...] = _dot(hmid.astype(_BF16), wd_ref[...])


def _experts(chunk_e, n_used, xs, wg, wu, wd):
    n_slots, w = xs.shape
    ch = EXPERT_CHUNK
    d, de = wg.shape[1], wg.shape[2]
    slot = lambda c, ce, nu: (jnp.minimum(c, nu[0] - 1), 0)
    return pl.pallas_call(
        _experts_body,
        out_shape=jax.ShapeDtypeStruct((n_slots, d), _F32),
        grid_spec=pltpu.PrefetchScalarGridSpec(
            num_scalar_prefetch=2,
            grid=(n_slots // ch,),
            in_specs=[pl.BlockSpec((ch, w), slot),
                      pl.BlockSpec((None, d, de), lambda c, ce, nu: (ce[c], 0, 0)),
                      pl.BlockSpec((None, d, de), lambda c, ce, nu: (ce[c], 0, 0)),
                      pl.BlockSpec((None, de, d), lambda c, ce, nu: (ce[c], 0, 0))],
            out_specs=pl.BlockSpec((ch, d), slot)),
        compiler_params=_params("arbitrary"),
        name="experts",
    )(chunk_e, n_used, xs, wg, wu, wd)


def _combine_body(ps_ref, er_ref, x1_ref, rinfo_ref, mod_ref, ys_ref, o_ref, buf_ref, sem):
    tf = x1_ref.shape[0]

    def row_copy(r, k, slot):
        return pltpu.make_async_copy(ys_ref.at[pl.ds(slot, 1)], buf_ref.at[k, pl.ds(r, 1)], sem)

    def issue(r, c):
        row_copy(r, 0, ps_ref[er_ref[0, 4 * r]] + er_ref[0, 4 * r + 2]).start(priority=0)
        row_copy(r, 1, ps_ref[er_ref[0, 4 * r + 1]] + er_ref[0, 4 * r + 3]).start(priority=1)
        return c

    def drain(r, c):
        row_copy(0, 0, 0).wait()
        row_copy(0, 0, 0).wait()
        return c

    lax.fori_loop(0, tf, issue, 0, unroll=DMA_UNROLL)
    lax.fori_loop(0, tf, drain, 0, unroll=DMA_UNROLL)
    w0 = rinfo_ref[:, 4:5]
    w1 = rinfo_ref[:, 5:6]
    y = w0 * buf_ref[0] + w1 * buf_ref[1]
    o_ref[...] = x1_ref[...] + mod_ref[:, 5 * D_MODEL:6 * D_MODEL] * y


def _combine(pstart, er, x1, rinfo, mod3, ys, seq):
    t, d = x1.shape
    tf = min(COMBINE_TILE, seq)
    per_b = seq // tf
    er3 = er.reshape(t // tf, 1, 4 * tf)
    return pl.pallas_call(
        _combine_body,
        out_shape=jax.ShapeDtypeStruct((t, d), _F32),
        grid_spec=pltpu.PrefetchScalarGridSpec(
            num_scalar_prefetch=1,
            grid=(t // tf,),
            in_specs=[pl.BlockSpec((None, 1, 4 * tf), lambda i, ps: (i, 0, 0), memory_space=pltpu.SMEM),
                      pl.BlockSpec((tf, d), lambda i, ps: (i, 0)),
                      pl.BlockSpec((tf, ROUTER_LANES), lambda i, ps: (i, 0)),
                      pl.BlockSpec((None, 1, mod3.shape[2]), lambda i, ps: (i // per_b, 0, 0)),
                      pl.BlockSpec(memory_space=pl.ANY)],
            out_specs=pl.BlockSpec((tf, d), lambda i, ps: (i, 0)),
            scratch_shapes=[pltpu.VMEM((2, tf, d), _F32), pltpu.SemaphoreType.DMA(())]),
        compiler_params=_params("arbitrary"),
        name="combine",
    )(pstart, er3, x1, rinfo, mod3, ys)


def _rope_tables(positions):
    half = ROPE_DIM // 2
    inv_freq = ROPE_THETA ** (-jnp.arange(0, ROPE_DIM, 2, dtype=_F32) / ROPE_DIM)
    ang = positions.reshape(-1).astype(_F32)[:, None] * inv_freq
    cos, sin = jnp.cos(ang), jnp.sin(ang)
    t = ang.shape[0]
    rest = HEAD_DIM - ROPE_DIM
    c = jnp.concatenate([cos, cos, jnp.ones((t, rest), _F32)], axis=1)
    s1 = jnp.concatenate([-sin, jnp.zeros((t, half + rest), _F32)], axis=1)
    s2 = jnp.concatenate([jnp.zeros((t, half), _F32), sin, jnp.zeros((t, rest), _F32)], axis=1)
    two = LANES // HEAD_DIM
    return tuple(jnp.tile(a, (1, two)) for a in (c, s1, s2))


def _layer(x, mod, positions, g_mix, g_ffn, w_in, w_bg, qg, kg, w_bd, w_bs, w_out, w_rg, b_rg, w_re, b_re,
           w_eg, w_eu, w_ed):
    batch, seq, d = x.shape
    t = batch * seq
    x2 = x.reshape(t, d)
    mod3 = mod.reshape(batch, 1, mod.shape[1])
    gw = GROUP_WIDTH

    lane = jnp.arange(gw)
    bd = jnp.where(lane[:, None] // HEAD_DIM == lane[None, :] // HEAD_DIM, 1.0 / HEAD_DIM, 0.0).astype(_BF16)
    ex = (jnp.arange(LANES)[:, None] == (lane[None, :] // HEAD_DIM) * LSE_SEG).astype(_BF16)
    tri = (jnp.arange(ROW_TILE)[:, None] > jnp.arange(ROW_TILE)[None, :]).astype(_BF16)
    rope = _rope_tables(positions)
    tile4 = lambda g: jnp.tile(g.astype(_F32), HEADS_PER_GROUP).reshape(1, gw)
    wr = jnp.zeros((d, ROUTER_LANES), _F32).at[:, :N_GROUPS].set(w_rg).at[:, N_GROUPS:N_GROUPS + N_EXPERTS].set(w_re)
    wrh = wr.astype(_BF16)
    wrl = (wr - wrh.astype(_F32)).astype(_BF16)
    br = jnp.zeros((1, ROUTER_LANES), _F32).at[0, :N_GROUPS].set(b_rg).at[0, N_GROUPS:N_GROUPS + N_EXPERTS].set(b_re)

    d0, d1, d2, sbp = _qkv(x2, mod3, g_mix.reshape(1, d), w_in.astype(_BF16), tile4(qg), tile4(kg), *rope, bd, seq)
    dil = [_dilated_group(a, dl, batch, seq) for a, (_, dl) in zip((d0, d1, d2), DIL_PATTERNS)]
    osb = _stick_breaking(sbp, batch, seq)

    x1, h2p, rinfo, cnt = _merge(
        x2, mod3, g_mix.reshape(1, d), g_ffn.reshape(1, d), [o for o, _ in dil], [l for _, l in dil], osb,
        w_bg.astype(_BF16), w_bd.astype(_BF16), w_bs.astype(_BF16), w_out.astype(_BF16), wrh, wrl, br, ex, tri, seq)

    ch = EXPERT_CHUNK
    counts = cnt[0, EXPERT_LANE0:EXPERT_LANE0 + N_EXPERTS].astype(jnp.int32)
    padded = (counts + ch - 1) // ch * ch
    pend = jnp.cumsum(padded)
    pstart = pend - padded
    er = rinfo[:, :4].astype(jnp.int32)
    n_chunks = -(-2 * t // ch) + N_EXPERTS
    chunk_start = jnp.arange(n_chunks, dtype=jnp.int32) * ch
    chunk_e = jnp.minimum(jnp.sum((pend[None, :] <= chunk_start[:, None]).astype(jnp.int32), axis=1), N_EXPERTS - 1)
    n_used = (pend[-1:] // ch).astype(jnp.int32)

    xs = _scatter(pstart, pend, er, h2p, n_chunks * ch)
    ys = _experts(chunk_e, n_used, xs, w_eg.astype(_BF16), w_eu.astype(_BF16), w_ed.astype(_BF16))
    out = _combine(pstart, er, x1, rinfo, mod3, ys, seq)
    return out.reshape(batch, seq, d)


def kernel(x, c, positions, w_ada, b_ada, g_norm_mix, g_norm_ffn, w_in, w_branch_gate, q_norm_g, k_norm_g,
           w_branch_dil, w_branch_sb, w_out, w_router_group, b_router_group, w_router_expert, b_router_expert,
           w_expert_gate, w_expert_up, w_expert_down):
    for l in range(w_ada.shape[0]):
        mod = _ada(c, w_ada[l], b_ada[l])
        x = _layer(x, mod, positions, g_norm_mix[l], g_norm_ffn[l], w_in[l], w_branch_gate[l], q_norm_g[l],
                   k_norm_g[l], w_branch_dil[l], w_branch_sb[l], w_out[l], w_router_group[l], b_router_group[l],
                   w_router_expert[l], b_router_expert[l], w_expert_gate[l], w_expert_up[l], w_expert_down[l])
    return x
```

```python
import functools

import jax
import jax.numpy as jnp
from jax import lax
from jax.experimental import pallas as pl
from jax.experimental.pallas import tpu as pltpu

D_MODEL = 1024
HEAD_DIM = 64
DIL_PATTERNS = ((128, 1), (512, 4), (2048, 16))
HEADS_PER_GROUP = 4
GROUP_WIDTH = HEADS_PER_GROUP * HEAD_DIM
N_DIL_GROUPS = len(DIL_PATTERNS)
DIL_WIDTH = N_DIL_GROUPS * GROUP_WIDTH
QKV_WIDTH = 3 * DIL_WIDTH + 3 * GROUP_WIDTH
WINDOW_KEYS = 128
ROPE_THETA = 500000.0
ROPE_DIM = HEAD_DIM // 4
N_GROUPS = 4
EXPERTS_PER_GROUP = 8
N_EXPERTS = N_GROUPS * EXPERTS_PER_GROUP
D_EXPERT = 512
RMS_EPS = 1e-6
ATTN_SCALE = HEAD_DIM ** -0.5

LANES = 128
ROUTER_LANES = LANES
EXPERT_LANE0 = N_GROUPS
LSE_SEG = LANES // HEADS_PER_GROUP
NEG_BIG = -1e30
SB_DEAD_LOG = -120.0
SB_HEAD_BLOCKS = 3

ROW_TILE = 512
QBLK = 128
DIL_QTILE = 512
EXPERT_CHUNK = 512
SCATTER_TILE = 1024
COMBINE_TILE = 512
DMA_UNROLL = 8
VMEM_LIMIT = 48 * 1024 * 1024

_BF16 = jnp.bfloat16
_F32 = jnp.float32
_NT = (((1,), (1,)), ((), ()))


def _dot(a, b):
    return jnp.dot(a, b, preferred_element_type=_F32)


def _dot_nt(a, b):
    return lax.dot_general(a, b, _NT, preferred_element_type=_F32)


def _split(a):
    hi = a.astype(_BF16)
    lo = (a - hi.astype(_F32)).astype(_BF16)
    return hi, lo


def _dot3(a, b):
    ah, al = _split(a)
    bh, bl = _split(b)
    return _dot(ah, bh) + (_dot(ah, bl) + _dot(al, bh))


def _rms_mod(x, g, scale, shift):
    y = x * lax.rsqrt(jnp.mean(x * x, axis=-1, keepdims=True) + RMS_EPS)
    return y * g * (1.0 + scale) + shift


def _params(*sem):
    return pltpu.CompilerParams(dimension_semantics=sem, vmem_limit_bytes=VMEM_LIMIT)


def _ada_body(c_ref, w_ref, b_ref, o_ref):
    c = c_ref[...]
    o_ref[...] = _dot3(c * jax.nn.sigmoid(c), w_ref[...]) + b_ref[...]


def _ada(c, w_ada, b_ada):
    b, d = c.shape
    n = w_ada.shape[1]
    rows = -(-b // 16) * 16
    cp = jnp.zeros((rows, d), _F32).at[:b].set(c)
    nt = 1536
    out = pl.pallas_call(
        _ada_body,
        out_shape=jax.ShapeDtypeStruct((rows, n), _F32),
        grid=(n // nt,),
        in_specs=[pl.BlockSpec((rows, d), lambda j: (0, 0)),
                  pl.BlockSpec((d, nt), lambda j: (0, j)),
                  pl.BlockSpec((1, nt), lambda j: (0, j))],
        out_specs=pl.BlockSpec((rows, nt), lambda j: (0, j)),
        compiler_params=_params("arbitrary"),
        name="ada",
    )(cp, w_ada, b_ada.reshape(1, n))
    return out[:b]


def _qkv_body(x_ref, mod_ref, g_ref, w_ref, qg_ref, kg_ref, c_ref, s1_ref, s2_ref, bd_ref,
              o0_ref, o1_ref, o2_ref, osb_ref, st_ref):
    d = D_MODEL
    tm = x_ref.shape[0]
    h = _rms_mod(x_ref[...], g_ref[...], mod_ref[:, d:2 * d], mod_ref[:, 0:d])
    hb = h.astype(_BF16)
    cc = jnp.concatenate([c_ref[...]] * 2, axis=1)
    s1 = jnp.concatenate([s1_ref[...]] * 2, axis=1)
    s2 = jnp.concatenate([s2_ref[...]] * 2, axis=1)
    bd = bd_ref[...]
    gw = GROUP_WIDTH

    def normed_rotated(col0, gain):
        acc = _dot(hb, w_ref[:, col0:col0 + gw])
        ms = _dot((acc * acc).astype(_BF16), bd)
        y = acc * lax.rsqrt(ms + RMS_EPS) * gain
        return y * cc + pltpu.roll(y, gw - ROPE_DIM // 2, 1) * s1 + pltpu.roll(y, ROPE_DIM // 2, 1) * s2

    def store(o_ref, dil, part, y):
        if dil == 1:
            o_ref[:, part * gw:(part + 1) * gw] = y.astype(_BF16)
            return
        for s in range(gw // LANES):
            st_ref[s] = y[:, s * LANES:(s + 1) * LANES]
        for r in range(dil):
            for s in range(gw // LANES):
                col0 = (3 * r + part) * gw + s * LANES
                o_ref[:, col0:col0 + LANES] = st_ref[s, pl.ds(r, tm // dil, stride=dil), :].astype(_BF16)

    for g, o_ref in enumerate((o0_ref, o1_ref, o2_ref)):
        dil = DIL_PATTERNS[g][1]
        store(o_ref, dil, 0, normed_rotated(g * gw, qg_ref[...]))
        store(o_ref, dil, 1, normed_rotated(DIL_WIDTH + g * gw, kg_ref[...]))
        store(o_ref, dil, 2, _dot(hb, w_ref[:, 2 * DIL_WIDTH + g * gw:2 * DIL_WIDTH + (g + 1) * gw]))
    for part in range(3):
        col0 = 3 * DIL_WIDTH + part * gw
        osb_ref[:, part * gw:(part + 1) * gw] = _dot(hb, w_ref[:, col0:col0 + gw]).astype(_BF16)


def _qkv(x2, mod3, g_mix, w_in, qg, kg, rope_c, rope_s1, rope_s2, bd, seq):
    t, d = x2.shape
    tm = ROW_TILE
    per_b = seq // tm
    row = lambda i: (i, 0)
    const = lambda i: (0, 0)
    width = 3 * GROUP_WIDTH
    dils = [dl for _, dl in DIL_PATTERNS] + [1]
    return pl.pallas_call(
        _qkv_body,
        out_shape=[jax.ShapeDtypeStruct((t // dl, dl * width), _BF16) for dl in dils],
        grid=(t // tm,),
        in_specs=[pl.BlockSpec((tm, d), row),
                  pl.BlockSpec((None, 1, mod3.shape[2]), lambda i: (i // per_b, 0, 0)),
                  pl.BlockSpec((1, d), const),
                  pl.BlockSpec(w_in.shape, const),
                  pl.BlockSpec((1, GROUP_WIDTH), const),
                  pl.BlockSpec((1, GROUP_WIDTH), const),
                  pl.BlockSpec((tm, LANES), row),
                  pl.BlockSpec((tm, LANES), row),
                  pl.BlockSpec((tm, LANES), row),
                  pl.BlockSpec(bd.shape, const)],
        out_specs=[pl.BlockSpec((tm // dl, dl * width), row) for dl in dils],
        scratch_shapes=[pltpu.VMEM((GROUP_WIDTH // LANES, tm, LANES), _F32)],
        compiler_params=_params("arbitrary"),
        name="qkv",
    )(x2, mod3, g_mix, w_in, qg, kg, rope_c, rope_s1, rope_s2, bd)


def _dil_body(q_ref, kp_ref, kc_ref, vp_ref, vc_ref, o_ref, lse_ref, kf_ref, vf_ref):
    tq = q_ref.shape[0]
    first = pl.program_id(2) == 0
    kf_ref[0:QBLK, :] = kp_ref[...]
    kf_ref[QBLK:, :] = kc_ref[...]
    vf_ref[0:QBLK, :] = vp_ref[...]
    vf_ref[QBLK:, :] = vc_ref[...]
    nh = HEADS_PER_GROUP
    row = lax.broadcasted_iota(jnp.int32, (nh * QBLK, 2 * QBLK), 0) & (QBLK - 1)
    col = lax.broadcasted_iota(jnp.int32, (nh * QBLK, 2 * QBLK), 1)
    band = (col >= row) & (col <= row + WINDOW_KEYS)
    lane = lax.broadcasted_iota(jnp.int32, (1, GROUP_WIDTH), 1)
    slane = lax.broadcasted_iota(jnp.int32, (1, LANES), 1)
    head_masks = [(lane >= h * HEAD_DIM) & (lane < (h + 1) * HEAD_DIM) for h in range(nh)]
    for j in range(tq // QBLK):
        qj = q_ref[j * QBLK:(j + 1) * QBLK, :]
        qs = jnp.concatenate([jnp.where(hm, qj, jnp.zeros_like(qj)) for hm in head_masks], axis=0)
        kcat = kf_ref[j * QBLK:(j + 2) * QBLK, :]
        vcat = vf_ref[j * QBLK:(j + 2) * QBLK, :]
        valid = band & ((col >= QBLK) | jnp.logical_not(first)) if j == 0 else band
        s = jnp.where(valid, _dot_nt(qs, kcat) * ATTN_SCALE, NEG_BIG)
        m = jnp.max(s, axis=1, keepdims=True)
        p = jnp.exp(s - m)
        l = jnp.sum(p, axis=1, keepdims=True)
        o_all = _dot(p.astype(_BF16), vcat) / l
        lse_all = m + jnp.log(l)
        o_acc = jnp.zeros((QBLK, GROUP_WIDTH), _F32)
        lse_t = jnp.zeros((QBLK, LANES), _F32)
        for h, hm in enumerate(head_masks):
            o_acc = jnp.where(hm, o_all[h * QBLK:(h + 1) * QBLK, :], o_acc)
            sm = (slane >= h * LSE_SEG) & (slane < (h + 1) * LSE_SEG)
            lse_t = jnp.where(sm, lse_all[h * QBLK:(h + 1) * QBLK, :], lse_t)
        o_ref[j * QBLK:(j + 1) * QBLK, :] = o_acc.astype(_BF16)
        lse_ref[j * QBLK:(j + 1) * QBLK, :] = lse_t


def _dilated_group(view2, dil, batch, seq):
    sd = seq // dil
    tq = min(DIL_QTILE, sd)
    per = tq // QBLK
    gw = GROUP_WIDTH
    view = view2.reshape(batch, sd, dil * 3 * gw)
    cur = lambda part: pl.BlockSpec((None, tq, gw), lambda b, r, i: (b, i, 3 * r + part))
    prev = lambda part: pl.BlockSpec((None, QBLK, gw),
                                     lambda b, r, i: (b, jnp.maximum(i * per - 1, 0), 3 * r + part))
    o, lse = pl.pallas_call(
        _dil_body,
        out_shape=[jax.ShapeDtypeStruct((batch, sd, dil * gw), _BF16),
                   jax.ShapeDtypeStruct((batch, sd, dil * LANES), _F32)],
        grid=(batch, dil, sd // tq),
        in_specs=[cur(0), prev(1), cur(1), prev(2), cur(2)],
        out_specs=[pl.BlockSpec((None, tq, gw), lambda b, r, i: (b, i, r)),
                   pl.BlockSpec((None, tq, LANES), lambda b, r, i: (b, i, r))],
        scratch_shapes=[pltpu.VMEM((tq + QBLK, gw), _BF16), pltpu.VMEM((tq + QBLK, gw), _BF16)],
        compiler_params=_params("arbitrary", "arbitrary", "arbitrary"),
        name=f"dil{dil}",
    )(view, view, view, view, view)
    return o.reshape(batch * sd, dil * gw), lse.reshape(batch * sd, dil * LANES)


def _sb_body(q_ref, k_ref, v_ref, o_ref, carry_ref, acc_ref, qs_ref):
    i = pl.program_id(1)
    q = q_ref[...]
    nh = HEADS_PER_GROUP
    row = lax.broadcasted_iota(jnp.int32, (nh * QBLK, QBLK), 0) & (QBLK - 1)
    col = lax.broadcasted_iota(jnp.int32, (nh * QBLK, QBLK), 1)
    strict = col < row
    ur = lax.broadcasted_iota(jnp.int32, (2 * QBLK, QBLK), 0) & (QBLK - 1)
    uc = lax.broadcasted_iota(jnp.int32, (2 * QBLK, QBLK), 1)
    u = jnp.where(ur > uc, 1.0, 0.0).astype(_BF16)
    lane = lax.broadcasted_iota(jnp.int32, (1, GROUP_WIDTH), 1)
    head_masks = [(lane >= h * HEAD_DIM) & (lane < (h + 1) * HEAD_DIM) for h in range(nh)]
    qs_ref[...] = jnp.concatenate([jnp.where(hm, q, jnp.zeros_like(q)) for hm in head_masks], axis=0)

    def softplus(z):
        return jnp.maximum(z, 0.0) + jnp.log(1.0 + jnp.exp(-jnp.abs(z)))

    def later_keys(log_1m):
        hi, lo = _split(log_1m)
        return _dot(jnp.concatenate([hi, lo], axis=1), u)

    nb = SB_HEAD_BLOCKS
    kbs = [i - (nb - 1) + j for j in range(nb)]
    starts = [pl.multiple_of(jnp.maximum(kb, 0) * QBLK, QBLK) for kb in kbs]
    keep = [strict if j == nb - 1 else (kbs[j] >= 0) for j in range(nb)]
    z = _dot_nt(qs_ref[...], jnp.concatenate([k_ref[pl.ds(s, QBLK), :] for s in starts], axis=0)) * ATTN_SCALE
    sp = softplus(z)
    log_1m = [jnp.where(keep[j], -sp[:, j * QBLK:(j + 1) * QBLK], 0.0) for j in range(nb)]
    totals = [jnp.sum(l, axis=1, keepdims=True) for l in log_1m]
    a_blocks = []
    after = jnp.zeros_like(totals[0])
    for j in reversed(range(nb)):
        cols = slice(j * QBLK, (j + 1) * QBLK)
        a = jnp.exp((z[:, cols] - sp[:, cols]) + later_keys(log_1m[j]) + after)
        a_blocks.insert(0, jnp.where(keep[j], a, 0.0).astype(_BF16))
        after = after + totals[j]
    acc_ref[...] = _dot(jnp.concatenate(a_blocks, axis=1),
                        jnp.concatenate([v_ref[pl.ds(s, QBLK), :] for s in starts], axis=0))
    carry_ref[...] = after

    def tile(kb):
        start = pl.multiple_of(kb * QBLK, QBLK)
        z = _dot_nt(qs_ref[...], k_ref[pl.ds(start, QBLK), :]) * ATTN_SCALE
        sp = softplus(z)
        log_1m = -sp
        a = jnp.exp((z - sp) + later_keys(log_1m) + carry_ref[...])
        acc_ref[...] += _dot(a.astype(_BF16), v_ref[pl.ds(start, QBLK), :])
        carry = carry_ref[...] + jnp.sum(log_1m, axis=1, keepdims=True)
        carry_ref[...] = carry
        return jnp.max(carry)

    def cond(st):
        return (st[0] >= 0) & (st[1] > SB_DEAD_LOG)

    def step(st):
        return st[0] - 1, tile(st[0])

    lax.while_loop(cond, step, (i - nb, jnp.max(after)))
    out = jnp.zeros((QBLK, GROUP_WIDTH), _F32)
    for h, hm in enumerate(head_masks):
        out = jnp.where(hm, acc_ref[h * QBLK:(h + 1) * QBLK, :], out)
    o_ref[...] = out.astype(_BF16)


def _stick_breaking(arr, batch, seq):
    gw = GROUP_WIDTH
    view = arr.reshape(batch, seq, 3 * gw)
    o = pl.pallas_call(
        _sb_body,
        out_shape=jax.ShapeDtypeStruct((batch, seq, gw), _BF16),
        grid=(batch, seq // QBLK),
        in_specs=[pl.BlockSpec((None, QBLK, gw), lambda b, i: (b, i, 0)),
                  pl.BlockSpec((None, seq, gw), lambda b, i: (b, 0, 1)),
                  pl.BlockSpec((None, seq, gw), lambda b, i: (b, 0, 2))],
        out_specs=pl.BlockSpec((None, QBLK, gw), lambda b, i: (b, i, 0)),
        scratch_shapes=[pltpu.VMEM((HEADS_PER_GROUP * QBLK, 1), _F32),
                        pltpu.VMEM((HEADS_PER_GROUP * QBLK, gw), _F32),
                        pltpu.VMEM((HEADS_PER_GROUP * QBLK, gw), _BF16)],
        compiler_params=_params("arbitrary", "arbitrary"),
        name="sb",
    )(view, view, view)
    return o.reshape(batch * seq, gw)


def _merge_body(x_ref, mod_ref, g1_ref, g2_ref, o0_ref, o1_ref, o2_ref, l0_ref, l1_ref, l2_ref, osb_ref,
                wbg_ref, wbd_ref, wbs_ref, wout_ref, wrh_ref, wrl_ref, br_ref, ex_ref, tri_ref,
                x1_ref, h2p_ref, rinfo_ref, er_ref, cnt_out_ref, cnt_ref, os1_ref, os2_ref, ls1_ref, ls2_ref):
    d = D_MODEL
    tm = x_ref.shape[0]
    x = x_ref[...]
    hb = _rms_mod(x, g1_ref[...], mod_ref[:, d:2 * d], mod_ref[:, 0:d]).astype(_BF16)
    gates = jax.nn.sigmoid(_dot(hb, wbg_ref[...]))

    def natural(ref, st_ref, dil):
        if dil == 1:
            return ref[...].astype(_F32)
        slabs = st_ref.shape[0]
        for r in range(dil):
            for s in range(slabs):
                col0 = (r * slabs + s) * LANES
                st_ref[s, pl.ds(r, tm // dil, stride=dil), :] = ref[:, col0:col0 + LANES].astype(_F32)
        return jnp.concatenate([st_ref[s] for s in range(slabs)], axis=1)

    dils = [dl for _, dl in DIL_PATTERNS]
    o_nat = [natural(r, s, dl) for r, s, dl in zip((o0_ref, o1_ref, o2_ref), (None, os1_ref, os2_ref), dils)]
    l0, l1, l2 = [natural(r, s, dl) for r, s, dl in zip((l0_ref, l1_ref, l2_ref), (None, ls1_ref, ls2_ref), dils)]

    lmax = jnp.maximum(jnp.maximum(l0, l1), l2)
    e0, e1, e2 = jnp.exp(l0 - lmax), jnp.exp(l1 - lmax), jnp.exp(l2 - lmax)
    inv = 1.0 / (e0 + e1 + e2)
    ex = ex_ref[...]

    def widen(w):
        hi, lo = _split(w)
        return _dot(hi, ex) + _dot(lo, ex)

    o_dil = widen(e0 * inv) * o_nat[0] + widen(e1 * inv) * o_nat[1] + widen(e2 * inv) * o_nat[2]
    merged = (gates[:, :d] * _dot(o_dil.astype(_BF16), wbd_ref[...])
              + gates[:, d:] * _dot(osb_ref[...], wbs_ref[...]))
    x1 = x + mod_ref[:, 2 * d:3 * d] * _dot(merged.astype(_BF16), wout_ref[...])
    x1_ref[...] = x1

    h2 = _rms_mod(x1, g2_ref[...], mod_ref[:, 4 * d:5 * d], mod_ref[:, 3 * d:4 * d])
    half = d // 2
    lo_bits = lax.bitcast_convert_type(h2[:, :half].astype(_BF16).astype(_F32), jnp.uint32) >> 16
    hi_bits = lax.bitcast_convert_type(h2[:, half:].astype(_BF16).astype(_F32), jnp.uint32) & jnp.uint32(0xFFFF0000)
    h2p_ref[...] = lo_bits | hi_bits

    hh, hl = _split(h2)
    logits = _dot(hh, wrh_ref[...]) + (_dot(hl, wrh_ref[...]) + _dot(hh, wrl_ref[...])) + br_ref[...]
    lane =lax.broadcasted_iota(jnp.int32, (tm, ROUTER_LANES), 1).astype(_F32)
    far = float(ROUTER_LANES)

    def top(vals):
        m = jnp.max(vals, axis=1, keepdims=True)
        return m, jnp.min(jnp.where(vals == m, lane, far), axis=1, keepdims=True)

    is_group = lane < N_GROUPS
    mg, gsel = top(jnp.where(is_group, logits, NEG_BIG))
    pg_top = 1.0 / jnp.sum(jnp.where(is_group, jnp.exp(logits - mg), 0.0), axis=1, keepdims=True)
    lane0 = EXPERT_LANE0 + EXPERTS_PER_GROUP * gsel
    le = jnp.where((lane >= lane0) & (lane < lane0 + EXPERTS_PER_GROUP), logits, NEG_BIG)
    m1, i1 = top(le)
    m2, i2 = top(jnp.where(lane == i1, NEG_BIG, le))
    t2 = jnp.exp(m2 - m1)
    w0 = pg_top / (1.0 + t2)
    w1 = pg_top * t2 / (1.0 + t2)

    @pl.when(pl.program_id(0) == 0)
    def _():
        cnt_ref[...] = jnp.zeros_like(cnt_ref)

    sel0, sel1 = lane == i1, lane == i2
    onehot = jnp.where(sel0 | sel1, 1.0, 0.0)
    before = _dot(tri_ref[...], onehot.astype(_BF16)) + cnt_ref[0:1, :]
    r0 = jnp.sum(jnp.where(sel0, before, 0.0), axis=1, keepdims=True)
    r1 = jnp.sum(jnp.where(sel1, before, 0.0), axis=1, keepdims=True)
    cnt_ref[...] += jnp.sum(onehot, axis=0, keepdims=True)
    cnt_out_ref[...] = cnt_ref[...]

    cols = (i1 - EXPERT_LANE0, i2 - EXPERT_LANE0, r0, r1, w0, w1)
    rinfo = jnp.zeros((tm, ROUTER_LANES), _F32)
    for c, v in enumerate(cols):
        rinfo = jnp.where(lane == float(c), v, rinfo)
    rinfo_ref[...] = rinfo
    er_ref[...] = jnp.transpose(rinfo)[0:8, :].astype(jnp.int32)


def _merge(x2, mod3, g1, g2, outs, lses, osb, wbg, wbd, wbs, wout, wrh, wrl, br, ex, tri, seq):
    t, d = x2.shape
    tm = ROW_TILE
    per_b = seq // tm
    row = lambda i: (i, 0)
    const = lambda i: (0, 0)
    full = lambda a: pl.BlockSpec(a.shape, const)
    gw = GROUP_WIDTH
    dils = [dl for _, dl in DIL_PATTERNS]
    return pl.pallas_call(
        _merge_body,
        out_shape=[jax.ShapeDtypeStruct((t, d), _F32),
                   jax.ShapeDtypeStruct((t, d // 2), jnp.uint32),
                   jax.ShapeDtypeStruct((t, ROUTER_LANES), _F32),
                   jax.ShapeDtypeStruct((8, t), jnp.int32),
                   jax.ShapeDtypeStruct((8, ROUTER_LANES), _F32)],
        grid=(t // tm,),
        in_specs=[pl.BlockSpec((tm, d), row),
                  pl.BlockSpec((None, 1, mod3.shape[2]), lambda i: (i // per_b, 0, 0)),
                  full(g1), full(g2)]
                 + [pl.BlockSpec((tm // dl, dl * gw), row) for dl in dils]
                 + [pl.BlockSpec((tm // dl, dl * LANES), row) for dl in dils]
                 + [pl.BlockSpec((tm, gw), row)]
                 + [full(a) for a in (wbg, wbd, wbs, wout, wrh, wrl, br, ex, tri)],
        out_specs=[pl.BlockSpec((tm, d), row),
                   pl.BlockSpec((tm, d // 2), row),
                   pl.BlockSpec((tm, ROUTER_LANES), row),
                   pl.BlockSpec((8, tm), lambda i: (0, i)),
                   pl.BlockSpec((8, ROUTER_LANES), const)],
        scratch_shapes=[pltpu.VMEM((8, ROUTER_LANES), _F32),
                        pltpu.VMEM((gw // LANES, tm, LANES), _F32), pltpu.VMEM((gw // LANES, tm, LANES), _F32),
                        pltpu.VMEM((1, tm, LANES), _F32), pltpu.VMEM((1, tm, LANES), _F32)],
        compiler_params=_params("arbitrary"),
        name="merge",
    )(x2, mod3, g1, g2, *outs, *lses, osb, wbg, wbd, wbs, wout, wrh, wrl, br, ex, tri)


def _scatter_body(ps_ref, pe_ref, er_ref, h_ref, xs_ref, z_ref, sem, zsem):
    ts = h_ref.shape[0]
    ch = z_ref.shape[0]

    @pl.when(pl.program_id(0) == 0)
    def _():
        z_ref[...] = jnp.zeros_like(z_ref)

        def tail(e):
            start = pl.multiple_of(pe_ref[e] - ch, ch)
            return pltpu.make_async_copy(z_ref, xs_ref.at[pl.ds(start, ch)], zsem)

        for e in range(N_EXPERTS):
            @pl.when(pe_ref[e] > ps_ref[e])
            def _():
                tail(e).start()
        for e in range(N_EXPERTS):
            @pl.when(pe_ref[e] > ps_ref[e])
            def _():
                tail(e).wait()

    def row_copy(r, slot):
        return pltpu.make_async_copy(h_ref.at[pl.ds(r, 1)], xs_ref.at[pl.ds(slot, 1)], sem)

    def issue(r, c):
        row_copy(r, ps_ref[er_ref[0, r]] + er_ref[2, r]).start(priority=0)
        row_copy(r, ps_ref[er_ref[1, r]] + er_ref[3, r]).start(priority=1)
        return c

    def drain(r, c):
        row_copy(0, 0).wait()
        row_copy(0, 0).wait()
        return c

    lax.fori_loop(0, ts, issue, 0, unroll=DMA_UNROLL)
    lax.fori_loop(0, ts, drain, 0, unroll=DMA_UNROLL)


def _scatter(pstart, pend, er, h2p, n_slots):
    t, w = h2p.shape
    ts = min(SCATTER_TILE, t)
    return pl.pallas_call(
        _scatter_body,
        out_shape=jax.ShapeDtypeStruct((n_slots, w), jnp.uint32),
        grid_spec=pltpu.PrefetchScalarGridSpec(
            num_scalar_prefetch=2,
            grid=(t // ts,),
            in_specs=[pl.BlockSpec((8, ts), lambda i, ps, pe: (0, i), memory_space=pltpu.SMEM),
                      pl.BlockSpec((ts, w), lambda i, ps, pe: (i, 0))],
            out_specs=pl.BlockSpec(memory_space=pl.ANY),
            scratch_shapes=[pltpu.VMEM((EXPERT_CHUNK, w), jnp.uint32),
                            pltpu.SemaphoreType.DMA(()), pltpu.SemaphoreType.DMA(())]),
        compiler_params=_params("arbitrary"),
        name="scatter",
    )(pstart, pend, er, h2p)


def _experts_body(ce_ref, nu_ref, xs_ref, wg_ref, wu_ref, wd_ref, ys_ref, wgb_ref, wub_ref, wdb_ref):
    c = pl.program_id(0)

    @pl.when(c < nu_ref[0])
    def _():
        @pl.when((c == 0) | (ce_ref[c] != ce_ref[jnp.maximum(c - 1, 0)]))
        def _():
            wgb_ref[...] = wg_ref[...].astype(_BF16)
            wub_ref[...] = wu_ref[...].astype(_BF16)
            wdb_ref[...] = wd_ref[...].astype(_BF16)

        w = xs_ref[...]
        x = jnp.concatenate(
            [lax.bitcast_convert_type(w << 16, _F32), lax.bitcast_convert_type(w & jnp.uint32(0xFFFF0000), _F32)],
            axis=1).astype(_BF16)
        g = _dot(x, wgb_ref[...])
        hmid = (g * jax.nn.sigmoid(g)) * _dot(x, wub_ref[...])
        ys_ref[...] = _dot(hmid.astype(_BF16), wdb_ref[...])


def _experts(chunk_e, n_used, xs, wg, wu, wd):
    n_slots, w = xs.shape
    ch = EXPERT_CHUNK
    d, de = wg.shape[1], wg.shape[2]
    slot = lambda c, ce, nu: (jnp.minimum(c, nu[0] - 1), 0)
    return pl.pallas_call(
        _experts_body,
        out_shape=jax.ShapeDtypeStruct((n_slots, d), _F32),
        grid_spec=pltpu.PrefetchScalarGridSpec(
            num_scalar_prefetch=2,
            grid=(n_slots // ch,),
            in_specs=[pl.BlockSpec((ch, w), slot),
                      pl.BlockSpec((None, d, de), lambda c, ce, nu: (ce[c], 0, 0)),
                      pl.BlockSpec((None, d, de), lambda c, ce, nu: (ce[c], 0, 0)),
                      pl.BlockSpec((None, de, d), lambda c, ce, nu: (ce[c], 0, 0))],
            out_specs=pl.BlockSpec((ch, d), slot),
            scratch_shapes=[pltpu.VMEM((d, de), _BF16), pltpu.VMEM((d, de), _BF16), pltpu.VMEM((de, d), _BF16)]),
        compiler_params=_params("arbitrary"),
        name="experts",
    )(chunk_e, n_used, xs, wg, wu, wd)


def _combine_body(ps_ref, er_ref, x1_ref, rinfo_ref, mod_ref, ys_ref, o_ref, buf_ref, sem):
    tf = x1_ref.shape[0]

    def row_copy(r, k, slot):
        return pltpu.make_async_copy(ys_ref.at[pl.ds(slot, 1)], buf_ref.at[k, pl.ds(r, 1)], sem)

    def issue(r, c):
        row_copy(r, 0, ps_ref[er_ref[0, r]] + er_ref[2, r]).start(priority=0)
        row_copy(r, 1, ps_ref[er_ref[1, r]] + er_ref[3, r]).start(priority=1)
        return c

    def drain(r, c):
        row_copy(0, 0, 0).wait()
        row_copy(0, 0, 0).wait()
        return c

    lax.fori_loop(0, tf, issue, 0, unroll=DMA_UNROLL)
    lax.fori_loop(0, tf, drain, 0, unroll=DMA_UNROLL)
    w0 = rinfo_ref[:, 4:5]
    w1 = rinfo_ref[:, 5:6]
    y = w0 * buf_ref[0] + w1 * buf_ref[1]
    o_ref[...] = x1_ref[...] + mod_ref[:, 5 * D_MODEL:6 * D_MODEL] * y


def _combine(pstart, er, x1, rinfo, mod3, ys, seq):
    t, d = x1.shape
    tf = min(COMBINE_TILE, seq)
    per_b = seq // tf
    return pl.pallas_call(
        _combine_body,
        out_shape=jax.ShapeDtypeStruct((t, d), _F32),
        grid_spec=pltpu.PrefetchScalarGridSpec(
            num_scalar_prefetch=1,
            grid=(t // tf,),
            in_specs=[pl.BlockSpec((8, tf), lambda i, ps: (0, i), memory_space=pltpu.SMEM),
                      pl.BlockSpec((tf, d), lambda i, ps: (i, 0)),
                      pl.BlockSpec((tf, ROUTER_LANES), lambda i, ps: (i, 0)),
                      pl.BlockSpec((None, 1, mod3.shape[2]), lambda i, ps: (i // per_b, 0, 0)),
                      pl.BlockSpec(memory_space=pl.ANY)],
            out_specs=pl.BlockSpec((tf, d), lambda i, ps: (i, 0)),
            scratch_shapes=[pltpu.VMEM((2, tf, d), _F32), pltpu.SemaphoreType.DMA(())]),
        compiler_params=_params("arbitrary"),
        name="combine",
    )(pstart, er, x1, rinfo, mod3, ys)


def _rope_tables(positions):
    half = ROPE_DIM // 2
    inv_freq = ROPE_THETA ** (-jnp.arange(0, ROPE_DIM, 2, dtype=_F32) / ROPE_DIM)
    lane = jnp.arange(LANES) % HEAD_DIM
    freq = jnp.where(lane < ROPE_DIM, inv_freq[lane % half], 0.0)
    ang = positions.reshape(-1).astype(_F32)[:, None] * freq[None, :]
    cos, sin = jnp.cos(ang), jnp.sin(ang)
    c = jnp.where(lane < ROPE_DIM, cos, 1.0)
    s1 = jnp.where(lane < half, -sin, 0.0)
    s2 = jnp.where((lane >= half) & (lane < ROPE_DIM), sin, 0.0)
    return c, s1, s2


def _layer(x, mod, positions, g_mix, g_ffn, w_in, w_bg, qg, kg, w_bd, w_bs, w_out, w_rg, b_rg, w_re, b_re,
           w_eg, w_eu, w_ed):
    batch, seq, d = x.shape
    t = batch * seq
    x2 = x.reshape(t, d)
    mod3 = mod.reshape(batch, 1, mod.shape[1])
    gw = GROUP_WIDTH

    lane = jnp.arange(gw)
    bd = jnp.where(lane[:, None] // HEAD_DIM == lane[None, :] // HEAD_DIM, 1.0 / HEAD_DIM, 0.0).astype(_BF16)
    ex = (jnp.arange(LANES)[:, None] == (lane[None, :] // HEAD_DIM) * LSE_SEG).astype(_BF16)
    tri = (jnp.arange(ROW_TILE)[:, None] > jnp.arange(ROW_TILE)[None, :]).astype(_BF16)
    rope = _rope_tables(positions)
    tile4 = lambda g: jnp.tile(g.astype(_F32), HEADS_PER_GROUP).reshape(1, gw)
    wr = jnp.zeros((d, ROUTER_LANES), _F32).at[:, :N_GROUPS].set(w_rg).at[:, N_GROUPS:N_GROUPS + N_EXPERTS].set(w_re)
    wrh = wr.astype(_BF16)
    wrl = (wr - wrh.astype(_F32)).astype(_BF16)
    br = jnp.zeros((1, ROUTER_LANES), _F32).at[0, :N_GROUPS].set(b_rg).at[0, N_GROUPS:N_GROUPS + N_EXPERTS].set(b_re)

    d0, d1, d2, sbp = _qkv(x2, mod3, g_mix.reshape(1, d), w_in.astype(_BF16), tile4(qg), tile4(kg), *rope, bd, seq)
    dil = [_dilated_group(a, dl, batch, seq) for a, (_, dl) in zip((d0, d1, d2), DIL_PATTERNS)]
    osb = _stick_breaking(sbp, batch, seq)

    x1, h2p, rinfo, er, cnt = _merge(
        x2, mod3, g_mix.reshape(1, d), g_ffn.reshape(1, d), [o for o, _ in dil], [l for _, l in dil], osb,
        w_bg.astype(_BF16), w_bd.astype(_BF16), w_bs.astype(_BF16), w_out.astype(_BF16), wrh, wrl, br, ex, tri, seq)

    ch = EXPERT_CHUNK
    counts = cnt[0, EXPERT_LANE0:EXPERT_LANE0 + N_EXPERTS].astype(jnp.int32)
    padded = (counts + ch - 1) // ch * ch
    pend = jnp.cumsum(padded)
    pstart = pend - padded
    n_chunks = -(-2 * t // ch) + N_EXPERTS
    chunk_start = jnp.arange(n_chunks, dtype=jnp.int32) * ch
    chunk_e = jnp.minimum(jnp.sum((pend[None, :] <= chunk_start[:, None]).astype(jnp.int32), axis=1), N_EXPERTS - 1)
    n_used = (pend[-1:] // ch).astype(jnp.int32)

    xs = _scatter(pstart, pend, er, h2p, n_chunks * ch)
    ys = _experts(chunk_e, n_used, xs, w_eg, w_eu, w_ed)
    out = _combine(pstart, er, x1, rinfo, mod3, ys, seq)
    return out.reshape(batch, seq, d)


def kernel(x, c, positions, w_ada, b_ada, g_norm_mix, g_norm_ffn, w_in, w_branch_gate, q_norm_g, k_norm_g,
           w_branch_dil, w_branch_sb, w_out, w_router_group, b_router_group, w_router_expert, b_router_expert,
           w_expert_gate, w_expert_up, w_expert_down):
    for l in range(w_ada.shape[0]):
        mod = _ada(c, w_ada[l], b_ada[l])
        x = _layer(x, mod, positions, g_norm_mix[l], g_norm_ffn[l], w_in[l], w_branch_gate[l], q_norm_g[l],
                   k_norm_g[l], w_branch_dil[l], w_branch_sb[l], w_out[l], w_router_group[l], b_router_group[l],
                   w_router_expert[l], b_router_expert[l], w_expert_gate[l], w_expert_up[l], w_expert_down[l])
    return x
```

```python
import functools

import jax
import jax.numpy as jnp
from jax import lax
from jax.experimental import pallas as pl
from jax.experimental.pallas import tpu as pltpu
from jax.experimental.pallas import tpu_sc as plsc

D_MODEL = 1024
HEAD_DIM = 64
DIL_PATTERNS = ((128, 1), (512, 4), (2048, 16))
HEADS_PER_GROUP = 4
GROUP_WIDTH = HEADS_PER_GROUP * HEAD_DIM
N_DIL_GROUPS = len(DIL_PATTERNS)
DIL_WIDTH = N_DIL_GROUPS * GROUP_WIDTH
QKV_WIDTH = 3 * DIL_WIDTH + 3 * GROUP_WIDTH
WINDOW_KEYS = 128
ROPE_THETA = 500000.0
ROPE_DIM = HEAD_DIM // 4
N_GROUPS = 4
EXPERTS_PER_GROUP = 8
N_EXPERTS = N_GROUPS * EXPERTS_PER_GROUP
D_EXPERT = 512
RMS_EPS = 1e-6
ATTN_SCALE = HEAD_DIM ** -0.5

LANES = 128
ROUTER_LANES = LANES
EXPERT_LANE0 = N_GROUPS
LSE_SEG = LANES // HEADS_PER_GROUP
NEG_BIG = -1e30
SB_DEAD_LOG = -120.0
SB_HEAD_BLOCKS = 3

ROW_TILE = 512
QBLK = 128
DIL_QTILE = 512
EXPERT_CHUNK = 512
COMBINE_TILE = 512
DEST_TILE = 8192
SC_CORES = 2
SC_SUBCORES = 16
SC_WORKERS = SC_CORES * SC_SUBCORES
SC_INDEX_WINDOW = 128
VMEM_LIMIT = 48 * 1024 * 1024

_BF16 = jnp.bfloat16
_F32 = jnp.float32
_NT = (((1,), (1,)), ((), ()))


def _dot(a, b):
    return jnp.dot(a, b, preferred_element_type=_F32)


def _dot_nt(a, b):
    return lax.dot_general(a, b, _NT, preferred_element_type=_F32)


def _split(a):
    hi = a.astype(_BF16)
    lo = (a - hi.astype(_F32)).astype(_BF16)
    return hi, lo


def _dot3(a, b):
    ah, al = _split(a)
    bh, bl = _split(b)
    return _dot(ah, bh) + (_dot(ah, bl) + _dot(al, bh))


def _rms_mod(x, g, scale, shift):
    y = x * lax.rsqrt(jnp.mean(x * x, axis=-1, keepdims=True) + RMS_EPS)
    return y * g * (1.0 + scale) + shift


def _params(*sem):
    return pltpu.CompilerParams(dimension_semantics=sem, vmem_limit_bytes=VMEM_LIMIT)


def _ada_body(c_ref, w_ref, b_ref, o_ref):
    c = c_ref[...]
    o_ref[...] = _dot3(c * jax.nn.sigmoid(c), w_ref[...]) + b_ref[...]


def _ada(c, w_ada, b_ada):
    b, d = c.shape
    n = w_ada.shape[1]
    rows = -(-b // 16) * 16
    cp = jnp.zeros((rows, d), _F32).at[:b].set(c)
    nt = 1536
    out = pl.pallas_call(
        _ada_body,
        out_shape=jax.ShapeDtypeStruct((rows, n), _F32),
        grid=(n // nt,),
        in_specs=[pl.BlockSpec((rows, d), lambda j: (0, 0)),
                  pl.BlockSpec((d, nt), lambda j: (0, j)),
                  pl.BlockSpec((1, nt), lambda j: (0, j))],
        out_specs=pl.BlockSpec((rows, nt), lambda j: (0, j)),
        compiler_params=_params("arbitrary"),
        name="ada",
    )(cp, w_ada, b_ada.reshape(1, n))
    return out[:b]


def _qkv_body(x_ref, mod_ref, g_ref, w_ref, qg_ref, kg_ref, c_ref, s1_ref, s2_ref, bd_ref,
              o0_ref, o1_ref, o2_ref, osb_ref, st_ref):
    d = D_MODEL
    tm = x_ref.shape[0]
    h = _rms_mod(x_ref[...], g_ref[...], mod_ref[:, d:2 * d], mod_ref[:, 0:d])
    hb = h.astype(_BF16)
    cc = jnp.concatenate([c_ref[...]] * 2, axis=1)
    s1 = jnp.concatenate([s1_ref[...]] * 2, axis=1)
    s2 = jnp.concatenate([s2_ref[...]] * 2, axis=1)
    bd = bd_ref[...]
    gw = GROUP_WIDTH

    def normed_rotated(col0, gain):
        acc = _dot(hb, w_ref[:, col0:col0 + gw])
        ms = _dot((acc * acc).astype(_BF16), bd)
        y = acc * lax.rsqrt(ms + RMS_EPS) * gain
        return y * cc + pltpu.roll(y, gw - ROPE_DIM // 2, 1) * s1 + pltpu.roll(y, ROPE_DIM // 2, 1) * s2

    def store(o_ref, dil, part, y):
        if dil == 1:
            o_ref[:, part * gw:(part + 1) * gw] = y.astype(_BF16)
            return
        for s in range(gw // LANES):
            st_ref[s] = y[:, s * LANES:(s + 1) * LANES]
        for r in range(dil):
            for s in range(gw // LANES):
                col0 = (3 * r + part) * gw + s * LANES
                o_ref[:, col0:col0 + LANES] = st_ref[s, pl.ds(r, tm // dil, stride=dil), :].astype(_BF16)

    for g, o_ref in enumerate((o0_ref, o1_ref, o2_ref)):
        dil = DIL_PATTERNS[g][1]
        store(o_ref, dil, 0, normed_rotated(g * gw, qg_ref[...]))
        store(o_ref, dil, 1, normed_rotated(DIL_WIDTH + g * gw, kg_ref[...]))
        store(o_ref, dil, 2, _dot(hb, w_ref[:, 2 * DIL_WIDTH + g * gw:2 * DIL_WIDTH + (g + 1) * gw]))
    for part in range(3):
        col0 = 3 * DIL_WIDTH + part * gw
        osb_ref[:, part * gw:(part + 1) * gw] = _dot(hb, w_ref[:, col0:col0 + gw]).astype(_BF16)


def _qkv(x2, mod3, g_mix, w_in, qg, kg, rope_c, rope_s1, rope_s2, bd, seq):
    t, d = x2.shape
    tm = ROW_TILE
    per_b = seq // tm
    row = lambda i: (i, 0)
    const = lambda i: (0, 0)
    width = 3 * GROUP_WIDTH
    dils = [dl for _, dl in DIL_PATTERNS] + [1]
    return pl.pallas_call(
        _qkv_body,
        out_shape=[jax.ShapeDtypeStruct((t // dl, dl * width), _BF16) for dl in dils],
        grid=(t // tm,),
        in_specs=[pl.BlockSpec((tm, d), row),
                  pl.BlockSpec((None, 1, mod3.shape[2]), lambda i: (i // per_b, 0, 0)),
                  pl.BlockSpec((1, d), const),
                  pl.BlockSpec(w_in.shape, const),
                  pl.BlockSpec((1, GROUP_WIDTH), const),
                  pl.BlockSpec((1, GROUP_WIDTH), const),
                  pl.BlockSpec((tm, LANES), row),
                  pl.BlockSpec((tm, LANES), row),
                  pl.BlockSpec((tm, LANES), row),
                  pl.BlockSpec(bd.shape, const)],
        out_specs=[pl.BlockSpec((tm // dl, dl * width), row) for dl in dils],
        scratch_shapes=[pltpu.VMEM((GROUP_WIDTH // LANES, tm, LANES), _F32)],
        compiler_params=_params("arbitrary"),
        name="qkv",
    )(x2, mod3, g_mix, w_in, qg, kg, rope_c, rope_s1, rope_s2, bd)


def _dil_body(q_ref, kp_ref, kc_ref, vp_ref, vc_ref, o_ref, lse_ref, kf_ref, vf_ref):
    tq = q_ref.shape[0]
    first = pl.program_id(2) == 0
    kf_ref[0:QBLK, :] = kp_ref[...]
    kf_ref[QBLK:, :] = kc_ref[...]
    vf_ref[0:QBLK, :] = vp_ref[...]
    vf_ref[QBLK:, :] = vc_ref[...]
    nh = HEADS_PER_GROUP
    row = lax.broadcasted_iota(jnp.int32, (nh * QBLK, 2 * QBLK), 0) & (QBLK - 1)
    col = lax.broadcasted_iota(jnp.int32, (nh * QBLK, 2 * QBLK), 1)
    band = (col >= row) & (col <= row + WINDOW_KEYS)
    lane = lax.broadcasted_iota(jnp.int32, (1, GROUP_WIDTH), 1)
    slane = lax.broadcasted_iota(jnp.int32, (1, LANES), 1)
    head_masks = [(lane >= h * HEAD_DIM) & (lane < (h + 1) * HEAD_DIM) for h in range(nh)]
    for j in range(tq // QBLK):
        qj = q_ref[j * QBLK:(j + 1) * QBLK, :]
        qs = jnp.concatenate([jnp.where(hm, qj, jnp.zeros_like(qj)) for hm in head_masks], axis=0)
        kcat = kf_ref[j * QBLK:(j + 2) * QBLK, :]
        vcat = vf_ref[j * QBLK:(j + 2) * QBLK, :]
        valid = band & ((col >= QBLK) | jnp.logical_not(first)) if j == 0 else band
        s = jnp.where(valid, _dot_nt(qs, kcat) * ATTN_SCALE, NEG_BIG)
        m = jnp.max(s, axis=1, keepdims=True)
        p = jnp.exp(s - m)
        l = jnp.sum(p, axis=1, keepdims=True)
        o_all = _dot(p.astype(_BF16), vcat) / l
        lse_all = m + jnp.log(l)
        o_acc = jnp.zeros((QBLK, GROUP_WIDTH), _F32)
        lse_t = jnp.zeros((QBLK, LANES), _F32)
        for h, hm in enumerate(head_masks):
            o_acc = jnp.where(hm, o_all[h * QBLK:(h + 1) * QBLK, :], o_acc)
            sm = (slane >= h * LSE_SEG) & (slane < (h + 1) * LSE_SEG)
            lse_t = jnp.where(sm, lse_all[h * QBLK:(h + 1) * QBLK, :], lse_t)
        o_ref[j * QBLK:(j + 1) * QBLK, :] = o_acc.astype(_BF16)
        lse_ref[j * QBLK:(j + 1) * QBLK, :] = lse_t


def _dilated_group(view2, dil, batch, seq):
    sd = seq // dil
    tq = min(DIL_QTILE, sd)
    per = tq // QBLK
    gw = GROUP_WIDTH
    view = view2.reshape(batch, sd, dil * 3 * gw)
    cur = lambda part: pl.BlockSpec((None, tq, gw), lambda b, r, i: (b, i, 3 * r + part))
    prev = lambda part: pl.BlockSpec((None, QBLK, gw),
                                     lambda b, r, i: (b, jnp.maximum(i * per - 1, 0), 3 * r + part))
    o, lse = pl.pallas_call(
        _dil_body,
        out_shape=[jax.ShapeDtypeStruct((batch, sd, dil * gw), _BF16),
                   jax.ShapeDtypeStruct((batch, sd, dil * LANES), _F32)],
        grid=(batch, dil, sd // tq),
        in_specs=[cur(0), prev(1), cur(1), prev(2), cur(2)],
        out_specs=[pl.BlockSpec((None, tq, gw), lambda b, r, i: (b, i, r)),
                   pl.BlockSpec((None, tq, LANES), lambda b, r, i: (b, i, r))],
        scratch_shapes=[pltpu.VMEM((tq + QBLK, gw), _BF16), pltpu.VMEM((tq + QBLK, gw), _BF16)],
        compiler_params=_params("arbitrary", "arbitrary", "arbitrary"),
        name=f"dil{dil}",
    )(view, view, view, view, view)
    return o.reshape(batch * sd, dil * gw), lse.reshape(batch * sd, dil * LANES)


def _sb_body(q_ref, k_ref, v_ref, o_ref, carry_ref, acc_ref, qs_ref):
    i = pl.program_id(1)
    q = q_ref[...]
    nh = HEADS_PER_GROUP
    row = lax.broadcasted_iota(jnp.int32, (nh * QBLK, QBLK), 0) & (QBLK - 1)
    col = lax.broadcasted_iota(jnp.int32, (nh * QBLK, QBLK), 1)
    strict = col < row
    ur = lax.broadcasted_iota(jnp.int32, (2 * QBLK, QBLK), 0) & (QBLK - 1)
    uc = lax.broadcasted_iota(jnp.int32, (2 * QBLK, QBLK), 1)
    u = jnp.where(ur > uc, 1.0, 0.0).astype(_BF16)
    lane = lax.broadcasted_iota(jnp.int32, (1, GROUP_WIDTH), 1)
    head_masks = [(lane >= h * HEAD_DIM) & (lane < (h + 1) * HEAD_DIM) for h in range(nh)]
    qs_ref[...] = jnp.concatenate([jnp.where(hm, q, jnp.zeros_like(q)) for hm in head_masks], axis=0)

    def softplus(z):
        return jnp.maximum(z, 0.0) + jnp.log(1.0 + jnp.exp(-jnp.abs(z)))

    def later_keys(log_1m):
        hi, lo = _split(log_1m)
        return _dot(jnp.concatenate([hi, lo], axis=1), u)

    nb = SB_HEAD_BLOCKS
    kbs = [i - (nb - 1) + j for j in range(nb)]
    starts = [pl.multiple_of(jnp.maximum(kb, 0) * QBLK, QBLK) for kb in kbs]
    keep = [strict if j == nb - 1 else (kbs[j] >= 0) for j in range(nb)]
    z = _dot_nt(qs_ref[...], jnp.concatenate([k_ref[pl.ds(s, QBLK), :] for s in starts], axis=0)) * ATTN_SCALE
    sp = softplus(z)
    log_1m = [jnp.where(keep[j], -sp[:, j * QBLK:(j + 1) * QBLK], 0.0) for j in range(nb)]
    totals = [jnp.sum(l, axis=1, keepdims=True) for l in log_1m]
    a_blocks = []
    after = jnp.zeros_like(totals[0])
    for j in reversed(range(nb)):
        cols = slice(j * QBLK, (j + 1) * QBLK)
        a = jnp.exp((z[:, cols] - sp[:, cols]) + later_keys(log_1m[j]) + after)
        a_blocks.insert(0, jnp.where(keep[j], a, 0.0).astype(_BF16))
        after = after + totals[j]
    acc_ref[...] = _dot(jnp.concatenate(a_blocks, axis=1),
                        jnp.concatenate([v_ref[pl.ds(s, QBLK), :] for s in starts], axis=0))
    carry_ref[...] = after

    def tile(kb):
        start = pl.multiple_of(kb * QBLK, QBLK)
        z = _dot_nt(qs_ref[...], k_ref[pl.ds(start, QBLK), :]) * ATTN_SCALE
        sp = softplus(z)
        log_1m = -sp
        a = jnp.exp((z - sp) + later_keys(log_1m) + carry_ref[...])
        acc_ref[...] += _dot(a.astype(_BF16), v_ref[pl.ds(start, QBLK), :])
        carry = carry_ref[...] + jnp.sum(log_1m, axis=1, keepdims=True)
        carry_ref[...] = carry
        return jnp.max(carry)

    def cond(st):
        return (st[0] >= 0) & (st[1] > SB_DEAD_LOG)

    def step(st):
        return st[0] - 1, tile(st[0])

    lax.while_loop(cond, step, (i - nb, jnp.max(after)))
    out = jnp.zeros((QBLK, GROUP_WIDTH), _F32)
    for h, hm in enumerate(head_masks):
        out = jnp.where(hm, acc_ref[h * QBLK:(h + 1) * QBLK, :], out)
    o_ref[...] = out.astype(_BF16)


def _stick_breaking(arr, batch, seq):
    gw = GROUP_WIDTH
    view = arr.reshape(batch, seq, 3 * gw)
    o = pl.pallas_call(
        _sb_body,
        out_shape=jax.ShapeDtypeStruct((batch, seq, gw), _BF16),
        grid=(batch, seq // QBLK),
        in_specs=[pl.BlockSpec((None, QBLK, gw), lambda b, i: (b, i, 0)),
                  pl.BlockSpec((None, seq, gw), lambda b, i: (b, 0, 1)),
                  pl.BlockSpec((None, seq, gw), lambda b, i: (b, 0, 2))],
        out_specs=pl.BlockSpec((None, QBLK, gw), lambda b, i: (b, i, 0)),
        scratch_shapes=[pltpu.VMEM((HEADS_PER_GROUP * QBLK, 1), _F32),
                        pltpu.VMEM((HEADS_PER_GROUP * QBLK, gw), _F32),
                        pltpu.VMEM((HEADS_PER_GROUP * QBLK, gw), _BF16)],
        compiler_params=_params("arbitrary", "arbitrary"),
        name="sb",
    )(view, view, view)
    return o.reshape(batch * seq, gw)


def _merge_body(x_ref, mod_ref, g1_ref, g2_ref, o0_ref, o1_ref, o2_ref, l0_ref, l1_ref, l2_ref, osb_ref,
                wbg_ref, wbd_ref, wbs_ref, wout_ref, wrh_ref, wrl_ref, br_ref, ex_ref, tri_ref,
                x1_ref, h2p_ref, rinfo_ref, er_ref, cnt_out_ref, cnt_ref, os1_ref, os2_ref, ls1_ref, ls2_ref):
    d = D_MODEL
    tm = x_ref.shape[0]
    x = x_ref[...]
    hb = _rms_mod(x, g1_ref[...], mod_ref[:, d:2 * d], mod_ref[:, 0:d]).astype(_BF16)
    gates = jax.nn.sigmoid(_dot(hb, wbg_ref[...]))

    def natural(ref, st_ref, dil):
        if dil == 1:
            return ref[...].astype(_F32)
        slabs = st_ref.shape[0]
        for r in range(dil):
            for s in range(slabs):
                col0 = (r * slabs + s) * LANES
                st_ref[s, pl.ds(r, tm // dil, stride=dil), :] = ref[:, col0:col0 + LANES].astype(_F32)
        return jnp.concatenate([st_ref[s] for s in range(slabs)], axis=1)

    dils = [dl for _, dl in DIL_PATTERNS]
    o_nat = [natural(r, s, dl) for r, s, dl in zip((o0_ref, o1_ref, o2_ref), (None, os1_ref, os2_ref), dils)]
    l0, l1, l2 = [natural(r, s, dl) for r, s, dl in zip((l0_ref, l1_ref, l2_ref), (None, ls1_ref, ls2_ref), dils)]

    lmax = jnp.maximum(jnp.maximum(l0, l1), l2)
    e0, e1, e2 = jnp.exp(l0 - lmax), jnp.exp(l1 - lmax), jnp.exp(l2 - lmax)
    inv = 1.0 / (e0 + e1 + e2)
    ex = ex_ref[...]

    def widen(w):
        hi, lo = _split(w)
        return _dot(hi, ex) + _dot(lo, ex)

    o_dil = widen(e0 * inv) * o_nat[0] + widen(e1 * inv) * o_nat[1] + widen(e2 * inv) * o_nat[2]
    merged = (gates[:, :d] * _dot(o_dil.astype(_BF16), wbd_ref[...])
              + gates[:, d:] * _dot(osb_ref[...], wbs_ref[...]))
    x1 = x + mod_ref[:, 2 * d:3 * d] * _dot(merged.astype(_BF16), wout_ref[...])
    x1_ref[...] = x1

    h2 = _rms_mod(x1, g2_ref[...], mod_ref[:, 4 * d:5 * d], mod_ref[:, 3 * d:4 * d])
    h2p_ref[...] = _pack_halves(h2)

    hh, hl = _split(h2)
    logits = _dot(hh, wrh_ref[...]) + (_dot(hl, wrh_ref[...]) + _dot(hh, wrl_ref[...])) + br_ref[...]
    lane =lax.broadcasted_iota(jnp.int32, (tm, ROUTER_LANES), 1).astype(_F32)
    far = float(ROUTER_LANES)

    def top(vals):
        m = jnp.max(vals, axis=1, keepdims=True)
        return m, jnp.min(jnp.where(vals == m, lane, far), axis=1, keepdims=True)

    is_group = lane < N_GROUPS
    mg, gsel = top(jnp.where(is_group, logits, NEG_BIG))
    pg_top = 1.0 / jnp.sum(jnp.where(is_group, jnp.exp(logits - mg), 0.0), axis=1, keepdims=True)
    lane0 = EXPERT_LANE0 + EXPERTS_PER_GROUP * gsel
    le = jnp.where((lane >= lane0) & (lane < lane0 + EXPERTS_PER_GROUP), logits, NEG_BIG)
    m1, i1 = top(le)
    m2, i2 = top(jnp.where(lane == i1, NEG_BIG, le))
    t2 = jnp.exp(m2 - m1)
    w0 = pg_top / (1.0 + t2)
    w1 = pg_top * t2 / (1.0 + t2)

    @pl.when(pl.program_id(0) == 0)
    def _():
        cnt_ref[...] = jnp.zeros_like(cnt_ref)

    sel0, sel1 = lane == i1, lane == i2
    onehot = jnp.where(sel0 | sel1, 1.0, 0.0)
    before = _dot(tri_ref[...], onehot.astype(_BF16)) + cnt_ref[0:1, :]
    r0 = jnp.sum(jnp.where(sel0, before, 0.0), axis=1, keepdims=True)
    r1 = jnp.sum(jnp.where(sel1, before, 0.0), axis=1, keepdims=True)
    cnt_ref[...] += jnp.sum(onehot, axis=0, keepdims=True)
    cnt_out_ref[...] = cnt_ref[...]

    cols = (i1 - EXPERT_LANE0, i2 - EXPERT_LANE0, r0, r1, w0, w1)
    rinfo = jnp.zeros((tm, ROUTER_LANES), _F32)
    for c, v in enumerate(cols):
        rinfo = jnp.where(lane == float(c), v, rinfo)
    rinfo_ref[...] = rinfo
    er_ref[...] = jnp.transpose(rinfo)[0:8, :].astype(jnp.int32)


def _merge(x2, mod3, g1, g2, outs, lses, osb, wbg, wbd, wbs, wout, wrh, wrl, br, ex, tri, seq):
    t, d = x2.shape
    tm = ROW_TILE
    per_b = seq // tm
    row = lambda i: (i, 0)
    const = lambda i: (0, 0)
    full = lambda a: pl.BlockSpec(a.shape, const)
    gw = GROUP_WIDTH
    dils = [dl for _, dl in DIL_PATTERNS]
    return pl.pallas_call(
        _merge_body,
        out_shape=[jax.ShapeDtypeStruct((t, d), _F32),
                   jax.ShapeDtypeStruct((t, d // 2), jnp.uint32),
                   jax.ShapeDtypeStruct((t, ROUTER_LANES), _F32),
                   jax.ShapeDtypeStruct((8, t), jnp.int32),
                   jax.ShapeDtypeStruct((8, ROUTER_LANES), _F32)],
        grid=(t // tm,),
        in_specs=[pl.BlockSpec((tm, d), row),
                  pl.BlockSpec((None, 1, mod3.shape[2]), lambda i: (i // per_b, 0, 0)),
                  full(g1), full(g2)]
                 + [pl.BlockSpec((tm // dl, dl * gw), row) for dl in dils]
                 + [pl.BlockSpec((tm // dl, dl * LANES), row) for dl in dils]
                 + [pl.BlockSpec((tm, gw), row)]
                 + [full(a) for a in (wbg, wbd, wbs, wout, wrh, wrl, br, ex, tri)],
        out_specs=[pl.BlockSpec((tm, d), row),
                   pl.BlockSpec((tm, d // 2), row),
                   pl.BlockSpec((tm, ROUTER_LANES), row),
                   pl.BlockSpec((8, tm), lambda i: (0, i)),
                   pl.BlockSpec((8, ROUTER_LANES), const)],
        scratch_shapes=[pltpu.VMEM((8, ROUTER_LANES), _F32),
                        pltpu.VMEM((gw // LANES, tm, LANES), _F32), pltpu.VMEM((gw // LANES, tm, LANES), _F32),
                        pltpu.VMEM((1, tm, LANES), _F32), pltpu.VMEM((1, tm, LANES), _F32)],
        compiler_params=_params("arbitrary"),
        name="merge",
    )(x2, mod3, g1, g2, *outs, *lses, osb, wbg, wbd, wbs, wout, wrh, wrl, br, ex, tri)


def _dest_body(ps_ref, er_ref, d_ref):
    e = er_ref[0:2, :]
    start = jnp.zeros_like(e)
    for x in range(N_EXPERTS):
        start = jnp.where(e == x, ps_ref[x], start)
    d_ref[...] = start + er_ref[2:4, :]


def _dest(pstart, er):
    t = er.shape[1]
    tw = min(DEST_TILE, t)
    return pl.pallas_call(
        _dest_body,
        out_shape=jax.ShapeDtypeStruct((2, t), jnp.int32),
        grid_spec=pltpu.PrefetchScalarGridSpec(
            num_scalar_prefetch=1,
            grid=(t // tw,),
            in_specs=[pl.BlockSpec((8, tw), lambda i, ps: (0, i))],
            out_specs=pl.BlockSpec((2, tw), lambda i, ps: (0, i))),
        compiler_params=_params("arbitrary"),
        name="dest",
    )(pstart, er)


def _sc_mesh():
    return plsc.VectorSubcoreMesh(core_axis_name="core", subcore_axis_name="subcore",
                                  num_cores=SC_CORES, num_subcores=SC_SUBCORES)


def _sc_worker():
    return lax.axis_index("subcore") * SC_CORES + lax.axis_index("core")


def _sc_scatter(x, idx0, idx1, n_slots):
    chunks = idx0.shape[0]
    per = chunks // SC_WORKERS
    win = idx0.shape[1]

    @functools.partial(
        pl.kernel, mesh=_sc_mesh(), out_type=jax.ShapeDtypeStruct((n_slots, x.shape[1]), x.dtype),
        scratch_types=[pltpu.VMEM((1, win), jnp.int32), pltpu.VMEM((1, win), jnp.int32),
                       pltpu.VMEM((win, x.shape[1]), x.dtype), pltpu.SemaphoreType.DMA],
        name="sc_scatter")
    def run(x_hbm, i0_hbm, i1_hbm, o_hbm, i0_v, i1_v, rows_v, sem):
        wid = _sc_worker()

        @pl.loop(0, per)
        def _(j):
            c = wid * per + j
            pltpu.sync_copy(i0_hbm.at[pl.ds(c, 1)], i0_v)
            pltpu.sync_copy(i1_hbm.at[pl.ds(c, 1)], i1_v)
            pltpu.sync_copy(x_hbm.at[pl.ds(c * win, win)], rows_v)
            first = pltpu.async_copy(rows_v, o_hbm.at[i0_v.at[0]], sem)
            second = pltpu.async_copy(rows_v, o_hbm.at[i1_v.at[0]], sem)
            first.wait()
            second.wait()

    return run(x, idx0, idx1)


def _sc_gather(table, idx):
    chunks, win = idx.shape
    per = chunks // SC_WORKERS

    @functools.partial(
        pl.kernel, mesh=_sc_mesh(), out_type=jax.ShapeDtypeStruct((chunks * win, table.shape[1]), table.dtype),
        scratch_types=[pltpu.VMEM((1, win), jnp.int32), pltpu.VMEM((win, table.shape[1]), table.dtype),
                       pltpu.SemaphoreType.DMA],
        name="sc_gather")
    def run(t_hbm, i_hbm, o_hbm, i_v, rows_v, sem):
        wid = _sc_worker()

        @pl.loop(0, per)
        def _(j):
            c = wid * per + j
            pltpu.sync_copy(i_hbm.at[pl.ds(c, 1)], i_v)
            pltpu.async_copy(t_hbm.at[i_v.at[0]], rows_v, sem).wait()
            pltpu.sync_copy(rows_v, o_hbm.at[pl.ds(c * win, win)])

    return run(table, idx)


def _pack_halves(a):
    h = a.shape[1] // 2
    lo = lax.bitcast_convert_type(a[:, :h].astype(_BF16).astype(_F32), jnp.uint32) >> 16
    hi = lax.bitcast_convert_type(a[:, h:].astype(_BF16).astype(_F32), jnp.uint32) & jnp.uint32(0xFFFF0000)
    return lo | hi


def _unpack_halves(w):
    return jnp.concatenate(
        [lax.bitcast_convert_type(w << 16, _F32), lax.bitcast_convert_type(w & jnp.uint32(0xFFFF0000), _F32)], axis=1)


def _experts_body(ce_ref, nv_ref, nu_ref, xs_ref, wg_ref, wu_ref, wd_ref, ys_ref, wgb_ref, wub_ref, wdb_ref):
    c = pl.program_id(0)

    @pl.when(c < nu_ref[0])
    def _():
        @pl.when((c == 0) | (ce_ref[c] != ce_ref[jnp.maximum(c - 1, 0)]))
        def _():
            wgb_ref[...] = wg_ref[...].astype(_BF16)
            wub_ref[...] = wu_ref[...].astype(_BF16)
            wdb_ref[...] = wd_ref[...].astype(_BF16)

        row = lax.broadcasted_iota(jnp.int32, xs_ref.shape, 0)
        x = _unpack_halves(jnp.where(row < nv_ref[c], xs_ref[...], jnp.uint32(0))).astype(_BF16)
        g = _dot(x, wgb_ref[...])
        hmid = (g * jax.nn.sigmoid(g)) * _dot(x, wub_ref[...])
        ys_ref[...] = _pack_halves(_dot(hmid.astype(_BF16), wdb_ref[...]))


def _experts(chunk_e, n_valid, n_used, xs, wg, wu, wd):
    n_slots, w = xs.shape
    ch = EXPERT_CHUNK
    d, de = wg.shape[1], wg.shape[2]
    slot = lambda c, ce, nv, nu: (jnp.minimum(c, nu[0] - 1), 0)
    weight = lambda c, ce, nv, nu: (ce[c], 0, 0)
    return pl.pallas_call(
        _experts_body,
        out_shape=jax.ShapeDtypeStruct((n_slots, d // 2), jnp.uint32),
        grid_spec=pltpu.PrefetchScalarGridSpec(
            num_scalar_prefetch=3,
            grid=(n_slots // ch,),
            in_specs=[pl.BlockSpec((ch, w), slot),
                      pl.BlockSpec((None, d, de), weight),
                      pl.BlockSpec((None, d, de), weight),
                      pl.BlockSpec((None, de, d), weight)],
            out_specs=pl.BlockSpec((ch, d // 2), slot),
            scratch_shapes=[pltpu.VMEM((d, de), _BF16), pltpu.VMEM((d, de), _BF16), pltpu.VMEM((de, d), _BF16)]),
        compiler_params=_params("arbitrary"),
        name="experts",
    )(chunk_e, n_valid, n_used, xs, wg, wu, wd)


def _combine_body(x1_ref, rinfo_ref, mod_ref, y0_ref, y1_ref, o_ref):
    y = rinfo_ref[:, 4:5] * _unpack_halves(y0_ref[...]) + rinfo_ref[:, 5:6] * _unpack_halves(y1_ref[...])
    o_ref[...] = x1_ref[...] + mod_ref[:, 5 * D_MODEL:6 * D_MODEL] * y


def _combine(x1, rinfo, mod3, gathered, seq):
    t, d = x1.shape
    tf = min(COMBINE_TILE, seq)
    per_b = seq // tf
    nt = t // tf
    return pl.pallas_call(
        _combine_body,
        out_shape=jax.ShapeDtypeStruct((t, d), _F32),
        grid=(nt,),
        in_specs=[pl.BlockSpec((tf, d), lambda i: (i, 0)),
                  pl.BlockSpec((tf, ROUTER_LANES), lambda i: (i, 0)),
                  pl.BlockSpec((None, 1, mod3.shape[2]), lambda i: (i // per_b, 0, 0)),
                  pl.BlockSpec((tf, d // 2), lambda i: (i, 0)),
                  pl.BlockSpec((tf, d // 2), lambda i: (i + nt, 0))],
        out_specs=pl.BlockSpec((tf, d), lambda i: (i, 0)),
        compiler_params=_params("arbitrary"),
        name="combine",
    )(x1, rinfo, mod3, gathered, gathered)


def _rope_tables(positions):
    half = ROPE_DIM // 2
    inv_freq = ROPE_THETA ** (-jnp.arange(0, ROPE_DIM, 2, dtype=_F32) / ROPE_DIM)
    lane = jnp.arange(LANES) % HEAD_DIM
    freq = jnp.where(lane < ROPE_DIM, inv_freq[lane % half], 0.0)
    ang = positions.reshape(-1).astype(_F32)[:, None] * freq[None, :]
    cos, sin = jnp.cos(ang), jnp.sin(ang)
    c = jnp.where(lane < ROPE_DIM, cos, 1.0)
    s1 = jnp.where(lane < half, -sin, 0.0)
    s2 = jnp.where((lane >= half) & (lane < ROPE_DIM), sin, 0.0)
    return c, s1, s2


def _layer(x, mod, positions, g_mix, g_ffn, w_in, w_bg, qg, kg, w_bd, w_bs, w_out, w_rg, b_rg, w_re, b_re,
           w_eg, w_eu, w_ed):
    batch, seq, d = x.shape
    t = batch * seq
    x2 = x.reshape(t, d)
    mod3 = mod.reshape(batch, 1, mod.shape[1])
    gw = GROUP_WIDTH

    lane = jnp.arange(gw)
    bd = jnp.where(lane[:, None] // HEAD_DIM == lane[None, :] // HEAD_DIM, 1.0 / HEAD_DIM, 0.0).astype(_BF16)
    ex = (jnp.arange(LANES)[:, None] == (lane[None, :] // HEAD_DIM) * LSE_SEG).astype(_BF16)
    tri = (jnp.arange(ROW_TILE)[:, None] > jnp.arange(ROW_TILE)[None, :]).astype(_BF16)
    rope = _rope_tables(positions)
    tile4 = lambda g: jnp.tile(g.astype(_F32), HEADS_PER_GROUP).reshape(1, gw)
    wr = jnp.zeros((d, ROUTER_LANES), _F32).at[:, :N_GROUPS].set(w_rg).at[:, N_GROUPS:N_GROUPS + N_EXPERTS].set(w_re)
    wrh = wr.astype(_BF16)
    wrl = (wr - wrh.astype(_F32)).astype(_BF16)
    br = jnp.zeros((1, ROUTER_LANES), _F32).at[0, :N_GROUPS].set(b_rg).at[0, N_GROUPS:N_GROUPS + N_EXPERTS].set(b_re)

    d0, d1, d2, sbp = _qkv(x2, mod3, g_mix.reshape(1, d), w_in.astype(_BF16), tile4(qg), tile4(kg), *rope, bd, seq)
    dil = [_dilated_group(a, dl, batch, seq) for a, (_, dl) in zip((d0, d1, d2), DIL_PATTERNS)]
    osb = _stick_breaking(sbp, batch, seq)

    x1, h2p, rinfo, er, cnt = _merge(
        x2, mod3, g_mix.reshape(1, d), g_ffn.reshape(1, d), [o for o, _ in dil], [l for _, l in dil], osb,
        w_bg.astype(_BF16), w_bd.astype(_BF16), w_bs.astype(_BF16), w_out.astype(_BF16), wrh, wrl, br, ex, tri, seq)

    ch = EXPERT_CHUNK
    counts = cnt[0, EXPERT_LANE0:EXPERT_LANE0 + N_EXPERTS].astype(jnp.int32)
    padded = (counts + ch - 1) // ch * ch
    pend = jnp.cumsum(padded)
    pstart = pend - padded
    n_chunks = -(-2 * t // ch) + N_EXPERTS
    chunk_start = jnp.arange(n_chunks, dtype=jnp.int32) * ch
    chunk_e = jnp.minimum(jnp.sum((pend[None, :] <= chunk_start[:, None]).astype(jnp.int32), axis=1), N_EXPERTS - 1)
    n_used = (pend[-1:] // ch).astype(jnp.int32)
    n_valid = jnp.clip(counts[chunk_e] - (chunk_start - pstart[chunk_e]), 0, ch).astype(jnp.int32)

    dest = _dest(pstart, er)
    win = SC_INDEX_WINDOW
    xs = _sc_scatter(h2p, dest[0].reshape(t // win, win), dest[1].reshape(t // win, win), n_chunks * ch)
    ys = _experts(chunk_e, n_valid, n_used, xs, w_eg, w_eu, w_ed)
    gathered = _sc_gather(ys, dest.reshape(2 * t // win, win))
    out = _combine(x1, rinfo, mod3, gathered, seq)
    return out.reshape(batch, seq, d)


def kernel(x, c, positions, w_ada, b_ada, g_norm_mix, g_norm_ffn, w_in, w_branch_gate, q_norm_g, k_norm_g,
           w_branch_dil, w_branch_sb, w_out, w_router_group, b_router_group, w_router_expert, b_router_expert,
           w_expert_gate, w_expert_up, w_expert_down):
    for l in range(w_ada.shape[0]):
        mod = _ada(c, w_ada[l], b_ada[l])
        x = _layer(x, mod, positions, g_norm_mix[l], g_norm_ffn[l], w_in[l], w_branch_gate[l], q_norm_g[l],
                   k_norm_g[l], w_branch_dil[l], w_branch_sb[l], w_out[l], w_router_group[l], b_router_group[l],
                   w_router_expert[l], b_router_expert[l], w_expert_gate[l], w_expert_up[l], w_expert_down[l])
    return x
```

```python
import functools

import jax
import jax.numpy as jnp
from jax import lax
from jax.experimental import pallas as pl
from jax.experimental.pallas import tpu as pltpu
from jax.experimental.pallas import tpu_sc as plsc

D_MODEL = 1024
HEAD_DIM = 64
DIL_PATTERNS = ((128, 1), (512, 4), (2048, 16))
HEADS_PER_GROUP = 4
GROUP_WIDTH = HEADS_PER_GROUP * HEAD_DIM
N_DIL_GROUPS = len(DIL_PATTERNS)
DIL_WIDTH = N_DIL_GROUPS * GROUP_WIDTH
QKV_WIDTH = 3 * DIL_WIDTH + 3 * GROUP_WIDTH
WINDOW_KEYS = 128
ROPE_THETA = 500000.0
ROPE_DIM = HEAD_DIM // 4
N_GROUPS = 4
EXPERTS_PER_GROUP = 8
N_EXPERTS = N_GROUPS * EXPERTS_PER_GROUP
D_EXPERT = 512
RMS_EPS = 1e-6
ATTN_SCALE = HEAD_DIM ** -0.5

LANES = 128
ROUTER_LANES = LANES
EXPERT_LANE0 = N_GROUPS
LSE_SEG = LANES // HEADS_PER_GROUP
NEG_BIG = -1e30
SB_DEAD_LOG = -120.0
SB_HEAD_BLOCKS = 3

ROW_TILE = 512
QBLK = 128
DIL_QTILE = 512
EXPERT_CHUNK = 512
COMBINE_TILE = 512
DEST_TILE = 8192
SC_CORES = 2
SC_SUBCORES = 16
SC_WORKERS = SC_CORES * SC_SUBCORES
SC_INDEX_WINDOW = 128
MOE_PARTS = 2
VMEM_LIMIT = 48 * 1024 * 1024

_BF16 = jnp.bfloat16
_F32 = jnp.float32
_NT = (((1,), (1,)), ((), ()))


def _dot(a, b):
    return jnp.dot(a, b, preferred_element_type=_F32)


def _dot_nt(a, b):
    return lax.dot_general(a, b, _NT, preferred_element_type=_F32)


def _split(a):
    hi = a.astype(_BF16)
    lo = (a - hi.astype(_F32)).astype(_BF16)
    return hi, lo


def _dot3(a, b):
    ah, al = _split(a)
    bh, bl = _split(b)
    return _dot(ah, bh) + (_dot(ah, bl) + _dot(al, bh))


def _rms_mod(x, g, scale, shift):
    y = x * lax.rsqrt(jnp.mean(x * x, axis=-1, keepdims=True) + RMS_EPS)
    return y * g * (1.0 + scale) + shift


def _params(*sem):
    return pltpu.CompilerParams(dimension_semantics=sem, vmem_limit_bytes=VMEM_LIMIT)


def _ada_body(c_ref, w_ref, b_ref, o_ref):
    c = c_ref[...]
    o_ref[...] = _dot3(c * jax.nn.sigmoid(c), w_ref[...]) + b_ref[...]


def _ada(c, w_ada, b_ada):
    b, d = c.shape
    n = w_ada.shape[1]
    rows = -(-b // 16) * 16
    cp = jnp.zeros((rows, d), _F32).at[:b].set(c)
    nt = 1536
    out = pl.pallas_call(
        _ada_body,
        out_shape=jax.ShapeDtypeStruct((rows, n), _F32),
        grid=(n // nt,),
        in_specs=[pl.BlockSpec((rows, d), lambda j: (0, 0)),
                  pl.BlockSpec((d, nt), lambda j: (0, j)),
                  pl.BlockSpec((1, nt), lambda j: (0, j))],
        out_specs=pl.BlockSpec((rows, nt), lambda j: (0, j)),
        compiler_params=_params("arbitrary"),
        name="ada",
    )(cp, w_ada, b_ada.reshape(1, n))
    return out[:b]


def _qkv_body(x_ref, mod_ref, g_ref, w_ref, qg_ref, kg_ref, c_ref, s1_ref, s2_ref, bd_ref,
              o0_ref, o1_ref, o2_ref, osb_ref, st_ref):
    d = D_MODEL
    tm = x_ref.shape[0]
    h = _rms_mod(x_ref[...], g_ref[...], mod_ref[:, d:2 * d], mod_ref[:, 0:d])
    hb = h.astype(_BF16)
    cc = jnp.concatenate([c_ref[...]] * 2, axis=1)
    s1 = jnp.concatenate([s1_ref[...]] * 2, axis=1)
    s2 = jnp.concatenate([s2_ref[...]] * 2, axis=1)
    bd = bd_ref[...]
    gw = GROUP_WIDTH

    def normed_rotated(col0, gain):
        acc = _dot(hb, w_ref[:, col0:col0 + gw])
        ms = _dot((acc * acc).astype(_BF16), bd)
        y = acc * lax.rsqrt(ms + RMS_EPS) * gain
        return y * cc + pltpu.roll(y, gw - ROPE_DIM // 2, 1) * s1 + pltpu.roll(y, ROPE_DIM // 2, 1) * s2

    def store(o_ref, dil, part, y):
        if dil == 1:
            o_ref[:, part * gw:(part + 1) * gw] = y.astype(_BF16)
            return
        for s in range(gw // LANES):
            st_ref[s] = y[:, s * LANES:(s + 1) * LANES]
        for r in range(dil):
            for s in range(gw // LANES):
                col0 = (3 * r + part) * gw + s * LANES
                o_ref[:, col0:col0 + LANES] = st_ref[s, pl.ds(r, tm // dil, stride=dil), :].astype(_BF16)

    for g, o_ref in enumerate((o0_ref, o1_ref, o2_ref)):
        dil = DIL_PATTERNS[g][1]
        store(o_ref, dil, 0, normed_rotated(g * gw, qg_ref[...]))
        store(o_ref, dil, 1, normed_rotated(DIL_WIDTH + g * gw, kg_ref[...]))
        store(o_ref, dil, 2, _dot(hb, w_ref[:, 2 * DIL_WIDTH + g * gw:2 * DIL_WIDTH + (g + 1) * gw]))
    for part in range(3):
        col0 = 3 * DIL_WIDTH + part * gw
        acc = _dot(hb, w_ref[:, col0:col0 + gw])
        osb_ref[:, part * gw:(part + 1) * gw] = (acc * ATTN_SCALE if part == 0 else acc).astype(_BF16)


def _qkv(x2, mod3, g_mix, w_in, qg, kg, rope_c, rope_s1, rope_s2, bd, seq):
    t, d = x2.shape
    tm = ROW_TILE
    per_b = seq // tm
    row = lambda i: (i, 0)
    const = lambda i: (0, 0)
    width = 3 * GROUP_WIDTH
    dils = [dl for _, dl in DIL_PATTERNS] + [1]
    return pl.pallas_call(
        _qkv_body,
        out_shape=[jax.ShapeDtypeStruct((t // dl, dl * width), _BF16) for dl in dils],
        grid=(t // tm,),
        in_specs=[pl.BlockSpec((tm, d), row),
                  pl.BlockSpec((None, 1, mod3.shape[2]), lambda i: (i // per_b, 0, 0)),
                  pl.BlockSpec((1, d), const),
                  pl.BlockSpec(w_in.shape, const),
                  pl.BlockSpec((1, GROUP_WIDTH), const),
                  pl.BlockSpec((1, GROUP_WIDTH), const),
                  pl.BlockSpec((tm, LANES), row),
                  pl.BlockSpec((tm, LANES), row),
                  pl.BlockSpec((tm, LANES), row),
                  pl.BlockSpec(bd.shape, const)],
        out_specs=[pl.BlockSpec((tm // dl, dl * width), row) for dl in dils],
        scratch_shapes=[pltpu.VMEM((GROUP_WIDTH // LANES, tm, LANES), _F32)],
        compiler_params=_params("arbitrary"),
        name="qkv",
    )(x2, mod3, g_mix, w_in, qg, kg, rope_c, rope_s1, rope_s2, bd)


def _dil_body(q_ref, kp_ref, kc_ref, vp_ref, vc_ref, o_ref, lse_ref, kf_ref, vf_ref):
    tq = q_ref.shape[0]
    first = pl.program_id(2) == 0
    kf_ref[0:QBLK, :] = kp_ref[...]
    kf_ref[QBLK:, :] = kc_ref[...]
    vf_ref[0:QBLK, :] = vp_ref[...]
    vf_ref[QBLK:, :] = vc_ref[...]
    nh = HEADS_PER_GROUP
    row = lax.broadcasted_iota(jnp.int32, (nh * QBLK, 2 * QBLK), 0) & (QBLK - 1)
    col = lax.broadcasted_iota(jnp.int32, (nh * QBLK, 2 * QBLK), 1)
    band = (col >= row) & (col <= row + WINDOW_KEYS)
    lane = lax.broadcasted_iota(jnp.int32, (1, GROUP_WIDTH), 1)
    slane = lax.broadcasted_iota(jnp.int32, (1, LANES), 1)
    head_masks = [(lane >= h * HEAD_DIM) & (lane < (h + 1) * HEAD_DIM) for h in range(nh)]
    for j in range(tq // QBLK):
        qj = q_ref[j * QBLK:(j + 1) * QBLK, :]
        qs = jnp.concatenate([jnp.where(hm, qj, jnp.zeros_like(qj)) for hm in head_masks], axis=0)
        kcat = kf_ref[j * QBLK:(j + 2) * QBLK, :]
        vcat = vf_ref[j * QBLK:(j + 2) * QBLK, :]
        valid = band & ((col >= QBLK) | jnp.logical_not(first)) if j == 0 else band
        s = jnp.where(valid, _dot_nt(qs, kcat), NEG_BIG)
        m = jnp.max(s, axis=1, keepdims=True)
        p = jnp.exp(s - m)
        l = jnp.sum(p, axis=1, keepdims=True)
        o_all = _dot(p.astype(_BF16), vcat) / l
        lse_all = m + jnp.log(l)
        o_acc = jnp.zeros((QBLK, GROUP_WIDTH), _F32)
        lse_t = jnp.zeros((QBLK, LANES), _F32)
        for h, hm in enumerate(head_masks):
            o_acc = jnp.where(hm, o_all[h * QBLK:(h + 1) * QBLK, :], o_acc)
            sm = (slane >= h * LSE_SEG) & (slane < (h + 1) * LSE_SEG)
            lse_t = jnp.where(sm, lse_all[h * QBLK:(h + 1) * QBLK, :], lse_t)
        o_ref[j * QBLK:(j + 1) * QBLK, :] = o_acc.astype(_BF16)
        lse_ref[j * QBLK:(j + 1) * QBLK, :] = lse_t


def _dilated_group(view2, dil, batch, seq):
    sd = seq // dil
    tq = min(DIL_QTILE, sd)
    per = tq // QBLK
    gw = GROUP_WIDTH
    view = view2.reshape(batch, sd, dil * 3 * gw)
    cur = lambda part: pl.BlockSpec((None, tq, gw), lambda b, r, i: (b, i, 3 * r + part))
    prev = lambda part: pl.BlockSpec((None, QBLK, gw),
                                     lambda b, r, i: (b, jnp.maximum(i * per - 1, 0), 3 * r + part))
    o, lse = pl.pallas_call(
        _dil_body,
        out_shape=[jax.ShapeDtypeStruct((batch, sd, dil * gw), _BF16),
                   jax.ShapeDtypeStruct((batch, sd, dil * LANES), _F32)],
        grid=(batch, dil, sd // tq),
        in_specs=[cur(0), prev(1), cur(1), prev(2), cur(2)],
        out_specs=[pl.BlockSpec((None, tq, gw), lambda b, r, i: (b, i, r)),
                   pl.BlockSpec((None, tq, LANES), lambda b, r, i: (b, i, r))],
        scratch_shapes=[pltpu.VMEM((tq + QBLK, gw), _BF16), pltpu.VMEM((tq + QBLK, gw), _BF16)],
        compiler_params=_params("arbitrary", "arbitrary", "arbitrary"),
        name=f"dil{dil}",
    )(view, view, view, view, view)
    return o.reshape(batch * sd, dil * gw), lse.reshape(batch * sd, dil * LANES)


def _sb_body(q_ref, k_ref, v_ref, o_ref, carry_ref, acc_ref, qs_ref):
    i = pl.program_id(1)
    q = q_ref[...]
    nh = HEADS_PER_GROUP
    row = lax.broadcasted_iota(jnp.int32, (nh * QBLK, QBLK), 0) & (QBLK - 1)
    col = lax.broadcasted_iota(jnp.int32, (nh * QBLK, QBLK), 1)
    strict = col < row
    ur = lax.broadcasted_iota(jnp.int32, (2 * QBLK, QBLK), 0) & (QBLK - 1)
    uc = lax.broadcasted_iota(jnp.int32, (2 * QBLK, QBLK), 1)
    u = jnp.where(ur > uc, 1.0, 0.0).astype(_BF16)
    lane = lax.broadcasted_iota(jnp.int32, (1, GROUP_WIDTH), 1)
    head_masks = [(lane >= h * HEAD_DIM) & (lane < (h + 1) * HEAD_DIM) for h in range(nh)]
    qs_ref[...] = jnp.concatenate([jnp.where(hm, q, jnp.zeros_like(q)) for hm in head_masks], axis=0)

    def softplus(z):
        return jnp.maximum(z, 0.0) + jnp.log(1.0 + jnp.exp(-jnp.abs(z)))

    def later_keys(log_1m):
        hi, lo = _split(log_1m)
        return _dot(jnp.concatenate([hi, lo], axis=1), u)

    nb = SB_HEAD_BLOCKS
    kbs = [i - (nb - 1) + j for j in range(nb)]
    starts = [pl.multiple_of(jnp.maximum(kb, 0) * QBLK, QBLK) for kb in kbs]
    keep = [strict if j == nb - 1 else (kbs[j] >= 0) for j in range(nb)]
    z = _dot_nt(qs_ref[...], jnp.concatenate([k_ref[pl.ds(s, QBLK), :] for s in starts], axis=0))
    sp = softplus(z)
    log_1m = [jnp.where(keep[j], -sp[:, j * QBLK:(j + 1) * QBLK], 0.0) for j in range(nb)]
    totals = [jnp.sum(l, axis=1, keepdims=True) for l in log_1m]
    a_blocks = []
    after = jnp.zeros_like(totals[0])
    for j in reversed(range(nb)):
        cols = slice(j * QBLK, (j + 1) * QBLK)
        a = jnp.exp((z[:, cols] - sp[:, cols]) + later_keys(log_1m[j]) + after)
        a_blocks.insert(0, jnp.where(keep[j], a, 0.0).astype(_BF16))
        after = after + totals[j]
    acc_ref[...] = _dot(jnp.concatenate(a_blocks, axis=1),
                        jnp.concatenate([v_ref[pl.ds(s, QBLK), :] for s in starts], axis=0))
    carry_ref[...] = after

    def tile(kb):
        start = pl.multiple_of(kb * QBLK, QBLK)
        z = _dot_nt(qs_ref[...], k_ref[pl.ds(start, QBLK), :])
        sp = softplus(z)
        log_1m = -sp
        a = jnp.exp((z - sp) + later_keys(log_1m) + carry_ref[...])
        acc_ref[...] += _dot(a.astype(_BF16), v_ref[pl.ds(start, QBLK), :])
        carry = carry_ref[...] + jnp.sum(log_1m, axis=1, keepdims=True)
        carry_ref[...] = carry
        return jnp.max(carry)

    def cond(st):
        return (st[0] >= 0) & (st[1] > SB_DEAD_LOG)

    def step(st):
        return st[0] - 1, tile(st[0])

    lax.while_loop(cond, step, (i - nb, jnp.max(after)))
    out = jnp.zeros((QBLK, GROUP_WIDTH), _F32)
    for h, hm in enumerate(head_masks):
        out = jnp.where(hm, acc_ref[h * QBLK:(h + 1) * QBLK, :], out)
    o_ref[...] = out.astype(_BF16)


def _stick_breaking(arr, batch, seq):
    gw = GROUP_WIDTH
    view = arr.reshape(batch, seq, 3 * gw)
    o = pl.pallas_call(
        _sb_body,
        out_shape=jax.ShapeDtypeStruct((batch, seq, gw), _BF16),
        grid=(batch, seq // QBLK),
        in_specs=[pl.BlockSpec((None, QBLK, gw), lambda b, i: (b, i, 0)),
                  pl.BlockSpec((None, seq, gw), lambda b, i: (b, 0, 1)),
                  pl.BlockSpec((None, seq, gw), lambda b, i: (b, 0, 2))],
        out_specs=pl.BlockSpec((None, QBLK, gw), lambda b, i: (b, i, 0)),
        scratch_shapes=[pltpu.VMEM((HEADS_PER_GROUP * QBLK, 1), _F32),
                        pltpu.VMEM((HEADS_PER_GROUP * QBLK, gw), _F32),
                        pltpu.VMEM((HEADS_PER_GROUP * QBLK, gw), _BF16)],
        compiler_params=_params("arbitrary", "arbitrary"),
        name="sb",
    )(view, view, view)
    return o.reshape(batch * seq, gw)


def _merge_body(x_ref, mod_ref, g1_ref, g2_ref, o0_ref, o1_ref, o2_ref, l0_ref, l1_ref, l2_ref, osb_ref,
                wbg_ref, wbd_ref, wbs_ref, wout_ref, wrh_ref, wrl_ref, br_ref, ex_ref, tri_ref,
                x1_ref, h2p_ref, rinfo_ref, er_ref, cnt_out_ref, cnt_ref, os1_ref, os2_ref, ls1_ref, ls2_ref):
    d = D_MODEL
    tm = x_ref.shape[0]
    x = x_ref[...]
    hb = _rms_mod(x, g1_ref[...], mod_ref[:, d:2 * d], mod_ref[:, 0:d]).astype(_BF16)
    gates = jax.nn.sigmoid(_dot(hb, wbg_ref[...]))

    def natural(ref, st_ref, dil):
        if dil == 1:
            return ref[...].astype(_F32)
        slabs = st_ref.shape[0]
        for r in range(dil):
            for s in range(slabs):
                col0 = (r * slabs + s) * LANES
                st_ref[s, pl.ds(r, tm // dil, stride=dil), :] = ref[:, col0:col0 + LANES].astype(_F32)
        return jnp.concatenate([st_ref[s] for s in range(slabs)], axis=1)

    dils = [dl for _, dl in DIL_PATTERNS]
    o_nat = [natural(r, s, dl) for r, s, dl in zip((o0_ref, o1_ref, o2_ref), (None, os1_ref, os2_ref), dils)]
    l0, l1, l2 = [natural(r, s, dl) for r, s, dl in zip((l0_ref, l1_ref, l2_ref), (None, ls1_ref, ls2_ref), dils)]

    lmax = jnp.maximum(jnp.maximum(l0, l1), l2)
    e0, e1, e2 = jnp.exp(l0 - lmax), jnp.exp(l1 - lmax), jnp.exp(l2 - lmax)
    inv = 1.0 / (e0 + e1 + e2)
    ex = ex_ref[...]

    def widen(w):
        hi, lo = _split(w)
        return _dot(hi, ex) + _dot(lo, ex)

    o_dil = widen(e0 * inv) * o_nat[0] + widen(e1 * inv) * o_nat[1] + widen(e2 * inv) * o_nat[2]
    merged = (gates[:, :d] * _dot(o_dil.astype(_BF16), wbd_ref[...])
              + gates[:, d:] * _dot(osb_ref[...], wbs_ref[...]))
    x1 = x + mod_ref[:, 2 * d:3 * d] * _dot(merged.astype(_BF16), wout_ref[...])
    x1_ref[...] = x1

    h2 = _rms_mod(x1, g2_ref[...], mod_ref[:, 4 * d:5 * d], mod_ref[:, 3 * d:4 * d])
    h2p_ref[...] = _pack_halves(h2)

    hh, hl = _split(h2)
    logits = _dot(hh, wrh_ref[...]) + (_dot(hl, wrh_ref[...]) + _dot(hh, wrl_ref[...])) + br_ref[...]
    lane =lax.broadcasted_iota(jnp.int32, (tm, ROUTER_LANES), 1).astype(_F32)
    far = float(ROUTER_LANES)

    def top(vals):
        m = jnp.max(vals, axis=1, keepdims=True)
        return m, jnp.min(jnp.where(vals == m, lane, far), axis=1, keepdims=True)

    is_group = lane < N_GROUPS
    mg, gsel = top(jnp.where(is_group, logits, NEG_BIG))
    pg_top = 1.0 / jnp.sum(jnp.where(is_group, jnp.exp(logits - mg), 0.0), axis=1, keepdims=True)
    lane0 = EXPERT_LANE0 + EXPERTS_PER_GROUP * gsel
    le = jnp.where((lane >= lane0) & (lane < lane0 + EXPERTS_PER_GROUP), logits, NEG_BIG)
    m1, i1 = top(le)
    m2, i2 = top(jnp.where(lane == i1, NEG_BIG, le))
    t2 = jnp.exp(m2 - m1)
    w0 = pg_top / (1.0 + t2)
    w1 = pg_top * t2 / (1.0 + t2)

    @pl.when(pl.program_id(0) == 0)
    def _():
        cnt_ref[...] = jnp.zeros_like(cnt_ref)

    sel0, sel1 = lane == i1, lane == i2
    onehot = jnp.where(sel0 | sel1, 1.0, 0.0)
    before = _dot(tri_ref[...], onehot.astype(_BF16)) + cnt_ref[0:1, :]
    r0 = jnp.sum(jnp.where(sel0, before, 0.0), axis=1, keepdims=True)
    r1 = jnp.sum(jnp.where(sel1, before, 0.0), axis=1, keepdims=True)
    cnt_ref[...] += jnp.sum(onehot, axis=0, keepdims=True)
    cnt_out_ref[...] = cnt_ref[...]

    cols = (i1 - EXPERT_LANE0, i2 - EXPERT_LANE0, r0, r1, w0, w1)
    rinfo = jnp.zeros((tm, ROUTER_LANES), _F32)
    for c, v in enumerate(cols):
        rinfo = jnp.where(lane == float(c), v, rinfo)
    rinfo_ref[...] = rinfo
    er_ref[...] = jnp.transpose(rinfo)[0:8, :].astype(jnp.int32)


def _merge(x2, mod3, g1, g2, outs, lses, osb, wbg, wbd, wbs, wout, wrh, wrl, br, ex, tri, seq, part):
    d = x2.shape[1]
    tm = ROW_TILE
    per_b = seq // tm
    t = x2.shape[0] // MOE_PARTS
    first = part * (t // tm)
    src = lambda i: (i + first, 0)
    row = lambda i: (i, 0)
    const = lambda i: (0, 0)
    full = lambda a: pl.BlockSpec(a.shape, const)
    gw = GROUP_WIDTH
    dils = [dl for _, dl in DIL_PATTERNS]
    return pl.pallas_call(
        _merge_body,
        out_shape=[jax.ShapeDtypeStruct((t, d), _F32),
                   jax.ShapeDtypeStruct((t, d // 2), jnp.uint32),
                   jax.ShapeDtypeStruct((t, ROUTER_LANES), _F32),
                   jax.ShapeDtypeStruct((8, t), jnp.int32),
                   jax.ShapeDtypeStruct((8, ROUTER_LANES), _F32)],
        grid=(t // tm,),
        in_specs=[pl.BlockSpec((tm, d), src),
                  pl.BlockSpec((None, 1, mod3.shape[2]), lambda i: ((i + first) // per_b, 0, 0)),
                  full(g1), full(g2)]
                 + [pl.BlockSpec((tm // dl, dl * gw), src) for dl in dils]
                 + [pl.BlockSpec((tm // dl, dl * LANES), src) for dl in dils]
                 + [pl.BlockSpec((tm, gw), src)]
                 + [full(a) for a in (wbg, wbd, wbs, wout, wrh, wrl, br, ex, tri)],
        out_specs=[pl.BlockSpec((tm, d), row),
                   pl.BlockSpec((tm, d // 2), row),
                   pl.BlockSpec((tm, ROUTER_LANES), row),
                   pl.BlockSpec((8, tm), lambda i: (0, i)),
                   pl.BlockSpec((8, ROUTER_LANES), const)],
        scratch_shapes=[pltpu.VMEM((8, ROUTER_LANES), _F32),
                        pltpu.VMEM((gw // LANES, tm, LANES), _F32), pltpu.VMEM((gw // LANES, tm, LANES), _F32),
                        pltpu.VMEM((1, tm, LANES), _F32), pltpu.VMEM((1, tm, LANES), _F32)],
        compiler_params=_params("arbitrary"),
        name="merge",
    )(x2, mod3, g1, g2, *outs, *lses, osb, wbg, wbd, wbs, wout, wrh, wrl, br, ex, tri)


def _dest_body(ps_ref, er_ref, d_ref):
    e = er_ref[0:2, :]
    start = jnp.zeros_like(e)
    for x in range(N_EXPERTS):
        start = jnp.where(e == x, ps_ref[x], start)
    d_ref[...] = start + er_ref[2:4, :]


def _dest(pstart, er):
    t = er.shape[1]
    tw = min(DEST_TILE, t)
    return pl.pallas_call(
        _dest_body,
        out_shape=jax.ShapeDtypeStruct((2, t), jnp.int32),
        grid_spec=pltpu.PrefetchScalarGridSpec(
            num_scalar_prefetch=1,
            grid=(t // tw,),
            in_specs=[pl.BlockSpec((8, tw), lambda i, ps: (0, i))],
            out_specs=pl.BlockSpec((2, tw), lambda i, ps: (0, i))),
        compiler_params=_params("arbitrary"),
        name="dest",
    )(pstart, er)


def _sc_mesh():
    return plsc.VectorSubcoreMesh(core_axis_name="core", subcore_axis_name="subcore",
                                  num_cores=SC_CORES, num_subcores=SC_SUBCORES)


def _sc_worker():
    return lax.axis_index("subcore") * SC_CORES + lax.axis_index("core")


def _sc_scatter(x, idx0, idx1, n_slots):
    chunks = idx0.shape[0]
    per = chunks // SC_WORKERS
    win = idx0.shape[1]

    @functools.partial(
        pl.kernel, mesh=_sc_mesh(), out_type=jax.ShapeDtypeStruct((n_slots, x.shape[1]), x.dtype),
        scratch_types=[pltpu.VMEM((1, win), jnp.int32), pltpu.VMEM((1, win), jnp.int32),
                       pltpu.VMEM((win, x.shape[1]), x.dtype), pltpu.SemaphoreType.DMA],
        name="sc_scatter")
    def run(x_hbm, i0_hbm, i1_hbm, o_hbm, i0_v, i1_v, rows_v, sem):
        wid = _sc_worker()

        @pl.loop(0, per)
        def _(j):
            c = wid * per + j
            pltpu.sync_copy(i0_hbm.at[pl.ds(c, 1)], i0_v)
            pltpu.sync_copy(i1_hbm.at[pl.ds(c, 1)], i1_v)
            pltpu.sync_copy(x_hbm.at[pl.ds(c * win, win)], rows_v)
            first = pltpu.async_copy(rows_v, o_hbm.at[i0_v.at[0]], sem)
            second = pltpu.async_copy(rows_v, o_hbm.at[i1_v.at[0]], sem)
            first.wait()
            second.wait()

    return run(x, idx0, idx1)


def _sc_gather(table, idx):
    chunks, win = idx.shape
    per = chunks // SC_WORKERS

    @functools.partial(
        pl.kernel, mesh=_sc_mesh(), out_type=jax.ShapeDtypeStruct((chunks * win, table.shape[1]), table.dtype),
        scratch_types=[pltpu.VMEM((1, win), jnp.int32), pltpu.VMEM((win, table.shape[1]), table.dtype),
                       pltpu.SemaphoreType.DMA],
        name="sc_gather")
    def run(t_hbm, i_hbm, o_hbm, i_v, rows_v, sem):
        wid = _sc_worker()

        @pl.loop(0, per)
        def _(j):
            c = wid * per + j
            pltpu.sync_copy(i_hbm.at[pl.ds(c, 1)], i_v)
            pltpu.async_copy(t_hbm.at[i_v.at[0]], rows_v, sem).wait()
            pltpu.sync_copy(rows_v, o_hbm.at[pl.ds(c * win, win)])

    return run(table, idx)


def _pack_halves(a):
    h = a.shape[1] // 2
    lo = lax.bitcast_convert_type(a[:, :h].astype(_BF16).astype(_F32), jnp.uint32) >> 16
    hi = lax.bitcast_convert_type(a[:, h:].astype(_BF16).astype(_F32), jnp.uint32) & jnp.uint32(0xFFFF0000)
    return lo | hi


def _unpack_halves(w):
    return jnp.concatenate(
        [lax.bitcast_convert_type(w << 16, _F32), lax.bitcast_convert_type(w & jnp.uint32(0xFFFF0000), _F32)], axis=1)


def _experts_body(ce_ref, nv_ref, nu_ref, xs_ref, wg_ref, wu_ref, wd_ref, ys_ref, wgb_ref, wub_ref, wdb_ref):
    c = pl.program_id(0)

    @pl.when(c < nu_ref[0])
    def _():
        @pl.when((c == 0) | (ce_ref[c] != ce_ref[jnp.maximum(c - 1, 0)]))
        def _():
            wgb_ref[...] = wg_ref[...].astype(_BF16)
            wub_ref[...] = wu_ref[...].astype(_BF16)
            wdb_ref[...] = wd_ref[...].astype(_BF16)

        row = lax.broadcasted_iota(jnp.int32, xs_ref.shape, 0)
        x = _unpack_halves(jnp.where(row < nv_ref[c], xs_ref[...], jnp.uint32(0))).astype(_BF16)
        g = _dot(x, wgb_ref[...])
        hmid = (g * jax.nn.sigmoid(g)) * _dot(x, wub_ref[...])
        ys_ref[...] = _pack_halves(_dot(hmid.astype(_BF16), wdb_ref[...]))


def _experts(chunk_e, n_valid, n_used, xs, wg, wu, wd):
    n_slots, w = xs.shape
    ch = EXPERT_CHUNK
    d, de = wg.shape[1], wg.shape[2]
    slot = lambda c, ce, nv, nu: (jnp.minimum(c, nu[0] - 1), 0)
    weight = lambda c, ce, nv, nu: (ce[c], 0, 0)
    return pl.pallas_call(
        _experts_body,
        out_shape=jax.ShapeDtypeStruct((n_slots, d // 2), jnp.uint32),
        grid_spec=pltpu.PrefetchScalarGridSpec(
            num_scalar_prefetch=3,
            grid=(n_slots // ch,),
            in_specs=[pl.BlockSpec((ch, w), slot),
                      pl.BlockSpec((None, d, de), weight),
                      pl.BlockSpec((None, d, de), weight),
                      pl.BlockSpec((None, de, d), weight)],
            out_specs=pl.BlockSpec((ch, d // 2), slot),
            scratch_shapes=[pltpu.VMEM((d, de), _BF16), pltpu.VMEM((d, de), _BF16), pltpu.VMEM((de, d), _BF16)]),
        compiler_params=_params("arbitrary"),
        name="experts",
    )(chunk_e, n_valid, n_used, xs, wg, wu, wd)


def _combine_body(x1_ref, rinfo_ref, mod_ref, y0_ref, y1_ref, *rest):
    o_ref = rest[-1]
    y = rinfo_ref[:, 4:5] * _unpack_halves(y0_ref[...]) + rinfo_ref[:, 5:6] * _unpack_halves(y1_ref[...])
    o_ref[...] = x1_ref[...] + mod_ref[:, 5 * D_MODEL:6 * D_MODEL] * y


def _combine(x1, rinfo, mod3, gathered, seq, part, out_so_far):
    t, d = x1.shape
    tf = min(COMBINE_TILE, seq)
    per_b = seq // tf
    nt = t // tf
    first = part * nt
    in_specs = [pl.BlockSpec((tf, d), lambda i: (i, 0)),
                pl.BlockSpec((tf, ROUTER_LANES), lambda i: (i, 0)),
                pl.BlockSpec((None, 1, mod3.shape[2]), lambda i: ((i + first) // per_b, 0, 0)),
                pl.BlockSpec((tf, d // 2), lambda i: (i, 0)),
                pl.BlockSpec((tf, d // 2), lambda i: (i + nt, 0))]
    args = [x1, rinfo, mod3, gathered, gathered]
    aliases = {}
    if out_so_far is not None:
        in_specs.append(pl.BlockSpec(memory_space=pl.ANY))
        args.append(out_so_far)
        aliases = {len(args) - 1: 0}
    return pl.pallas_call(
        _combine_body,
        out_shape=jax.ShapeDtypeStruct((t * MOE_PARTS, d), _F32),
        grid=(nt,),
        in_specs=in_specs,
        out_specs=pl.BlockSpec((tf, d), lambda i: (i + first, 0)),
        input_output_aliases=aliases,
        compiler_params=_params("arbitrary"),
        name="combine",
    )(*args)


def _rope_tables(positions):
    half = ROPE_DIM // 2
    inv_freq = ROPE_THETA ** (-jnp.arange(0, ROPE_DIM, 2, dtype=_F32) / ROPE_DIM)
    ang = positions.reshape(-1).astype(_F32)[:, None] * inv_freq
    cos, sin = jnp.cos(ang), jnp.sin(ang)
    lane = jnp.arange(LANES) % HEAD_DIM
    pick = (lane[None, :] % half == jnp.arange(half)[:, None]).astype(_F32)
    low, high = (lane < half).astype(_F32), ((lane >= half) & (lane < ROPE_DIM)).astype(_F32)
    spread = lambda a, m: jnp.dot(a, pick * m, precision=lax.Precision.HIGHEST)
    c = spread(cos, low + high) + (lane >= ROPE_DIM).astype(_F32)
    return c, -spread(sin, low), spread(sin, high)


def _layer(x, mod, positions, g_mix, g_ffn, w_in, w_bg, qg, kg, w_bd, w_bs, w_out, w_rg, b_rg, w_re, b_re,
           w_eg, w_eu, w_ed):
    batch, seq, d = x.shape
    t = batch * seq
    x2 = x.reshape(t, d)
    mod3 = mod.reshape(batch, 1, mod.shape[1])
    gw = GROUP_WIDTH

    lane = jnp.arange(gw)
    bd = jnp.where(lane[:, None] // HEAD_DIM == lane[None, :] // HEAD_DIM, 1.0 / HEAD_DIM, 0.0).astype(_BF16)
    ex = (jnp.arange(LANES)[:, None] == (lane[None, :] // HEAD_DIM) * LSE_SEG).astype(_BF16)
    tri = (jnp.arange(ROW_TILE)[:, None] > jnp.arange(ROW_TILE)[None, :]).astype(_BF16)
    rope = _rope_tables(positions)
    tile4 = lambda g: jnp.tile(g.astype(_F32), HEADS_PER_GROUP).reshape(1, gw)
    wr = jnp.zeros((d, ROUTER_LANES), _F32).at[:, :N_GROUPS].set(w_rg).at[:, N_GROUPS:N_GROUPS + N_EXPERTS].set(w_re)
    wrh = wr.astype(_BF16)
    wrl = (wr - wrh.astype(_F32)).astype(_BF16)
    br = jnp.zeros((1, ROUTER_LANES), _F32).at[0, :N_GROUPS].set(b_rg).at[0, N_GROUPS:N_GROUPS + N_EXPERTS].set(b_re)

    d0, d1, d2, sbp = _qkv(x2, mod3, g_mix.reshape(1, d), w_in.astype(_BF16), tile4(qg) * ATTN_SCALE, tile4(kg),
                           *rope, bd, seq)
    dil = [_dilated_group(a, dl, batch, seq) for a, (_, dl) in zip((d0, d1, d2), DIL_PATTERNS)]
    osb = _stick_breaking(sbp, batch, seq)

    merge_weights = (w_bg.astype(_BF16), w_bd.astype(_BF16), w_bs.astype(_BF16), w_out.astype(_BF16))
    tp = t // MOE_PARTS
    ch = EXPERT_CHUNK
    win = SC_INDEX_WINDOW
    n_chunks = -(-2 * tp // ch) + N_EXPERTS
    chunk_start = jnp.arange(n_chunks, dtype=jnp.int32) * ch
    out = None
    for part in range(MOE_PARTS):
        x1, h2p, rinfo, er, cnt = _merge(
            x2, mod3, g_mix.reshape(1, d), g_ffn.reshape(1, d), [o for o, _ in dil], [l for _, l in dil], osb,
            *merge_weights, wrh, wrl, br, ex, tri, seq, part)

        counts = cnt[0, EXPERT_LANE0:EXPERT_LANE0 + N_EXPERTS].astype(jnp.int32)
        padded = (counts + ch - 1) // ch * ch
        pend = jnp.cumsum(padded)
        pstart = pend - padded
        chunk_e = jnp.minimum(jnp.sum((pend[None, :] <= chunk_start[:, None]).astype(jnp.int32), axis=1),
                              N_EXPERTS - 1)
        n_used = (pend[-1:] // ch).astype(jnp.int32)
        n_valid = jnp.clip(counts[chunk_e] - (chunk_start - pstart[chunk_e]), 0, ch).astype(jnp.int32)

        dest = _dest(pstart, er)
        xs = _sc_scatter(h2p, dest[0].reshape(tp // win, win), dest[1].reshape(tp // win, win), n_chunks * ch)
        ys = _experts(chunk_e, n_valid, n_used, xs, w_eg, w_eu, w_ed)
        gathered = _sc_gather(ys, dest.reshape(2 * tp // win, win))
        out = _combine(x1, rinfo, mod3, gathered, seq, part, out)
    return out.reshape(batch, seq, d)


def kernel(x, c, positions, w_ada, b_ada, g_norm_mix, g_norm_ffn, w_in, w_branch_gate, q_norm_g, k_norm_g,
           w_branch_dil, w_branch_sb, w_out, w_router_group, b_router_group, w_router_expert, b_router_expert,
           w_expert_gate, w_expert_up, w_expert_down):
    for l in range(w_ada.shape[0]):
        mod = _ada(c, w_ada[l], b_ada[l])
        x = _layer(x, mod, positions, g_norm_mix[l], g_norm_ffn[l], w_in[l], w_branch_gate[l], q_norm_g[l],
                   k_norm_g[l], w_branch_dil[l], w_branch_sb[l], w_out[l], w_router_group[l], b_router_group[l],
                   w_router_expert[l], b_router_expert[l], w_expert_gate[l], w_expert_up[l], w_expert_down[l])
    return x
```

```python
import functools

import jax
import jax.numpy as jnp
from jax import lax
from jax.experimental import pallas as pl
from jax.experimental.pallas import tpu as pltpu
from jax.experimental.pallas import tpu_sc as plsc

D_MODEL = 1024
HEAD_DIM = 64
DIL_PATTERNS = ((128, 1), (512, 4), (2048, 16))
HEADS_PER_GROUP = 4
GROUP_WIDTH = HEADS_PER_GROUP * HEAD_DIM
N_DIL_GROUPS = len(DIL_PATTERNS)
DIL_WIDTH = N_DIL_GROUPS * GROUP_WIDTH
QKV_WIDTH = 3 * DIL_WIDTH + 3 * GROUP_WIDTH
WINDOW_KEYS = 128
ROPE_THETA = 500000.0
ROPE_DIM = HEAD_DIM // 4
N_GROUPS = 4
EXPERTS_PER_GROUP = 8
N_EXPERTS = N_GROUPS * EXPERTS_PER_GROUP
D_EXPERT = 512
RMS_EPS = 1e-6
ATTN_SCALE = HEAD_DIM ** -0.5

LANES = 128
ROUTER_LANES = LANES
EXPERT_LANE0 = N_GROUPS
LSE_SEG = LANES // HEADS_PER_GROUP
NEG_BIG = -1e30
SB_DEAD_LOG = -120.0
SB_HEAD_BLOCKS = 3

ROW_TILE = 512
QBLK = 128
DIL_QTILE = 512
EXPERT_CHUNK = 512
COMBINE_TILE = 512
DEST_TILE = 8192
SC_CORES = 2
SC_SUBCORES = 16
SC_WORKERS = SC_CORES * SC_SUBCORES
SC_INDEX_WINDOW = 128
MOE_PARTS = 2
VMEM_LIMIT = 48 * 1024 * 1024

_BF16 = jnp.bfloat16
_F32 = jnp.float32
_NT = (((1,), (1,)), ((), ()))


def _dot(a, b):
    return jnp.dot(a, b, preferred_element_type=_F32)


def _dot_nt(a, b):
    return lax.dot_general(a, b, _NT, preferred_element_type=_F32)


def _split(a):
    hi = a.astype(_BF16)
    lo = (a - hi.astype(_F32)).astype(_BF16)
    return hi, lo


def _dot3(a, b):
    ah, al = _split(a)
    bh, bl = _split(b)
    return _dot(ah, bh) + (_dot(ah, bl) + _dot(al, bh))


def _rms_mod(x, g, scale, shift):
    y = x * lax.rsqrt(jnp.mean(x * x, axis=-1, keepdims=True) + RMS_EPS)
    return y * g * (1.0 + scale) + shift


def _params(*sem):
    return pltpu.CompilerParams(dimension_semantics=sem, vmem_limit_bytes=VMEM_LIMIT)


def _ada_body(c_ref, w_ref, b_ref, o_ref):
    c = c_ref[...]
    o_ref[...] = _dot3(c * jax.nn.sigmoid(c), w_ref[...]) + b_ref[...]


def _ada(c, w_ada, b_ada):
    b, d = c.shape
    n = w_ada.shape[1]
    rows = -(-b // 16) * 16
    cp = jnp.zeros((rows, d), _F32).at[:b].set(c)
    nt = 1536
    out = pl.pallas_call(
        _ada_body,
        out_shape=jax.ShapeDtypeStruct((rows, n), _F32),
        grid=(n // nt,),
        in_specs=[pl.BlockSpec((rows, d), lambda j: (0, 0)),
                  pl.BlockSpec((d, nt), lambda j: (0, j)),
                  pl.BlockSpec((1, nt), lambda j: (0, j))],
        out_specs=pl.BlockSpec((rows, nt), lambda j: (0, j)),
        compiler_params=_params("arbitrary"),
        name="ada",
    )(cp, w_ada, b_ada.reshape(1, n))
    return out[:b]


def _qkv_body(x_ref, mod_ref, g_ref, w_ref, qg_ref, kg_ref, c_ref, s1_ref, s2_ref, bd_ref,
              o0_ref, o1_ref, o2_ref, osb_ref, st_ref, acc_ref):
    d = D_MODEL
    tm = x_ref.shape[0]
    h = _rms_mod(x_ref[...], g_ref[...], mod_ref[:, d:2 * d], mod_ref[:, 0:d])
    hb = h.astype(_BF16)
    cc = jnp.concatenate([c_ref[...]] * 2, axis=1)
    s1 = jnp.concatenate([s1_ref[...]] * 2, axis=1)
    s2 = jnp.concatenate([s2_ref[...]] * 2, axis=1)
    bd = bd_ref[...]
    gw = GROUP_WIDTH

    def normed_rotated(acc, gain):
        ms = _dot((acc * acc).astype(_BF16), bd)
        y = acc * lax.rsqrt(ms + RMS_EPS) * gain
        return y * cc + pltpu.roll(y, gw - ROPE_DIM // 2, 1) * s1 + pltpu.roll(y, ROPE_DIM // 2, 1) * s2

    def store(o_ref, dil, part, y):
        if dil == 1:
            o_ref[:, part * gw:(part + 1) * gw] = y.astype(_BF16)
            return
        for s in range(gw // LANES):
            st_ref[s] = y[:, s * LANES:(s + 1) * LANES]
        for r in range(dil):
            for s in range(gw // LANES):
                col0 = (3 * r + part) * gw + s * LANES
                o_ref[:, col0:col0 + LANES] = st_ref[s, pl.ds(r, tm // dil, stride=dil), :].astype(_BF16)

    def project(col0):
        return _dot(hb, w_ref[:, col0:col0 + gw])

    outs = (o0_ref, o1_ref, o2_ref)
    dils = [dl for _, dl in DIL_PATTERNS]
    normed = [(g, part, part * DIL_WIDTH + g * gw) for g in range(N_DIL_GROUPS) for part in (0, 1)]
    for n, (_, _, col0) in enumerate(normed):
        acc_ref[n] = project(col0)
    plain = [("v", g) for g in range(N_DIL_GROUPS)] + [("sb", part) for part in range(3)]
    for n, (kind, j) in enumerate(plain):
        if kind == "v":
            store(outs[j], dils[j], 2, project(2 * DIL_WIDTH + j * gw))
        else:
            acc = project(3 * DIL_WIDTH + j * gw)
            osb_ref[:, j * gw:(j + 1) * gw] = (acc * ATTN_SCALE if j == 0 else acc).astype(_BF16)
        g, part, _ = normed[n]
        gain = qg_ref[...] if part == 0 else kg_ref[...]
        store(outs[g], dils[g], part, normed_rotated(acc_ref[n], gain))


def _qkv(x2, mod3, g_mix, w_in, qg, kg, rope_c, rope_s1, rope_s2, bd, seq):
    t, d = x2.shape
    tm = ROW_TILE
    per_b = seq // tm
    row = lambda i: (i, 0)
    const = lambda i: (0, 0)
    width = 3 * GROUP_WIDTH
    dils = [dl for _, dl in DIL_PATTERNS] + [1]
    return pl.pallas_call(
        _qkv_body,
        out_shape=[jax.ShapeDtypeStruct((t // dl, dl * width), _BF16) for dl in dils],
        grid=(t // tm,),
        in_specs=[pl.BlockSpec((tm, d), row),
                  pl.BlockSpec((None, 1, mod3.shape[2]), lambda i: (i // per_b, 0, 0)),
                  pl.BlockSpec((1, d), const),
                  pl.BlockSpec(w_in.shape, const),
                  pl.BlockSpec((1, GROUP_WIDTH), const),
                  pl.BlockSpec((1, GROUP_WIDTH), const),
                  pl.BlockSpec((tm, LANES), row),
                  pl.BlockSpec((tm, LANES), row),
                  pl.BlockSpec((tm, LANES), row),
                  pl.BlockSpec(bd.shape, const)],
        out_specs=[pl.BlockSpec((tm // dl, dl * width), row) for dl in dils],
        scratch_shapes=[pltpu.VMEM((GROUP_WIDTH // LANES, tm, LANES), _F32),
                        pltpu.VMEM((2 * N_DIL_GROUPS, tm, GROUP_WIDTH), _F32)],
        compiler_params=_params("arbitrary"),
        name="qkv",
    )(x2, mod3, g_mix, w_in, qg, kg, rope_c, rope_s1, rope_s2, bd)


def _dil_body(q_ref, kp_ref, kc_ref, vp_ref, vc_ref, o_ref, lse_ref, kf_ref, vf_ref):
    tq = q_ref.shape[0]
    first = pl.program_id(2) == 0
    kf_ref[0:QBLK, :] = kp_ref[...]
    kf_ref[QBLK:, :] = kc_ref[...]
    vf_ref[0:QBLK, :] = vp_ref[...]
    vf_ref[QBLK:, :] = vc_ref[...]
    nh = HEADS_PER_GROUP
    row = lax.broadcasted_iota(jnp.int32, (nh * QBLK, 2 * QBLK), 0) & (QBLK - 1)
    col = lax.broadcasted_iota(jnp.int32, (nh * QBLK, 2 * QBLK), 1)
    band = (col >= row) & (col <= row + WINDOW_KEYS)
    lane = lax.broadcasted_iota(jnp.int32, (1, GROUP_WIDTH), 1)
    slane = lax.broadcasted_iota(jnp.int32, (1, LANES), 1)
    head_masks = [(lane >= h * HEAD_DIM) & (lane < (h + 1) * HEAD_DIM) for h in range(nh)]
    for j in range(tq // QBLK):
        qj = q_ref[j * QBLK:(j + 1) * QBLK, :]
        qs = jnp.concatenate([jnp.where(hm, qj, jnp.zeros_like(qj)) for hm in head_masks], axis=0)
        kcat = kf_ref[j * QBLK:(j + 2) * QBLK, :]
        vcat = vf_ref[j * QBLK:(j + 2) * QBLK, :]
        valid = band & ((col >= QBLK) | jnp.logical_not(first)) if j == 0 else band
        s = jnp.where(valid, _dot_nt(qs, kcat), NEG_BIG)
        m = jnp.max(s, axis=1, keepdims=True)
        p = jnp.exp(s - m)
        l = jnp.sum(p, axis=1, keepdims=True)
        o_all = _dot(p.astype(_BF16), vcat) / l
        lse_all = m + jnp.log(l)
        o_acc = jnp.zeros((QBLK, GROUP_WIDTH), _F32)
        lse_t = jnp.zeros((QBLK, LANES), _F32)
        for h, hm in enumerate(head_masks):
            o_acc = jnp.where(hm, o_all[h * QBLK:(h + 1) * QBLK, :], o_acc)
            sm = (slane >= h * LSE_SEG) & (slane < (h + 1) * LSE_SEG)
            lse_t = jnp.where(sm, lse_all[h * QBLK:(h + 1) * QBLK, :], lse_t)
        o_ref[j * QBLK:(j + 1) * QBLK, :] = o_acc.astype(_BF16)
        lse_ref[j * QBLK:(j + 1) * QBLK, :] = lse_t


def _dilated_group(view2, dil, batch, seq):
    sd = seq // dil
    tq = min(DIL_QTILE, sd)
    per = tq // QBLK
    gw = GROUP_WIDTH
    view = view2.reshape(batch, sd, dil * 3 * gw)
    cur = lambda part: pl.BlockSpec((None, tq, gw), lambda b, r, i: (b, i, 3 * r + part))
    prev = lambda part: pl.BlockSpec((None, QBLK, gw),
                                     lambda b, r, i: (b, jnp.maximum(i * per - 1, 0), 3 * r + part))
    o, lse = pl.pallas_call(
        _dil_body,
        out_shape=[jax.ShapeDtypeStruct((batch, sd, dil * gw), _BF16),
                   jax.ShapeDtypeStruct((batch, sd, dil * LANES), _F32)],
        grid=(batch, dil, sd // tq),
        in_specs=[cur(0), prev(1), cur(1), prev(2), cur(2)],
        out_specs=[pl.BlockSpec((None, tq, gw), lambda b, r, i: (b, i, r)),
                   pl.BlockSpec((None, tq, LANES), lambda b, r, i: (b, i, r))],
        scratch_shapes=[pltpu.VMEM((tq + QBLK, gw), _BF16), pltpu.VMEM((tq + QBLK, gw), _BF16)],
        compiler_params=_params("arbitrary", "arbitrary", "arbitrary"),
        name=f"dil{dil}",
    )(view, view, view, view, view)
    return o.reshape(batch * sd, dil * gw), lse.reshape(batch * sd, dil * LANES)


def _sb_body(q_ref, k_ref, v_ref, o_ref, carry_ref, acc_ref, qs_ref):
    i = pl.program_id(1)
    q = q_ref[...]
    nh = HEADS_PER_GROUP
    row = lax.broadcasted_iota(jnp.int32, (nh * QBLK, QBLK), 0) & (QBLK - 1)
    col = lax.broadcasted_iota(jnp.int32, (nh * QBLK, QBLK), 1)
    strict = col < row
    ur = lax.broadcasted_iota(jnp.int32, (2 * QBLK, QBLK), 0) & (QBLK - 1)
    uc = lax.broadcasted_iota(jnp.int32, (2 * QBLK, QBLK), 1)
    u = jnp.where(ur > uc, 1.0, 0.0).astype(_BF16)
    lane = lax.broadcasted_iota(jnp.int32, (1, GROUP_WIDTH), 1)
    head_masks = [(lane >= h * HEAD_DIM) & (lane < (h + 1) * HEAD_DIM) for h in range(nh)]
    qs_ref[...] = jnp.concatenate([jnp.where(hm, q, jnp.zeros_like(q)) for hm in head_masks], axis=0)

    def softplus(z):
        return jnp.maximum(z, 0.0) + jnp.log(1.0 + jnp.exp(-jnp.abs(z)))

    def later_keys(log_1m):
        hi, lo = _split(log_1m)
        return _dot(jnp.concatenate([hi, lo], axis=1), u)

    nb = SB_HEAD_BLOCKS
    kbs = [i - (nb - 1) + j for j in range(nb)]
    starts = [pl.multiple_of(jnp.maximum(kb, 0) * QBLK, QBLK) for kb in kbs]
    keep = [strict if j == nb - 1 else (kbs[j] >= 0) for j in range(nb)]
    z = _dot_nt(qs_ref[...], jnp.concatenate([k_ref[pl.ds(s, QBLK), :] for s in starts], axis=0))
    sp = softplus(z)
    log_1m = [jnp.where(keep[j], -sp[:, j * QBLK:(j + 1) * QBLK], 0.0) for j in range(nb)]
    totals = [jnp.sum(l, axis=1, keepdims=True) for l in log_1m]
    a_blocks = []
    after = jnp.zeros_like(totals[0])
    for j in reversed(range(nb)):
        cols = slice(j * QBLK, (j + 1) * QBLK)
        a = jnp.exp((z[:, cols] - sp[:, cols]) + later_keys(log_1m[j]) + after)
        a_blocks.insert(0, jnp.where(keep[j], a, 0.0).astype(_BF16))
        after = after + totals[j]
    acc_ref[...] = _dot(jnp.concatenate(a_blocks, axis=1),
                        jnp.concatenate([v_ref[pl.ds(s, QBLK), :] for s in starts], axis=0))
    carry_ref[...] = after

    def tile(kb):
        start = pl.multiple_of(kb * QBLK, QBLK)
        z = _dot_nt(qs_ref[...], k_ref[pl.ds(start, QBLK), :])
        sp = softplus(z)
        log_1m = -sp
        a = jnp.exp((z - sp) + later_keys(log_1m) + carry_ref[...])
        acc_ref[...] += _dot(a.astype(_BF16), v_ref[pl.ds(start, QBLK), :])
        carry = carry_ref[...] + jnp.sum(log_1m, axis=1, keepdims=True)
        carry_ref[...] = carry
        return jnp.max(carry)

    def cond(st):
        return (st[0] >= 0) & (st[1] > SB_DEAD_LOG)

    def step(st):
        return st[0] - 1, tile(st[0])

    lax.while_loop(cond, step, (i - nb, jnp.max(after)))
    out = jnp.zeros((QBLK, GROUP_WIDTH), _F32)
    for h, hm in enumerate(head_masks):
        out = jnp.where(hm, acc_ref[h * QBLK:(h + 1) * QBLK, :], out)
    o_ref[...] = out.astype(_BF16)


def _stick_breaking(arr, batch, seq):
    gw = GROUP_WIDTH
    view = arr.reshape(batch, seq, 3 * gw)
    o = pl.pallas_call(
        _sb_body,
        out_shape=jax.ShapeDtypeStruct((batch, seq, gw), _BF16),
        grid=(batch, seq // QBLK),
        in_specs=[pl.BlockSpec((None, QBLK, gw), lambda b, i: (b, i, 0)),
                  pl.BlockSpec((None, seq, gw), lambda b, i: (b, 0, 1)),
                  pl.BlockSpec((None, seq, gw), lambda b, i: (b, 0, 2))],
        out_specs=pl.BlockSpec((None, QBLK, gw), lambda b, i: (b, i, 0)),
        scratch_shapes=[pltpu.VMEM((HEADS_PER_GROUP * QBLK, 1), _F32),
                        pltpu.VMEM((HEADS_PER_GROUP * QBLK, gw), _F32),
                        pltpu.VMEM((HEADS_PER_GROUP * QBLK, gw), _BF16)],
        compiler_params=_params("arbitrary", "arbitrary"),
        name="sb",
    )(view, view, view)
    return o.reshape(batch * seq, gw)


def _merge_body(x_ref, mod_ref, g1_ref, g2_ref, o0_ref, o1_ref, o2_ref, l0_ref, l1_ref, l2_ref, osb_ref,
                wbg_ref, wbd_ref, wbs_ref, wout_ref, wrh_ref, wrl_ref, br_ref, ex_ref, tri_ref,
                x1_ref, h2p_ref, rinfo_ref, er_ref, cnt_out_ref, cnt_ref, os1_ref, os2_ref, ls1_ref, ls2_ref):
    d = D_MODEL
    tm = x_ref.shape[0]
    x = x_ref[...]
    hb = _rms_mod(x, g1_ref[...], mod_ref[:, d:2 * d], mod_ref[:, 0:d]).astype(_BF16)

    def natural(ref, st_ref, dil):
        if dil == 1:
            return ref[...].astype(_F32)
        slabs = st_ref.shape[0]
        for r in range(dil):
            for s in range(slabs):
                col0 = (r * slabs + s) * LANES
                st_ref[s, pl.ds(r, tm // dil, stride=dil), :] = ref[:, col0:col0 + LANES].astype(_F32)
        return jnp.concatenate([st_ref[s] for s in range(slabs)], axis=1)

    dils = [dl for _, dl in DIL_PATTERNS]
    o_nat = [natural(r, s, dl) for r, s, dl in zip((o0_ref, o1_ref, o2_ref), (None, os1_ref, os2_ref), dils)]
    l0, l1, l2 = [natural(r, s, dl) for r, s, dl in zip((l0_ref, l1_ref, l2_ref), (None, ls1_ref, ls2_ref), dils)]

    lmax = jnp.maximum(jnp.maximum(l0, l1), l2)
    e0, e1, e2 = jnp.exp(l0 - lmax), jnp.exp(l1 - lmax), jnp.exp(l2 - lmax)
    inv = 1.0 / (e0 + e1 + e2)
    ex = ex_ref[...]

    def widen(w):
        hi, lo = _split(w)
        return _dot(jnp.concatenate([hi, lo], axis=1), ex)

    w_groups = [widen(e * inv) for e in (e0, e1, e2)]
    gate_dil = jax.nn.sigmoid(_dot(hb, wbg_ref[:, :d]))
    o_dil = w_groups[0] * o_nat[0] + w_groups[1] * o_nat[1] + w_groups[2] * o_nat[2]
    branch_dil = _dot(o_dil.astype(_BF16), wbd_ref[...])
    branch_sb = _dot(osb_ref[...], wbs_ref[...])
    gate_sb = jax.nn.sigmoid(_dot(hb, wbg_ref[:, d:]))
    merged = gate_dil * branch_dil + gate_sb * branch_sb
    x1 = x + mod_ref[:, 2 * d:3 * d] * _dot(merged.astype(_BF16), wout_ref[...])
    x1_ref[...] = x1

    h2 = _rms_mod(x1, g2_ref[...], mod_ref[:, 4 * d:5 * d], mod_ref[:, 3 * d:4 * d])
    h2p_ref[...] = _pack_halves(h2)

    hh, hl = _split(h2)
    logits = _dot(hh, wrh_ref[...]) + (_dot(hl, wrh_ref[...]) + _dot(hh, wrl_ref[...])) + br_ref[...]
    lane =lax.broadcasted_iota(jnp.int32, (tm, ROUTER_LANES), 1).astype(_F32)
    far = float(ROUTER_LANES)

    def top(vals):
        m = jnp.max(vals, axis=1, keepdims=True)
        return m, jnp.min(jnp.where(vals == m, lane, far), axis=1, keepdims=True)

    is_group = lane < N_GROUPS
    mg, gsel = top(jnp.where(is_group, logits, NEG_BIG))
    pg_top = 1.0 / jnp.sum(jnp.where(is_group, jnp.exp(logits - mg), 0.0), axis=1, keepdims=True)
    lane0 = EXPERT_LANE0 + EXPERTS_PER_GROUP * gsel
    le = jnp.where((lane >= lane0) & (lane < lane0 + EXPERTS_PER_GROUP), logits, NEG_BIG)
    m1, i1 = top(le)
    m2, i2 = top(jnp.where(lane == i1, NEG_BIG, le))
    t2 = jnp.exp(m2 - m1)
    w0 = pg_top / (1.0 + t2)
    w1 = pg_top * t2 / (1.0 + t2)

    @pl.when(pl.program_id(0) == 0)
    def _():
        cnt_ref[...] = jnp.zeros_like(cnt_ref)

    sel0, sel1 = lane == i1, lane == i2
    onehot = jnp.where(sel0 | sel1, 1.0, 0.0)
    before = _dot(tri_ref[...], onehot.astype(_BF16)) + cnt_ref[0:1, :]
    r0 = jnp.sum(jnp.where(sel0, before, 0.0), axis=1, keepdims=True)
    r1 = jnp.sum(jnp.where(sel1, before, 0.0), axis=1, keepdims=True)
    cnt_ref[...] += jnp.sum(onehot, axis=0, keepdims=True)
    cnt_out_ref[...] = cnt_ref[...]

    cols = (i1 - EXPERT_LANE0, i2 - EXPERT_LANE0, r0, r1, w0, w1)
    rinfo = jnp.zeros((tm, ROUTER_LANES), _F32)
    for c, v in enumerate(cols):
        rinfo = jnp.where(lane == float(c), v, rinfo)
    rinfo_ref[...] = rinfo
    er_ref[...] = jnp.transpose(rinfo)[0:8, :].astype(jnp.int32)


def _merge(x2, mod3, g1, g2, outs, lses, osb, wbg, wbd, wbs, wout, wrh, wrl, br, ex, tri, seq, part):
    d = x2.shape[1]
    tm = ROW_TILE
    per_b = seq // tm
    t = x2.shape[0] // MOE_PARTS
    first = part * (t // tm)
    src = lambda i: (i + first, 0)
    row = lambda i: (i, 0)
    const = lambda i: (0, 0)
    full = lambda a: pl.BlockSpec(a.shape, const)
    gw = GROUP_WIDTH
    dils = [dl for _, dl in DIL_PATTERNS]
    return pl.pallas_call(
        _merge_body,
        out_shape=[jax.ShapeDtypeStruct((t, d), _F32),
                   jax.ShapeDtypeStruct((t, d // 2), jnp.uint32),
                   jax.ShapeDtypeStruct((t, ROUTER_LANES), _F32),
                   jax.ShapeDtypeStruct((8, t), jnp.int32),
                   jax.ShapeDtypeStruct((8, ROUTER_LANES), _F32)],
        grid=(t // tm,),
        in_specs=[pl.BlockSpec((tm, d), src),
                  pl.BlockSpec((None, 1, mod3.shape[2]), lambda i: ((i + first) // per_b, 0, 0)),
                  full(g1), full(g2)]
                 + [pl.BlockSpec((tm // dl, dl * gw), src) for dl in dils]
                 + [pl.BlockSpec((tm // dl, dl * LANES), src) for dl in dils]
                 + [pl.BlockSpec((tm, gw), src)]
                 + [full(a) for a in (wbg, wbd, wbs, wout, wrh, wrl, br, ex, tri)],
        out_specs=[pl.BlockSpec((tm, d), row),
                   pl.BlockSpec((tm, d // 2), row),
                   pl.BlockSpec((tm, ROUTER_LANES), row),
                   pl.BlockSpec((8, tm), lambda i: (0, i)),
                   pl.BlockSpec((8, ROUTER_LANES), const)],
        scratch_shapes=[pltpu.VMEM((8, ROUTER_LANES), _F32),
                        pltpu.VMEM((gw // LANES, tm, LANES), _F32), pltpu.VMEM((gw // LANES, tm, LANES), _F32),
                        pltpu.VMEM((1, tm, LANES), _F32), pltpu.VMEM((1, tm, LANES), _F32)],
        compiler_params=_params("arbitrary"),
        name="merge",
    )(x2, mod3, g1, g2, *outs, *lses, osb, wbg, wbd, wbs, wout, wrh, wrl, br, ex, tri)


def _dest_body(ps_ref, er_ref, d_ref):
    e = er_ref[0:2, :]
    start = jnp.zeros_like(e)
    for x in range(N_EXPERTS):
        start = jnp.where(e == x, ps_ref[x], start)
    d_ref[...] = start + er_ref[2:4, :]


def _dest(pstart, er):
    t = er.shape[1]
    tw = min(DEST_TILE, t)
    return pl.pallas_call(
        _dest_body,
        out_shape=jax.ShapeDtypeStruct((2, t), jnp.int32),
        grid_spec=pltpu.PrefetchScalarGridSpec(
            num_scalar_prefetch=1,
            grid=(t // tw,),
            in_specs=[pl.BlockSpec((8, tw), lambda i, ps: (0, i))],
            out_specs=pl.BlockSpec((2, tw), lambda i, ps: (0, i))),
        compiler_params=_params("arbitrary"),
        name="dest",
    )(pstart, er)


def _sc_mesh():
    return plsc.VectorSubcoreMesh(core_axis_name="core", subcore_axis_name="subcore",
                                  num_cores=SC_CORES, num_subcores=SC_SUBCORES)


def _sc_worker():
    return lax.axis_index("subcore") * SC_CORES + lax.axis_index("core")


def _sc_scatter(x, idx0, idx1, n_slots):
    chunks = idx0.shape[0]
    per = chunks // SC_WORKERS
    win = idx0.shape[1]

    @functools.partial(
        pl.kernel, mesh=_sc_mesh(), out_type=jax.ShapeDtypeStruct((n_slots, x.shape[1]), x.dtype),
        scratch_types=[pltpu.VMEM((1, win), jnp.int32), pltpu.VMEM((1, win), jnp.int32),
                       pltpu.VMEM((win, x.shape[1]), x.dtype), pltpu.SemaphoreType.DMA],
        name="sc_scatter")
    def run(x_hbm, i0_hbm, i1_hbm, o_hbm, i0_v, i1_v, rows_v, sem):
        wid = _sc_worker()

        @pl.loop(0, per)
        def _(j):
            c = wid * per + j
            pltpu.sync_copy(i0_hbm.at[pl.ds(c, 1)], i0_v)
            pltpu.sync_copy(i1_hbm.at[pl.ds(c, 1)], i1_v)
            pltpu.sync_copy(x_hbm.at[pl.ds(c * win, win)], rows_v)
            first = pltpu.async_copy(rows_v, o_hbm.at[i0_v.at[0]], sem)
            second = pltpu.async_copy(rows_v, o_hbm.at[i1_v.at[0]], sem)
            first.wait()
            second.wait()

    return run(x, idx0, idx1)


def _sc_gather(table, idx):
    chunks, win = idx.shape
    per = chunks // SC_WORKERS

    @functools.partial(
        pl.kernel, mesh=_sc_mesh(), out_type=jax.ShapeDtypeStruct((chunks * win, table.shape[1]), table.dtype),
        scratch_types=[pltpu.VMEM((1, win), jnp.int32), pltpu.VMEM((win, table.shape[1]), table.dtype),
                       pltpu.SemaphoreType.DMA],
        name="sc_gather")
    def run(t_hbm, i_hbm, o_hbm, i_v, rows_v, sem):
        wid = _sc_worker()

        @pl.loop(0, per)
        def _(j):
            c = wid * per + j
            pltpu.sync_copy(i_hbm.at[pl.ds(c, 1)], i_v)
            pltpu.async_copy(t_hbm.at[i_v.at[0]], rows_v, sem).wait()
            pltpu.sync_copy(rows_v, o_hbm.at[pl.ds(c * win, win)])

    return run(table, idx)


def _pack_halves(a):
    h = a.shape[1] // 2
    lo = lax.bitcast_convert_type(a[:, :h].astype(_BF16).astype(_F32), jnp.uint32) >> 16
    hi = lax.bitcast_convert_type(a[:, h:].astype(_BF16).astype(_F32), jnp.uint32) & jnp.uint32(0xFFFF0000)
    return lo | hi


def _unpack_halves(w):
    return jnp.concatenate(
        [lax.bitcast_convert_type(w << 16, _F32), lax.bitcast_convert_type(w & jnp.uint32(0xFFFF0000), _F32)], axis=1)


def _experts_body(ce_ref, nv_ref, nu_ref, xs_ref, wg_ref, wu_ref, wd_ref, ys_ref, wgb_ref, wub_ref, wdb_ref):
    c = pl.program_id(0)

    @pl.when(c < nu_ref[0])
    def _():
        @pl.when((c == 0) | (ce_ref[c] != ce_ref[jnp.maximum(c - 1, 0)]))
        def _():
            wgb_ref[...] = wg_ref[...].astype(_BF16)
            wub_ref[...] = wu_ref[...].astype(_BF16)
            wdb_ref[...] = wd_ref[...].astype(_BF16)

        row = lax.broadcasted_iota(jnp.int32, xs_ref.shape, 0)
        x = _unpack_halves(jnp.where(row < nv_ref[c], xs_ref[...], jnp.uint32(0))).astype(_BF16)
        g = _dot(x, wgb_ref[...])
        hmid = (g * jax.nn.sigmoid(g)) * _dot(x, wub_ref[...])
        ys_ref[...] = _pack_halves(_dot(hmid.astype(_BF16), wdb_ref[...]))


def _experts(chunk_e, n_valid, n_used, xs, wg, wu, wd):
    n_slots, w = xs.shape
    ch = EXPERT_CHUNK
    d, de = wg.shape[1], wg.shape[2]
    slot = lambda c, ce, nv, nu: (jnp.minimum(c, nu[0] - 1), 0)
    weight = lambda c, ce, nv, nu: (ce[c], 0, 0)
    return pl.pallas_call(
        _experts_body,
        out_shape=jax.ShapeDtypeStruct((n_slots, d // 2), jnp.uint32),
        grid_spec=pltpu.PrefetchScalarGridSpec(
            num_scalar_prefetch=3,
            grid=(n_slots // ch,),
            in_specs=[pl.BlockSpec((ch, w), slot),
                      pl.BlockSpec((None, d, de), weight),
                      pl.BlockSpec((None, d, de), weight),
                      pl.BlockSpec((None, de, d), weight)],
            out_specs=pl.BlockSpec((ch, d // 2), slot),
            scratch_shapes=[pltpu.VMEM((d, de), _BF16), pltpu.VMEM((d, de), _BF16), pltpu.VMEM((de, d), _BF16)]),
        compiler_params=_params("arbitrary"),
        name="experts",
    )(chunk_e, n_valid, n_used, xs, wg, wu, wd)


def _combine_body(x1_ref, rinfo_ref, mod_ref, y0_ref, y1_ref, *rest):
    o_ref = rest[-1]
    y = rinfo_ref[:, 4:5] * _unpack_halves(y0_ref[...]) + rinfo_ref[:, 5:6] * _unpack_halves(y1_ref[...])
    o_ref[...] = x1_ref[...] + mod_ref[:, 5 * D_MODEL:6 * D_MODEL] * y


def _combine(x1, rinfo, mod3, gathered, seq, part, out_so_far):
    t, d = x1.shape
    tf = min(COMBINE_TILE, seq)
    per_b = seq // tf
    nt = t // tf
    first = part * nt
    in_specs = [pl.BlockSpec((tf, d), lambda i: (i, 0)),
                pl.BlockSpec((tf, ROUTER_LANES), lambda i: (i, 0)),
                pl.BlockSpec((None, 1, mod3.shape[2]), lambda i: ((i + first) // per_b, 0, 0)),
                pl.BlockSpec((tf, d // 2), lambda i: (i, 0)),
                pl.BlockSpec((tf, d // 2), lambda i: (i + nt, 0))]
    args = [x1, rinfo, mod3, gathered, gathered]
    aliases = {}
    if out_so_far is not None:
        in_specs.append(pl.BlockSpec(memory_space=pl.ANY))
        args.append(out_so_far)
        aliases = {len(args) - 1: 0}
    return pl.pallas_call(
        _combine_body,
        out_shape=jax.ShapeDtypeStruct((t * MOE_PARTS, d), _F32),
        grid=(nt,),
        in_specs=in_specs,
        out_specs=pl.BlockSpec((tf, d), lambda i: (i + first, 0)),
        input_output_aliases=aliases,
        compiler_params=_params("arbitrary"),
        name="combine",
    )(*args)


def _rope_tables(positions):
    half = ROPE_DIM // 2
    inv_freq = ROPE_THETA ** (-jnp.arange(0, ROPE_DIM, 2, dtype=_F32) / ROPE_DIM)
    lane = jnp.arange(LANES) % HEAD_DIM
    freq = jnp.where(lane < ROPE_DIM, inv_freq[lane % half], 0.0)
    ang = positions.reshape(-1).astype(_F32)[:, None] * freq[None, :]
    cos, sin = jnp.cos(ang), jnp.sin(ang)
    c = jnp.where(lane < ROPE_DIM, cos, 1.0)
    s1 = jnp.where(lane < half, -sin, 0.0)
    s2 = jnp.where((lane >= half) & (lane < ROPE_DIM), sin, 0.0)
    return c, s1, s2


def _layer(x, mod, positions, g_mix, g_ffn, w_in, w_bg, qg, kg, w_bd, w_bs, w_out, w_rg, b_rg, w_re, b_re,
           w_eg, w_eu, w_ed):
    batch, seq, d = x.shape
    t = batch * seq
    x2 = x.reshape(t, d)
    mod3 = mod.reshape(batch, 1, mod.shape[1])
    gw = GROUP_WIDTH

    lane = jnp.arange(gw)
    bd = jnp.where(lane[:, None] // HEAD_DIM == lane[None, :] // HEAD_DIM, 1.0 / HEAD_DIM, 0.0).astype(_BF16)
    ex = (jnp.arange(LANES)[:, None] == (lane[None, :] // HEAD_DIM) * LSE_SEG).astype(_BF16)
    ex = jnp.concatenate([ex, ex], axis=0)
    tri = (jnp.arange(ROW_TILE)[:, None] > jnp.arange(ROW_TILE)[None, :]).astype(_BF16)
    rope = _rope_tables(positions)
    tile4 = lambda g: jnp.tile(g.astype(_F32), HEADS_PER_GROUP).reshape(1, gw)
    wr = jnp.zeros((d, ROUTER_LANES), _F32).at[:, :N_GROUPS].set(w_rg).at[:, N_GROUPS:N_GROUPS + N_EXPERTS].set(w_re)
    wrh = wr.astype(_BF16)
    wrl = (wr - wrh.astype(_F32)).astype(_BF16)
    br = jnp.zeros((1, ROUTER_LANES), _F32).at[0, :N_GROUPS].set(b_rg).at[0, N_GROUPS:N_GROUPS + N_EXPERTS].set(b_re)

    d0, d1, d2, sbp = _qkv(x2, mod3, g_mix.reshape(1, d), w_in.astype(_BF16), tile4(qg) * ATTN_SCALE, tile4(kg),
                           *rope, bd, seq)
    dil = [_dilated_group(a, dl, batch, seq) for a, (_, dl) in zip((d0, d1, d2), DIL_PATTERNS)]
    osb = _stick_breaking(sbp, batch, seq)

    merge_weights = (w_bg.astype(_BF16), w_bd.astype(_BF16), w_bs.astype(_BF16), w_out.astype(_BF16))
    tp = t // MOE_PARTS
    ch = EXPERT_CHUNK
    win = SC_INDEX_WINDOW
    n_chunks = -(-2 * tp // ch) + N_EXPERTS
    chunk_start = jnp.arange(n_chunks, dtype=jnp.int32) * ch
    out = None
    for part in range(MOE_PARTS):
        x1, h2p, rinfo, er, cnt = _merge(
            x2, mod3, g_mix.reshape(1, d), g_ffn.reshape(1, d), [o for o, _ in dil], [l for _, l in dil], osb,
            *merge_weights, wrh, wrl, br, ex, tri, seq, part)

        counts = cnt[0, EXPERT_LANE0:EXPERT_LANE0 + N_EXPERTS].astype(jnp.int32)
        padded = (counts + ch - 1) // ch * ch
        pend = jnp.cumsum(padded)
        pstart = pend - padded
        chunk_e = jnp.minimum(jnp.sum((pend[None, :] <= chunk_start[:, None]).astype(jnp.int32), axis=1),
                              N_EXPERTS - 1)
        n_used = (pend[-1:] // ch).astype(jnp.int32)
        begin = chunk_start[:, None]
        inside = (pstart[None, :] <= begin) & (begin < pend[None, :])
        n_valid = jnp.sum(jnp.where(inside, jnp.clip(counts[None, :] - (begin - pstart[None, :]), 0, ch), 0), axis=1)

        dest = _dest(pstart, er)
        xs = _sc_scatter(h2p, dest[0].reshape(tp // win, win), dest[1].reshape(tp // win, win), n_chunks * ch)
        ys = _experts(chunk_e, n_valid, n_used, xs, w_eg, w_eu, w_ed)
        gathered = _sc_gather(ys, dest.reshape(2 * tp // win, win))
        out = _combine(x1, rinfo, mod3, gathered, seq, part, out)
    return out.reshape(batch, seq, d)


def kernel(x, c, positions, w_ada, b_ada, g_norm_mix, g_norm_ffn, w_in, w_branch_gate, q_norm_g, k_norm_g,
           w_branch_dil, w_branch_sb, w_out, w_router_group, b_router_group, w_router_expert, b_router_expert,
           w_expert_gate, w_expert_up, w_expert_down):
    for l in range(w_ada.shape[0]):
        mod = _ada(c, w_ada[l], b_ada[l])
        x = _layer(x, mod, positions, g_norm_mix[l], g_norm_ffn[l], w_in[l], w_branch_gate[l], q_norm_g[l],
                   k_norm_g[l], w_branch_dil[l], w_branch_sb[l], w_out[l], w_router_group[l], b_router_group[l],
                   w_router_expert[l], b_router_expert[l], w_expert_gate[l], w_expert_up[l], w_expert_down[l])
    return x
```

```python
import functools

import jax
import jax.numpy as jnp
from jax import lax
from jax.experimental import pallas as pl
from jax.experimental.pallas import tpu as pltpu
from jax.experimental.pallas import tpu_sc as plsc

D_MODEL = 1024
HEAD_DIM = 64
DIL_PATTERNS = ((128, 1), (512, 4), (2048, 16))
HEADS_PER_GROUP = 4
GROUP_WIDTH = HEADS_PER_GROUP * HEAD_DIM
N_DIL_GROUPS = len(DIL_PATTERNS)
DIL_WIDTH = N_DIL_GROUPS * GROUP_WIDTH
QKV_WIDTH = 3 * DIL_WIDTH + 3 * GROUP_WIDTH
WINDOW_KEYS = 128
ROPE_THETA = 500000.0
ROPE_DIM = HEAD_DIM // 4
N_GROUPS = 4
EXPERTS_PER_GROUP = 8
N_EXPERTS = N_GROUPS * EXPERTS_PER_GROUP
D_EXPERT = 512
RMS_EPS = 1e-6
ATTN_SCALE = HEAD_DIM ** -0.5

LANES = 128
ROUTER_LANES = LANES
EXPERT_LANE0 = N_GROUPS
LSE_SEG = LANES // HEADS_PER_GROUP
NEG_BIG = -1e30
SB_DEAD_LOG = -120.0
SB_HEAD_BLOCKS = 3
SB_CHAINS = 4

ROW_TILE = 512
QBLK = 128
DIL_QTILE = 512
EXPERT_CHUNK = 512
COMBINE_TILE = 512
DEST_TILE = 8192
SC_CORES = 2
SC_SUBCORES = 16
SC_WORKERS = SC_CORES * SC_SUBCORES
SC_INDEX_WINDOW = 128
MOE_PARTS = 2
VMEM_LIMIT = 48 * 1024 * 1024

_BF16 = jnp.bfloat16
_F32 = jnp.float32
_NT = (((1,), (1,)), ((), ()))


def _dot(a, b):
    return jnp.dot(a, b, preferred_element_type=_F32)


def _dot_nt(a, b):
    return lax.dot_general(a, b, _NT, preferred_element_type=_F32)


def _split(a):
    hi = a.astype(_BF16)
    lo = (a - hi.astype(_F32)).astype(_BF16)
    return hi, lo


def _dot3(a, b):
    ah, al = _split(a)
    bh, bl = _split(b)
    return _dot(ah, bh) + (_dot(ah, bl) + _dot(al, bh))


def _rms_mod(x, g, scale, shift):
    y = x * lax.rsqrt(jnp.mean(x * x, axis=-1, keepdims=True) + RMS_EPS)
    return y * g * (1.0 + scale) + shift


def _params(*sem):
    return pltpu.CompilerParams(dimension_semantics=sem, vmem_limit_bytes=VMEM_LIMIT)


def _ada_body(c_ref, w_ref, b_ref, o_ref):
    c = c_ref[...]
    o_ref[...] = _dot3(c * jax.nn.sigmoid(c), w_ref[...]) + b_ref[...]


def _ada(c, w_ada, b_ada):
    b, d = c.shape
    n = w_ada.shape[1]
    rows = -(-b // 16) * 16
    cp = jnp.zeros((rows, d), _F32).at[:b].set(c)
    nt = 1536
    out = pl.pallas_call(
        _ada_body,
        out_shape=jax.ShapeDtypeStruct((rows, n), _F32),
        grid=(n // nt,),
        in_specs=[pl.BlockSpec((rows, d), lambda j: (0, 0)),
                  pl.BlockSpec((d, nt), lambda j: (0, j)),
                  pl.BlockSpec((1, nt), lambda j: (0, j))],
        out_specs=pl.BlockSpec((rows, nt), lambda j: (0, j)),
        compiler_params=_params("arbitrary"),
        name="ada",
    )(cp, w_ada, b_ada.reshape(1, n))
    return out[:b]


def _qkv_body(x_ref, mod_ref, g_ref, w_ref, qg_ref, kg_ref, c_ref, s1_ref, s2_ref, bd_ref,
              o0_ref, o1_ref, o2_ref, osb_ref, st_ref, acc_ref):
    d = D_MODEL
    tm = x_ref.shape[0]
    h = _rms_mod(x_ref[...], g_ref[...], mod_ref[:, d:2 * d], mod_ref[:, 0:d])
    hb = h.astype(_BF16)
    cc = jnp.concatenate([c_ref[...]] * 2, axis=1)
    s1 = jnp.concatenate([s1_ref[...]] * 2, axis=1)
    s2 = jnp.concatenate([s2_ref[...]] * 2, axis=1)
    bd = bd_ref[...]
    gw = GROUP_WIDTH

    def normed_rotated(acc, gain):
        ms = _dot((acc * acc).astype(_BF16), bd)
        y = acc * lax.rsqrt(ms + RMS_EPS) * gain
        return y * cc + pltpu.roll(y, gw - ROPE_DIM // 2, 1) * s1 + pltpu.roll(y, ROPE_DIM // 2, 1) * s2

    def store(o_ref, dil, part, y):
        if dil == 1:
            o_ref[:, part * gw:(part + 1) * gw] = y.astype(_BF16)
            return
        for s in range(gw // LANES):
            st_ref[s] = y[:, s * LANES:(s + 1) * LANES]
        for r in range(dil):
            for s in range(gw // LANES):
                col0 = (3 * r + part) * gw + s * LANES
                o_ref[:, col0:col0 + LANES] = st_ref[s, pl.ds(r, tm // dil, stride=dil), :].astype(_BF16)

    def project(col0):
        return _dot(hb, w_ref[:, col0:col0 + gw])

    outs = (o0_ref, o1_ref, o2_ref)
    dils = [dl for _, dl in DIL_PATTERNS]
    normed = [(g, part, part * DIL_WIDTH + g * gw) for g in range(N_DIL_GROUPS) for part in (0, 1)]
    for n, (_, _, col0) in enumerate(normed):
        acc_ref[n] = project(col0)
    plain = [("v", g) for g in range(N_DIL_GROUPS)] + [("sb", part) for part in range(3)]
    for n, (kind, j) in enumerate(plain):
        if kind == "v":
            store(outs[j], dils[j], 2, project(2 * DIL_WIDTH + j * gw))
        else:
            acc = project(3 * DIL_WIDTH + j * gw)
            osb_ref[:, j * gw:(j + 1) * gw] = (acc * ATTN_SCALE if j == 0 else acc).astype(_BF16)
        g, part, _ = normed[n]
        gain = qg_ref[...] if part == 0 else kg_ref[...]
        store(outs[g], dils[g], part, normed_rotated(acc_ref[n], gain))


def _qkv(x2, mod3, g_mix, w_in, qg, kg, rope_c, rope_s1, rope_s2, bd, seq):
    t, d = x2.shape
    tm = ROW_TILE
    per_b = seq // tm
    row = lambda i: (i, 0)
    const = lambda i: (0, 0)
    width = 3 * GROUP_WIDTH
    dils = [dl for _, dl in DIL_PATTERNS] + [1]
    return pl.pallas_call(
        _qkv_body,
        out_shape=[jax.ShapeDtypeStruct((t // dl, dl * width), _BF16) for dl in dils],
        grid=(t // tm,),
        in_specs=[pl.BlockSpec((tm, d), row),
                  pl.BlockSpec((None, 1, mod3.shape[2]), lambda i: (i // per_b, 0, 0)),
                  pl.BlockSpec((1, d), const),
                  pl.BlockSpec(w_in.shape, const),
                  pl.BlockSpec((1, GROUP_WIDTH), const),
                  pl.BlockSpec((1, GROUP_WIDTH), const),
                  pl.BlockSpec((tm, LANES), row),
                  pl.BlockSpec((tm, LANES), row),
                  pl.BlockSpec((tm, LANES), row),
                  pl.BlockSpec(bd.shape, const)],
        out_specs=[pl.BlockSpec((tm // dl, dl * width), row) for dl in dils],
        scratch_shapes=[pltpu.VMEM((GROUP_WIDTH // LANES, tm, LANES), _F32),
                        pltpu.VMEM((2 * N_DIL_GROUPS, tm, GROUP_WIDTH), _F32)],
        compiler_params=_params("arbitrary"),
        name="qkv",
    )(x2, mod3, g_mix, w_in, qg, kg, rope_c, rope_s1, rope_s2, bd)


def _dil_body(q_ref, kp_ref, kc_ref, vp_ref, vc_ref, o_ref, lse_ref, kf_ref, vf_ref):
    tq = q_ref.shape[0]
    first = pl.program_id(2) == 0
    kf_ref[0:QBLK, :] = kp_ref[...]
    kf_ref[QBLK:, :] = kc_ref[...]
    vf_ref[0:QBLK, :] = vp_ref[...]
    vf_ref[QBLK:, :] = vc_ref[...]
    nh = HEADS_PER_GROUP
    row = lax.broadcasted_iota(jnp.int32, (nh * QBLK, 2 * QBLK), 0) & (QBLK - 1)
    col = lax.broadcasted_iota(jnp.int32, (nh * QBLK, 2 * QBLK), 1)
    band = (col >= row) & (col <= row + WINDOW_KEYS)
    lane = lax.broadcasted_iota(jnp.int32, (1, GROUP_WIDTH), 1)
    slane = lax.broadcasted_iota(jnp.int32, (1, LANES), 1)
    head_masks = [(lane >= h * HEAD_DIM) & (lane < (h + 1) * HEAD_DIM) for h in range(nh)]
    subs = range(tq // QBLK)
    rows = [slice(j * QBLK, (j + 1) * QBLK) for j in subs]
    window = [slice(j * QBLK, (j + 2) * QBLK) for j in subs]

    def stack(qj):
        return jnp.concatenate([jnp.where(hm, qj, jnp.zeros_like(qj)) for hm in head_masks], axis=0)

    valid = [band & ((col >= QBLK) | jnp.logical_not(first)) if j == 0 else band for j in subs]
    s = [jnp.where(valid[j], _dot_nt(stack(q_ref[rows[j], :]), kf_ref[window[j], :]), NEG_BIG) for j in subs]
    m = [jnp.max(s[j], axis=1, keepdims=True) for j in subs]
    p = [jnp.exp(s[j] - m[j]) for j in subs]
    l = [jnp.sum(p[j], axis=1, keepdims=True) for j in subs]
    o_all = [_dot(p[j].astype(_BF16), vf_ref[window[j], :]) / l[j] for j in subs]
    for j in subs:
        lse_all = m[j] + jnp.log(l[j])
        o_acc = jnp.zeros((QBLK, GROUP_WIDTH), _F32)
        lse_t = jnp.zeros((QBLK, LANES), _F32)
        for h, hm in enumerate(head_masks):
            o_acc = jnp.where(hm, o_all[j][h * QBLK:(h + 1) * QBLK, :], o_acc)
            sm = (slane >= h * LSE_SEG) & (slane < (h + 1) * LSE_SEG)
            lse_t = jnp.where(sm, lse_all[h * QBLK:(h + 1) * QBLK, :], lse_t)
        o_ref[rows[j], :] = o_acc.astype(_BF16)
        lse_ref[rows[j], :] = lse_t


def _dilated_group(view2, dil, batch, seq):
    sd = seq // dil
    tq = min(DIL_QTILE, sd)
    per = tq // QBLK
    gw = GROUP_WIDTH
    view = view2.reshape(batch, sd, dil * 3 * gw)
    cur = lambda part: pl.BlockSpec((None, tq, gw), lambda b, r, i: (b, i, 3 * r + part))
    prev = lambda part: pl.BlockSpec((None, QBLK, gw),
                                     lambda b, r, i: (b, jnp.maximum(i * per - 1, 0), 3 * r + part))
    o, lse = pl.pallas_call(
        _dil_body,
        out_shape=[jax.ShapeDtypeStruct((batch, sd, dil * gw), _BF16),
                   jax.ShapeDtypeStruct((batch, sd, dil * LANES), _F32)],
        grid=(batch, dil, sd // tq),
        in_specs=[cur(0), prev(1), cur(1), prev(2), cur(2)],
        out_specs=[pl.BlockSpec((None, tq, gw), lambda b, r, i: (b, i, r)),
                   pl.BlockSpec((None, tq, LANES), lambda b, r, i: (b, i, r))],
        scratch_shapes=[pltpu.VMEM((tq + QBLK, gw), _BF16), pltpu.VMEM((tq + QBLK, gw), _BF16)],
        compiler_params=_params("arbitrary", "arbitrary", "arbitrary"),
        name=f"dil{dil}",
    )(view, view, view, view, view)
    return o.reshape(batch * sd, dil * gw), lse.reshape(batch * sd, dil * LANES)


def _sb_body(q_ref, k_ref, v_ref, o_ref, carry_ref, acc_ref, qs_ref):
    step_id = pl.program_id(1)
    chains = range(SB_CHAINS)
    blk = [step_id * SB_CHAINS + c for c in chains]
    nh = HEADS_PER_GROUP
    row = lax.broadcasted_iota(jnp.int32, (nh * QBLK, QBLK), 0) & (QBLK - 1)
    col = lax.broadcasted_iota(jnp.int32, (nh * QBLK, QBLK), 1)
    strict = col < row
    ur = lax.broadcasted_iota(jnp.int32, (2 * QBLK, QBLK), 0) & (QBLK - 1)
    uc = lax.broadcasted_iota(jnp.int32, (2 * QBLK, QBLK), 1)
    u = jnp.where(ur > uc, 1.0, 0.0).astype(_BF16)
    lane = lax.broadcasted_iota(jnp.int32, (1, GROUP_WIDTH), 1)
    head_masks = [(lane >= h * HEAD_DIM) & (lane < (h + 1) * HEAD_DIM) for h in range(nh)]
    for c in chains:
        q = q_ref[c * QBLK:(c + 1) * QBLK, :]
        qs_ref[c] = jnp.concatenate([jnp.where(hm, q, jnp.zeros_like(q)) for hm in head_masks], axis=0)

    def softplus(z):
        return jnp.maximum(z, 0.0) + jnp.log(1.0 + jnp.exp(-jnp.abs(z)))

    def later_keys(log_1m):
        hi, lo = _split(log_1m)
        return _dot(jnp.concatenate([hi, lo], axis=1), u)

    nb = SB_HEAD_BLOCKS
    cols = [slice(j * QBLK, (j + 1) * QBLK) for j in range(nb)]
    kbs = [[blk[c] - (nb - 1) + j for j in range(nb)] for c in chains]
    starts = [[pl.multiple_of(jnp.maximum(kb, 0) * QBLK, QBLK) for kb in kbs[c]] for c in chains]
    keep = [[strict if j == nb - 1 else (kbs[c][j] >= 0) for j in range(nb)] for c in chains]
    z = [_dot_nt(qs_ref[c], jnp.concatenate([k_ref[pl.ds(s, QBLK), :] for s in starts[c]], axis=0)) for c in chains]
    sp = [softplus(z[c]) for c in chains]
    log_1m = [[jnp.where(keep[c][j], -sp[c][:, cols[j]], 0.0) for j in range(nb)] for c in chains]
    totals = [[jnp.sum(l, axis=1, keepdims=True) for l in log_1m[c]] for c in chains]
    later = [[later_keys(log_1m[c][j]) for j in range(nb)] for c in chains]
    for c in chains:
        a_blocks = []
        after = jnp.zeros_like(totals[c][0])
        for j in reversed(range(nb)):
            a = jnp.exp((z[c][:, cols[j]] - sp[c][:, cols[j]]) + later[c][j] + after)
            a_blocks.insert(0, jnp.where(keep[c][j], a, 0.0).astype(_BF16))
            after = after + totals[c][j]
        acc_ref[c] = _dot(jnp.concatenate(a_blocks, axis=1),
                          jnp.concatenate([v_ref[pl.ds(s, QBLK), :] for s in starts[c]], axis=0))
        carry_ref[c] = after

    for c in chains:
        def tile(kb, c=c):
            start = pl.multiple_of(kb * QBLK, QBLK)
            z = _dot_nt(qs_ref[c], k_ref[pl.ds(start, QBLK), :])
            sp = softplus(z)
            log_1m = -sp
            a = jnp.exp((z - sp) + later_keys(log_1m) + carry_ref[c])
            acc_ref[c] += _dot(a.astype(_BF16), v_ref[pl.ds(start, QBLK), :])
            carry = carry_ref[c] + jnp.sum(log_1m, axis=1, keepdims=True)
            carry_ref[c] = carry
            return jnp.max(carry)

        def cond(st):
            return (st[0] >= 0) & (st[1] > SB_DEAD_LOG)

        def step(st, tile=tile):
            return st[0] - 1, tile(st[0])

        lax.while_loop(cond, step, (blk[c] - nb, jnp.max(carry_ref[c])))
        out = jnp.zeros((QBLK, GROUP_WIDTH), _F32)
        for h, hm in enumerate(head_masks):
            out = jnp.where(hm, acc_ref[c, h * QBLK:(h + 1) * QBLK, :], out)
        o_ref[c * QBLK:(c + 1) * QBLK, :] = out.astype(_BF16)


def _stick_breaking(arr, batch, seq):
    gw = GROUP_WIDTH
    qt = SB_CHAINS * QBLK
    view = arr.reshape(batch, seq, 3 * gw)
    o = pl.pallas_call(
        _sb_body,
        out_shape=jax.ShapeDtypeStruct((batch, seq, gw), _BF16),
        grid=(batch, seq // qt),
        in_specs=[pl.BlockSpec((None, qt, gw), lambda b, i: (b, i, 0)),
                  pl.BlockSpec((None, seq, gw), lambda b, i: (b, 0, 1)),
                  pl.BlockSpec((None, seq, gw), lambda b, i: (b, 0, 2))],
        out_specs=pl.BlockSpec((None, qt, gw), lambda b, i: (b, i, 0)),
        scratch_shapes=[pltpu.VMEM((SB_CHAINS, HEADS_PER_GROUP * QBLK, 1), _F32),
                        pltpu.VMEM((SB_CHAINS, HEADS_PER_GROUP * QBLK, gw), _F32),
                        pltpu.VMEM((SB_CHAINS, HEADS_PER_GROUP * QBLK, gw), _BF16)],
        compiler_params=_params("arbitrary", "arbitrary"),
        name="sb",
    )(view, view, view)
    return o.reshape(batch * seq, gw)


def _merge_body(x_ref, mod_ref, g1_ref, g2_ref, o0_ref, o1_ref, o2_ref, l0_ref, l1_ref, l2_ref, osb_ref,
                wbg_ref, wbd_ref, wbs_ref, wout_ref, wrh_ref, wrl_ref, br_ref, ex_ref, tri_ref,
                x1_ref, h2p_ref, rinfo_ref, er_ref, cnt_out_ref, cnt_ref, os1_ref, os2_ref, ls1_ref, ls2_ref):
    d = D_MODEL
    tm = x_ref.shape[0]
    x = x_ref[...]
    hb = _rms_mod(x, g1_ref[...], mod_ref[:, d:2 * d], mod_ref[:, 0:d]).astype(_BF16)

    def natural(ref, st_ref, dil):
        if dil == 1:
            return ref[...].astype(_F32)
        slabs = st_ref.shape[0]
        for r in range(dil):
            for s in range(slabs):
                col0 = (r * slabs + s) * LANES
                st_ref[s, pl.ds(r, tm // dil, stride=dil), :] = ref[:, col0:col0 + LANES].astype(_F32)
        return jnp.concatenate([st_ref[s] for s in range(slabs)], axis=1)

    dils = [dl for _, dl in DIL_PATTERNS]
    o_nat = [natural(r, s, dl) for r, s, dl in zip((o0_ref, o1_ref, o2_ref), (None, os1_ref, os2_ref), dils)]
    l0, l1, l2 = [natural(r, s, dl) for r, s, dl in zip((l0_ref, l1_ref, l2_ref), (None, ls1_ref, ls2_ref), dils)]

    lmax = jnp.maximum(jnp.maximum(l0, l1), l2)
    e0, e1, e2 = jnp.exp(l0 - lmax), jnp.exp(l1 - lmax), jnp.exp(l2 - lmax)
    inv = 1.0 / (e0 + e1 + e2)
    ex = ex_ref[...]

    def widen(w):
        hi, lo = _split(w)
        return _dot(jnp.concatenate([hi, lo], axis=1), ex)

    w_groups = [widen(e * inv) for e in (e0, e1, e2)]
    gate_dil = jax.nn.sigmoid(_dot(hb, wbg_ref[:, :d]))
    o_dil = w_groups[0] * o_nat[0] + w_groups[1] * o_nat[1] + w_groups[2] * o_nat[2]
    branch_dil = _dot(o_dil.astype(_BF16), wbd_ref[...])
    branch_sb = _dot(osb_ref[...], wbs_ref[...])
    gate_sb = jax.nn.sigmoid(_dot(hb, wbg_ref[:, d:]))
    merged = gate_dil * branch_dil + gate_sb * branch_sb
    x1 = x + mod_ref[:, 2 * d:3 * d] * _dot(merged.astype(_BF16), wout_ref[...])
    x1_ref[...] = x1

    h2 = _rms_mod(x1, g2_ref[...], mod_ref[:, 4 * d:5 * d], mod_ref[:, 3 * d:4 * d])
    h2p_ref[...] = _pack_halves(h2)

    hh, hl = _split(h2)
    logits = _dot(hh, wrh_ref[...]) + (_dot(hl, wrh_ref[...]) + _dot(hh, wrl_ref[...])) + br_ref[...]
    lane =lax.broadcasted_iota(jnp.int32, (tm, ROUTER_LANES), 1).astype(_F32)
    far = float(ROUTER_LANES)

    def top(vals):
        m = jnp.max(vals, axis=1, keepdims=True)
        return m, jnp.min(jnp.where(vals == m, lane, far), axis=1, keepdims=True)

    is_group = lane < N_GROUPS
    mg, gsel = top(jnp.where(is_group, logits, NEG_BIG))
    pg_top = 1.0 / jnp.sum(jnp.where(is_group, jnp.exp(logits - mg), 0.0), axis=1, keepdims=True)
    lane0 = EXPERT_LANE0 + EXPERTS_PER_GROUP * gsel
    le = jnp.where((lane >= lane0) & (lane < lane0 + EXPERTS_PER_GROUP), logits, NEG_BIG)
    m1, i1 = top(le)
    m2, i2 = top(jnp.where(lane == i1, NEG_BIG, le))
    t2 = jnp.exp(m2 - m1)
    w0 = pg_top / (1.0 + t2)
    w1 = pg_top * t2 / (1.0 + t2)

    @pl.when(pl.program_id(0) == 0)
    def _():
        cnt_ref[...] = jnp.zeros_like(cnt_ref)

    sel0, sel1 = lane == i1, lane == i2
    onehot = jnp.where(sel0 | sel1, 1.0, 0.0)
    before = _dot(tri_ref[...], onehot.astype(_BF16)) + cnt_ref[0:1, :]
    r0 = jnp.sum(jnp.where(sel0, before, 0.0), axis=1, keepdims=True)
    r1 = jnp.sum(jnp.where(sel1, before, 0.0), axis=1, keepdims=True)
    cnt_ref[...] += jnp.sum(onehot, axis=0, keepdims=True)
    cnt_out_ref[...] = cnt_ref[...]

    cols = (i1 - EXPERT_LANE0, i2 - EXPERT_LANE0, r0, r1, w0, w1)
    rinfo = jnp.zeros((tm, ROUTER_LANES), _F32)
    for c, v in enumerate(cols):
        rinfo = jnp.where(lane == float(c), v, rinfo)
    rinfo_ref[...] = rinfo
    er_ref[...] = jnp.transpose(rinfo)[0:8, :].astype(jnp.int32)


def _merge(x2, mod3, g1, g2, outs, lses, osb, wbg, wbd, wbs, wout, wrh, wrl, br, ex, tri, seq, part):
    d = x2.shape[1]
    tm = ROW_TILE
    per_b = seq // tm
    t = x2.shape[0] // MOE_PARTS
    first = part * (t // tm)
    src = lambda i: (i + first, 0)
    row = lambda i: (i, 0)
    const = lambda i: (0, 0)
    full = lambda a: pl.BlockSpec(a.shape, const)
    gw = GROUP_WIDTH
    dils = [dl for _, dl in DIL_PATTERNS]
    return pl.pallas_call(
        _merge_body,
        out_shape=[jax.ShapeDtypeStruct((t, d), _F32),
                   jax.ShapeDtypeStruct((t, d // 2), jnp.uint32),
                   jax.ShapeDtypeStruct((t, ROUTER_LANES), _F32),
                   jax.ShapeDtypeStruct((8, t), jnp.int32),
                   jax.ShapeDtypeStruct((8, ROUTER_LANES), _F32)],
        grid=(t // tm,),
        in_specs=[pl.BlockSpec((tm, d), src),
                  pl.BlockSpec((None, 1, mod3.shape[2]), lambda i: ((i + first) // per_b, 0, 0)),
                  full(g1), full(g2)]
                 + [pl.BlockSpec((tm // dl, dl * gw), src) for dl in dils]
                 + [pl.BlockSpec((tm // dl, dl * LANES), src) for dl in dils]
                 + [pl.BlockSpec((tm, gw), src)]
                 + [full(a) for a in (wbg, wbd, wbs, wout, wrh, wrl, br, ex, tri)],
        out_specs=[pl.BlockSpec((tm, d), row),
                   pl.BlockSpec((tm, d // 2), row),
                   pl.BlockSpec((tm, ROUTER_LANES), row),
                   pl.BlockSpec((8, tm), lambda i: (0, i)),
                   pl.BlockSpec((8, ROUTER_LANES), const)],
        scratch_shapes=[pltpu.VMEM((8, ROUTER_LANES), _F32),
                        pltpu.VMEM((gw // LANES, tm, LANES), _F32), pltpu.VMEM((gw // LANES, tm, LANES), _F32),
                        pltpu.VMEM((1, tm, LANES), _F32), pltpu.VMEM((1, tm, LANES), _F32)],
        compiler_params=_params("arbitrary"),
        name="merge",
    )(x2, mod3, g1, g2, *outs, *lses, osb, wbg, wbd, wbs, wout, wrh, wrl, br, ex, tri)


def _dest_body(ps_ref, er_ref, d_ref):
    e = er_ref[0:2, :]
    start = jnp.zeros_like(e)
    for x in range(N_EXPERTS):
        start = jnp.where(e == x, ps_ref[x], start)
    d_ref[...] = start + er_ref[2:4, :]


def _dest(pstart, er):
    t = er.shape[1]
    tw = min(DEST_TILE, t)
    return pl.pallas_call(
        _dest_body,
        out_shape=jax.ShapeDtypeStruct((2, t), jnp.int32),
        grid_spec=pltpu.PrefetchScalarGridSpec(
            num_scalar_prefetch=1,
            grid=(t // tw,),
            in_specs=[pl.BlockSpec((8, tw), lambda i, ps: (0, i))],
            out_specs=pl.BlockSpec((2, tw), lambda i, ps: (0, i))),
        compiler_params=_params("arbitrary"),
        name="dest",
    )(pstart, er)


def _sc_mesh():
    return plsc.VectorSubcoreMesh(core_axis_name="core", subcore_axis_name="subcore",
                                  num_cores=SC_CORES, num_subcores=SC_SUBCORES)


def _sc_worker():
    return lax.axis_index("subcore") * SC_CORES + lax.axis_index("core")


def _sc_scatter(x, idx0, idx1, n_slots):
    chunks = idx0.shape[0]
    per = chunks // SC_WORKERS
    win = idx0.shape[1]

    @functools.partial(
        pl.kernel, mesh=_sc_mesh(), out_type=jax.ShapeDtypeStruct((n_slots, x.shape[1]), x.dtype),
        scratch_types=[pltpu.VMEM((1, win), jnp.int32), pltpu.VMEM((1, win), jnp.int32),
                       pltpu.VMEM((win, x.shape[1]), x.dtype), pltpu.SemaphoreType.DMA],
        name="sc_scatter")
    def run(x_hbm, i0_hbm, i1_hbm, o_hbm, i0_v, i1_v, rows_v, sem):
        wid = _sc_worker()

        @pl.loop(0, per)
        def _(j):
            c = wid * per + j
            pltpu.sync_copy(i0_hbm.at[pl.ds(c, 1)], i0_v)
            pltpu.sync_copy(i1_hbm.at[pl.ds(c, 1)], i1_v)
            pltpu.sync_copy(x_hbm.at[pl.ds(c * win, win)], rows_v)
            first = pltpu.async_copy(rows_v, o_hbm.at[i0_v.at[0]], sem)
            second = pltpu.async_copy(rows_v, o_hbm.at[i1_v.at[0]], sem)
            first.wait()
            second.wait()

    return run(x, idx0, idx1)


def _sc_gather(table, idx):
    chunks, win = idx.shape
    per = chunks // SC_WORKERS

    @functools.partial(
        pl.kernel, mesh=_sc_mesh(), out_type=jax.ShapeDtypeStruct((chunks * win, table.shape[1]), table.dtype),
        scratch_types=[pltpu.VMEM((1, win), jnp.int32), pltpu.VMEM((win, table.shape[1]), table.dtype),
                       pltpu.SemaphoreType.DMA],
        name="sc_gather")
    def run(t_hbm, i_hbm, o_hbm, i_v, rows_v, sem):
        wid = _sc_worker()

        @pl.loop(0, per)
        def _(j):
            c = wid * per + j
            pltpu.sync_copy(i_hbm.at[pl.ds(c, 1)], i_v)
            pltpu.async_copy(t_hbm.at[i_v.at[0]], rows_v, sem).wait()
            pltpu.sync_copy(rows_v, o_hbm.at[pl.ds(c * win, win)])

    return run(table, idx)


def _pack_halves(a):
    h = a.shape[1] // 2
    lo = lax.bitcast_convert_type(a[:, :h].astype(_BF16).astype(_F32), jnp.uint32) >> 16
    hi = lax.bitcast_convert_type(a[:, h:].astype(_BF16).astype(_F32), jnp.uint32) & jnp.uint32(0xFFFF0000)
    return lo | hi


def _unpack_halves(w):
    return jnp.concatenate(
        [lax.bitcast_convert_type(w << 16, _F32), lax.bitcast_convert_type(w & jnp.uint32(0xFFFF0000), _F32)], axis=1)


def _experts_body(ce_ref, nv_ref, nu_ref, xs_ref, wg_ref, wu_ref, wd_ref, ys_ref, wgb_ref, wub_ref, wdb_ref):
    c = pl.program_id(0)

    @pl.when(c < nu_ref[0])
    def _():
        @pl.when((c == 0) | (ce_ref[c] != ce_ref[jnp.maximum(c - 1, 0)]))
        def _():
            wgb_ref[...] = wg_ref[...].astype(_BF16)
            wub_ref[...] = wu_ref[...].astype(_BF16)
            wdb_ref[...] = wd_ref[...].astype(_BF16)

        row = lax.broadcasted_iota(jnp.int32, xs_ref.shape, 0)
        x = _unpack_halves(jnp.where(row < nv_ref[c], xs_ref[...], jnp.uint32(0))).astype(_BF16)
        half = x.shape[0] // 2
        ups = [(_dot(x[r:r + half], wgb_ref[...]), _dot(x[r:r + half], wub_ref[...])) for r in (0, half)]
        for (g, u), r in zip(ups, (0, half)):
            hmid = (g * jax.nn.sigmoid(g)) * u
            ys_ref[r:r + half, :] = _pack_halves(_dot(hmid.astype(_BF16), wdb_ref[...]))


def _experts(chunk_e, n_valid, n_used, xs, wg, wu, wd):
    n_slots, w = xs.shape
    ch = EXPERT_CHUNK
    d, de = wg.shape[1], wg.shape[2]
    slot = lambda c, ce, nv, nu: (jnp.minimum(c, nu[0] - 1), 0)
    weight = lambda c, ce, nv, nu: (ce[c], 0, 0)
    return pl.pallas_call(
        _experts_body,
        out_shape=jax.ShapeDtypeStruct((n_slots, d // 2), jnp.uint32),
        grid_spec=pltpu.PrefetchScalarGridSpec(
            num_scalar_prefetch=3,
            grid=(n_slots // ch,),
            in_specs=[pl.BlockSpec((ch, w), slot),
                      pl.BlockSpec((None, d, de), weight),
                      pl.BlockSpec((None, d, de), weight),
                      pl.BlockSpec((None, de, d), weight)],
            out_specs=pl.BlockSpec((ch, d // 2), slot),
            scratch_shapes=[pltpu.VMEM((d, de), _BF16), pltpu.VMEM((d, de), _BF16), pltpu.VMEM((de, d), _BF16)]),
        compiler_params=_params("arbitrary"),
        name="experts",
    )(chunk_e, n_valid, n_used, xs, wg, wu, wd)


def _combine_body(x1_ref, rinfo_ref, mod_ref, y0_ref, y1_ref, *rest):
    o_ref = rest[-1]
    y = rinfo_ref[:, 4:5] * _unpack_halves(y0_ref[...]) + rinfo_ref[:, 5:6] * _unpack_halves(y1_ref[...])
    o_ref[...] = x1_ref[...] + mod_ref[:, 5 * D_MODEL:6 * D_MODEL] * y


def _combine(x1, rinfo, mod3, gathered, seq, part, out_so_far):
    t, d = x1.shape
    tf = min(COMBINE_TILE, seq)
    per_b = seq // tf
    nt = t // tf
    first = part * nt
    in_specs = [pl.BlockSpec((tf, d), lambda i: (i, 0)),
                pl.BlockSpec((tf, ROUTER_LANES), lambda i: (i, 0)),
                pl.BlockSpec((None, 1, mod3.shape[2]), lambda i: ((i + first) // per_b, 0, 0)),
                pl.BlockSpec((tf, d // 2), lambda i: (i, 0)),
                pl.BlockSpec((tf, d // 2), lambda i: (i + nt, 0))]
    args = [x1, rinfo, mod3, gathered, gathered]
    aliases = {}
    if out_so_far is not None:
        in_specs.append(pl.BlockSpec(memory_space=pl.ANY))
        args.append(out_so_far)
        aliases = {len(args) - 1: 0}
    return pl.pallas_call(
        _combine_body,
        out_shape=jax.ShapeDtypeStruct((t * MOE_PARTS, d), _F32),
        grid=(nt,),
        in_specs=in_specs,
        out_specs=pl.BlockSpec((tf, d), lambda i: (i + first, 0)),
        input_output_aliases=aliases,
        compiler_params=_params("arbitrary"),
        name="combine",
    )(*args)


def _rope_tables(positions):
    half = ROPE_DIM // 2
    inv_freq = ROPE_THETA ** (-jnp.arange(0, ROPE_DIM, 2, dtype=_F32) / ROPE_DIM)
    lane = jnp.arange(LANES) % HEAD_DIM
    freq = jnp.where(lane < ROPE_DIM, inv_freq[lane % half], 0.0)
    ang = positions.reshape(-1).astype(_F32)[:, None] * freq[None, :]
    cos, sin = jnp.cos(ang), jnp.sin(ang)
    c = jnp.where(lane < ROPE_DIM, cos, 1.0)
    s1 = jnp.where(lane < half, -sin, 0.0)
    s2 = jnp.where((lane >= half) & (lane < ROPE_DIM), sin, 0.0)
    return c, s1, s2


def _layer(x, mod, positions, g_mix, g_ffn, w_in, w_bg, qg, kg, w_bd, w_bs, w_out, w_rg, b_rg, w_re, b_re,
           w_eg, w_eu, w_ed):
    batch, seq, d = x.shape
    t = batch * seq
    x2 = x.reshape(t, d)
    mod3 = mod.reshape(batch, 1, mod.shape[1])
    gw = GROUP_WIDTH

    lane = jnp.arange(gw)
    bd = jnp.where(lane[:, None] // HEAD_DIM == lane[None, :] // HEAD_DIM, 1.0 / HEAD_DIM, 0.0).astype(_BF16)
    ex = (jnp.arange(LANES)[:, None] == (lane[None, :] // HEAD_DIM) * LSE_SEG).astype(_BF16)
    ex = jnp.concatenate([ex, ex], axis=0)
    tri = (jnp.arange(ROW_TILE)[:, None] > jnp.arange(ROW_TILE)[None, :]).astype(_BF16)
    rope = _rope_tables(positions)
    tile4 = lambda g: jnp.tile(g.astype(_F32), HEADS_PER_GROUP).reshape(1, gw)
    wr = jnp.zeros((d, ROUTER_LANES), _F32).at[:, :N_GROUPS].set(w_rg).at[:, N_GROUPS:N_GROUPS + N_EXPERTS].set(w_re)
    wrh = wr.astype(_BF16)
    wrl = (wr - wrh.astype(_F32)).astype(_BF16)
    br = jnp.zeros((1, ROUTER_LANES), _F32).at[0, :N_GROUPS].set(b_rg).at[0, N_GROUPS:N_GROUPS + N_EXPERTS].set(b_re)

    d0, d1, d2, sbp = _qkv(x2, mod3, g_mix.reshape(1, d), w_in.astype(_BF16), tile4(qg) * ATTN_SCALE, tile4(kg),
                           *rope, bd, seq)
    dil = [_dilated_group(a, dl, batch, seq) for a, (_, dl) in zip((d0, d1, d2), DIL_PATTERNS)]
    osb = _stick_breaking(sbp, batch, seq)

    merge_weights = (w_bg.astype(_BF16), w_bd.astype(_BF16), w_bs.astype(_BF16), w_out.astype(_BF16))
    tp = t // MOE_PARTS
    ch = EXPERT_CHUNK
    win = SC_INDEX_WINDOW
    n_chunks = -(-2 * tp // ch) + N_EXPERTS
    chunk_start = jnp.arange(n_chunks, dtype=jnp.int32) * ch
    out = None
    for part in range(MOE_PARTS):
        x1, h2p, rinfo, er, cnt = _merge(
            x2, mod3, g_mix.reshape(1, d), g_ffn.reshape(1, d), [o for o, _ in dil], [l for _, l in dil], osb,
            *merge_weights, wrh, wrl, br, ex, tri, seq, part)

        counts = cnt[0, EXPERT_LANE0:EXPERT_LANE0 + N_EXPERTS].astype(jnp.int32)
        padded = (counts + ch - 1) // ch * ch
        pend = jnp.cumsum(padded)
        pstart = pend - padded
        chunk_e = jnp.minimum(jnp.sum((pend[None, :] <= chunk_start[:, None]).astype(jnp.int32), axis=1),
                              N_EXPERTS - 1)
        n_used = (pend[-1:] // ch).astype(jnp.int32)
        begin = chunk_start[:, None]
        inside = (pstart[None, :] <= begin) & (begin < pend[None, :])
        n_valid = jnp.sum(jnp.where(inside, jnp.clip(counts[None, :] - (begin - pstart[None, :]), 0, ch), 0), axis=1)

        dest = _dest(pstart, er)
        xs = _sc_scatter(h2p, dest[0].reshape(tp // win, win), dest[1].reshape(tp // win, win), n_chunks * ch)
        ys = _experts(chunk_e, n_valid, n_used, xs, w_eg, w_eu, w_ed)
        gathered = _sc_gather(ys, dest.reshape(2 * tp // win, win))
        out = _combine(x1, rinfo, mod3, gathered, seq, part, out)
    return out.reshape(batch, seq, d)


def kernel(x, c, positions, w_ada, b_ada, g_norm_mix, g_norm_ffn, w_in, w_branch_gate, q_norm_g, k_norm_g,
           w_branch_dil, w_branch_sb, w_out, w_router_group, b_router_group, w_router_expert, b_router_expert,
           w_expert_gate, w_expert_up, w_expert_down):
    for l in range(w_ada.shape[0]):
        mod = _ada(c, w_ada[l], b_ada[l])
        x = _layer(x, mod, positions, g_norm_mix[l], g_norm_ffn[l], w_in[l], w_branch_gate[l], q_norm_g[l],
                   k_norm_g[l], w_branch_dil[l], w_branch_sb[l], w_out[l], w_router_group[l], b_router_group[l],
                   w_router_expert[l], b_router_expert[l], w_expert_gate[l], w_expert_up[l], w_expert_down[l])
    return x
```

```python
import functools

import jax
import jax.numpy as jnp
from jax import lax
from jax.experimental import pallas as pl
from jax.experimental.pallas import tpu as pltpu
from jax.experimental.pallas import tpu_sc as plsc

D_MODEL = 1024
HEAD_DIM = 64
DIL_PATTERNS = ((128, 1), (512, 4), (2048, 16))
HEADS_PER_GROUP = 4
GROUP_WIDTH = HEADS_PER_GROUP * HEAD_DIM
N_DIL_GROUPS = len(DIL_PATTERNS)
DIL_WIDTH = N_DIL_GROUPS * GROUP_WIDTH
QKV_WIDTH = 3 * DIL_WIDTH + 3 * GROUP_WIDTH
WINDOW_KEYS = 128
ROPE_THETA = 500000.0
ROPE_DIM = HEAD_DIM // 4
N_GROUPS = 4
EXPERTS_PER_GROUP = 8
N_EXPERTS = N_GROUPS * EXPERTS_PER_GROUP
D_EXPERT = 512
RMS_EPS = 1e-6
ATTN_SCALE = HEAD_DIM ** -0.5

LANES = 128
ROUTER_LANES = LANES
EXPERT_LANE0 = N_GROUPS
LSE_SEG = LANES // HEADS_PER_GROUP
NEG_BIG = -1e30
SB_DEAD_LOG = -120.0
SB_HEAD_BLOCKS = 3
SB_CHAINS = 4

ROW_TILE = 512
QBLK = 128
DIL_QTILE = 512
EXPERT_CHUNK = 512
COMBINE_TILE = 512
DEST_TILE = 8192
SC_CORES = 2
SC_SUBCORES = 16
SC_WORKERS = SC_CORES * SC_SUBCORES
SC_INDEX_WINDOW = 128
MOE_PARTS = 2
VMEM_LIMIT = 48 * 1024 * 1024

_BF16 = jnp.bfloat16
_F32 = jnp.float32
_NT = (((1,), (1,)), ((), ()))


def _dot(a, b):
    return jnp.dot(a, b, preferred_element_type=_F32)


def _dot_nt(a, b):
    return lax.dot_general(a, b, _NT, preferred_element_type=_F32)


def _split(a):
    hi = a.astype(_BF16)
    lo = (a - hi.astype(_F32)).astype(_BF16)
    return hi, lo


def _dot3(a, b):
    ah, al = _split(a)
    bh, bl = _split(b)
    return _dot(ah, bh) + (_dot(ah, bl) + _dot(al, bh))


def _rms_mod(x, g, scale, shift):
    y = x * lax.rsqrt(jnp.mean(x * x, axis=-1, keepdims=True) + RMS_EPS)
    return y * g * (1.0 + scale) + shift


def _params(*sem):
    return pltpu.CompilerParams(dimension_semantics=sem, vmem_limit_bytes=VMEM_LIMIT)


def _ada_body(c_ref, w_ref, b_ref, o_ref):
    c = c_ref[...]
    o_ref[...] = _dot3(c * jax.nn.sigmoid(c), w_ref[...]) + b_ref[...]


def _ada(c, w_ada, b_ada):
    b, d = c.shape
    n = w_ada.shape[1]
    rows = -(-b // 16) * 16
    cp = jnp.zeros((rows, d), _F32).at[:b].set(c)
    nt = 1536
    out = pl.pallas_call(
        _ada_body,
        out_shape=jax.ShapeDtypeStruct((rows, n), _F32),
        grid=(n // nt,),
        in_specs=[pl.BlockSpec((rows, d), lambda j: (0, 0)),
                  pl.BlockSpec((d, nt), lambda j: (0, j)),
                  pl.BlockSpec((1, nt), lambda j: (0, j))],
        out_specs=pl.BlockSpec((rows, nt), lambda j: (0, j)),
        compiler_params=_params("arbitrary"),
        name="ada",
    )(cp, w_ada, b_ada.reshape(1, n))
    return out[:b]


def _qkv_body(x_ref, mod_ref, g_ref, w_ref, qg_ref, kg_ref, c_ref, s1_ref, s2_ref, bd_ref,
              o0_ref, o1_ref, o2_ref, osb_ref, st_ref, acc_ref):
    d = D_MODEL
    tm = x_ref.shape[0]
    h = _rms_mod(x_ref[...], g_ref[...], mod_ref[:, d:2 * d], mod_ref[:, 0:d])
    hb = h.astype(_BF16)
    cc = jnp.concatenate([c_ref[...]] * 2, axis=1)
    s1 = jnp.concatenate([s1_ref[...]] * 2, axis=1)
    s2 = jnp.concatenate([s2_ref[...]] * 2, axis=1)
    bd = bd_ref[...]
    gw = GROUP_WIDTH

    def normed_rotated(acc, gain):
        ms = _dot((acc * acc).astype(_BF16), bd)
        y = acc * lax.rsqrt(ms + RMS_EPS) * gain
        return y * cc + pltpu.roll(y, gw - ROPE_DIM // 2, 1) * s1 + pltpu.roll(y, ROPE_DIM // 2, 1) * s2

    def store(o_ref, dil, part, y):
        if dil == 1:
            o_ref[:, part * gw:(part + 1) * gw] = y.astype(_BF16)
            return
        for s in range(gw // LANES):
            st_ref[s] = y[:, s * LANES:(s + 1) * LANES]
        for r in range(dil):
            for s in range(gw // LANES):
                col0 = (3 * r + part) * gw + s * LANES
                o_ref[:, col0:col0 + LANES] = st_ref[s, pl.ds(r, tm // dil, stride=dil), :].astype(_BF16)

    def project(col0):
        return _dot(hb, w_ref[:, col0:col0 + gw])

    outs = (o0_ref, o1_ref, o2_ref)
    dils = [dl for _, dl in DIL_PATTERNS]
    normed = [(g, part, part * DIL_WIDTH + g * gw) for g in range(N_DIL_GROUPS) for part in (0, 1)]
    for n, (_, _, col0) in enumerate(normed):
        acc_ref[n] = project(col0)
    plain = [("v", g) for g in range(N_DIL_GROUPS)] + [("sb", part) for part in range(3)]
    for n, (kind, j) in enumerate(plain):
        if kind == "v":
            store(outs[j], dils[j], 2, project(2 * DIL_WIDTH + j * gw))
        else:
            acc = project(3 * DIL_WIDTH + j * gw)
            osb_ref[:, j * gw:(j + 1) * gw] = (acc * ATTN_SCALE if j == 0 else acc).astype(_BF16)
        g, part, _ = normed[n]
        gain = qg_ref[...] if part == 0 else kg_ref[...]
        store(outs[g], dils[g], part, normed_rotated(acc_ref[n], gain))


def _qkv(x2, mod3, g_mix, w_in, qg, kg, rope_c, rope_s1, rope_s2, bd, seq):
    t, d = x2.shape
    tm = ROW_TILE
    per_b = seq // tm
    row = lambda i: (i, 0)
    const = lambda i: (0, 0)
    width = 3 * GROUP_WIDTH
    dils = [dl for _, dl in DIL_PATTERNS] + [1]
    return pl.pallas_call(
        _qkv_body,
        out_shape=[jax.ShapeDtypeStruct((t // dl, dl * width), _BF16) for dl in dils],
        grid=(t // tm,),
        in_specs=[pl.BlockSpec((tm, d), row),
                  pl.BlockSpec((None, 1, mod3.shape[2]), lambda i: (i // per_b, 0, 0)),
                  pl.BlockSpec((1, d), const),
                  pl.BlockSpec(w_in.shape, const),
                  pl.BlockSpec((1, GROUP_WIDTH), const),
                  pl.BlockSpec((1, GROUP_WIDTH), const),
                  pl.BlockSpec((tm, LANES), row),
                  pl.BlockSpec((tm, LANES), row),
                  pl.BlockSpec((tm, LANES), row),
                  pl.BlockSpec(bd.shape, const)],
        out_specs=[pl.BlockSpec((tm // dl, dl * width), row) for dl in dils],
        scratch_shapes=[pltpu.VMEM((GROUP_WIDTH // LANES, tm, LANES), _F32),
                        pltpu.VMEM((2 * N_DIL_GROUPS, tm, GROUP_WIDTH), _F32)],
        compiler_params=_params("arbitrary"),
        name="qkv",
    )(x2, mod3, g_mix, w_in, qg, kg, rope_c, rope_s1, rope_s2, bd)


def _dil_body(q_ref, kp_ref, kc_ref, vp_ref, vc_ref, o_ref, lse_ref, kf_ref, vf_ref):
    tq = q_ref.shape[0]
    first = pl.program_id(2) == 0
    kf_ref[0:QBLK, :] = kp_ref[...]
    kf_ref[QBLK:, :] = kc_ref[...]
    vf_ref[0:QBLK, :] = vp_ref[...]
    vf_ref[QBLK:, :] = vc_ref[...]
    nh = HEADS_PER_GROUP
    row = lax.broadcasted_iota(jnp.int32, (nh * QBLK, 2 * QBLK), 0) & (QBLK - 1)
    col = lax.broadcasted_iota(jnp.int32, (nh * QBLK, 2 * QBLK), 1)
    band = (col >= row) & (col <= row + WINDOW_KEYS)
    lane = lax.broadcasted_iota(jnp.int32, (1, GROUP_WIDTH), 1)
    slane = lax.broadcasted_iota(jnp.int32, (1, LANES), 1)
    head_masks = [(lane >= h * HEAD_DIM) & (lane < (h + 1) * HEAD_DIM) for h in range(nh)]
    subs = range(tq // QBLK)
    rows = [slice(j * QBLK, (j + 1) * QBLK) for j in subs]
    window = [slice(j * QBLK, (j + 2) * QBLK) for j in subs]

    def stack(qj):
        return jnp.concatenate([jnp.where(hm, qj, jnp.zeros_like(qj)) for hm in head_masks], axis=0)

    valid = [band & ((col >= QBLK) | jnp.logical_not(first)) if j == 0 else band for j in subs]
    s = [jnp.where(valid[j], _dot_nt(stack(q_ref[rows[j], :]), kf_ref[window[j], :]), NEG_BIG) for j in subs]
    m = [jnp.max(s[j], axis=1, keepdims=True) for j in subs]
    p = [jnp.exp(s[j] - m[j]) for j in subs]
    l = [jnp.sum(p[j], axis=1, keepdims=True) for j in subs]
    o_all = [_dot(p[j].astype(_BF16), vf_ref[window[j], :]) / l[j] for j in subs]
    for j in subs:
        lse_all = m[j] + jnp.log(l[j])
        o_acc = jnp.zeros((QBLK, GROUP_WIDTH), _F32)
        lse_t = jnp.zeros((QBLK, LANES), _F32)
        for h, hm in enumerate(head_masks):
            o_acc = jnp.where(hm, o_all[j][h * QBLK:(h + 1) * QBLK, :], o_acc)
            sm = (slane >= h * LSE_SEG) & (slane < (h + 1) * LSE_SEG)
            lse_t = jnp.where(sm, lse_all[h * QBLK:(h + 1) * QBLK, :], lse_t)
        o_ref[rows[j], :] = o_acc.astype(_BF16)
        lse_ref[rows[j], :] = lse_t


def _dilated_group(view2, dil, batch, seq):
    sd = seq // dil
    tq = min(DIL_QTILE, sd)
    per = tq // QBLK
    gw = GROUP_WIDTH
    view = view2.reshape(batch, sd, dil * 3 * gw)
    cur = lambda part: pl.BlockSpec((None, tq, gw), lambda b, r, i: (b, i, 3 * r + part))
    prev = lambda part: pl.BlockSpec((None, QBLK, gw),
                                     lambda b, r, i: (b, jnp.maximum(i * per - 1, 0), 3 * r + part))
    o, lse = pl.pallas_call(
        _dil_body,
        out_shape=[jax.ShapeDtypeStruct((batch, sd, dil * gw), _BF16),
                   jax.ShapeDtypeStruct((batch, sd, dil * LANES), _F32)],
        grid=(batch, dil, sd // tq),
        in_specs=[cur(0), prev(1), cur(1), prev(2), cur(2)],
        out_specs=[pl.BlockSpec((None, tq, gw), lambda b, r, i: (b, i, r)),
                   pl.BlockSpec((None, tq, LANES), lambda b, r, i: (b, i, r))],
        scratch_shapes=[pltpu.VMEM((tq + QBLK, gw), _BF16), pltpu.VMEM((tq + QBLK, gw), _BF16)],
        compiler_params=_params("arbitrary", "arbitrary", "arbitrary"),
        name=f"dil{dil}",
    )(view, view, view, view, view)
    return o.reshape(batch * sd, dil * gw), lse.reshape(batch * sd, dil * LANES)


def _sb_body(q_ref, k_ref, v_ref, o_ref, carry_ref, acc_ref, qs_ref):
    step_id = pl.program_id(1)
    chains = range(SB_CHAINS)
    blk = [step_id * SB_CHAINS + c for c in chains]
    nh = HEADS_PER_GROUP
    row = lax.broadcasted_iota(jnp.int32, (nh * QBLK, QBLK), 0) & (QBLK - 1)
    col = lax.broadcasted_iota(jnp.int32, (nh * QBLK, QBLK), 1)
    strict = col < row
    ur = lax.broadcasted_iota(jnp.int32, (2 * QBLK, QBLK), 0) & (QBLK - 1)
    uc = lax.broadcasted_iota(jnp.int32, (2 * QBLK, QBLK), 1)
    u = jnp.where(ur > uc, 1.0, 0.0).astype(_BF16)
    lane = lax.broadcasted_iota(jnp.int32, (1, GROUP_WIDTH), 1)
    head_masks = [(lane >= h * HEAD_DIM) & (lane < (h + 1) * HEAD_DIM) for h in range(nh)]
    for c in chains:
        q = q_ref[c * QBLK:(c + 1) * QBLK, :]
        qs_ref[c] = jnp.concatenate([jnp.where(hm, q, jnp.zeros_like(q)) for hm in head_masks], axis=0)

    def softplus(z):
        return jnp.maximum(z, 0.0) + jnp.log(1.0 + jnp.exp(-jnp.abs(z)))

    def later_keys(log_1m):
        hi, lo = _split(log_1m)
        return _dot(jnp.concatenate([hi, lo], axis=1), u)

    nb = SB_HEAD_BLOCKS
    cols = [slice(j * QBLK, (j + 1) * QBLK) for j in range(nb)]
    kbs = [[blk[c] - (nb - 1) + j for j in range(nb)] for c in chains]
    starts = [[pl.multiple_of(jnp.maximum(kb, 0) * QBLK, QBLK) for kb in kbs[c]] for c in chains]
    keep = [[strict if j == nb - 1 else (kbs[c][j] >= 0) for j in range(nb)] for c in chains]
    z = [_dot_nt(qs_ref[c], jnp.concatenate([k_ref[pl.ds(s, QBLK), :] for s in starts[c]], axis=0)) for c in chains]
    sp = [softplus(z[c]) for c in chains]
    log_1m = [[jnp.where(keep[c][j], -sp[c][:, cols[j]], 0.0) for j in range(nb)] for c in chains]
    totals = [[jnp.sum(l, axis=1, keepdims=True) for l in log_1m[c]] for c in chains]
    later = [[later_keys(log_1m[c][j]) for j in range(nb)] for c in chains]
    for c in chains:
        a_blocks = []
        after = jnp.zeros_like(totals[c][0])
        for j in reversed(range(nb)):
            a = jnp.exp((z[c][:, cols[j]] - sp[c][:, cols[j]]) + later[c][j] + after)
            a_blocks.insert(0, jnp.where(keep[c][j], a, 0.0).astype(_BF16))
            after = after + totals[c][j]
        acc_ref[c] = _dot(jnp.concatenate(a_blocks, axis=1),
                          jnp.concatenate([v_ref[pl.ds(s, QBLK), :] for s in starts[c]], axis=0))
        carry_ref[c] = after

    for c in chains:
        def tile(kb, c=c):
            start = pl.multiple_of(kb * QBLK, QBLK)
            z = _dot_nt(qs_ref[c], k_ref[pl.ds(start, QBLK), :])
            sp = softplus(z)
            log_1m = -sp
            a = jnp.exp((z - sp) + later_keys(log_1m) + carry_ref[c])
            acc_ref[c] += _dot(a.astype(_BF16), v_ref[pl.ds(start, QBLK), :])
            carry = carry_ref[c] + jnp.sum(log_1m, axis=1, keepdims=True)
            carry_ref[c] = carry
            return jnp.max(carry)

        def cond(st):
            return (st[0] >= 0) & (st[1] > SB_DEAD_LOG)

        def step(st, tile=tile):
            return st[0] - 1, tile(st[0])

        lax.while_loop(cond, step, (blk[c] - nb, jnp.max(carry_ref[c])))
        out = jnp.zeros((QBLK, GROUP_WIDTH), _F32)
        for h, hm in enumerate(head_masks):
            out = jnp.where(hm, acc_ref[c, h * QBLK:(h + 1) * QBLK, :], out)
        o_ref[c * QBLK:(c + 1) * QBLK, :] = out.astype(_BF16)


def _stick_breaking(arr, batch, seq):
    gw = GROUP_WIDTH
    qt = SB_CHAINS * QBLK
    view = arr.reshape(batch, seq, 3 * gw)
    o = pl.pallas_call(
        _sb_body,
        out_shape=jax.ShapeDtypeStruct((batch, seq, gw), _BF16),
        grid=(batch, seq // qt),
        in_specs=[pl.BlockSpec((None, qt, gw), lambda b, i: (b, i, 0)),
                  pl.BlockSpec((None, seq, gw), lambda b, i: (b, 0, 1)),
                  pl.BlockSpec((None, seq, gw), lambda b, i: (b, 0, 2))],
        out_specs=pl.BlockSpec((None, qt, gw), lambda b, i: (b, i, 0)),
        scratch_shapes=[pltpu.VMEM((SB_CHAINS, HEADS_PER_GROUP * QBLK, 1), _F32),
                        pltpu.VMEM((SB_CHAINS, HEADS_PER_GROUP * QBLK, gw), _F32),
                        pltpu.VMEM((SB_CHAINS, HEADS_PER_GROUP * QBLK, gw), _BF16)],
        compiler_params=_params("arbitrary", "arbitrary"),
        name="sb",
    )(view, view, view)
    return o.reshape(batch * seq, gw)


def _merge_body(x_ref, mod_ref, g1_ref, g2_ref, o0_ref, o1_ref, o2_ref, l0_ref, l1_ref, l2_ref, osb_ref,
                wbg_ref, wbd_ref, wbs_ref, wout_ref, wrh_ref, wrl_ref, br_ref, ex_ref, tri_ref,
                x1_ref, h2p_ref, rinfo_ref, er_ref, cnt_out_ref, cnt_ref, os1_ref, os2_ref, ls1_ref, ls2_ref):
    d = D_MODEL
    tm = x_ref.shape[0]
    x = x_ref[...]
    hb = _rms_mod(x, g1_ref[...], mod_ref[:, d:2 * d], mod_ref[:, 0:d]).astype(_BF16)

    def natural(ref, st_ref, dil):
        if dil == 1:
            return ref[...].astype(_F32)
        slabs = st_ref.shape[0]
        for r in range(dil):
            for s in range(slabs):
                col0 = (r * slabs + s) * LANES
                st_ref[s, pl.ds(r, tm // dil, stride=dil), :] = ref[:, col0:col0 + LANES].astype(_F32)
        return jnp.concatenate([st_ref[s] for s in range(slabs)], axis=1)

    dils = [dl for _, dl in DIL_PATTERNS]
    o_nat = [natural(r, s, dl) for r, s, dl in zip((o0_ref, o1_ref, o2_ref), (None, os1_ref, os2_ref), dils)]
    l0, l1, l2 = [natural(r, s, dl) for r, s, dl in zip((l0_ref, l1_ref, l2_ref), (None, ls1_ref, ls2_ref), dils)]

    lmax = jnp.maximum(jnp.maximum(l0, l1), l2)
    e0, e1, e2 = jnp.exp(l0 - lmax), jnp.exp(l1 - lmax), jnp.exp(l2 - lmax)
    inv = 1.0 / (e0 + e1 + e2)
    ex = ex_ref[...]

    def widen(w):
        hi, lo = _split(w)
        return _dot(jnp.concatenate([hi, lo], axis=1), ex)

    w_groups = [widen(e * inv) for e in (e0, e1, e2)]
    gate_dil = jax.nn.sigmoid(_dot(hb, wbg_ref[:, :d]))
    o_dil = w_groups[0] * o_nat[0] + w_groups[1] * o_nat[1] + w_groups[2] * o_nat[2]
    branch_dil = _dot(o_dil.astype(_BF16), wbd_ref[...])
    branch_sb = _dot(osb_ref[...], wbs_ref[...])
    gate_sb = jax.nn.sigmoid(_dot(hb, wbg_ref[:, d:]))
    merged = gate_dil * branch_dil + gate_sb * branch_sb
    x1 = x + mod_ref[:, 2 * d:3 * d] * _dot(merged.astype(_BF16), wout_ref[...])
    x1_ref[...] = x1

    h2 = _rms_mod(x1, g2_ref[...], mod_ref[:, 4 * d:5 * d], mod_ref[:, 3 * d:4 * d])
    h2p_ref[...] = _pack_halves(h2)

    hh, hl = _split(h2)
    logits = _dot(hh, wrh_ref[...]) + (_dot(hl, wrh_ref[...]) + _dot(hh, wrl_ref[...])) + br_ref[...]
    lane =lax.broadcasted_iota(jnp.int32, (tm, ROUTER_LANES), 1).astype(_F32)
    far = float(ROUTER_LANES)

    def top(vals):
        m = jnp.max(vals, axis=1, keepdims=True)
        return m, jnp.min(jnp.where(vals == m, lane, far), axis=1, keepdims=True)

    is_group = lane < N_GROUPS
    mg, gsel = top(jnp.where(is_group, logits, NEG_BIG))
    pg_top = 1.0 / jnp.sum(jnp.where(is_group, jnp.exp(logits - mg), 0.0), axis=1, keepdims=True)
    lane0 = EXPERT_LANE0 + EXPERTS_PER_GROUP * gsel
    le = jnp.where((lane >= lane0) & (lane < lane0 + EXPERTS_PER_GROUP), logits, NEG_BIG)
    m1, i1 = top(le)
    m2, i2 = top(jnp.where(lane == i1, NEG_BIG, le))
    t2 = jnp.exp(m2 - m1)
    w0 = pg_top / (1.0 + t2)
    w1 = pg_top * t2 / (1.0 + t2)

    @pl.when(pl.program_id(0) == 0)
    def _():
        cnt_ref[...] = jnp.zeros_like(cnt_ref)

    sel0, sel1 = lane == i1, lane == i2
    onehot = jnp.where(sel0 | sel1, 1.0, 0.0)
    before = _dot(tri_ref[...], onehot.astype(_BF16)) + cnt_ref[0:1, :]
    r0 = jnp.sum(jnp.where(sel0, before, 0.0), axis=1, keepdims=True)
    r1 = jnp.sum(jnp.where(sel1, before, 0.0), axis=1, keepdims=True)
    cnt_ref[...] += jnp.sum(onehot, axis=0, keepdims=True)
    cnt_out_ref[...] = cnt_ref[...]

    cols = (i1 - EXPERT_LANE0, i2 - EXPERT_LANE0, r0, r1, w0, w1)
    rinfo = jnp.zeros((tm, ROUTER_LANES), _F32)
    for c, v in enumerate(cols):
        rinfo = jnp.where(lane == float(c), v, rinfo)
    rinfo_ref[...] = rinfo
    er_ref[...] = jnp.transpose(rinfo)[0:8, :].astype(jnp.int32)


def _merge(x2, mod3, g1, g2, outs, lses, osb, wbg, wbd, wbs, wout, wrh, wrl, br, ex, tri, seq, part):
    d = x2.shape[1]
    tm = ROW_TILE
    per_b = seq // tm
    t = x2.shape[0] // MOE_PARTS
    first = part * (t // tm)
    src = lambda i: (i + first, 0)
    row = lambda i: (i, 0)
    const = lambda i: (0, 0)
    full = lambda a: pl.BlockSpec(a.shape, const)
    gw = GROUP_WIDTH
    dils = [dl for _, dl in DIL_PATTERNS]
    return pl.pallas_call(
        _merge_body,
        out_shape=[jax.ShapeDtypeStruct((t, d), _F32),
                   jax.ShapeDtypeStruct((t, d // 2), jnp.uint32),
                   jax.ShapeDtypeStruct((t, ROUTER_LANES), _F32),
                   jax.ShapeDtypeStruct((8, t), jnp.int32),
                   jax.ShapeDtypeStruct((8, ROUTER_LANES), _F32)],
        grid=(t // tm,),
        in_specs=[pl.BlockSpec((tm, d), src),
                  pl.BlockSpec((None, 1, mod3.shape[2]), lambda i: ((i + first) // per_b, 0, 0)),
                  full(g1), full(g2)]
                 + [pl.BlockSpec((tm // dl, dl * gw), src) for dl in dils]
                 + [pl.BlockSpec((tm // dl, dl * LANES), src) for dl in dils]
                 + [pl.BlockSpec((tm, gw), src)]
                 + [full(a) for a in (wbg, wbd, wbs, wout, wrh, wrl, br, ex, tri)],
        out_specs=[pl.BlockSpec((tm, d), row),
                   pl.BlockSpec((tm, d // 2), row),
                   pl.BlockSpec((tm, ROUTER_LANES), row),
                   pl.BlockSpec((8, tm), lambda i: (0, i)),
                   pl.BlockSpec((8, ROUTER_LANES), const)],
        scratch_shapes=[pltpu.VMEM((8, ROUTER_LANES), _F32),
                        pltpu.VMEM((gw // LANES, tm, LANES), _F32), pltpu.VMEM((gw // LANES, tm, LANES), _F32),
                        pltpu.VMEM((1, tm, LANES), _F32), pltpu.VMEM((1, tm, LANES), _F32)],
        compiler_params=_params("arbitrary"),
        name="merge",
    )(x2, mod3, g1, g2, *outs, *lses, osb, wbg, wbd, wbs, wout, wrh, wrl, br, ex, tri)


def _dest_body(ps_ref, er_ref, d_ref):
    e = er_ref[0:2, :]
    start = jnp.zeros_like(e)
    for x in range(N_EXPERTS):
        start = jnp.where(e == x, ps_ref[x], start)
    d_ref[...] = start + er_ref[2:4, :]


def _dest(pstart, er):
    t = er.shape[1]
    tw = min(DEST_TILE, t)
    return pl.pallas_call(
        _dest_body,
        out_shape=jax.ShapeDtypeStruct((2, t), jnp.int32),
        grid_spec=pltpu.PrefetchScalarGridSpec(
            num_scalar_prefetch=1,
            grid=(t // tw,),
            in_specs=[pl.BlockSpec((8, tw), lambda i, ps: (0, i))],
            out_specs=pl.BlockSpec((2, tw), lambda i, ps: (0, i))),
        compiler_params=_params("arbitrary"),
        name="dest",
    )(pstart, er)


def _sc_mesh():
    return plsc.VectorSubcoreMesh(core_axis_name="core", subcore_axis_name="subcore",
                                  num_cores=SC_CORES, num_subcores=SC_SUBCORES)


def _sc_worker():
    return lax.axis_index("subcore") * SC_CORES + lax.axis_index("core")


def _sc_scatter(x, idx0, idx1, n_slots):
    chunks = idx0.shape[0]
    per = chunks // SC_WORKERS
    win = idx0.shape[1]

    @functools.partial(
        pl.kernel, mesh=_sc_mesh(), out_type=jax.ShapeDtypeStruct((n_slots, x.shape[1]), x.dtype),
        scratch_types=[pltpu.VMEM((1, win), jnp.int32), pltpu.VMEM((1, win), jnp.int32),
                       pltpu.VMEM((win, x.shape[1]), x.dtype), pltpu.SemaphoreType.DMA],
        name="sc_scatter")
    def run(x_hbm, i0_hbm, i1_hbm, o_hbm, i0_v, i1_v, rows_v, sem):
        wid = _sc_worker()

        @pl.loop(0, per)
        def _(j):
            c = wid * per + j
            pltpu.sync_copy(i0_hbm.at[pl.ds(c, 1)], i0_v)
            pltpu.sync_copy(i1_hbm.at[pl.ds(c, 1)], i1_v)
            pltpu.sync_copy(x_hbm.at[pl.ds(c * win, win)], rows_v)
            first = pltpu.async_copy(rows_v, o_hbm.at[i0_v.at[0]], sem)
            second = pltpu.async_copy(rows_v, o_hbm.at[i1_v.at[0]], sem)
            first.wait()
            second.wait()

    return run(x, idx0, idx1)


def _sc_gather(table, idx):
    chunks, win = idx.shape
    per = chunks // SC_WORKERS

    @functools.partial(
        pl.kernel, mesh=_sc_mesh(), out_type=jax.ShapeDtypeStruct((chunks * win, table.shape[1]), table.dtype),
        scratch_types=[pltpu.VMEM((1, win), jnp.int32), pltpu.VMEM((win, table.shape[1]), table.dtype),
                       pltpu.SemaphoreType.DMA],
        name="sc_gather")
    def run(t_hbm, i_hbm, o_hbm, i_v, rows_v, sem):
        wid = _sc_worker()

        @pl.loop(0, per)
        def _(j):
            c = wid * per + j
            pltpu.sync_copy(i_hbm.at[pl.ds(c, 1)], i_v)
            pltpu.async_copy(t_hbm.at[i_v.at[0]], rows_v, sem).wait()
            pltpu.sync_copy(rows_v, o_hbm.at[pl.ds(c * win, win)])

    return run(table, idx)


def _pack_halves(a):
    h = a.shape[1] // 2
    lo = lax.bitcast_convert_type(a[:, :h].astype(_BF16).astype(_F32), jnp.uint32) >> 16
    hi = lax.bitcast_convert_type(a[:, h:].astype(_BF16).astype(_F32), jnp.uint32) & jnp.uint32(0xFFFF0000)
    return lo | hi


def _unpack_halves(w):
    return jnp.concatenate(
        [lax.bitcast_convert_type(w << 16, _F32), lax.bitcast_convert_type(w & jnp.uint32(0xFFFF0000), _F32)], axis=1)


def _experts_body(ce_ref, nv_ref, nu_ref, seg_ref, nxt_ref, xs_ref, wg_ref, wu_ref, wd_ref, ys_ref,
                  wgb_ref, wub_ref, wdb_ref, wgf_ref, wuf_ref, wdf_ref, sem):
    c = pl.program_id(0)

    def fetch(expert, slot):
        return [pltpu.make_async_copy(src.at[expert], dst.at[slot], sem.at[slot])
                for src, dst in ((wg_ref, wgf_ref), (wu_ref, wuf_ref), (wd_ref, wdf_ref))]

    @pl.when(c < nu_ref[0])
    def _():
        @pl.when(seg_ref[c] >= 0)
        def _():
            slot = seg_ref[c] & 1

            @pl.when(c == 0)
            def _():
                for copy in fetch(ce_ref[0], 0):
                    copy.start()

            for copy in fetch(ce_ref[c], slot):
                copy.wait()
            wgb_ref[...] = wgf_ref[slot].astype(_BF16)
            wub_ref[...] = wuf_ref[slot].astype(_BF16)
            wdb_ref[...] = wdf_ref[slot].astype(_BF16)

            @pl.when(nxt_ref[c] >= 0)
            def _():
                for copy in fetch(nxt_ref[c], 1 - slot):
                    copy.start()

        row = lax.broadcasted_iota(jnp.int32, xs_ref.shape, 0)
        x = _unpack_halves(jnp.where(row < nv_ref[c], xs_ref[...], jnp.uint32(0))).astype(_BF16)
        half = x.shape[0] // 2
        ups = [(_dot(x[r:r + half], wgb_ref[...]), _dot(x[r:r + half], wub_ref[...])) for r in (0, half)]
        for (g, u), r in zip(ups, (0, half)):
            hmid = (g * jax.nn.sigmoid(g)) * u
            ys_ref[r:r + half, :] = _pack_halves(_dot(hmid.astype(_BF16), wdb_ref[...]))


def _experts(chunk_e, n_valid, n_used, seg, nxt, xs, wg, wu, wd):
    n_slots, w = xs.shape
    ch = EXPERT_CHUNK
    d, de = wg.shape[1], wg.shape[2]
    slot = lambda c, ce, nv, nu, sg, nx: (jnp.minimum(c, nu[0] - 1), 0)
    hbm = pl.BlockSpec(memory_space=pl.ANY)
    return pl.pallas_call(
        _experts_body,
        out_shape=jax.ShapeDtypeStruct((n_slots, d // 2), jnp.uint32),
        grid_spec=pltpu.PrefetchScalarGridSpec(
            num_scalar_prefetch=5,
            grid=(n_slots // ch,),
            in_specs=[pl.BlockSpec((ch, w), slot), hbm, hbm, hbm],
            out_specs=pl.BlockSpec((ch, d // 2), slot),
            scratch_shapes=[pltpu.VMEM((d, de), _BF16), pltpu.VMEM((d, de), _BF16), pltpu.VMEM((de, d), _BF16),
                            pltpu.VMEM((2, d, de), _F32), pltpu.VMEM((2, d, de), _F32), pltpu.VMEM((2, de, d), _F32),
                            pltpu.SemaphoreType.DMA((2,))]),
        compiler_params=_params("arbitrary"),
        name="experts",
    )(chunk_e, n_valid, n_used, seg, nxt, xs, wg, wu, wd)


def _combine_body(x1_ref, rinfo_ref, mod_ref, y0_ref, y1_ref, *rest):
    o_ref = rest[-1]
    y = rinfo_ref[:, 4:5] * _unpack_halves(y0_ref[...]) + rinfo_ref[:, 5:6] * _unpack_halves(y1_ref[...])
    o_ref[...] = x1_ref[...] + mod_ref[:, 5 * D_MODEL:6 * D_MODEL] * y


def _combine(x1, rinfo, mod3, gathered, seq, part, out_so_far):
    t, d = x1.shape
    tf = min(COMBINE_TILE, seq)
    per_b = seq // tf
    nt = t // tf
    first = part * nt
    in_specs = [pl.BlockSpec((tf, d), lambda i: (i, 0)),
                pl.BlockSpec((tf, ROUTER_LANES), lambda i: (i, 0)),
                pl.BlockSpec((None, 1, mod3.shape[2]), lambda i: ((i + first) // per_b, 0, 0)),
                pl.BlockSpec((tf, d // 2), lambda i: (i, 0)),
                pl.BlockSpec((tf, d // 2), lambda i: (i + nt, 0))]
    args = [x1, rinfo, mod3, gathered, gathered]
    aliases = {}
    if out_so_far is not None:
        in_specs.append(pl.BlockSpec(memory_space=pl.ANY))
        args.append(out_so_far)
        aliases = {len(args) - 1: 0}
    return pl.pallas_call(
        _combine_body,
        out_shape=jax.ShapeDtypeStruct((t * MOE_PARTS, d), _F32),
        grid=(nt,),
        in_specs=in_specs,
        out_specs=pl.BlockSpec((tf, d), lambda i: (i + first, 0)),
        input_output_aliases=aliases,
        compiler_params=_params("arbitrary"),
        name="combine",
    )(*args)


def _rope_tables(positions):
    half = ROPE_DIM // 2
    inv_freq = ROPE_THETA ** (-jnp.arange(0, ROPE_DIM, 2, dtype=_F32) / ROPE_DIM)
    lane = jnp.arange(LANES) % HEAD_DIM
    freq = jnp.where(lane < ROPE_DIM, inv_freq[lane % half], 0.0)
    ang = positions.reshape(-1).astype(_F32)[:, None] * freq[None, :]
    cos, sin = jnp.cos(ang), jnp.sin(ang)
    c = jnp.where(lane < ROPE_DIM, cos, 1.0)
    s1 = jnp.where(lane < half, -sin, 0.0)
    s2 = jnp.where((lane >= half) & (lane < ROPE_DIM), sin, 0.0)
    return c, s1, s2


def _layer(x, mod, positions, g_mix, g_ffn, w_in, w_bg, qg, kg, w_bd, w_bs, w_out, w_rg, b_rg, w_re, b_re,
           w_eg, w_eu, w_ed):
    batch, seq, d = x.shape
    t = batch * seq
    x2 = x.reshape(t, d)
    mod3 = mod.reshape(batch, 1, mod.shape[1])
    gw = GROUP_WIDTH

    lane = jnp.arange(gw)
    bd = jnp.where(lane[:, None] // HEAD_DIM == lane[None, :] // HEAD_DIM, 1.0 / HEAD_DIM, 0.0).astype(_BF16)
    ex = (jnp.arange(LANES)[:, None] == (lane[None, :] // HEAD_DIM) * LSE_SEG).astype(_BF16)
    ex = jnp.concatenate([ex, ex], axis=0)
    tri = (jnp.arange(ROW_TILE)[:, None] > jnp.arange(ROW_TILE)[None, :]).astype(_BF16)
    rope = _rope_tables(positions)
    tile4 = lambda g: jnp.tile(g.astype(_F32), HEADS_PER_GROUP).reshape(1, gw)
    wr = jnp.zeros((d, ROUTER_LANES), _F32).at[:, :N_GROUPS].set(w_rg).at[:, N_GROUPS:N_GROUPS + N_EXPERTS].set(w_re)
    wrh = wr.astype(_BF16)
    wrl = (wr - wrh.astype(_F32)).astype(_BF16)
    br = jnp.zeros((1, ROUTER_LANES), _F32).at[0, :N_GROUPS].set(b_rg).at[0, N_GROUPS:N_GROUPS + N_EXPERTS].set(b_re)

    d0, d1, d2, sbp = _qkv(x2, mod3, g_mix.reshape(1, d), w_in.astype(_BF16), tile4(qg) * ATTN_SCALE, tile4(kg),
                           *rope, bd, seq)
    dil = [_dilated_group(a, dl, batch, seq) for a, (_, dl) in zip((d0, d1, d2), DIL_PATTERNS)]
    osb = _stick_breaking(sbp, batch, seq)

    merge_weights = (w_bg.astype(_BF16), w_bd.astype(_BF16), w_bs.astype(_BF16), w_out.astype(_BF16))
    tp = t // MOE_PARTS
    ch = EXPERT_CHUNK
    win = SC_INDEX_WINDOW
    n_chunks = -(-2 * tp // ch) + N_EXPERTS
    chunk_start = jnp.arange(n_chunks, dtype=jnp.int32) * ch
    expert_ids = jnp.arange(N_EXPERTS, dtype=jnp.int32)
    out = None
    for part in range(MOE_PARTS):
        x1, h2p, rinfo, er, cnt = _merge(
            x2, mod3, g_mix.reshape(1, d), g_ffn.reshape(1, d), [o for o, _ in dil], [l for _, l in dil], osb,
            *merge_weights, wrh, wrl, br, ex, tri, seq, part)

        counts = cnt[0, EXPERT_LANE0:EXPERT_LANE0 + N_EXPERTS].astype(jnp.int32)
        padded = (counts + ch - 1) // ch * ch
        pend = jnp.cumsum(padded)
        pstart = pend - padded
        chunk_e = jnp.minimum(jnp.sum((pend[None, :] <= chunk_start[:, None]).astype(jnp.int32), axis=1),
                              N_EXPERTS - 1)
        n_used = (pend[-1:] // ch).astype(jnp.int32)
        begin = chunk_start[:, None]
        inside = (pstart[None, :] <= begin) & (begin < pend[None, :])
        n_valid = jnp.sum(jnp.where(inside, jnp.clip(counts[None, :] - (begin - pstart[None, :]), 0, ch), 0), axis=1)

        dest = _dest(pstart, er)
        xs = _sc_scatter(h2p, dest[0].reshape(tp // win, win), dest[1].reshape(tp // win, win), n_chunks * ch)
        first = (chunk_e != jnp.concatenate([jnp.full((1,), -1, jnp.int32), chunk_e[:-1]])) & (chunk_start < pend[-1])
        seg_no = jnp.cumsum(first.astype(jnp.int32)) - 1
        seg = jnp.where(first, seg_no, -1 - seg_no)
        later = (expert_ids[None, :] > expert_ids[:, None]) & (padded > 0)[None, :]
        next_expert = jnp.min(jnp.where(later, expert_ids[None, :], N_EXPERTS), axis=1)
        next_expert = jnp.where(next_expert == N_EXPERTS, -1, next_expert)
        nxt = jnp.sum(jnp.where(chunk_e[:, None] == expert_ids[None, :], next_expert[None, :], 0), axis=1)
        ys = _experts(chunk_e, n_valid, n_used, seg.astype(jnp.int32), nxt.astype(jnp.int32), xs, w_eg, w_eu, w_ed)
        gathered = _sc_gather(ys, dest.reshape(2 * tp // win, win))
        out = _combine(x1, rinfo, mod3, gathered, seq, part, out)
    return out.reshape(batch, seq, d)


def kernel(x, c, positions, w_ada, b_ada, g_norm_mix, g_norm_ffn, w_in, w_branch_gate, q_norm_g, k_norm_g,
           w_branch_dil, w_branch_sb, w_out, w_router_group, b_router_group, w_router_expert, b_router_expert,
           w_expert_gate, w_expert_up, w_expert_down):
    for l in range(w_ada.shape[0]):
        mod = _ada(c, w_ada[l], b_ada[l])
        x = _layer(x, mod, positions, g_norm_mix[l], g_norm_ffn[l], w_in[l], w_branch_gate[l], q_norm_g[l],
                   k_norm_g[l], w_branch_dil[l], w_branch_sb[l], w_out[l], w_router_group[l], b_router_group[l],
                   w_router_expert[l], b_router_expert[l], w_expert_gate[l], w_expert_up[l], w_expert_down[l])
    return x
```

```python
import functools

import jax
import jax.numpy as jnp
from jax import lax
from jax.experimental import pallas as pl
from jax.experimental.pallas import tpu as pltpu
from jax.experimental.pallas import tpu_sc as plsc

D_MODEL = 1024
HEAD_DIM = 64
DIL_PATTERNS = ((128, 1), (512, 4), (2048, 16))
HEADS_PER_GROUP = 4
GROUP_WIDTH = HEADS_PER_GROUP * HEAD_DIM
N_DIL_GROUPS = len(DIL_PATTERNS)
DIL_WIDTH = N_DIL_GROUPS * GROUP_WIDTH
QKV_WIDTH = 3 * DIL_WIDTH + 3 * GROUP_WIDTH
WINDOW_KEYS = 128
ROPE_THETA = 500000.0
ROPE_DIM = HEAD_DIM // 4
N_GROUPS = 4
EXPERTS_PER_GROUP = 8
N_EXPERTS = N_GROUPS * EXPERTS_PER_GROUP
D_EXPERT = 512
RMS_EPS = 1e-6
ATTN_SCALE = HEAD_DIM ** -0.5

LANES = 128
ROUTER_LANES = LANES
EXPERT_LANE0 = N_GROUPS
LSE_SEG = LANES // HEADS_PER_GROUP
NEG_BIG = -1e30
SB_DEAD_LOG = -120.0
SB_HEAD_BLOCKS = 3
SB_CHAINS = 4

ROW_TILE = 512
QBLK = 128
DIL_QTILE = 1024
EXPERT_CHUNK = 512
COMBINE_TILE = 1024
DEST_TILE = 8192
SC_CORES = 2
SC_SUBCORES = 16
SC_WORKERS = SC_CORES * SC_SUBCORES
SC_INDEX_WINDOW = 128
MOE_PARTS = 2
VMEM_LIMIT = 48 * 1024 * 1024

_BF16 = jnp.bfloat16
_F32 = jnp.float32
_NT = (((1,), (1,)), ((), ()))


def _dot(a, b):
    return jnp.dot(a, b, preferred_element_type=_F32)


def _dot_nt(a, b):
    return lax.dot_general(a, b, _NT, preferred_element_type=_F32)


def _split(a):
    hi = a.astype(_BF16)
    lo = (a - hi.astype(_F32)).astype(_BF16)
    return hi, lo


def _dot3(a, b):
    ah, al = _split(a)
    bh, bl = _split(b)
    return _dot(ah, bh) + (_dot(ah, bl) + _dot(al, bh))


def _rms_mod(x, g, scale, shift):
    y = x * lax.rsqrt(jnp.mean(x * x, axis=-1, keepdims=True) + RMS_EPS)
    return y * g * (1.0 + scale) + shift


def _params(*sem):
    return pltpu.CompilerParams(dimension_semantics=sem, vmem_limit_bytes=VMEM_LIMIT)


def _ada_body(c_ref, w_ref, b_ref, o_ref):
    c = c_ref[...]
    o_ref[...] = _dot3(c * jax.nn.sigmoid(c), w_ref[...]) + b_ref[...]


def _ada(c, w_ada, b_ada):
    b, d = c.shape
    n = w_ada.shape[1]
    rows = -(-b // 16) * 16
    cp = jnp.zeros((rows, d), _F32).at[:b].set(c)
    nt = 1536
    out = pl.pallas_call(
        _ada_body,
        out_shape=jax.ShapeDtypeStruct((rows, n), _F32),
        grid=(n // nt,),
        in_specs=[pl.BlockSpec((rows, d), lambda j: (0, 0)),
                  pl.BlockSpec((d, nt), lambda j: (0, j)),
                  pl.BlockSpec((1, nt), lambda j: (0, j))],
        out_specs=pl.BlockSpec((rows, nt), lambda j: (0, j)),
        compiler_params=_params("arbitrary"),
        name="ada",
    )(cp, w_ada, b_ada.reshape(1, n))
    return out[:b]


def _qkv_body(x_ref, mod_ref, g_ref, w_ref, qg_ref, kg_ref, trig_ref, bd_ref,
              o0_ref, o1_ref, o2_ref, osb_ref, st_ref, acc_ref):
    d = D_MODEL
    tm = x_ref.shape[0]
    h = _rms_mod(x_ref[...], g_ref[...], mod_ref[:, d:2 * d], mod_ref[:, 0:d])
    hb = h.astype(_BF16)
    half = ROPE_DIM // 2
    in_head = lax.broadcasted_iota(jnp.int32, (tm, LANES), 1) & (HEAD_DIM - 1)
    freq = in_head & (half - 1)
    trig = trig_ref[...]
    cos = jnp.take_along_axis(trig, freq, axis=1)
    sin = jnp.take_along_axis(trig, freq + half, axis=1)
    cc = jnp.where(in_head < ROPE_DIM, cos, 1.0)
    s1 = jnp.where(in_head < half, -sin, 0.0)
    s2 = jnp.where((in_head >= half) & (in_head < ROPE_DIM), sin, 0.0)
    cc, s1, s2 = [jnp.concatenate([a, a], axis=1) for a in (cc, s1, s2)]
    bd = bd_ref[...]
    gw = GROUP_WIDTH

    def normed_rotated(acc, gain):
        ms = _dot((acc * acc).astype(_BF16), bd)
        y = acc * lax.rsqrt(ms + RMS_EPS) * gain
        return y * cc + pltpu.roll(y, gw - ROPE_DIM // 2, 1) * s1 + pltpu.roll(y, ROPE_DIM // 2, 1) * s2

    def store(o_ref, dil, part, y):
        if dil == 1:
            o_ref[:, part * gw:(part + 1) * gw] = y.astype(_BF16)
            return
        for s in range(gw // LANES):
            st_ref[s] = y[:, s * LANES:(s + 1) * LANES]
        for r in range(dil):
            for s in range(gw // LANES):
                col0 = (3 * r + part) * gw + s * LANES
                o_ref[:, col0:col0 + LANES] = st_ref[s, pl.ds(r, tm // dil, stride=dil), :].astype(_BF16)

    def project(col0):
        return _dot(hb, w_ref[:, col0:col0 + gw])

    outs = (o0_ref, o1_ref, o2_ref)
    dils = [dl for _, dl in DIL_PATTERNS]
    normed = [(g, part, part * DIL_WIDTH + g * gw) for g in range(N_DIL_GROUPS) for part in (0, 1)]
    for n, (_, _, col0) in enumerate(normed):
        acc_ref[n] = project(col0)
    plain = [("v", g) for g in range(N_DIL_GROUPS)] + [("sb", part) for part in range(3)]
    for n, (kind, j) in enumerate(plain):
        if kind == "v":
            store(outs[j], dils[j], 2, project(2 * DIL_WIDTH + j * gw))
        else:
            acc = project(3 * DIL_WIDTH + j * gw)
            osb_ref[:, j * gw:(j + 1) * gw] = (acc * ATTN_SCALE if j == 0 else acc).astype(_BF16)
        g, part, _ = normed[n]
        gain = qg_ref[...] if part == 0 else kg_ref[...]
        store(outs[g], dils[g], part, normed_rotated(acc_ref[n], gain))


def _qkv(x2, mod3, g_mix, w_in, qg, kg, trig, bd, seq):
    t, d = x2.shape
    tm = ROW_TILE
    per_b = seq // tm
    row = lambda i: (i, 0)
    const = lambda i: (0, 0)
    width = 3 * GROUP_WIDTH
    dils = [dl for _, dl in DIL_PATTERNS] + [1]
    return pl.pallas_call(
        _qkv_body,
        out_shape=[jax.ShapeDtypeStruct((t // dl, dl * width), _BF16) for dl in dils],
        grid=(t // tm,),
        in_specs=[pl.BlockSpec((tm, d), row),
                  pl.BlockSpec((None, 1, mod3.shape[2]), lambda i: (i // per_b, 0, 0)),
                  pl.BlockSpec((1, d), const),
                  pl.BlockSpec(w_in.shape, const),
                  pl.BlockSpec((1, GROUP_WIDTH), const),
                  pl.BlockSpec((1, GROUP_WIDTH), const),
                  pl.BlockSpec((tm, LANES), row),
                  pl.BlockSpec(bd.shape, const)],
        out_specs=[pl.BlockSpec((tm // dl, dl * width), row) for dl in dils],
        scratch_shapes=[pltpu.VMEM((GROUP_WIDTH // LANES, tm, LANES), _F32),
                        pltpu.VMEM((2 * N_DIL_GROUPS, tm, GROUP_WIDTH), _F32)],
        compiler_params=_params("arbitrary"),
        name="qkv",
    )(x2, mod3, g_mix, w_in, qg, kg, trig, bd)


def _dil_body(q_ref, kp_ref, kc_ref, vp_ref, vc_ref, o_ref, lse_ref, kf_ref, vf_ref):
    tq = q_ref.shape[0]
    first = pl.program_id(2) == 0
    kf_ref[0:QBLK, :] = kp_ref[...]
    kf_ref[QBLK:, :] = kc_ref[...]
    vf_ref[0:QBLK, :] = vp_ref[...]
    vf_ref[QBLK:, :] = vc_ref[...]
    nh = HEADS_PER_GROUP
    row = lax.broadcasted_iota(jnp.int32, (nh * QBLK, 2 * QBLK), 0) & (QBLK - 1)
    col = lax.broadcasted_iota(jnp.int32, (nh * QBLK, 2 * QBLK), 1)
    band = (col >= row) & (col <= row + WINDOW_KEYS)
    lane = lax.broadcasted_iota(jnp.int32, (1, GROUP_WIDTH), 1)
    slane = lax.broadcasted_iota(jnp.int32, (1, LANES), 1)
    head_masks = [(lane >= h * HEAD_DIM) & (lane < (h + 1) * HEAD_DIM) for h in range(nh)]
    subs = range(tq // QBLK)
    rows = [slice(j * QBLK, (j + 1) * QBLK) for j in subs]
    window = [slice(j * QBLK, (j + 2) * QBLK) for j in subs]

    def stack(qj):
        return jnp.concatenate([jnp.where(hm, qj, jnp.zeros_like(qj)) for hm in head_masks], axis=0)

    valid = [band & ((col >= QBLK) | jnp.logical_not(first)) if j == 0 else band for j in subs]
    s = [jnp.where(valid[j], _dot_nt(stack(q_ref[rows[j], :]), kf_ref[window[j], :]), NEG_BIG) for j in subs]
    m = [jnp.max(s[j], axis=1, keepdims=True) for j in subs]
    p = [jnp.exp(s[j] - m[j]) for j in subs]
    l = [jnp.sum(p[j], axis=1, keepdims=True) for j in subs]
    o_all = [_dot(p[j].astype(_BF16), vf_ref[window[j], :]) / l[j] for j in subs]
    for j in subs:
        lse_all = m[j] + jnp.log(l[j])
        o_acc = jnp.zeros((QBLK, GROUP_WIDTH), _F32)
        lse_t = jnp.zeros((QBLK, LANES), _F32)
        for h, hm in enumerate(head_masks):
            o_acc = jnp.where(hm, o_all[j][h * QBLK:(h + 1) * QBLK, :], o_acc)
            sm = (slane >= h * LSE_SEG) & (slane < (h + 1) * LSE_SEG)
            lse_t = jnp.where(sm, lse_all[h * QBLK:(h + 1) * QBLK, :], lse_t)
        o_ref[rows[j], :] = o_acc.astype(_BF16)
        lse_ref[rows[j], :] = lse_t


def _dilated_group(view2, dil, batch, seq):
    sd = seq // dil
    tq = min(DIL_QTILE, sd)
    per = tq // QBLK
    gw = GROUP_WIDTH
    view = view2.reshape(batch, sd, dil * 3 * gw)
    cur = lambda part: pl.BlockSpec((None, tq, gw), lambda b, r, i: (b, i, 3 * r + part))
    prev = lambda part: pl.BlockSpec((None, QBLK, gw),
                                     lambda b, r, i: (b, jnp.maximum(i * per - 1, 0), 3 * r + part))
    o, lse = pl.pallas_call(
        _dil_body,
        out_shape=[jax.ShapeDtypeStruct((batch, sd, dil * gw), _BF16),
                   jax.ShapeDtypeStruct((batch, sd, dil * LANES), _F32)],
        grid=(batch, dil, sd // tq),
        in_specs=[cur(0), prev(1), cur(1), prev(2), cur(2)],
        out_specs=[pl.BlockSpec((None, tq, gw), lambda b, r, i: (b, i, r)),
                   pl.BlockSpec((None, tq, LANES), lambda b, r, i: (b, i, r))],
        scratch_shapes=[pltpu.VMEM((tq + QBLK, gw), _BF16), pltpu.VMEM((tq + QBLK, gw), _BF16)],
        compiler_params=_params("arbitrary", "arbitrary", "arbitrary"),
        name=f"dil{dil}",
    )(view, view, view, view, view)
    return o.reshape(batch * sd, dil * gw), lse.reshape(batch * sd, dil * LANES)


def _sb_body(q_ref, k_ref, v_ref, o_ref, carry_ref, acc_ref, qs_ref):
    step_id = pl.program_id(1)
    chains = range(SB_CHAINS)
    blk = [step_id * SB_CHAINS + c for c in chains]
    nh = HEADS_PER_GROUP
    row = lax.broadcasted_iota(jnp.int32, (nh * QBLK, QBLK), 0) & (QBLK - 1)
    col = lax.broadcasted_iota(jnp.int32, (nh * QBLK, QBLK), 1)
    strict = col < row
    ur = lax.broadcasted_iota(jnp.int32, (2 * QBLK, QBLK), 0) & (QBLK - 1)
    uc = lax.broadcasted_iota(jnp.int32, (2 * QBLK, QBLK), 1)
    u = jnp.where(ur > uc, 1.0, 0.0).astype(_BF16)
    lane = lax.broadcasted_iota(jnp.int32, (1, GROUP_WIDTH), 1)
    head_masks = [(lane >= h * HEAD_DIM) & (lane < (h + 1) * HEAD_DIM) for h in range(nh)]
    for c in chains:
        q = q_ref[c * QBLK:(c + 1) * QBLK, :]
        qs_ref[c] = jnp.concatenate([jnp.where(hm, q, jnp.zeros_like(q)) for hm in head_masks], axis=0)

    def softplus(z):
        return jnp.maximum(z, 0.0) + jnp.log(1.0 + jnp.exp(-jnp.abs(z)))

    def later_keys(log_1m):
        hi, lo = _split(log_1m)
        return _dot(jnp.concatenate([hi, lo], axis=1), u)

    nb = SB_HEAD_BLOCKS
    cols = [slice(j * QBLK, (j + 1) * QBLK) for j in range(nb)]
    kbs = [[blk[c] - (nb - 1) + j for j in range(nb)] for c in chains]
    starts = [[pl.multiple_of(jnp.maximum(kb, 0) * QBLK, QBLK) for kb in kbs[c]] for c in chains]
    keep = [[strict if j == nb - 1 else (kbs[c][j] >= 0) for j in range(nb)] for c in chains]
    z = [_dot_nt(qs_ref[c], jnp.concatenate([k_ref[pl.ds(s, QBLK), :] for s in starts[c]], axis=0)) for c in chains]
    sp = [softplus(z[c]) for c in chains]
    log_1m = [[jnp.where(keep[c][j], -sp[c][:, cols[j]], 0.0) for j in range(nb)] for c in chains]
    totals = [[jnp.sum(l, axis=1, keepdims=True) for l in log_1m[c]] for c in chains]
    later = [[later_keys(log_1m[c][j]) for j in range(nb)] for c in chains]
    for c in chains:
        a_blocks = []
        after = jnp.zeros_like(totals[c][0])
        for j in reversed(range(nb)):
            a = jnp.exp((z[c][:, cols[j]] - sp[c][:, cols[j]]) + later[c][j] + after)
            a_blocks.insert(0, jnp.where(keep[c][j], a, 0.0).astype(_BF16))
            after = after + totals[c][j]
        acc_ref[c] = _dot(jnp.concatenate(a_blocks, axis=1),
                          jnp.concatenate([v_ref[pl.ds(s, QBLK), :] for s in starts[c]], axis=0))
        carry_ref[c] = after

    for c in chains:
        def tile(kb, c=c):
            start = pl.multiple_of(kb * QBLK, QBLK)
            z = _dot_nt(qs_ref[c], k_ref[pl.ds(start, QBLK), :])
            sp = softplus(z)
            log_1m = -sp
            a = jnp.exp((z - sp) + later_keys(log_1m) + carry_ref[c])
            acc_ref[c] += _dot(a.astype(_BF16), v_ref[pl.ds(start, QBLK), :])
            carry = carry_ref[c] + jnp.sum(log_1m, axis=1, keepdims=True)
            carry_ref[c] = carry
            return jnp.max(carry)

        def cond(st):
            return (st[0] >= 0) & (st[1] > SB_DEAD_LOG)

        def step(st, tile=tile):
            return st[0] - 1, tile(st[0])

        lax.while_loop(cond, step, (blk[c] - nb, jnp.max(carry_ref[c])))
        out = jnp.zeros((QBLK, GROUP_WIDTH), _F32)
        for h, hm in enumerate(head_masks):
            out = jnp.where(hm, acc_ref[c, h * QBLK:(h + 1) * QBLK, :], out)
        o_ref[c * QBLK:(c + 1) * QBLK, :] = out.astype(_BF16)


def _stick_breaking(arr, batch, seq):
    gw = GROUP_WIDTH
    qt = SB_CHAINS * QBLK
    view = arr.reshape(batch, seq, 3 * gw)
    o = pl.pallas_call(
        _sb_body,
        out_shape=jax.ShapeDtypeStruct((batch, seq, gw), _BF16),
        grid=(batch, seq // qt),
        in_specs=[pl.BlockSpec((None, qt, gw), lambda b, i: (b, i, 0)),
                  pl.BlockSpec((None, seq, gw), lambda b, i: (b, 0, 1)),
                  pl.BlockSpec((None, seq, gw), lambda b, i: (b, 0, 2))],
        out_specs=pl.BlockSpec((None, qt, gw), lambda b, i: (b, i, 0)),
        scratch_shapes=[pltpu.VMEM((SB_CHAINS, HEADS_PER_GROUP * QBLK, 1), _F32),
                        pltpu.VMEM((SB_CHAINS, HEADS_PER_GROUP * QBLK, gw), _F32),
                        pltpu.VMEM((SB_CHAINS, HEADS_PER_GROUP * QBLK, gw), _BF16)],
        compiler_params=_params("arbitrary", "arbitrary"),
        name="sb",
    )(view, view, view)
    return o.reshape(batch * seq, gw)


def _merge_body(x_ref, mod_ref, g1_ref, g2_ref, o0_ref, o1_ref, o2_ref, l0_ref, l1_ref, l2_ref, osb_ref,
                wbg_ref, wbd_ref, wbs_ref, wout_ref, wrh_ref, wrl_ref, br_ref, ex_ref, tri_ref,
                x1_ref, h2p_ref, rinfo_ref, er_ref, cnt_out_ref, cnt_ref, os1_ref, os2_ref, ls1_ref, ls2_ref):
    d = D_MODEL
    tm = x_ref.shape[0]
    x = x_ref[...]
    hb = _rms_mod(x, g1_ref[...], mod_ref[:, d:2 * d], mod_ref[:, 0:d]).astype(_BF16)

    def natural(ref, st_ref, dil):
        if dil == 1:
            return ref[...].astype(_F32)
        slabs = st_ref.shape[0]
        for r in range(dil):
            for s in range(slabs):
                col0 = (r * slabs + s) * LANES
                st_ref[s, pl.ds(r, tm // dil, stride=dil), :] = ref[:, col0:col0 + LANES].astype(_F32)
        return jnp.concatenate([st_ref[s] for s in range(slabs)], axis=1)

    dils = [dl for _, dl in DIL_PATTERNS]
    o_nat = [natural(r, s, dl) for r, s, dl in zip((o0_ref, o1_ref, o2_ref), (None, os1_ref, os2_ref), dils)]
    l0, l1, l2 = [natural(r, s, dl) for r, s, dl in zip((l0_ref, l1_ref, l2_ref), (None, ls1_ref, ls2_ref), dils)]

    lmax = jnp.maximum(jnp.maximum(l0, l1), l2)
    e0, e1, e2 = jnp.exp(l0 - lmax), jnp.exp(l1 - lmax), jnp.exp(l2 - lmax)
    inv = 1.0 / (e0 + e1 + e2)
    ex = ex_ref[...]

    def widen(w):
        hi, lo = _split(w)
        return _dot(jnp.concatenate([hi, lo], axis=1), ex)

    w_groups = [widen(e * inv) for e in (e0, e1, e2)]
    gate_dil = jax.nn.sigmoid(_dot(hb, wbg_ref[:, :d]))
    o_dil = w_groups[0] * o_nat[0] + w_groups[1] * o_nat[1] + w_groups[2] * o_nat[2]
    branch_dil = _dot(o_dil.astype(_BF16), wbd_ref[...])
    branch_sb = _dot(osb_ref[...], wbs_ref[...])
    gate_sb = jax.nn.sigmoid(_dot(hb, wbg_ref[:, d:]))
    merged = gate_dil * branch_dil + gate_sb * branch_sb
    x1 = x + mod_ref[:, 2 * d:3 * d] * _dot(merged.astype(_BF16), wout_ref[...])
    x1_ref[...] = x1

    h2 = _rms_mod(x1, g2_ref[...], mod_ref[:, 4 * d:5 * d], mod_ref[:, 3 * d:4 * d])
    h2p_ref[...] = _pack_halves(h2)

    hh, hl = _split(h2)
    logits = _dot(hh, wrh_ref[...]) + (_dot(hl, wrh_ref[...]) + _dot(hh, wrl_ref[...])) + br_ref[...]
    lane =lax.broadcasted_iota(jnp.int32, (tm, ROUTER_LANES), 1).astype(_F32)
    far = float(ROUTER_LANES)

    def top(vals):
        m = jnp.max(vals, axis=1, keepdims=True)
        return m, jnp.min(jnp.where(vals == m, lane, far), axis=1, keepdims=True)

    is_group = lane < N_GROUPS
    mg, gsel = top(jnp.where(is_group, logits, NEG_BIG))
    pg_top = 1.0 / jnp.sum(jnp.where(is_group, jnp.exp(logits - mg), 0.0), axis=1, keepdims=True)
    lane0 = EXPERT_LANE0 + EXPERTS_PER_GROUP * gsel
    le = jnp.where((lane >= lane0) & (lane < lane0 + EXPERTS_PER_GROUP), logits, NEG_BIG)
    m1, i1 = top(le)
    m2, i2 = top(jnp.where(lane == i1, NEG_BIG, le))
    t2 = jnp.exp(m2 - m1)
    w0 = pg_top / (1.0 + t2)
    w1 = pg_top * t2 / (1.0 + t2)

    @pl.when(pl.program_id(0) == 0)
    def _():
        cnt_ref[...] = jnp.zeros_like(cnt_ref)

    sel0, sel1 = lane == i1, lane == i2
    onehot = jnp.where(sel0 | sel1, 1.0, 0.0)
    before = _dot(tri_ref[...], onehot.astype(_BF16)) + cnt_ref[0:1, :]
    r0 = jnp.sum(jnp.where(sel0, before, 0.0), axis=1, keepdims=True)
    r1 = jnp.sum(jnp.where(sel1, before, 0.0), axis=1, keepdims=True)
    cnt_ref[...] += jnp.sum(onehot, axis=0, keepdims=True)
    cnt_out_ref[...] = cnt_ref[...]

    cols = (i1 - EXPERT_LANE0, i2 - EXPERT_LANE0, r0, r1, w0, w1)
    rinfo = jnp.zeros((tm, ROUTER_LANES), _F32)
    for c, v in enumerate(cols):
        rinfo = jnp.where(lane == float(c), v, rinfo)
    rinfo_ref[...] = rinfo
    er_ref[...] = jnp.transpose(rinfo)[0:8, :].astype(jnp.int32)


def _merge(x2, mod3, g1, g2, outs, lses, osb, wbg, wbd, wbs, wout, wrh, wrl, br, ex, tri, seq, part):
    d = x2.shape[1]
    tm = ROW_TILE
    per_b = seq // tm
    t = x2.shape[0] // MOE_PARTS
    first = part * (t // tm)
    src = lambda i: (i + first, 0)
    row = lambda i: (i, 0)
    const = lambda i: (0, 0)
    full = lambda a: pl.BlockSpec(a.shape, const)
    gw = GROUP_WIDTH
    dils = [dl for _, dl in DIL_PATTERNS]
    return pl.pallas_call(
        _merge_body,
        out_shape=[jax.ShapeDtypeStruct((t, d), _F32),
                   jax.ShapeDtypeStruct((t, d // 2), jnp.uint32),
                   jax.ShapeDtypeStruct((t, ROUTER_LANES), _F32),
                   jax.ShapeDtypeStruct((8, t), jnp.int32),
                   jax.ShapeDtypeStruct((8, ROUTER_LANES), _F32)],
        grid=(t // tm,),
        in_specs=[pl.BlockSpec((tm, d), src),
                  pl.BlockSpec((None, 1, mod3.shape[2]), lambda i: ((i + first) // per_b, 0, 0)),
                  full(g1), full(g2)]
                 + [pl.BlockSpec((tm // dl, dl * gw), src) for dl in dils]
                 + [pl.BlockSpec((tm // dl, dl * LANES), src) for dl in dils]
                 + [pl.BlockSpec((tm, gw), src)]
                 + [full(a) for a in (wbg, wbd, wbs, wout, wrh, wrl, br, ex, tri)],
        out_specs=[pl.BlockSpec((tm, d), row),
                   pl.BlockSpec((tm, d // 2), row),
                   pl.BlockSpec((tm, ROUTER_LANES), row),
                   pl.BlockSpec((8, tm), lambda i: (0, i)),
                   pl.BlockSpec((8, ROUTER_LANES), const)],
        scratch_shapes=[pltpu.VMEM((8, ROUTER_LANES), _F32),
                        pltpu.VMEM((gw // LANES, tm, LANES), _F32), pltpu.VMEM((gw // LANES, tm, LANES), _F32),
                        pltpu.VMEM((1, tm, LANES), _F32), pltpu.VMEM((1, tm, LANES), _F32)],
        compiler_params=_params("arbitrary"),
        name="merge",
    )(x2, mod3, g1, g2, *outs, *lses, osb, wbg, wbd, wbs, wout, wrh, wrl, br, ex, tri)


def _dest_body(ps_ref, er_ref, d_ref):
    e = er_ref[0:2, :]
    start = jnp.zeros_like(e)
    for x in range(N_EXPERTS):
        start = jnp.where(e == x, ps_ref[x], start)
    d_ref[...] = start + er_ref[2:4, :]


def _dest(pstart, er):
    t = er.shape[1]
    tw = min(DEST_TILE, t)
    return pl.pallas_call(
        _dest_body,
        out_shape=jax.ShapeDtypeStruct((2, t), jnp.int32),
        grid_spec=pltpu.PrefetchScalarGridSpec(
            num_scalar_prefetch=1,
            grid=(t // tw,),
            in_specs=[pl.BlockSpec((8, tw), lambda i, ps: (0, i))],
            out_specs=pl.BlockSpec((2, tw), lambda i, ps: (0, i))),
        compiler_params=_params("arbitrary"),
        name="dest",
    )(pstart, er)


def _sc_mesh():
    return plsc.VectorSubcoreMesh(core_axis_name="core", subcore_axis_name="subcore",
                                  num_cores=SC_CORES, num_subcores=SC_SUBCORES)


def _sc_worker():
    return lax.axis_index("subcore") * SC_CORES + lax.axis_index("core")


def _sc_scatter(x, idx0, idx1, n_slots):
    chunks = idx0.shape[0]
    per = chunks // SC_WORKERS
    win = idx0.shape[1]

    @functools.partial(
        pl.kernel, mesh=_sc_mesh(), out_type=jax.ShapeDtypeStruct((n_slots, x.shape[1]), x.dtype),
        scratch_types=[pltpu.VMEM((1, win), jnp.int32), pltpu.VMEM((1, win), jnp.int32),
                       pltpu.VMEM((win, x.shape[1]), x.dtype), pltpu.SemaphoreType.DMA],
        name="sc_scatter")
    def run(x_hbm, i0_hbm, i1_hbm, o_hbm, i0_v, i1_v, rows_v, sem):
        wid = _sc_worker()

        @pl.loop(0, per)
        def _(j):
            c = wid * per + j
            pltpu.sync_copy(i0_hbm.at[pl.ds(c, 1)], i0_v)
            pltpu.sync_copy(i1_hbm.at[pl.ds(c, 1)], i1_v)
            pltpu.sync_copy(x_hbm.at[pl.ds(c * win, win)], rows_v)
            first = pltpu.async_copy(rows_v, o_hbm.at[i0_v.at[0]], sem)
            second = pltpu.async_copy(rows_v, o_hbm.at[i1_v.at[0]], sem)
            first.wait()
            second.wait()

    return run(x, idx0, idx1)


def _sc_gather(table, idx):
    chunks, win = idx.shape
    per = chunks // SC_WORKERS

    @functools.partial(
        pl.kernel, mesh=_sc_mesh(), out_type=jax.ShapeDtypeStruct((chunks * win, table.shape[1]), table.dtype),
        scratch_types=[pltpu.VMEM((1, win), jnp.int32), pltpu.VMEM((win, table.shape[1]), table.dtype),
                       pltpu.SemaphoreType.DMA],
        name="sc_gather")
    def run(t_hbm, i_hbm, o_hbm, i_v, rows_v, sem):
        wid = _sc_worker()

        @pl.loop(0, per)
        def _(j):
            c = wid * per + j
            pltpu.sync_copy(i_hbm.at[pl.ds(c, 1)], i_v)
            pltpu.async_copy(t_hbm.at[i_v.at[0]], rows_v, sem).wait()
            pltpu.sync_copy(rows_v, o_hbm.at[pl.ds(c * win, win)])

    return run(table, idx)


def _pack_halves(a):
    h = a.shape[1] // 2
    lo = lax.bitcast_convert_type(a[:, :h].astype(_BF16).astype(_F32), jnp.uint32) >> 16
    hi = lax.bitcast_convert_type(a[:, h:].astype(_BF16).astype(_F32), jnp.uint32) & jnp.uint32(0xFFFF0000)
    return lo | hi


def _unpack_halves(w):
    return jnp.concatenate(
        [lax.bitcast_convert_type(w << 16, _F32), lax.bitcast_convert_type(w & jnp.uint32(0xFFFF0000), _F32)], axis=1)


def _experts_body(ce_ref, nv_ref, nu_ref, seg_ref, nxt_ref, xs_ref, wg_ref, wu_ref, wd_ref, ys_ref,
                  wgb_ref, wub_ref, wdb_ref, wgf_ref, wuf_ref, wdf_ref, sem):
    c = pl.program_id(0)

    def fetch(expert, slot):
        return [pltpu.make_async_copy(src.at[expert], dst.at[slot], sem.at[slot])
                for src, dst in ((wg_ref, wgf_ref), (wu_ref, wuf_ref), (wd_ref, wdf_ref))]

    @pl.when(c < nu_ref[0])
    def _():
        @pl.when(seg_ref[c] >= 0)
        def _():
            slot = seg_ref[c] & 1

            @pl.when(c == 0)
            def _():
                for copy in fetch(ce_ref[0], 0):
                    copy.start()

            for copy in fetch(ce_ref[c], slot):
                copy.wait()
            wgb_ref[...] = wgf_ref[slot].astype(_BF16)
            wub_ref[...] = wuf_ref[slot].astype(_BF16)
            wdb_ref[...] = wdf_ref[slot].astype(_BF16)

            @pl.when(nxt_ref[c] >= 0)
            def _():
                for copy in fetch(nxt_ref[c], 1 - slot):
                    copy.start()

        row = lax.broadcasted_iota(jnp.int32, xs_ref.shape, 0)
        x = _unpack_halves(jnp.where(row < nv_ref[c], xs_ref[...], jnp.uint32(0))).astype(_BF16)
        half = x.shape[0] // 2
        ups = [(_dot(x[r:r + half], wgb_ref[...]), _dot(x[r:r + half], wub_ref[...])) for r in (0, half)]
        for (g, u), r in zip(ups, (0, half)):
            hmid = (g * jax.nn.sigmoid(g)) * u
            ys_ref[r:r + half, :] = _pack_halves(_dot(hmid.astype(_BF16), wdb_ref[...]))


def _experts(chunk_e, n_valid, n_used, seg, nxt, xs, wg, wu, wd):
    n_slots, w = xs.shape
    ch = EXPERT_CHUNK
    d, de = wg.shape[1], wg.shape[2]
    slot = lambda c, ce, nv, nu, sg, nx: (jnp.minimum(c, nu[0] - 1), 0)
    hbm = pl.BlockSpec(memory_space=pl.ANY)
    return pl.pallas_call(
        _experts_body,
        out_shape=jax.ShapeDtypeStruct((n_slots, d // 2), jnp.uint32),
        grid_spec=pltpu.PrefetchScalarGridSpec(
            num_scalar_prefetch=5,
            grid=(n_slots // ch,),
            in_specs=[pl.BlockSpec((ch, w), slot), hbm, hbm, hbm],
            out_specs=pl.BlockSpec((ch, d // 2), slot),
            scratch_shapes=[pltpu.VMEM((d, de), _BF16), pltpu.VMEM((d, de), _BF16), pltpu.VMEM((de, d), _BF16),
                            pltpu.VMEM((2, d, de), _F32), pltpu.VMEM((2, d, de), _F32), pltpu.VMEM((2, de, d), _F32),
                            pltpu.SemaphoreType.DMA((2,))]),
        compiler_params=_params("arbitrary"),
        name="experts",
    )(chunk_e, n_valid, n_used, seg, nxt, xs, wg, wu, wd)


def _combine_body(x1_ref, rinfo_ref, mod_ref, y0_ref, y1_ref, *rest):
    o_ref = rest[-1]
    y = rinfo_ref[:, 4:5] * _unpack_halves(y0_ref[...]) + rinfo_ref[:, 5:6] * _unpack_halves(y1_ref[...])
    o_ref[...] = x1_ref[...] + mod_ref[:, 5 * D_MODEL:6 * D_MODEL] * y


def _combine(x1, rinfo, mod3, gathered, seq, part, out_so_far):
    t, d = x1.shape
    tf = min(COMBINE_TILE, seq)
    per_b = seq // tf
    nt = t // tf
    first = part * nt
    in_specs = [pl.BlockSpec((tf, d), lambda i: (i, 0)),
                pl.BlockSpec((tf, ROUTER_LANES), lambda i: (i, 0)),
                pl.BlockSpec((None, 1, mod3.shape[2]), lambda i: ((i + first) // per_b, 0, 0)),
                pl.BlockSpec((tf, d // 2), lambda i: (i, 0)),
                pl.BlockSpec((tf, d // 2), lambda i: (i + nt, 0))]
    args = [x1, rinfo, mod3, gathered, gathered]
    aliases = {}
    if out_so_far is not None:
        in_specs.append(pl.BlockSpec(memory_space=pl.ANY))
        args.append(out_so_far)
        aliases = {len(args) - 1: 0}
    return pl.pallas_call(
        _combine_body,
        out_shape=jax.ShapeDtypeStruct((t * MOE_PARTS, d), _F32),
        grid=(nt,),
        in_specs=in_specs,
        out_specs=pl.BlockSpec((tf, d), lambda i: (i + first, 0)),
        input_output_aliases=aliases,
        compiler_params=_params("arbitrary"),
        name="combine",
    )(*args)


def _rope_trig(positions):
    inv_freq = ROPE_THETA ** (-jnp.arange(0, ROPE_DIM, 2, dtype=_F32) / ROPE_DIM)
    ang = positions.reshape(-1).astype(_F32)[:, None] * inv_freq
    return jnp.pad(jnp.concatenate([jnp.cos(ang), jnp.sin(ang)], axis=1), ((0, 0), (0, LANES - ROPE_DIM)))


def _layer(x, mod, positions, g_mix, g_ffn, w_in, w_bg, qg, kg, w_bd, w_bs, w_out, w_rg, b_rg, w_re, b_re,
           w_eg, w_eu, w_ed):
    batch, seq, d = x.shape
    t = batch * seq
    x2 = x.reshape(t, d)
    mod3 = mod.reshape(batch, 1, mod.shape[1])
    gw = GROUP_WIDTH

    lane = jnp.arange(gw)
    bd = jnp.where(lane[:, None] // HEAD_DIM == lane[None, :] // HEAD_DIM, 1.0 / HEAD_DIM, 0.0).astype(_BF16)
    ex = (jnp.arange(LANES)[:, None] == (lane[None, :] // HEAD_DIM) * LSE_SEG).astype(_BF16)
    ex = jnp.concatenate([ex, ex], axis=0)
    tri = (jnp.arange(ROW_TILE)[:, None] > jnp.arange(ROW_TILE)[None, :]).astype(_BF16)
    tile4 = lambda g: jnp.tile(g.astype(_F32), HEADS_PER_GROUP).reshape(1, gw)
    wr = jnp.zeros((d, ROUTER_LANES), _F32).at[:, :N_GROUPS].set(w_rg).at[:, N_GROUPS:N_GROUPS + N_EXPERTS].set(w_re)
    wrh = wr.astype(_BF16)
    wrl = (wr - wrh.astype(_F32)).astype(_BF16)
    br = jnp.zeros((1, ROUTER_LANES), _F32).at[0, :N_GROUPS].set(b_rg).at[0, N_GROUPS:N_GROUPS + N_EXPERTS].set(b_re)

    d0, d1, d2, sbp = _qkv(x2, mod3, g_mix.reshape(1, d), w_in.astype(_BF16), tile4(qg) * ATTN_SCALE, tile4(kg),
                           _rope_trig(positions), bd, seq)
    dil = [_dilated_group(a, dl, batch, seq) for a, (_, dl) in zip((d0, d1, d2), DIL_PATTERNS)]
    osb = _stick_breaking(sbp, batch, seq)

    merge_weights = (w_bg.astype(_BF16), w_bd.astype(_BF16), w_bs.astype(_BF16), w_out.astype(_BF16))
    tp = t // MOE_PARTS
    ch = EXPERT_CHUNK
    win = SC_INDEX_WINDOW
    n_chunks = -(-2 * tp // ch) + N_EXPERTS
    chunk_start = jnp.arange(n_chunks, dtype=jnp.int32) * ch
    expert_ids = jnp.arange(N_EXPERTS, dtype=jnp.int32)
    out = None
    for part in range(MOE_PARTS):
        x1, h2p, rinfo, er, cnt = _merge(
            x2, mod3, g_mix.reshape(1, d), g_ffn.reshape(1, d), [o for o, _ in dil], [l for _, l in dil], osb,
            *merge_weights, wrh, wrl, br, ex, tri, seq, part)

        counts = cnt[0, EXPERT_LANE0:EXPERT_LANE0 + N_EXPERTS].astype(jnp.int32)
        padded = (counts + ch - 1) // ch * ch
        pend = jnp.cumsum(padded)
        pstart = pend - padded
        chunk_e = jnp.minimum(jnp.sum((pend[None, :] <= chunk_start[:, None]).astype(jnp.int32), axis=1),
                              N_EXPERTS - 1)
        n_used = (pend[-1:] // ch).astype(jnp.int32)
        begin = chunk_start[:, None]
        inside = (pstart[None, :] <= begin) & (begin < pend[None, :])
        n_valid = jnp.sum(jnp.where(inside, jnp.clip(counts[None, :] - (begin - pstart[None, :]), 0, ch), 0), axis=1)

        dest = _dest(pstart, er)
        xs = _sc_scatter(h2p, dest[0].reshape(tp // win, win), dest[1].reshape(tp // win, win), n_chunks * ch)
        first = (chunk_e != jnp.concatenate([jnp.full((1,), -1, jnp.int32), chunk_e[:-1]])) & (chunk_start < pend[-1])
        seg_no = jnp.cumsum(first.astype(jnp.int32)) - 1
        seg = jnp.where(first, seg_no, -1 - seg_no)
        later = (expert_ids[None, :] > expert_ids[:, None]) & (padded > 0)[None, :]
        next_expert = jnp.min(jnp.where(later, expert_ids[None, :], N_EXPERTS), axis=1)
        next_expert = jnp.where(next_expert == N_EXPERTS, -1, next_expert)
        nxt = jnp.sum(jnp.where(chunk_e[:, None] == expert_ids[None, :], next_expert[None, :], 0), axis=1)
        ys = _experts(chunk_e, n_valid, n_used, seg.astype(jnp.int32), nxt.astype(jnp.int32), xs, w_eg, w_eu, w_ed)
        gathered = _sc_gather(ys, dest.reshape(2 * tp // win, win))
        out = _combine(x1, rinfo, mod3, gathered, seq, part, out)
    return out.reshape(batch, seq, d)


def kernel(x, c, positions, w_ada, b_ada, g_norm_mix, g_norm_ffn, w_in, w_branch_gate, q_norm_g, k_norm_g,
           w_branch_dil, w_branch_sb, w_out, w_router_group, b_router_group, w_router_expert, b_router_expert,
           w_expert_gate, w_expert_up, w_expert_down):
    for l in range(w_ada.shape[0]):
        mod = _ada(c, w_ada[l], b_ada[l])
        x = _layer(x, mod, positions, g_norm_mix[l], g_norm_ffn[l], w_in[l], w_branch_gate[l], q_norm_g[l],
                   k_norm_g[l], w_branch_dil[l], w_branch_sb[l], w_out[l], w_router_group[l], b_router_group[l],
                   w_router_expert[l], b_router_expert[l], w_expert_gate[l], w_expert_up[l], w_expert_down[l])
    return x
```

```python
import functools

import jax
import jax.numpy as jnp
from jax import lax
from jax.experimental import pallas as pl
from jax.experimental.pallas import tpu as pltpu
from jax.experimental.pallas import tpu_sc as plsc

D_MODEL = 1024
HEAD_DIM = 64
DIL_PATTERNS = ((128, 1), (512, 4), (2048, 16))
HEADS_PER_GROUP = 4
GROUP_WIDTH = HEADS_PER_GROUP * HEAD_DIM
N_DIL_GROUPS = len(DIL_PATTERNS)
DIL_WIDTH = N_DIL_GROUPS * GROUP_WIDTH
QKV_WIDTH = 3 * DIL_WIDTH + 3 * GROUP_WIDTH
WINDOW_KEYS = 128
ROPE_THETA = 500000.0
ROPE_DIM = HEAD_DIM // 4
N_GROUPS = 4
EXPERTS_PER_GROUP = 8
N_EXPERTS = N_GROUPS * EXPERTS_PER_GROUP
D_EXPERT = 512
RMS_EPS = 1e-6
ATTN_SCALE = HEAD_DIM ** -0.5

LANES = 128
ROUTER_LANES = LANES
EXPERT_LANE0 = N_GROUPS
LSE_SEG = LANES // HEADS_PER_GROUP
NEG_BIG = -1e30
SB_DEAD_LOG = -120.0
SB_HEAD_BLOCKS = 3
SB_CHAINS = 4

ROW_TILE = 512
QBLK = 128
DIL_QTILE = 1024
EXPERT_CHUNK = 512
COMBINE_TILE = 1024
DEST_TILE = 8192
SC_CORES = 2
SC_SUBCORES = 16
SC_WORKERS = SC_CORES * SC_SUBCORES
SC_INDEX_WINDOW = 128
MOE_PARTS = 2
VMEM_LIMIT = 48 * 1024 * 1024

_BF16 = jnp.bfloat16
_F32 = jnp.float32
_NT = (((1,), (1,)), ((), ()))


def _dot(a, b):
    return jnp.dot(a, b, preferred_element_type=_F32)


def _dot_nt(a, b):
    return lax.dot_general(a, b, _NT, preferred_element_type=_F32)


def _split(a):
    hi = a.astype(_BF16)
    lo = (a - hi.astype(_F32)).astype(_BF16)
    return hi, lo


def _dot3(a, b):
    ah, al = _split(a)
    bh, bl = _split(b)
    return _dot(ah, bh) + (_dot(ah, bl) + _dot(al, bh))


def _rms_mod(x, g, scale, shift):
    y = x * lax.rsqrt(jnp.mean(x * x, axis=-1, keepdims=True) + RMS_EPS)
    return y * g * (1.0 + scale) + shift


def _params(*sem):
    return pltpu.CompilerParams(dimension_semantics=sem, vmem_limit_bytes=VMEM_LIMIT)


def _ada_body(c_ref, w_ref, b_ref, o_ref):
    c = c_ref[...]
    o_ref[...] = _dot3(c * jax.nn.sigmoid(c), w_ref[...]) + b_ref[...]


def _ada(c, w_ada, b_ada):
    b, d = c.shape
    n = w_ada.shape[1]
    rows = -(-b // 16) * 16
    cp = jnp.zeros((rows, d), _F32).at[:b].set(c)
    nt = 1536
    out = pl.pallas_call(
        _ada_body,
        out_shape=jax.ShapeDtypeStruct((rows, n), _F32),
        grid=(n // nt,),
        in_specs=[pl.BlockSpec((rows, d), lambda j: (0, 0)),
                  pl.BlockSpec((d, nt), lambda j: (0, j)),
                  pl.BlockSpec((1, nt), lambda j: (0, j))],
        out_specs=pl.BlockSpec((rows, nt), lambda j: (0, j)),
        compiler_params=_params("arbitrary"),
        name="ada",
    )(cp, w_ada, b_ada.reshape(1, n))
    return out[:b]


def _qkv_body(x_ref, mod_ref, g_ref, w_ref, qg_ref, kg_ref, trig_ref, bd_ref,
              o0_ref, o1_ref, o2_ref, osb_ref, st_ref, acc_ref):
    d = D_MODEL
    tm = x_ref.shape[0]
    h = _rms_mod(x_ref[...], g_ref[...], mod_ref[:, d:2 * d], mod_ref[:, 0:d])
    hb = h.astype(_BF16)
    half = ROPE_DIM // 2
    in_head = lax.broadcasted_iota(jnp.int32, (tm, LANES), 1) & (HEAD_DIM - 1)
    freq = in_head & (half - 1)
    trig = trig_ref[...]
    cos = jnp.take_along_axis(trig, freq, axis=1)
    sin = jnp.take_along_axis(trig, freq + half, axis=1)
    cc = jnp.where(in_head < ROPE_DIM, cos, 1.0)
    s1 = jnp.where(in_head < half, -sin, 0.0)
    s2 = jnp.where((in_head >= half) & (in_head < ROPE_DIM), sin, 0.0)
    cc, s1, s2 = [jnp.concatenate([a, a], axis=1) for a in (cc, s1, s2)]
    bd = bd_ref[...]
    gw = GROUP_WIDTH

    def normed_rotated(acc, gain):
        ms = _dot((acc * acc).astype(_BF16), bd)
        y = acc * lax.rsqrt(ms + RMS_EPS) * gain
        return y * cc + pltpu.roll(y, gw - ROPE_DIM // 2, 1) * s1 + pltpu.roll(y, ROPE_DIM // 2, 1) * s2

    def store(o_ref, dil, part, y):
        if dil == 1:
            o_ref[:, part * gw:(part + 1) * gw] = y.astype(_BF16)
            return
        for s in range(gw // LANES):
            st_ref[s] = y[:, s * LANES:(s + 1) * LANES]
        for r in range(dil):
            for s in range(gw // LANES):
                col0 = (3 * r + part) * gw + s * LANES
                o_ref[:, col0:col0 + LANES] = st_ref[s, pl.ds(r, tm // dil, stride=dil), :].astype(_BF16)

    def project(col0):
        return _dot(hb, w_ref[:, col0:col0 + gw])

    outs = (o0_ref, o1_ref, o2_ref)
    dils = [dl for _, dl in DIL_PATTERNS]
    normed = [(g, part, part * DIL_WIDTH + g * gw) for g in range(N_DIL_GROUPS) for part in (0, 1)]
    for n, (_, _, col0) in enumerate(normed):
        acc_ref[n] = project(col0)
    plain = [("v", g) for g in range(N_DIL_GROUPS)] + [("sb", part) for part in range(3)]
    for n, (kind, j) in enumerate(plain):
        if kind == "v":
            store(outs[j], dils[j], 2, project(2 * DIL_WIDTH + j * gw))
        else:
            acc = project(3 * DIL_WIDTH + j * gw)
            osb_ref[:, j * gw:(j + 1) * gw] = (acc * ATTN_SCALE if j == 0 else acc).astype(_BF16)
        g, part, _ = normed[n]
        gain = qg_ref[...] if part == 0 else kg_ref[...]
        store(outs[g], dils[g], part, normed_rotated(acc_ref[n], gain))


def _qkv(x2, mod3, g_mix, w_in, qg, kg, trig, bd, seq):
    t, d = x2.shape
    tm = ROW_TILE
    per_b = seq // tm
    row = lambda i: (i, 0)
    const = lambda i: (0, 0)
    width = 3 * GROUP_WIDTH
    dils = [dl for _, dl in DIL_PATTERNS] + [1]
    return pl.pallas_call(
        _qkv_body,
        out_shape=[jax.ShapeDtypeStruct((t // dl, dl * width), _BF16) for dl in dils],
        grid=(t // tm,),
        in_specs=[pl.BlockSpec((tm, d), row),
                  pl.BlockSpec((None, 1, mod3.shape[2]), lambda i: (i // per_b, 0, 0)),
                  pl.BlockSpec((1, d), const),
                  pl.BlockSpec(w_in.shape, const),
                  pl.BlockSpec((1, GROUP_WIDTH), const),
                  pl.BlockSpec((1, GROUP_WIDTH), const),
                  pl.BlockSpec((tm, LANES), row),
                  pl.BlockSpec(bd.shape, const)],
        out_specs=[pl.BlockSpec((tm // dl, dl * width), row) for dl in dils],
        scratch_shapes=[pltpu.VMEM((GROUP_WIDTH // LANES, tm, LANES), _F32),
                        pltpu.VMEM((2 * N_DIL_GROUPS, tm, GROUP_WIDTH), _F32)],
        compiler_params=_params("arbitrary"),
        name="qkv",
    )(x2, mod3, g_mix, w_in, qg, kg, trig, bd)


def _dil_body(q_ref, kp_ref, kc_ref, vp_ref, vc_ref, o_ref, lse_ref, kf_ref, vf_ref):
    tq = q_ref.shape[0]
    first = pl.program_id(2) == 0
    kf_ref[0:QBLK, :] = kp_ref[...]
    kf_ref[QBLK:, :] = kc_ref[...]
    vf_ref[0:QBLK, :] = vp_ref[...]
    vf_ref[QBLK:, :] = vc_ref[...]
    nh = HEADS_PER_GROUP
    row = lax.broadcasted_iota(jnp.int32, (nh * QBLK, 2 * QBLK), 0) & (QBLK - 1)
    col = lax.broadcasted_iota(jnp.int32, (nh * QBLK, 2 * QBLK), 1)
    band = (col >= row) & (col <= row + WINDOW_KEYS)
    lane = lax.broadcasted_iota(jnp.int32, (1, GROUP_WIDTH), 1)
    slane = lax.broadcasted_iota(jnp.int32, (1, LANES), 1)
    head_masks = [(lane >= h * HEAD_DIM) & (lane < (h + 1) * HEAD_DIM) for h in range(nh)]
    subs = range(tq // QBLK)
    rows = [slice(j * QBLK, (j + 1) * QBLK) for j in subs]
    window = [slice(j * QBLK, (j + 2) * QBLK) for j in subs]

    def stack(qj):
        return jnp.concatenate([jnp.where(hm, qj, jnp.zeros_like(qj)) for hm in head_masks], axis=0)

    valid = [band & ((col >= QBLK) | jnp.logical_not(first)) if j == 0 else band for j in subs]
    s = [jnp.where(valid[j], _dot_nt(stack(q_ref[rows[j], :]), kf_ref[window[j], :]), NEG_BIG) for j in subs]
    m = [jnp.max(s[j], axis=1, keepdims=True) for j in subs]
    p = [jnp.exp(s[j] - m[j]) for j in subs]
    l = [jnp.sum(p[j], axis=1, keepdims=True) for j in subs]
    o_all = [_dot(p[j].astype(_BF16), vf_ref[window[j], :]) / l[j] for j in subs]
    for j in subs:
        lse_all = m[j] + jnp.log(l[j])
        o_acc = jnp.zeros((QBLK, GROUP_WIDTH), _F32)
        lse_t = jnp.zeros((QBLK, LANES), _F32)
        for h, hm in enumerate(head_masks):
            o_acc = jnp.where(hm, o_all[j][h * QBLK:(h + 1) * QBLK, :], o_acc)
            sm = (slane >= h * LSE_SEG) & (slane < (h + 1) * LSE_SEG)
            lse_t = jnp.where(sm, lse_all[h * QBLK:(h + 1) * QBLK, :], lse_t)
        o_ref[rows[j], :] = o_acc.astype(_BF16)
        lse_ref[rows[j], :] = lse_t


def _dilated_group(view2, dil, batch, seq):
    sd = seq // dil
    tq = min(DIL_QTILE, sd)
    per = tq // QBLK
    gw = GROUP_WIDTH
    view = view2.reshape(batch, sd, dil * 3 * gw)
    cur = lambda part: pl.BlockSpec((None, tq, gw), lambda b, r, i: (b, i, 3 * r + part))
    prev = lambda part: pl.BlockSpec((None, QBLK, gw),
                                     lambda b, r, i: (b, jnp.maximum(i * per - 1, 0), 3 * r + part))
    o, lse = pl.pallas_call(
        _dil_body,
        out_shape=[jax.ShapeDtypeStruct((batch, sd, dil * gw), _BF16),
                   jax.ShapeDtypeStruct((batch, sd, dil * LANES), _F32)],
        grid=(batch, dil, sd // tq),
        in_specs=[cur(0), prev(1), cur(1), prev(2), cur(2)],
        out_specs=[pl.BlockSpec((None, tq, gw), lambda b, r, i: (b, i, r)),
                   pl.BlockSpec((None, tq, LANES), lambda b, r, i: (b, i, r))],
        scratch_shapes=[pltpu.VMEM((tq + QBLK, gw), _BF16), pltpu.VMEM((tq + QBLK, gw), _BF16)],
        compiler_params=_params("arbitrary", "arbitrary", "arbitrary"),
        name=f"dil{dil}",
    )(view, view, view, view, view)
    return o.reshape(batch * sd, dil * gw), lse.reshape(batch * sd, dil * LANES)


def _sb_body(q_ref, k_ref, v_ref, o_ref, carry_ref, acc_ref, qs_ref):
    step_id = pl.program_id(1)
    chains = range(SB_CHAINS)
    blk = [step_id * SB_CHAINS + c for c in chains]
    nh = HEADS_PER_GROUP
    row = lax.broadcasted_iota(jnp.int32, (nh * QBLK, QBLK), 0) & (QBLK - 1)
    col = lax.broadcasted_iota(jnp.int32, (nh * QBLK, QBLK), 1)
    strict = col < row
    ur = lax.broadcasted_iota(jnp.int32, (2 * QBLK, QBLK), 0) & (QBLK - 1)
    uc = lax.broadcasted_iota(jnp.int32, (2 * QBLK, QBLK), 1)
    u = jnp.where(ur > uc, 1.0, 0.0).astype(_BF16)
    lane = lax.broadcasted_iota(jnp.int32, (1, GROUP_WIDTH), 1)
    head_masks = [(lane >= h * HEAD_DIM) & (lane < (h + 1) * HEAD_DIM) for h in range(nh)]
    for c in chains:
        q = q_ref[c * QBLK:(c + 1) * QBLK, :]
        qs_ref[c] = jnp.concatenate([jnp.where(hm, q, jnp.zeros_like(q)) for hm in head_masks], axis=0)

    def softplus(z):
        return jnp.maximum(z, 0.0) + jnp.log(1.0 + jnp.exp(-jnp.abs(z)))

    def later_keys(log_1m):
        hi, lo = _split(log_1m)
        return _dot(jnp.concatenate([hi, lo], axis=1), u)

    nb = SB_HEAD_BLOCKS
    cols = [slice(j * QBLK, (j + 1) * QBLK) for j in range(nb)]
    kbs = [[blk[c] - (nb - 1) + j for j in range(nb)] for c in chains]
    starts = [[pl.multiple_of(jnp.maximum(kb, 0) * QBLK, QBLK) for kb in kbs[c]] for c in chains]
    keep = [[strict if j == nb - 1 else (kbs[c][j] >= 0) for j in range(nb)] for c in chains]
    z = [_dot_nt(qs_ref[c], jnp.concatenate([k_ref[pl.ds(s, QBLK), :] for s in starts[c]], axis=0)) for c in chains]
    sp = [softplus(z[c]) for c in chains]
    log_1m = [[jnp.where(keep[c][j], -sp[c][:, cols[j]], 0.0) for j in range(nb)] for c in chains]
    totals = [[jnp.sum(l, axis=1, keepdims=True) for l in log_1m[c]] for c in chains]
    later = [[later_keys(log_1m[c][j]) for j in range(nb)] for c in chains]
    for c in chains:
        a_blocks = []
        after = jnp.zeros_like(totals[c][0])
        for j in reversed(range(nb)):
            a = jnp.exp((z[c][:, cols[j]] - sp[c][:, cols[j]]) + later[c][j] + after)
            a_blocks.insert(0, jnp.where(keep[c][j], a, 0.0).astype(_BF16))
            after = after + totals[c][j]
        acc_ref[c] = _dot(jnp.concatenate(a_blocks, axis=1),
                          jnp.concatenate([v_ref[pl.ds(s, QBLK), :] for s in starts[c]], axis=0))
        carry_ref[c] = after

    for c in chains:
        def tile(kb, c=c):
            start = pl.multiple_of(kb * QBLK, QBLK)
            z = _dot_nt(qs_ref[c], k_ref[pl.ds(start, QBLK), :])
            sp = softplus(z)
            log_1m = -sp
            a = jnp.exp((z - sp) + later_keys(log_1m) + carry_ref[c])
            acc_ref[c] += _dot(a.astype(_BF16), v_ref[pl.ds(start, QBLK), :])
            carry = carry_ref[c] + jnp.sum(log_1m, axis=1, keepdims=True)
            carry_ref[c] = carry
            return jnp.max(carry)

        def cond(st):
            return (st[0] >= 0) & (st[1] > SB_DEAD_LOG)

        def step(st, tile=tile):
            return st[0] - 1, tile(st[0])

        lax.while_loop(cond, step, (blk[c] - nb, jnp.max(carry_ref[c])))
        out = jnp.zeros((QBLK, GROUP_WIDTH), _F32)
        for h, hm in enumerate(head_masks):
            out = jnp.where(hm, acc_ref[c, h * QBLK:(h + 1) * QBLK, :], out)
        o_ref[c * QBLK:(c + 1) * QBLK, :] = out.astype(_BF16)


def _stick_breaking(arr, batch, seq):
    gw = GROUP_WIDTH
    qt = SB_CHAINS * QBLK
    view = arr.reshape(batch, seq, 3 * gw)
    o = pl.pallas_call(
        _sb_body,
        out_shape=jax.ShapeDtypeStruct((batch, seq, gw), _BF16),
        grid=(batch, seq // qt),
        in_specs=[pl.BlockSpec((None, qt, gw), lambda b, i: (b, i, 0)),
                  pl.BlockSpec((None, seq, gw), lambda b, i: (b, 0, 1)),
                  pl.BlockSpec((None, seq, gw), lambda b, i: (b, 0, 2))],
        out_specs=pl.BlockSpec((None, qt, gw), lambda b, i: (b, i, 0)),
        scratch_shapes=[pltpu.VMEM((SB_CHAINS, HEADS_PER_GROUP * QBLK, 1), _F32),
                        pltpu.VMEM((SB_CHAINS, HEADS_PER_GROUP * QBLK, gw), _F32),
                        pltpu.VMEM((SB_CHAINS, HEADS_PER_GROUP * QBLK, gw), _BF16)],
        compiler_params=_params("arbitrary", "arbitrary"),
        name="sb",
    )(view, view, view)
    return o.reshape(batch * seq, gw)


def _merge_body(*refs):
    cnt_ref, hh_a, hl_a, hh_b, hl_b = refs[25], refs[30], refs[31], refs[32], refs[33]
    step = pl.program_id(0)

    @pl.when(step == 0)
    def _():
        cnt_ref[...] = jnp.zeros_like(cnt_ref)
        hh_b[...] = jnp.zeros_like(hh_b)
        hl_b[...] = jnp.zeros_like(hl_b)

    @pl.when(step % 2 == 0)
    def _():
        _merge_step(*refs[:30], hh_a, hl_a, hh_b, hl_b)

    @pl.when(step % 2 == 1)
    def _():
        _merge_step(*refs[:30], hh_b, hl_b, hh_a, hl_a)


def _merge_step(x_ref, mod_ref, g1_ref, g2_ref, o0_ref, o1_ref, o2_ref, l0_ref, l1_ref, l2_ref, osb_ref,
                wbg_ref, wbd_ref, wbs_ref, wout_ref, wrh_ref, wrl_ref, br_ref, ex_ref, tri_ref,
                x1_ref, h2p_ref, rinfo_ref, er_ref, cnt_out_ref, cnt_ref, os1_ref, os2_ref, ls1_ref, ls2_ref,
                keep_hh_ref, keep_hl_ref, prev_hh_ref, prev_hl_ref):
    d = D_MODEL
    tm = x_ref.shape[0]

    hh, hl = prev_hh_ref[...], prev_hl_ref[...]
    logits = _dot(hh, wrh_ref[...]) + (_dot(hl, wrh_ref[...]) + _dot(hh, wrl_ref[...])) + br_ref[...]
    lane = lax.broadcasted_iota(jnp.int32, (tm, ROUTER_LANES), 1).astype(_F32)
    far = float(ROUTER_LANES)

    def top(vals):
        m = jnp.max(vals, axis=1, keepdims=True)
        return m, jnp.min(jnp.where(vals == m, lane, far), axis=1, keepdims=True)

    is_group = lane < N_GROUPS
    mg, gsel = top(jnp.where(is_group, logits, NEG_BIG))
    pg_top = 1.0 / jnp.sum(jnp.where(is_group, jnp.exp(logits - mg), 0.0), axis=1, keepdims=True)
    lane0 = EXPERT_LANE0 + EXPERTS_PER_GROUP * gsel
    le = jnp.where((lane >= lane0) & (lane < lane0 + EXPERTS_PER_GROUP), logits, NEG_BIG)
    m1, i1 = top(le)
    m2, i2 = top(jnp.where(lane == i1, NEG_BIG, le))
    t2 = jnp.exp(m2 - m1)
    w0 = pg_top / (1.0 + t2)
    w1 = pg_top * t2 / (1.0 + t2)
    sel0, sel1 = lane == i1, lane == i2
    onehot = jnp.where((sel0 | sel1) & (pl.program_id(0) > 0), 1.0, 0.0)

    x = x_ref[...]
    hb = _rms_mod(x, g1_ref[...], mod_ref[:, d:2 * d], mod_ref[:, 0:d]).astype(_BF16)

    def natural(ref, st_ref, dil):
        if dil == 1:
            return ref[...].astype(_F32)
        slabs = st_ref.shape[0]
        for r in range(dil):
            for s in range(slabs):
                col0 = (r * slabs + s) * LANES
                st_ref[s, pl.ds(r, tm // dil, stride=dil), :] = ref[:, col0:col0 + LANES].astype(_F32)
        return jnp.concatenate([st_ref[s] for s in range(slabs)], axis=1)

    dils = [dl for _, dl in DIL_PATTERNS]
    o_nat = [natural(r, s, dl) for r, s, dl in zip((o0_ref, o1_ref, o2_ref), (None, os1_ref, os2_ref), dils)]
    l0, l1, l2 = [natural(r, s, dl) for r, s, dl in zip((l0_ref, l1_ref, l2_ref), (None, ls1_ref, ls2_ref), dils)]

    lmax = jnp.maximum(jnp.maximum(l0, l1), l2)
    e0, e1, e2 = jnp.exp(l0 - lmax), jnp.exp(l1 - lmax), jnp.exp(l2 - lmax)
    inv = 1.0 / (e0 + e1 + e2)
    ex = ex_ref[...]

    def widen(w):
        hi, lo = _split(w)
        return _dot(jnp.concatenate([hi, lo], axis=1), ex)

    w_groups = [widen(e * inv) for e in (e0, e1, e2)]
    gate_dil = jax.nn.sigmoid(_dot(hb, wbg_ref[:, :d]))
    o_dil = w_groups[0] * o_nat[0] + w_groups[1] * o_nat[1] + w_groups[2] * o_nat[2]
    branch_dil = _dot(o_dil.astype(_BF16), wbd_ref[...])
    branch_sb = _dot(osb_ref[...], wbs_ref[...])
    gate_sb = jax.nn.sigmoid(_dot(hb, wbg_ref[:, d:]))
    merged = gate_dil * branch_dil + gate_sb * branch_sb
    x1 = x + mod_ref[:, 2 * d:3 * d] * _dot(merged.astype(_BF16), wout_ref[...])
    x1_ref[...] = x1

    h2 = _rms_mod(x1, g2_ref[...], mod_ref[:, 4 * d:5 * d], mod_ref[:, 3 * d:4 * d])
    h2p_ref[...] = _pack_halves(h2)
    hh, hl = _split(h2)
    keep_hh_ref[...] = hh
    keep_hl_ref[...] = hl

    before = _dot(tri_ref[...], onehot.astype(_BF16)) + cnt_ref[0:1, :]
    r0 = jnp.sum(jnp.where(sel0, before, 0.0), axis=1, keepdims=True)
    r1 = jnp.sum(jnp.where(sel1, before, 0.0), axis=1, keepdims=True)
    cnt_ref[...] += jnp.sum(onehot, axis=0, keepdims=True)
    cnt_out_ref[...] = cnt_ref[...]
    cols = (i1 - EXPERT_LANE0, i2 - EXPERT_LANE0, r0, r1, w0, w1)
    rinfo = jnp.zeros((tm, ROUTER_LANES), _F32)
    for c, v in enumerate(cols):
        rinfo = jnp.where(lane == float(c), v, rinfo)
    rinfo_ref[...] = rinfo
    er_ref[...] = jnp.transpose(rinfo)[0:8, :].astype(jnp.int32)


def _merge(x2, mod3, g1, g2, outs, lses, osb, wbg, wbd, wbs, wout, wrh, wrl, br, ex, tri, seq, part):
    d = x2.shape[1]
    tm = ROW_TILE
    per_b = seq // tm
    t = x2.shape[0] // MOE_PARTS
    nt = t // tm
    first = part * nt
    tile = lambda i: jnp.minimum(i, nt - 1)
    src = lambda i: (tile(i) + first, 0)
    row = lambda i: (tile(i), 0)
    routed = lambda i: jnp.maximum(i - 1, 0)
    const = lambda i: (0, 0)
    full = lambda a: pl.BlockSpec(a.shape, const)
    gw = GROUP_WIDTH
    dils = [dl for _, dl in DIL_PATTERNS]
    return pl.pallas_call(
        _merge_body,
        out_shape=[jax.ShapeDtypeStruct((t, d), _F32),
                   jax.ShapeDtypeStruct((t, d // 2), jnp.uint32),
                   jax.ShapeDtypeStruct((t, ROUTER_LANES), _F32),
                   jax.ShapeDtypeStruct((8, t), jnp.int32),
                   jax.ShapeDtypeStruct((8, ROUTER_LANES), _F32)],
        grid=(nt + 1,),
        in_specs=[pl.BlockSpec((tm, d), src),
                  pl.BlockSpec((None, 1, mod3.shape[2]), lambda i: ((tile(i) + first) // per_b, 0, 0)),
                  full(g1), full(g2)]
                 + [pl.BlockSpec((tm // dl, dl * gw), src) for dl in dils]
                 + [pl.BlockSpec((tm // dl, dl * LANES), src) for dl in dils]
                 + [pl.BlockSpec((tm, gw), src)]
                 + [full(a) for a in (wbg, wbd, wbs, wout, wrh, wrl, br, ex, tri)],
        out_specs=[pl.BlockSpec((tm, d), row),
                   pl.BlockSpec((tm, d // 2), row),
                   pl.BlockSpec((tm, ROUTER_LANES), lambda i: (routed(i), 0)),
                   pl.BlockSpec((8, tm), lambda i: (0, routed(i))),
                   pl.BlockSpec((8, ROUTER_LANES), const)],
        scratch_shapes=[pltpu.VMEM((8, ROUTER_LANES), _F32),
                        pltpu.VMEM((gw // LANES, tm, LANES), _F32), pltpu.VMEM((gw // LANES, tm, LANES), _F32),
                        pltpu.VMEM((1, tm, LANES), _F32), pltpu.VMEM((1, tm, LANES), _F32)]
                       + [pltpu.VMEM((tm, d), _BF16)] * 4,
        compiler_params=_params("arbitrary"),
        name="merge",
    )(x2, mod3, g1, g2, *outs, *lses, osb, wbg, wbd, wbs, wout, wrh, wrl, br, ex, tri)


def _dest_body(ps_ref, er_ref, d_ref):
    e = er_ref[0:2, :]
    start = jnp.zeros_like(e)
    for x in range(N_EXPERTS):
        start = jnp.where(e == x, ps_ref[x], start)
    d_ref[...] = start + er_ref[2:4, :]


def _dest(pstart, er):
    t = er.shape[1]
    tw = min(DEST_TILE, t)
    return pl.pallas_call(
        _dest_body,
        out_shape=jax.ShapeDtypeStruct((2, t), jnp.int32),
        grid_spec=pltpu.PrefetchScalarGridSpec(
            num_scalar_prefetch=1,
            grid=(t // tw,),
            in_specs=[pl.BlockSpec((8, tw), lambda i, ps: (0, i))],
            out_specs=pl.BlockSpec((2, tw), lambda i, ps: (0, i))),
        compiler_params=_params("arbitrary"),
        name="dest",
    )(pstart, er)


def _sc_mesh():
    return plsc.VectorSubcoreMesh(core_axis_name="core", subcore_axis_name="subcore",
                                  num_cores=SC_CORES, num_subcores=SC_SUBCORES)


def _sc_worker():
    return lax.axis_index("subcore") * SC_CORES + lax.axis_index("core")


def _sc_scatter(x, idx0, idx1, n_slots):
    chunks = idx0.shape[0]
    per = chunks // SC_WORKERS
    win = idx0.shape[1]

    @functools.partial(
        pl.kernel, mesh=_sc_mesh(), out_type=jax.ShapeDtypeStruct((n_slots, x.shape[1]), x.dtype),
        scratch_types=[pltpu.VMEM((1, win), jnp.int32), pltpu.VMEM((1, win), jnp.int32),
                       pltpu.VMEM((win, x.shape[1]), x.dtype), pltpu.SemaphoreType.DMA],
        name="sc_scatter")
    def run(x_hbm, i0_hbm, i1_hbm, o_hbm, i0_v, i1_v, rows_v, sem):
        wid = _sc_worker()

        @pl.loop(0, per)
        def _(j):
            c = wid * per + j
            pltpu.sync_copy(i0_hbm.at[pl.ds(c, 1)], i0_v)
            pltpu.sync_copy(i1_hbm.at[pl.ds(c, 1)], i1_v)
            pltpu.sync_copy(x_hbm.at[pl.ds(c * win, win)], rows_v)
            first = pltpu.async_copy(rows_v, o_hbm.at[i0_v.at[0]], sem)
            second = pltpu.async_copy(rows_v, o_hbm.at[i1_v.at[0]], sem)
            first.wait()
            second.wait()

    return run(x, idx0, idx1)


def _sc_gather(table, idx):
    chunks, win = idx.shape
    per = chunks // SC_WORKERS

    @functools.partial(
        pl.kernel, mesh=_sc_mesh(), out_type=jax.ShapeDtypeStruct((chunks * win, table.shape[1]), table.dtype),
        scratch_types=[pltpu.VMEM((1, win), jnp.int32), pltpu.VMEM((win, table.shape[1]), table.dtype),
                       pltpu.SemaphoreType.DMA],
        name="sc_gather")
    def run(t_hbm, i_hbm, o_hbm, i_v, rows_v, sem):
        wid = _sc_worker()

        @pl.loop(0, per)
        def _(j):
            c = wid * per + j
            pltpu.sync_copy(i_hbm.at[pl.ds(c, 1)], i_v)
            pltpu.async_copy(t_hbm.at[i_v.at[0]], rows_v, sem).wait()
            pltpu.sync_copy(rows_v, o_hbm.at[pl.ds(c * win, win)])

    return run(table, idx)


def _pack_halves(a):
    h = a.shape[1] // 2
    lo = lax.bitcast_convert_type(a[:, :h].astype(_BF16).astype(_F32), jnp.uint32) >> 16
    hi = lax.bitcast_convert_type(a[:, h:].astype(_BF16).astype(_F32), jnp.uint32) & jnp.uint32(0xFFFF0000)
    return lo | hi


def _unpack_halves(w):
    return jnp.concatenate(
        [lax.bitcast_convert_type(w << 16, _F32), lax.bitcast_convert_type(w & jnp.uint32(0xFFFF0000), _F32)], axis=1)


def _experts_body(ce_ref, nv_ref, nu_ref, seg_ref, nxt_ref, xs_ref, wg_ref, wu_ref, wd_ref, ys_ref,
                  wgb_ref, wub_ref, wdb_ref, wgf_ref, wuf_ref, wdf_ref, sem):
    c = pl.program_id(0)

    def fetch(expert, slot):
        return [pltpu.make_async_copy(src.at[expert], dst.at[slot], sem.at[slot])
                for src, dst in ((wg_ref, wgf_ref), (wu_ref, wuf_ref), (wd_ref, wdf_ref))]

    @pl.when(c < nu_ref[0])
    def _():
        @pl.when(seg_ref[c] >= 0)
        def _():
            slot = seg_ref[c] & 1

            @pl.when(c == 0)
            def _():
                for copy in fetch(ce_ref[0], 0):
                    copy.start()

            for copy in fetch(ce_ref[c], slot):
                copy.wait()
            wgb_ref[...] = wgf_ref[slot].astype(_BF16)
            wub_ref[...] = wuf_ref[slot].astype(_BF16)
            wdb_ref[...] = wdf_ref[slot].astype(_BF16)

            @pl.when(nxt_ref[c] >= 0)
            def _():
                for copy in fetch(nxt_ref[c], 1 - slot):
                    copy.start()

        row = lax.broadcasted_iota(jnp.int32, xs_ref.shape, 0)
        x = _unpack_halves(jnp.where(row < nv_ref[c], xs_ref[...], jnp.uint32(0))).astype(_BF16)
        half = x.shape[0] // 2
        ups = [(_dot(x[r:r + half], wgb_ref[...]), _dot(x[r:r + half], wub_ref[...])) for r in (0, half)]
        for (g, u), r in zip(ups, (0, half)):
            hmid = (g * jax.nn.sigmoid(g)) * u
            ys_ref[r:r + half, :] = _pack_halves(_dot(hmid.astype(_BF16), wdb_ref[...]))


def _experts(chunk_e, n_valid, n_used, seg, nxt, xs, wg, wu, wd):
    n_slots, w = xs.shape
    ch = EXPERT_CHUNK
    d, de = wg.shape[1], wg.shape[2]
    slot = lambda c, ce, nv, nu, sg, nx: (jnp.minimum(c, nu[0] - 1), 0)
    hbm = pl.BlockSpec(memory_space=pl.ANY)
    return pl.pallas_call(
        _experts_body,
        out_shape=jax.ShapeDtypeStruct((n_slots, d // 2), jnp.uint32),
        grid_spec=pltpu.PrefetchScalarGridSpec(
            num_scalar_prefetch=5,
            grid=(n_slots // ch,),
            in_specs=[pl.BlockSpec((ch, w), slot), hbm, hbm, hbm],
            out_specs=pl.BlockSpec((ch, d // 2), slot),
            scratch_shapes=[pltpu.VMEM((d, de), _BF16), pltpu.VMEM((d, de), _BF16), pltpu.VMEM((de, d), _BF16),
                            pltpu.VMEM((2, d, de), _F32), pltpu.VMEM((2, d, de), _F32), pltpu.VMEM((2, de, d), _F32),
                            pltpu.SemaphoreType.DMA((2,))]),
        compiler_params=_params("arbitrary"),
        name="experts",
    )(chunk_e, n_valid, n_used, seg, nxt, xs, wg, wu, wd)


def _combine_body(x1_ref, rinfo_ref, mod_ref, y0_ref, y1_ref, *rest):
    o_ref = rest[-1]
    y = rinfo_ref[:, 4:5] * _unpack_halves(y0_ref[...]) + rinfo_ref[:, 5:6] * _unpack_halves(y1_ref[...])
    o_ref[...] = x1_ref[...] + mod_ref[:, 5 * D_MODEL:6 * D_MODEL] * y


def _combine(x1, rinfo, mod3, gathered, seq, part, out_so_far):
    t, d = x1.shape
    tf = min(COMBINE_TILE, seq)
    per_b = seq // tf
    nt = t // tf
    first = part * nt
    in_specs = [pl.BlockSpec((tf, d), lambda i: (i, 0)),
                pl.BlockSpec((tf, ROUTER_LANES), lambda i: (i, 0)),
                pl.BlockSpec((None, 1, mod3.shape[2]), lambda i: ((i + first) // per_b, 0, 0)),
                pl.BlockSpec((tf, d // 2), lambda i: (i, 0)),
                pl.BlockSpec((tf, d // 2), lambda i: (i + nt, 0))]
    args = [x1, rinfo, mod3, gathered, gathered]
    aliases = {}
    if out_so_far is not None:
        in_specs.append(pl.BlockSpec(memory_space=pl.ANY))
        args.append(out_so_far)
        aliases = {len(args) - 1: 0}
    return pl.pallas_call(
        _combine_body,
        out_shape=jax.ShapeDtypeStruct((t * MOE_PARTS, d), _F32),
        grid=(nt,),
        in_specs=in_specs,
        out_specs=pl.BlockSpec((tf, d), lambda i: (i + first, 0)),
        input_output_aliases=aliases,
        compiler_params=_params("arbitrary"),
        name="combine",
    )(*args)


def _rope_trig(positions):
    inv_freq = ROPE_THETA ** (-jnp.arange(0, ROPE_DIM, 2, dtype=_F32) / ROPE_DIM)
    ang = positions.reshape(-1).astype(_F32)[:, None] * inv_freq
    return jnp.pad(jnp.concatenate([jnp.cos(ang), jnp.sin(ang)], axis=1), ((0, 0), (0, LANES - ROPE_DIM)))


def _layer(x, mod, positions, g_mix, g_ffn, w_in, w_bg, qg, kg, w_bd, w_bs, w_out, w_rg, b_rg, w_re, b_re,
           w_eg, w_eu, w_ed):
    batch, seq, d = x.shape
    t = batch * seq
    x2 = x.reshape(t, d)
    mod3 = mod.reshape(batch, 1, mod.shape[1])
    gw = GROUP_WIDTH

    lane = jnp.arange(gw)
    bd = jnp.where(lane[:, None] // HEAD_DIM == lane[None, :] // HEAD_DIM, 1.0 / HEAD_DIM, 0.0).astype(_BF16)
    ex = (jnp.arange(LANES)[:, None] == (lane[None, :] // HEAD_DIM) * LSE_SEG).astype(_BF16)
    ex = jnp.concatenate([ex, ex], axis=0)
    tri = (jnp.arange(ROW_TILE)[:, None] > jnp.arange(ROW_TILE)[None, :]).astype(_BF16)
    tile4 = lambda g: jnp.tile(g.astype(_F32), HEADS_PER_GROUP).reshape(1, gw)
    wr = jnp.zeros((d, ROUTER_LANES), _F32).at[:, :N_GROUPS].set(w_rg).at[:, N_GROUPS:N_GROUPS + N_EXPERTS].set(w_re)
    wrh = wr.astype(_BF16)
    wrl = (wr - wrh.astype(_F32)).astype(_BF16)
    br = jnp.zeros((1, ROUTER_LANES), _F32).at[0, :N_GROUPS].set(b_rg).at[0, N_GROUPS:N_GROUPS + N_EXPERTS].set(b_re)

    d0, d1, d2, sbp = _qkv(x2, mod3, g_mix.reshape(1, d), w_in.astype(_BF16), tile4(qg) * ATTN_SCALE, tile4(kg),
                           _rope_trig(positions), bd, seq)
    dil = [_dilated_group(a, dl, batch, seq) for a, (_, dl) in zip((d0, d1, d2), DIL_PATTERNS)]
    osb = _stick_breaking(sbp, batch, seq)

    merge_weights = (w_bg.astype(_BF16), w_bd.astype(_BF16), w_bs.astype(_BF16), w_out.astype(_BF16))
    tp = t // MOE_PARTS
    ch = EXPERT_CHUNK
    win = SC_INDEX_WINDOW
    n_chunks = -(-2 * tp // ch) + N_EXPERTS
    chunk_start = jnp.arange(n_chunks, dtype=jnp.int32) * ch
    expert_ids = jnp.arange(N_EXPERTS, dtype=jnp.int32)
    out = None
    for part in range(MOE_PARTS):
        x1, h2p, rinfo, er, cnt = _merge(
            x2, mod3, g_mix.reshape(1, d), g_ffn.reshape(1, d), [o for o, _ in dil], [l for _, l in dil], osb,
            *merge_weights, wrh, wrl, br, ex, tri, seq, part)

        counts = cnt[0, EXPERT_LANE0:EXPERT_LANE0 + N_EXPERTS].astype(jnp.int32)
        padded = (counts + ch - 1) // ch * ch
        pend = jnp.cumsum(padded)
        pstart = pend - padded
        chunk_e = jnp.minimum(jnp.sum((pend[None, :] <= chunk_start[:, None]).astype(jnp.int32), axis=1),
                              N_EXPERTS - 1)
        n_used = (pend[-1:] // ch).astype(jnp.int32)
        begin = chunk_start[:, None]
        inside = (pstart[None, :] <= begin) & (begin < pend[None, :])
        n_valid = jnp.sum(jnp.where(inside, jnp.clip(counts[None, :] - (begin - pstart[None, :]), 0, ch), 0), axis=1)

        dest = _dest(pstart, er)
        xs = _sc_scatter(h2p, dest[0].reshape(tp // win, win), dest[1].reshape(tp // win, win), n_chunks * ch)
        first = (chunk_e != jnp.concatenate([jnp.full((1,), -1, jnp.int32), chunk_e[:-1]])) & (chunk_start < pend[-1])
        seg_no = jnp.cumsum(first.astype(jnp.int32)) - 1
        seg = jnp.where(first, seg_no, -1 - seg_no)
        later = (expert_ids[None, :] > expert_ids[:, None]) & (padded > 0)[None, :]
        next_expert = jnp.min(jnp.where(later, expert_ids[None, :], N_EXPERTS), axis=1)
        next_expert = jnp.where(next_expert == N_EXPERTS, -1, next_expert)
        nxt = jnp.sum(jnp.where(chunk_e[:, None] == expert_ids[None, :], next_expert[None, :], 0), axis=1)
        ys = _experts(chunk_e, n_valid, n_used, seg.astype(jnp.int32), nxt.astype(jnp.int32), xs, w_eg, w_eu, w_ed)
        gathered = _sc_gather(ys, dest.reshape(2 * tp // win, win))
        out = _combine(x1, rinfo, mod3, gathered, seq, part, out)
    return out.reshape(batch, seq, d)


def kernel(x, c, positions, w_ada, b_ada, g_norm_mix, g_norm_ffn, w_in, w_branch_gate, q_norm_g, k_norm_g,
           w_branch_dil, w_branch_sb, w_out, w_router_group, b_router_group, w_router_expert, b_router_expert,
           w_expert_gate, w_expert_up, w_expert_down):
    for l in range(w_ada.shape[0]):
        mod = _ada(c, w_ada[l], b_ada[l])
        x = _layer(x, mod, positions, g_norm_mix[l], g_norm_ffn[l], w_in[l], w_branch_gate[l], q_norm_g[l],
                   k_norm_g[l], w_branch_dil[l], w_branch_sb[l], w_out[l], w_router_group[l], b_router_group[l],
                   w_router_expert[l], b_router_expert[l], w_expert_gate[l], w_expert_up[l], w_expert_down[l])
    return x
```

```python
import functools

import jax
import jax.numpy as jnp
from jax import lax
from jax.experimental import pallas as pl
from jax.experimental.pallas import tpu as pltpu
from jax.experimental.pallas import tpu_sc as plsc

D_MODEL = 1024
HEAD_DIM = 64
DIL_PATTERNS = ((128, 1), (512, 4), (2048, 16))
HEADS_PER_GROUP = 4
GROUP_WIDTH = HEADS_PER_GROUP * HEAD_DIM
N_DIL_GROUPS = len(DIL_PATTERNS)
DIL_WIDTH = N_DIL_GROUPS * GROUP_WIDTH
QKV_WIDTH = 3 * DIL_WIDTH + 3 * GROUP_WIDTH
WINDOW_KEYS = 128
ROPE_THETA = 500000.0
ROPE_DIM = HEAD_DIM // 4
N_GROUPS = 4
EXPERTS_PER_GROUP = 8
N_EXPERTS = N_GROUPS * EXPERTS_PER_GROUP
D_EXPERT = 512
RMS_EPS = 1e-6
ATTN_SCALE = HEAD_DIM ** -0.5

LANES = 128
ROUTER_LANES = LANES
EXPERT_LANE0 = N_GROUPS
LSE_SEG = LANES // HEADS_PER_GROUP
NEG_BIG = -1e30
SB_DEAD_LOG = -120.0
SB_HEAD_BLOCKS = 3
SB_CHAINS = 4

ROW_TILE = 512
QBLK = 128
DIL_QTILE = 1024
EXPERT_CHUNK = 512
COMBINE_TILE = 1024
DEST_TILE = 8192
SC_CORES = 2
SC_SUBCORES = 16
SC_WORKERS = SC_CORES * SC_SUBCORES
SC_INDEX_WINDOW = 128
MOE_PARTS = 2
VMEM_LIMIT = 48 * 1024 * 1024

_BF16 = jnp.bfloat16
_F32 = jnp.float32
_NT = (((1,), (1,)), ((), ()))


def _dot(a, b):
    return jnp.dot(a, b, preferred_element_type=_F32)


def _dot_nt(a, b):
    return lax.dot_general(a, b, _NT, preferred_element_type=_F32)


def _split(a):
    hi = a.astype(_BF16)
    lo = (a - hi.astype(_F32)).astype(_BF16)
    return hi, lo


def _dot3(a, b):
    ah, al = _split(a)
    bh, bl = _split(b)
    return _dot(ah, bh) + (_dot(ah, bl) + _dot(al, bh))


def _rms_mod(x, g, scale, shift):
    y = x * lax.rsqrt(jnp.mean(x * x, axis=-1, keepdims=True) + RMS_EPS)
    return y * g * (1.0 + scale) + shift


def _params(*sem):
    return pltpu.CompilerParams(dimension_semantics=sem, vmem_limit_bytes=VMEM_LIMIT)


def _ada_body(c_ref, w_ref, b_ref, o_ref):
    c = c_ref[...]
    o_ref[...] = _dot3(c * jax.nn.sigmoid(c), w_ref[...]) + b_ref[...]


def _ada(c, w_ada, b_ada):
    b, d = c.shape
    n = w_ada.shape[1]
    rows = -(-b // 16) * 16
    cp = jnp.zeros((rows, d), _F32).at[:b].set(c)
    nt = 1536
    out = pl.pallas_call(
        _ada_body,
        out_shape=jax.ShapeDtypeStruct((rows, n), _F32),
        grid=(n // nt,),
        in_specs=[pl.BlockSpec((rows, d), lambda j: (0, 0)),
                  pl.BlockSpec((d, nt), lambda j: (0, j)),
                  pl.BlockSpec((1, nt), lambda j: (0, j))],
        out_specs=pl.BlockSpec((rows, nt), lambda j: (0, j)),
        compiler_params=_params("arbitrary"),
        name="ada",
    )(cp, w_ada, b_ada.reshape(1, n))
    return out[:b]


def _qkv_body(x_ref, mod_ref, g_ref, w_ref, qg_ref, kg_ref, trig_ref, bd_ref,
              o0_ref, o1_ref, o2_ref, osb_ref, st_ref, acc_ref):
    d = D_MODEL
    tm = x_ref.shape[0]
    h = _rms_mod(x_ref[...], g_ref[...], mod_ref[:, d:2 * d], mod_ref[:, 0:d])
    hb = h.astype(_BF16)
    half = ROPE_DIM // 2
    in_head = lax.broadcasted_iota(jnp.int32, (tm, LANES), 1) & (HEAD_DIM - 1)
    freq = in_head & (half - 1)
    trig = jnp.concatenate([trig_ref[...], jnp.zeros((tm, LANES - ROPE_DIM), _F32)], axis=1)
    cos = jnp.take_along_axis(trig, freq, axis=1)
    sin = jnp.take_along_axis(trig, freq + half, axis=1)
    cc = jnp.where(in_head < ROPE_DIM, cos, 1.0)
    s1 = jnp.where(in_head < half, -sin, 0.0)
    s2 = jnp.where((in_head >= half) & (in_head < ROPE_DIM), sin, 0.0)
    cc, s1, s2 = [jnp.concatenate([a, a], axis=1) for a in (cc, s1, s2)]
    bd = bd_ref[...]
    gw = GROUP_WIDTH

    def normed_rotated(acc, gain):
        ms = _dot((acc * acc).astype(_BF16), bd)
        y = acc * lax.rsqrt(ms + RMS_EPS) * gain
        return y * cc + pltpu.roll(y, gw - ROPE_DIM // 2, 1) * s1 + pltpu.roll(y, ROPE_DIM // 2, 1) * s2

    def store(o_ref, dil, part, y):
        if dil == 1:
            o_ref[:, part * gw:(part + 1) * gw] = y.astype(_BF16)
            return
        for s in range(gw // LANES):
            st_ref[s] = y[:, s * LANES:(s + 1) * LANES]
        for r in range(dil):
            for s in range(gw // LANES):
                col0 = (3 * r + part) * gw + s * LANES
                o_ref[:, col0:col0 + LANES] = st_ref[s, pl.ds(r, tm // dil, stride=dil), :].astype(_BF16)

    def project(col0):
        return _dot(hb, w_ref[:, col0:col0 + gw])

    outs = (o0_ref, o1_ref, o2_ref)
    dils = [dl for _, dl in DIL_PATTERNS]
    normed = [(g, part, part * DIL_WIDTH + g * gw) for g in range(N_DIL_GROUPS) for part in (0, 1)]
    for n, (_, _, col0) in enumerate(normed):
        acc_ref[n] = project(col0)
    plain = [("v", g) for g in range(N_DIL_GROUPS)] + [("sb", part) for part in range(3)]
    for n, (kind, j) in enumerate(plain):
        if kind == "v":
            store(outs[j], dils[j], 2, project(2 * DIL_WIDTH + j * gw))
        else:
            acc = project(3 * DIL_WIDTH + j * gw)
            osb_ref[:, j * gw:(j + 1) * gw] = (acc * ATTN_SCALE if j == 0 else acc).astype(_BF16)
        g, part, _ = normed[n]
        gain = qg_ref[...] if part == 0 else kg_ref[...]
        store(outs[g], dils[g], part, normed_rotated(acc_ref[n], gain))


def _qkv(x2, mod3, g_mix, w_in, qg, kg, trig, bd, seq):
    t, d = x2.shape
    tm = ROW_TILE
    per_b = seq // tm
    row = lambda i: (i, 0)
    const = lambda i: (0, 0)
    width = 3 * GROUP_WIDTH
    dils = [dl for _, dl in DIL_PATTERNS] + [1]
    return pl.pallas_call(
        _qkv_body,
        out_shape=[jax.ShapeDtypeStruct((t // dl, dl * width), _BF16) for dl in dils],
        grid=(t // tm,),
        in_specs=[pl.BlockSpec((tm, d), row),
                  pl.BlockSpec((None, 1, mod3.shape[2]), lambda i: (i // per_b, 0, 0)),
                  pl.BlockSpec((1, d), const),
                  pl.BlockSpec(w_in.shape, const),
                  pl.BlockSpec((1, GROUP_WIDTH), const),
                  pl.BlockSpec((1, GROUP_WIDTH), const),
                  pl.BlockSpec((tm, ROPE_DIM), row),
                  pl.BlockSpec(bd.shape, const)],
        out_specs=[pl.BlockSpec((tm // dl, dl * width), row) for dl in dils],
        scratch_shapes=[pltpu.VMEM((GROUP_WIDTH // LANES, tm, LANES), _F32),
                        pltpu.VMEM((2 * N_DIL_GROUPS, tm, GROUP_WIDTH), _F32)],
        compiler_params=_params("arbitrary"),
        name="qkv",
    )(x2, mod3, g_mix, w_in, qg, kg, trig, bd)


def _dil_body(q_ref, kp_ref, kc_ref, vp_ref, vc_ref, o_ref, lse_ref, kf_ref, vf_ref):
    tq = q_ref.shape[0]
    first = pl.program_id(2) == 0
    kf_ref[0:QBLK, :] = kp_ref[...]
    kf_ref[QBLK:, :] = kc_ref[...]
    vf_ref[0:QBLK, :] = vp_ref[...]
    vf_ref[QBLK:, :] = vc_ref[...]
    nh = HEADS_PER_GROUP
    row = lax.broadcasted_iota(jnp.int32, (nh * QBLK, 2 * QBLK), 0) & (QBLK - 1)
    col = lax.broadcasted_iota(jnp.int32, (nh * QBLK, 2 * QBLK), 1)
    band = (col >= row) & (col <= row + WINDOW_KEYS)
    lane = lax.broadcasted_iota(jnp.int32, (1, GROUP_WIDTH), 1)
    slane = lax.broadcasted_iota(jnp.int32, (1, LANES), 1)
    head_masks = [(lane >= h * HEAD_DIM) & (lane < (h + 1) * HEAD_DIM) for h in range(nh)]
    subs = range(tq // QBLK)
    rows = [slice(j * QBLK, (j + 1) * QBLK) for j in subs]
    window = [slice(j * QBLK, (j + 2) * QBLK) for j in subs]

    def stack(qj):
        return jnp.concatenate([jnp.where(hm, qj, jnp.zeros_like(qj)) for hm in head_masks], axis=0)

    valid = [band & ((col >= QBLK) | jnp.logical_not(first)) if j == 0 else band for j in subs]
    s = [jnp.where(valid[j], _dot_nt(stack(q_ref[rows[j], :]), kf_ref[window[j], :]), NEG_BIG) for j in subs]
    m = [jnp.max(s[j], axis=1, keepdims=True) for j in subs]
    p = [jnp.exp(s[j] - m[j]) for j in subs]
    l = [jnp.sum(p[j], axis=1, keepdims=True) for j in subs]
    o_all = [_dot(p[j].astype(_BF16), vf_ref[window[j], :]) / l[j] for j in subs]
    for j in subs:
        lse_all = m[j] + jnp.log(l[j])
        o_acc = jnp.zeros((QBLK, GROUP_WIDTH), _F32)
        lse_t = jnp.zeros((QBLK, LANES), _F32)
        for h, hm in enumerate(head_masks):
            o_acc = jnp.where(hm, o_all[j][h * QBLK:(h + 1) * QBLK, :], o_acc)
            sm = (slane >= h * LSE_SEG) & (slane < (h + 1) * LSE_SEG)
            lse_t = jnp.where(sm, lse_all[h * QBLK:(h + 1) * QBLK, :], lse_t)
        o_ref[rows[j], :] = o_acc.astype(_BF16)
        lse_ref[rows[j], :] = lse_t


def _dilated_group(view2, dil, batch, seq):
    sd = seq // dil
    tq = min(DIL_QTILE, sd)
    per = tq // QBLK
    gw = GROUP_WIDTH
    view = view2.reshape(batch, sd, dil * 3 * gw)
    cur = lambda part: pl.BlockSpec((None, tq, gw), lambda b, r, i: (b, i, 3 * r + part))
    prev = lambda part: pl.BlockSpec((None, QBLK, gw),
                                     lambda b, r, i: (b, jnp.maximum(i * per - 1, 0), 3 * r + part))
    o, lse = pl.pallas_call(
        _dil_body,
        out_shape=[jax.ShapeDtypeStruct((batch, sd, dil * gw), _BF16),
                   jax.ShapeDtypeStruct((batch, sd, dil * LANES), _F32)],
        grid=(batch, dil, sd // tq),
        in_specs=[cur(0), prev(1), cur(1), prev(2), cur(2)],
        out_specs=[pl.BlockSpec((None, tq, gw), lambda b, r, i: (b, i, r)),
                   pl.BlockSpec((None, tq, LANES), lambda b, r, i: (b, i, r))],
        scratch_shapes=[pltpu.VMEM((tq + QBLK, gw), _BF16), pltpu.VMEM((tq + QBLK, gw), _BF16)],
        compiler_params=_params("arbitrary", "arbitrary", "arbitrary"),
        name=f"dil{dil}",
    )(view, view, view, view, view)
    return o.reshape(batch * sd, dil * gw), lse.reshape(batch * sd, dil * LANES)


def _sb_body(q_ref, k_ref, v_ref, o_ref, carry_ref, acc_ref, qs_ref):
    step_id = pl.program_id(1)
    chains = range(SB_CHAINS)
    blk = [step_id * SB_CHAINS + c for c in chains]
    nh = HEADS_PER_GROUP
    row = lax.broadcasted_iota(jnp.int32, (nh * QBLK, QBLK), 0) & (QBLK - 1)
    col = lax.broadcasted_iota(jnp.int32, (nh * QBLK, QBLK), 1)
    strict = col < row
    ur = lax.broadcasted_iota(jnp.int32, (2 * QBLK, QBLK), 0) & (QBLK - 1)
    uc = lax.broadcasted_iota(jnp.int32, (2 * QBLK, QBLK), 1)
    u = jnp.where(ur > uc, 1.0, 0.0).astype(_BF16)
    lane = lax.broadcasted_iota(jnp.int32, (1, GROUP_WIDTH), 1)
    head_masks = [(lane >= h * HEAD_DIM) & (lane < (h + 1) * HEAD_DIM) for h in range(nh)]
    for c in chains:
        q = q_ref[c * QBLK:(c + 1) * QBLK, :]
        qs_ref[c] = jnp.concatenate([jnp.where(hm, q, jnp.zeros_like(q)) for hm in head_masks], axis=0)

    def softplus(z):
        return jnp.maximum(z, 0.0) + jnp.log(1.0 + jnp.exp(-jnp.abs(z)))

    def later_keys(log_1m):
        hi, lo = _split(log_1m)
        return _dot(jnp.concatenate([hi, lo], axis=1), u)

    nb = SB_HEAD_BLOCKS
    cols = [slice(j * QBLK, (j + 1) * QBLK) for j in range(nb)]
    kbs = [[blk[c] - (nb - 1) + j for j in range(nb)] for c in chains]
    starts = [[pl.multiple_of(jnp.maximum(kb, 0) * QBLK, QBLK) for kb in kbs[c]] for c in chains]
    keep = [[strict if j == nb - 1 else (kbs[c][j] >= 0) for j in range(nb)] for c in chains]
    z = [_dot_nt(qs_ref[c], jnp.concatenate([k_ref[pl.ds(s, QBLK), :] for s in starts[c]], axis=0)) for c in chains]
    sp = [softplus(z[c]) for c in chains]
    log_1m = [[jnp.where(keep[c][j], -sp[c][:, cols[j]], 0.0) for j in range(nb)] for c in chains]
    totals = [[jnp.sum(l, axis=1, keepdims=True) for l in log_1m[c]] for c in chains]
    later = [[later_keys(log_1m[c][j]) for j in range(nb)] for c in chains]
    for c in chains:
        a_blocks = []
        after = jnp.zeros_like(totals[c][0])
        for j in reversed(range(nb)):
            a = jnp.exp((z[c][:, cols[j]] - sp[c][:, cols[j]]) + later[c][j] + after)
            a_blocks.insert(0, jnp.where(keep[c][j], a, 0.0).astype(_BF16))
            after = after + totals[c][j]
        acc_ref[c] = _dot(jnp.concatenate(a_blocks, axis=1),
                          jnp.concatenate([v_ref[pl.ds(s, QBLK), :] for s in starts[c]], axis=0))
        carry_ref[c] = after

    for c in chains:
        def tile(kb, c=c):
            start = pl.multiple_of(kb * QBLK, QBLK)
            z = _dot_nt(qs_ref[c], k_ref[pl.ds(start, QBLK), :])
            sp = softplus(z)
            log_1m = -sp
            a = jnp.exp((z - sp) + later_keys(log_1m) + carry_ref[c])
            acc_ref[c] += _dot(a.astype(_BF16), v_ref[pl.ds(start, QBLK), :])
            carry = carry_ref[c] + jnp.sum(log_1m, axis=1, keepdims=True)
            carry_ref[c] = carry
            return jnp.max(carry)

        def cond(st):
            return (st[0] >= 0) & (st[1] > SB_DEAD_LOG)

        def step(st, tile=tile):
            return st[0] - 1, tile(st[0])

        lax.while_loop(cond, step, (blk[c] - nb, jnp.max(carry_ref[c])))
        out = jnp.zeros((QBLK, GROUP_WIDTH), _F32)
        for h, hm in enumerate(head_masks):
            out = jnp.where(hm, acc_ref[c, h * QBLK:(h + 1) * QBLK, :], out)
        o_ref[c * QBLK:(c + 1) * QBLK, :] = out.astype(_BF16)


def _stick_breaking(arr, batch, seq):
    gw = GROUP_WIDTH
    qt = SB_CHAINS * QBLK
    view = arr.reshape(batch, seq, 3 * gw)
    o = pl.pallas_call(
        _sb_body,
        out_shape=jax.ShapeDtypeStruct((batch, seq, gw), _BF16),
        grid=(batch, seq // qt),
        in_specs=[pl.BlockSpec((None, qt, gw), lambda b, i: (b, i, 0)),
                  pl.BlockSpec((None, seq, gw), lambda b, i: (b, 0, 1)),
                  pl.BlockSpec((None, seq, gw), lambda b, i: (b, 0, 2))],
        out_specs=pl.BlockSpec((None, qt, gw), lambda b, i: (b, i, 0)),
        scratch_shapes=[pltpu.VMEM((SB_CHAINS, HEADS_PER_GROUP * QBLK, 1), _F32),
                        pltpu.VMEM((SB_CHAINS, HEADS_PER_GROUP * QBLK, gw), _F32),
                        pltpu.VMEM((SB_CHAINS, HEADS_PER_GROUP * QBLK, gw), _BF16)],
        compiler_params=_params("arbitrary", "arbitrary"),
        name="sb",
    )(view, view, view)
    return o.reshape(batch * seq, gw)


def _merge_body(x_ref, mod_ref, g1_ref, g2_ref, o0_ref, o1_ref, o2_ref, l0_ref, l1_ref, l2_ref, osb_ref,
                wbg_ref, wbd_ref, wbs_ref, wout_ref, wrh_ref, wrl_ref, br_ref, ex_ref, tri_ref,
                x1_ref, h2p_ref, rinfo_ref, er_ref, cnt_out_ref, cnt_ref, os1_ref, os2_ref, ls1_ref, ls2_ref):
    d = D_MODEL
    tm = x_ref.shape[0]
    x = x_ref[...]
    hb = _rms_mod(x, g1_ref[...], mod_ref[:, d:2 * d], mod_ref[:, 0:d]).astype(_BF16)

    def natural(ref, st_ref, dil):
        if dil == 1:
            return ref[...].astype(_F32)
        slabs = st_ref.shape[0]
        for r in range(dil):
            for s in range(slabs):
                col0 = (r * slabs + s) * LANES
                st_ref[s, pl.ds(r, tm // dil, stride=dil), :] = ref[:, col0:col0 + LANES].astype(_F32)
        return jnp.concatenate([st_ref[s] for s in range(slabs)], axis=1)

    dils = [dl for _, dl in DIL_PATTERNS]
    o_nat = [natural(r, s, dl) for r, s, dl in zip((o0_ref, o1_ref, o2_ref), (None, os1_ref, os2_ref), dils)]
    l0, l1, l2 = [natural(r, s, dl) for r, s, dl in zip((l0_ref, l1_ref, l2_ref), (None, ls1_ref, ls2_ref), dils)]

    lmax = jnp.maximum(jnp.maximum(l0, l1), l2)
    e0, e1, e2 = jnp.exp(l0 - lmax), jnp.exp(l1 - lmax), jnp.exp(l2 - lmax)
    inv = 1.0 / (e0 + e1 + e2)
    ex = ex_ref[...]

    def widen(w):
        hi, lo = _split(w)
        return _dot(jnp.concatenate([hi, lo], axis=1), ex)

    w_groups = [widen(e * inv) for e in (e0, e1, e2)]
    gate_dil = jax.nn.sigmoid(_dot(hb, wbg_ref[:, :d]))
    o_dil = w_groups[0] * o_nat[0] + w_groups[1] * o_nat[1] + w_groups[2] * o_nat[2]
    branch_dil = _dot(o_dil.astype(_BF16), wbd_ref[...])
    branch_sb = _dot(osb_ref[...], wbs_ref[...])
    gate_sb = jax.nn.sigmoid(_dot(hb, wbg_ref[:, d:]))
    merged = gate_dil * branch_dil + gate_sb * branch_sb
    x1 = x + mod_ref[:, 2 * d:3 * d] * _dot(merged.astype(_BF16), wout_ref[...])
    x1_ref[...] = x1

    h2 = _rms_mod(x1, g2_ref[...], mod_ref[:, 4 * d:5 * d], mod_ref[:, 3 * d:4 * d])
    h2p_ref[...] = _pack_halves(h2)

    hh, hl = _split(h2)
    logits = _dot(hh, wrh_ref[...]) + (_dot(hl, wrh_ref[...]) + _dot(hh, wrl_ref[...])) + br_ref[...]
    lane = lax.broadcasted_iota(jnp.int32, (tm, ROUTER_LANES), 1).astype(_F32)
    far = float(ROUTER_LANES)

    def top(vals):
        m = jnp.max(vals, axis=1, keepdims=True)
        return m, jnp.min(jnp.where(vals == m, lane, far), axis=1, keepdims=True)

    is_group = lane < N_GROUPS
    mg, gsel = top(jnp.where(is_group, logits, NEG_BIG))
    pg_top = 1.0 / jnp.sum(jnp.where(is_group, jnp.exp(logits - mg), 0.0), axis=1, keepdims=True)
    lane0 = EXPERT_LANE0 + EXPERTS_PER_GROUP * gsel
    le = jnp.where((lane >= lane0) & (lane < lane0 + EXPERTS_PER_GROUP), logits, NEG_BIG)
    m1, i1 = top(le)
    m2, i2 = top(jnp.where(lane == i1, NEG_BIG, le))
    t2 = jnp.exp(m2 - m1)
    w0 = pg_top / (1.0 + t2)
    w1 = pg_top * t2 / (1.0 + t2)

    @pl.when(pl.program_id(0) == 0)
    def _():
        cnt_ref[...] = jnp.zeros_like(cnt_ref)

    sel0, sel1 = lane == i1, lane == i2
    onehot = jnp.where(sel0 | sel1, 1.0, 0.0)
    before = _dot(tri_ref[...], onehot.astype(_BF16)) + cnt_ref[0:1, :]
    r0 = jnp.sum(jnp.where(sel0, before, 0.0), axis=1, keepdims=True)
    r1 = jnp.sum(jnp.where(sel1, before, 0.0), axis=1, keepdims=True)
    cnt_ref[...] += jnp.sum(onehot, axis=0, keepdims=True)
    cnt_out_ref[...] = cnt_ref[...]
    cols = (i1 - EXPERT_LANE0, i2 - EXPERT_LANE0, r0, r1, w0, w1)
    rinfo = jnp.zeros((tm, ROUTER_LANES), _F32)
    for c, v in enumerate(cols):
        rinfo = jnp.where(lane == float(c), v, rinfo)
    rinfo_ref[...] = rinfo
    er_ref[...] = jnp.transpose(rinfo)[0:8, :].astype(jnp.int32)


def _merge(x2, mod3, g1, g2, outs, lses, osb, wbg, wbd, wbs, wout, wrh, wrl, br, ex, tri, seq, part):
    d = x2.shape[1]
    tm = ROW_TILE
    per_b = seq // tm
    t = x2.shape[0] // MOE_PARTS
    first = part * (t // tm)
    src = lambda i: (i + first, 0)
    row = lambda i: (i, 0)
    const = lambda i: (0, 0)
    full = lambda a: pl.BlockSpec(a.shape, const)
    gw = GROUP_WIDTH
    dils = [dl for _, dl in DIL_PATTERNS]
    return pl.pallas_call(
        _merge_body,
        out_shape=[jax.ShapeDtypeStruct((t, d), _F32),
                   jax.ShapeDtypeStruct((t, d // 2), jnp.uint32),
                   jax.ShapeDtypeStruct((t, ROUTER_LANES), _F32),
                   jax.ShapeDtypeStruct((8, t), jnp.int32),
                   jax.ShapeDtypeStruct((8, ROUTER_LANES), _F32)],
        grid=(t // tm,),
        in_specs=[pl.BlockSpec((tm, d), src),
                  pl.BlockSpec((None, 1, mod3.shape[2]), lambda i: ((i + first) // per_b, 0, 0)),
                  full(g1), full(g2)]
                 + [pl.BlockSpec((tm // dl, dl * gw), src) for dl in dils]
                 + [pl.BlockSpec((tm // dl, dl * LANES), src) for dl in dils]
                 + [pl.BlockSpec((tm, gw), src)]
                 + [full(a) for a in (wbg, wbd, wbs, wout, wrh, wrl, br, ex, tri)],
        out_specs=[pl.BlockSpec((tm, d), row),
                   pl.BlockSpec((tm, d // 2), row),
                   pl.BlockSpec((tm, ROUTER_LANES), row),
                   pl.BlockSpec((8, tm), lambda i: (0, i)),
                   pl.BlockSpec((8, ROUTER_LANES), const)],
        scratch_shapes=[pltpu.VMEM((8, ROUTER_LANES), _F32),
                        pltpu.VMEM((gw // LANES, tm, LANES), _F32), pltpu.VMEM((gw // LANES, tm, LANES), _F32),
                        pltpu.VMEM((1, tm, LANES), _F32), pltpu.VMEM((1, tm, LANES), _F32)],
        compiler_params=_params("arbitrary"),
        name="merge",
    )(x2, mod3, g1, g2, *outs, *lses, osb, wbg, wbd, wbs, wout, wrh, wrl, br, ex, tri)


def _dest_body(ps_ref, er_ref, d_ref):
    e = er_ref[0:2, :]
    start = jnp.zeros_like(e)
    for x in range(N_EXPERTS):
        start = jnp.where(e == x, ps_ref[x], start)
    d_ref[...] = start + er_ref[2:4, :]


def _dest(pstart, er):
    t = er.shape[1]
    tw = min(DEST_TILE, t)
    return pl.pallas_call(
        _dest_body,
        out_shape=jax.ShapeDtypeStruct((2, t), jnp.int32),
        grid_spec=pltpu.PrefetchScalarGridSpec(
            num_scalar_prefetch=1,
            grid=(t // tw,),
            in_specs=[pl.BlockSpec((8, tw), lambda i, ps: (0, i))],
            out_specs=pl.BlockSpec((2, tw), lambda i, ps: (0, i))),
        compiler_params=_params("arbitrary"),
        name="dest",
    )(pstart, er)


def _sc_mesh():
    return plsc.VectorSubcoreMesh(core_axis_name="core", subcore_axis_name="subcore",
                                  num_cores=SC_CORES, num_subcores=SC_SUBCORES)


def _sc_worker():
    return lax.axis_index("subcore") * SC_CORES + lax.axis_index("core")


def _sc_scatter(x, idx0, idx1, n_slots):
    chunks = idx0.shape[0]
    per = chunks // SC_WORKERS
    win = idx0.shape[1]

    @functools.partial(
        pl.kernel, mesh=_sc_mesh(), out_type=jax.ShapeDtypeStruct((n_slots, x.shape[1]), x.dtype),
        scratch_types=[pltpu.VMEM((1, win), jnp.int32), pltpu.VMEM((1, win), jnp.int32),
                       pltpu.VMEM((win, x.shape[1]), x.dtype), pltpu.SemaphoreType.DMA],
        name="sc_scatter")
    def run(x_hbm, i0_hbm, i1_hbm, o_hbm, i0_v, i1_v, rows_v, sem):
        wid = _sc_worker()

        @pl.loop(0, per)
        def _(j):
            c = wid * per + j
            pltpu.sync_copy(i0_hbm.at[pl.ds(c, 1)], i0_v)
            pltpu.sync_copy(i1_hbm.at[pl.ds(c, 1)], i1_v)
            pltpu.sync_copy(x_hbm.at[pl.ds(c * win, win)], rows_v)
            first = pltpu.async_copy(rows_v, o_hbm.at[i0_v.at[0]], sem)
            second = pltpu.async_copy(rows_v, o_hbm.at[i1_v.at[0]], sem)
            first.wait()
            second.wait()

    return run(x, idx0, idx1)


def _sc_gather(table, idx):
    chunks, win = idx.shape
    per = chunks // SC_WORKERS

    @functools.partial(
        pl.kernel, mesh=_sc_mesh(), out_type=jax.ShapeDtypeStruct((chunks * win, table.shape[1]), table.dtype),
        scratch_types=[pltpu.VMEM((1, win), jnp.int32), pltpu.VMEM((win, table.shape[1]), table.dtype),
                       pltpu.SemaphoreType.DMA],
        name="sc_gather")
    def run(t_hbm, i_hbm, o_hbm, i_v, rows_v, sem):
        wid = _sc_worker()

        @pl.loop(0, per)
        def _(j):
            c = wid * per + j
            pltpu.sync_copy(i_hbm.at[pl.ds(c, 1)], i_v)
            pltpu.async_copy(t_hbm.at[i_v.at[0]], rows_v, sem).wait()
            pltpu.sync_copy(rows_v, o_hbm.at[pl.ds(c * win, win)])

    return run(table, idx)


def _pack_halves(a):
    h = a.shape[1] // 2
    lo = lax.bitcast_convert_type(a[:, :h].astype(_BF16).astype(_F32), jnp.uint32) >> 16
    hi = lax.bitcast_convert_type(a[:, h:].astype(_BF16).astype(_F32), jnp.uint32) & jnp.uint32(0xFFFF0000)
    return lo | hi


def _unpack_halves(w):
    return jnp.concatenate(
        [lax.bitcast_convert_type(w << 16, _F32), lax.bitcast_convert_type(w & jnp.uint32(0xFFFF0000), _F32)], axis=1)


def _experts_body(ce_ref, nv_ref, nu_ref, seg_ref, nxt_ref, xs_ref, wg_ref, wu_ref, wd_ref, ys_ref,
                  wgb_ref, wub_ref, wdb_ref, wgf_ref, wuf_ref, wdf_ref, sem):
    c = pl.program_id(0)

    def fetch(expert, slot):
        return [pltpu.make_async_copy(src.at[expert], dst.at[slot], sem.at[slot])
                for src, dst in ((wg_ref, wgf_ref), (wu_ref, wuf_ref), (wd_ref, wdf_ref))]

    @pl.when(c < nu_ref[0])
    def _():
        @pl.when(seg_ref[c] >= 0)
        def _():
            slot = seg_ref[c] & 1

            @pl.when(c == 0)
            def _():
                for copy in fetch(ce_ref[0], 0):
                    copy.start()

            for copy in fetch(ce_ref[c], slot):
                copy.wait()
            wgb_ref[...] = wgf_ref[slot].astype(_BF16)
            wub_ref[...] = wuf_ref[slot].astype(_BF16)
            wdb_ref[...] = wdf_ref[slot].astype(_BF16)

            @pl.when(nxt_ref[c] >= 0)
            def _():
                for copy in fetch(nxt_ref[c], 1 - slot):
                    copy.start()

        row = lax.broadcasted_iota(jnp.int32, xs_ref.shape, 0)
        x = _unpack_halves(jnp.where(row < nv_ref[c], xs_ref[...], jnp.uint32(0))).astype(_BF16)
        half = x.shape[0] // 2
        ups = [(_dot(x[r:r + half], wgb_ref[...]), _dot(x[r:r + half], wub_ref[...])) for r in (0, half)]
        for (g, u), r in zip(ups, (0, half)):
            hmid = (g * jax.nn.sigmoid(g)) * u
            ys_ref[r:r + half, :] = _pack_halves(_dot(hmid.astype(_BF16), wdb_ref[...]))


def _experts(chunk_e, n_valid, n_used, seg, nxt, xs, wg, wu, wd):
    n_slots, w = xs.shape
    ch = EXPERT_CHUNK
    d, de = wg.shape[1], wg.shape[2]
    slot = lambda c, ce, nv, nu, sg, nx: (jnp.minimum(c, nu[0] - 1), 0)
    hbm = pl.BlockSpec(memory_space=pl.ANY)
    return pl.pallas_call(
        _experts_body,
        out_shape=jax.ShapeDtypeStruct((n_slots, d // 2), jnp.uint32),
        grid_spec=pltpu.PrefetchScalarGridSpec(
            num_scalar_prefetch=5,
            grid=(n_slots // ch,),
            in_specs=[pl.BlockSpec((ch, w), slot), hbm, hbm, hbm],
            out_specs=pl.BlockSpec((ch, d // 2), slot),
            scratch_shapes=[pltpu.VMEM((d, de), _BF16), pltpu.VMEM((d, de), _BF16), pltpu.VMEM((de, d), _BF16),
                            pltpu.VMEM((2, d, de), _F32), pltpu.VMEM((2, d, de), _F32), pltpu.VMEM((2, de, d), _F32),
                            pltpu.SemaphoreType.DMA((2,))]),
        compiler_params=_params("arbitrary"),
        name="experts",
    )(chunk_e, n_valid, n_used, seg, nxt, xs, wg, wu, wd)


def _combine_body(x1_ref, rinfo_ref, mod_ref, y0_ref, y1_ref, *rest):
    o_ref = rest[-1]
    y = rinfo_ref[:, 4:5] * _unpack_halves(y0_ref[...]) + rinfo_ref[:, 5:6] * _unpack_halves(y1_ref[...])
    o_ref[...] = x1_ref[...] + mod_ref[:, 5 * D_MODEL:6 * D_MODEL] * y


def _combine(x1, rinfo, mod3, gathered, seq, part, out_so_far):
    t, d = x1.shape
    tf = min(COMBINE_TILE, seq)
    per_b = seq // tf
    nt = t // tf
    first = part * nt
    in_specs = [pl.BlockSpec((tf, d), lambda i: (i, 0)),
                pl.BlockSpec((tf, ROUTER_LANES), lambda i: (i, 0)),
                pl.BlockSpec((None, 1, mod3.shape[2]), lambda i: ((i + first) // per_b, 0, 0)),
                pl.BlockSpec((tf, d // 2), lambda i: (i, 0)),
                pl.BlockSpec((tf, d // 2), lambda i: (i + nt, 0))]
    args = [x1, rinfo, mod3, gathered, gathered]
    aliases = {}
    if out_so_far is not None:
        in_specs.append(pl.BlockSpec(memory_space=pl.ANY))
        args.append(out_so_far)
        aliases = {len(args) - 1: 0}
    return pl.pallas_call(
        _combine_body,
        out_shape=jax.ShapeDtypeStruct((t * MOE_PARTS, d), _F32),
        grid=(nt,),
        in_specs=in_specs,
        out_specs=pl.BlockSpec((tf, d), lambda i: (i + first, 0)),
        input_output_aliases=aliases,
        compiler_params=_params("arbitrary"),
        name="combine",
    )(*args)


def _rope_trig(positions):
    inv_freq = ROPE_THETA ** (-jnp.arange(0, ROPE_DIM, 2, dtype=_F32) / ROPE_DIM)
    ang = positions.reshape(-1).astype(_F32)[:, None] * inv_freq
    return jnp.concatenate([jnp.cos(ang), jnp.sin(ang)], axis=1)


def _layer(x, mod, positions, g_mix, g_ffn, w_in, w_bg, qg, kg, w_bd, w_bs, w_out, w_rg, b_rg, w_re, b_re,
           w_eg, w_eu, w_ed):
    batch, seq, d = x.shape
    t = batch * seq
    x2 = x.reshape(t, d)
    mod3 = mod.reshape(batch, 1, mod.shape[1])
    gw = GROUP_WIDTH

    lane = jnp.arange(gw)
    bd = jnp.where(lane[:, None] // HEAD_DIM == lane[None, :] // HEAD_DIM, 1.0 / HEAD_DIM, 0.0).astype(_BF16)
    ex = (jnp.arange(LANES)[:, None] == (lane[None, :] // HEAD_DIM) * LSE_SEG).astype(_BF16)
    ex = jnp.concatenate([ex, ex], axis=0)
    tri = (jnp.arange(ROW_TILE)[:, None] > jnp.arange(ROW_TILE)[None, :]).astype(_BF16)
    tile4 = lambda g: jnp.tile(g.astype(_F32), HEADS_PER_GROUP).reshape(1, gw)
    wr = jnp.zeros((d, ROUTER_LANES), _F32).at[:, :N_GROUPS].set(w_rg).at[:, N_GROUPS:N_GROUPS + N_EXPERTS].set(w_re)
    wrh = wr.astype(_BF16)
    wrl = (wr - wrh.astype(_F32)).astype(_BF16)
    br = jnp.zeros((1, ROUTER_LANES), _F32).at[0, :N_GROUPS].set(b_rg).at[0, N_GROUPS:N_GROUPS + N_EXPERTS].set(b_re)

    d0, d1, d2, sbp = _qkv(x2, mod3, g_mix.reshape(1, d), w_in.astype(_BF16), tile4(qg) * ATTN_SCALE, tile4(kg),
                           _rope_trig(positions), bd, seq)
    dil = [_dilated_group(a, dl, batch, seq) for a, (_, dl) in zip((d0, d1, d2), DIL_PATTERNS)]
    osb = _stick_breaking(sbp, batch, seq)

    merge_weights = (w_bg.astype(_BF16), w_bd.astype(_BF16), w_bs.astype(_BF16), w_out.astype(_BF16))
    tp = t // MOE_PARTS
    ch = EXPERT_CHUNK
    win = SC_INDEX_WINDOW
    n_chunks = -(-2 * tp // ch) + N_EXPERTS
    chunk_start = jnp.arange(n_chunks, dtype=jnp.int32) * ch
    expert_ids = jnp.arange(N_EXPERTS, dtype=jnp.int32)
    out = None
    for part in range(MOE_PARTS):
        x1, h2p, rinfo, er, cnt = _merge(
            x2, mod3, g_mix.reshape(1, d), g_ffn.reshape(1, d), [o for o, _ in dil], [l for _, l in dil], osb,
            *merge_weights, wrh, wrl, br, ex, tri, seq, part)

        counts = cnt[0, EXPERT_LANE0:EXPERT_LANE0 + N_EXPERTS].astype(jnp.int32)
        padded = (counts + ch - 1) // ch * ch
        pend = jnp.cumsum(padded)
        pstart = pend - padded
        chunk_e = jnp.minimum(jnp.sum((pend[None, :] <= chunk_start[:, None]).astype(jnp.int32), axis=1),
                              N_EXPERTS - 1)
        n_used = (pend[-1:] // ch).astype(jnp.int32)
        begin = chunk_start[:, None]
        inside = (pstart[None, :] <= begin) & (begin < pend[None, :])
        n_valid = jnp.sum(jnp.where(inside, jnp.clip(counts[None, :] - (begin - pstart[None, :]), 0, ch), 0), axis=1)

        dest = _dest(pstart, er)
        xs = _sc_scatter(h2p, dest[0].reshape(tp // win, win), dest[1].reshape(tp // win, win), n_chunks * ch)
        first = (chunk_e != jnp.concatenate([jnp.full((1,), -1, jnp.int32), chunk_e[:-1]])) & (chunk_start < pend[-1])
        seg_no = jnp.cumsum(first.astype(jnp.int32)) - 1
        seg = jnp.where(first, seg_no, -1 - seg_no)
        later = (expert_ids[None, :] > expert_ids[:, None]) & (padded > 0)[None, :]
        next_expert = jnp.min(jnp.where(later, expert_ids[None, :], N_EXPERTS), axis=1)
        next_expert = jnp.where(next_expert == N_EXPERTS, -1, next_expert)
        nxt = jnp.sum(jnp.where(chunk_e[:, None] == expert_ids[None, :], next_expert[None, :], 0), axis=1)
        ys = _experts(chunk_e, n_valid, n_used, seg.astype(jnp.int32), nxt.astype(jnp.int32), xs, w_eg, w_eu, w_ed)
        gathered = _sc_gather(ys, dest.reshape(2 * tp // win, win))
        out = _combine(x1, rinfo, mod3, gathered, seq, part, out)
    return out.reshape(batch, seq, d)


def kernel(x, c, positions, w_ada, b_ada, g_norm_mix, g_norm_ffn, w_in, w_branch_gate, q_norm_g, k_norm_g,
           w_branch_dil, w_branch_sb, w_out, w_router_group, b_router_group, w_router_expert, b_router_expert,
           w_expert_gate, w_expert_up, w_expert_down):
    for l in range(w_ada.shape[0]):
        mod = _ada(c, w_ada[l], b_ada[l])
        x = _layer(x, mod, positions, g_norm_mix[l], g_norm_ffn[l], w_in[l], w_branch_gate[l], q_norm_g[l],
                   k_norm_g[l], w_branch_dil[l], w_branch_sb[l], w_out[l], w_router_group[l], b_router_group[l],
                   w_router_expert[l], b_router_expert[l], w_expert_gate[l], w_expert_up[l], w_expert_down[l])
    return x
```

```python
import functools

import jax
import jax.numpy as jnp
from jax import lax
from jax.experimental import pallas as pl
from jax.experimental.pallas import tpu as pltpu
from jax.experimental.pallas import tpu_sc as plsc

D_MODEL = 1024
HEAD_DIM = 64
DIL_PATTERNS = ((128, 1), (512, 4), (2048, 16))
HEADS_PER_GROUP = 4
GROUP_WIDTH = HEADS_PER_GROUP * HEAD_DIM
N_DIL_GROUPS = len(DIL_PATTERNS)
DIL_WIDTH = N_DIL_GROUPS * GROUP_WIDTH
QKV_WIDTH = 3 * DIL_WIDTH + 3 * GROUP_WIDTH
WINDOW_KEYS = 128
ROPE_THETA = 500000.0
ROPE_DIM = HEAD_DIM // 4
N_GROUPS = 4
EXPERTS_PER_GROUP = 8
N_EXPERTS = N_GROUPS * EXPERTS_PER_GROUP
D_EXPERT = 512
RMS_EPS = 1e-6
ATTN_SCALE = HEAD_DIM ** -0.5

LANES = 128
ROUTER_LANES = LANES
EXPERT_LANE0 = N_GROUPS
LSE_SEG = LANES // HEADS_PER_GROUP
NEG_BIG = -1e30
SB_DEAD_LOG = -120.0
SB_HEAD_BLOCKS = 3
SB_CHAINS = 8

ROW_TILE = 512
QKV_TILE = 1024
QBLK = 128
DIL_QTILE = 1024
EXPERT_CHUNK = 512
COMBINE_TILE = 1024
DEST_TILE = 8192
SC_CORES = 2
SC_SUBCORES = 16
SC_WORKERS = SC_CORES * SC_SUBCORES
SC_INDEX_WINDOW = 128
MOE_PARTS = 2
VMEM_LIMIT = 48 * 1024 * 1024

_BF16 = jnp.bfloat16
_F32 = jnp.float32
_NT = (((1,), (1,)), ((), ()))


def _dot(a, b):
    return jnp.dot(a, b, preferred_element_type=_F32)


def _dot_nt(a, b):
    return lax.dot_general(a, b, _NT, preferred_element_type=_F32)


def _split(a):
    hi = a.astype(_BF16)
    lo = (a - hi.astype(_F32)).astype(_BF16)
    return hi, lo


def _dot3(a, b):
    ah, al = _split(a)
    bh, bl = _split(b)
    return _dot(ah, bh) + (_dot(ah, bl) + _dot(al, bh))


def _rms_mod(x, g, scale, shift):
    y = x * lax.rsqrt(jnp.mean(x * x, axis=-1, keepdims=True) + RMS_EPS)
    return y * g * (1.0 + scale) + shift


def _params(*sem):
    return pltpu.CompilerParams(dimension_semantics=sem, vmem_limit_bytes=VMEM_LIMIT)


def _ada_body(c_ref, w_ref, b_ref, o_ref):
    c = c_ref[...]
    o_ref[...] = _dot3(c * jax.nn.sigmoid(c), w_ref[...]) + b_ref[...]


def _ada(c, w_ada, b_ada):
    b, d = c.shape
    n = w_ada.shape[1]
    rows = -(-b // 16) * 16
    cp = jnp.zeros((rows, d), _F32).at[:b].set(c)
    nt = 1536
    out = pl.pallas_call(
        _ada_body,
        out_shape=jax.ShapeDtypeStruct((rows, n), _F32),
        grid=(n // nt,),
        in_specs=[pl.BlockSpec((rows, d), lambda j: (0, 0)),
                  pl.BlockSpec((d, nt), lambda j: (0, j)),
                  pl.BlockSpec((1, nt), lambda j: (0, j))],
        out_specs=pl.BlockSpec((rows, nt), lambda j: (0, j)),
        compiler_params=_params("arbitrary"),
        name="ada",
    )(cp, w_ada, b_ada.reshape(1, n))
    return out[:b]


def _qkv_body(x_ref, mod_ref, g_ref, w_ref, qg_ref, kg_ref, trig_ref, bd_ref,
              o0_ref, o1_ref, o2_ref, osb_ref, st_ref, acc_ref):
    d = D_MODEL
    tm = x_ref.shape[0]
    h = _rms_mod(x_ref[...], g_ref[...], mod_ref[:, d:2 * d], mod_ref[:, 0:d])
    hb = h.astype(_BF16)
    half = ROPE_DIM // 2
    in_head = lax.broadcasted_iota(jnp.int32, (tm, LANES), 1) & (HEAD_DIM - 1)
    freq = in_head & (half - 1)
    trig = jnp.concatenate([trig_ref[...], jnp.zeros((tm, LANES - ROPE_DIM), _F32)], axis=1)
    cos = jnp.take_along_axis(trig, freq, axis=1)
    sin = jnp.take_along_axis(trig, freq + half, axis=1)
    cc = jnp.where(in_head < ROPE_DIM, cos, 1.0)
    s1 = jnp.where(in_head < half, -sin, 0.0)
    s2 = jnp.where((in_head >= half) & (in_head < ROPE_DIM), sin, 0.0)
    cc, s1, s2 = [jnp.concatenate([a, a], axis=1) for a in (cc, s1, s2)]
    bd = bd_ref[...]
    gw = GROUP_WIDTH

    def normed_rotated(acc, gain):
        ms = _dot((acc * acc).astype(_BF16), bd)
        y = acc * lax.rsqrt(ms + RMS_EPS) * gain
        return y * cc + pltpu.roll(y, gw - ROPE_DIM // 2, 1) * s1 + pltpu.roll(y, ROPE_DIM // 2, 1) * s2

    def store(o_ref, dil, part, y):
        if dil == 1:
            o_ref[:, part * gw:(part + 1) * gw] = y.astype(_BF16)
            return
        for s in range(gw // LANES):
            st_ref[s] = y[:, s * LANES:(s + 1) * LANES]
        for r in range(dil):
            for s in range(gw // LANES):
                col0 = (3 * r + part) * gw + s * LANES
                o_ref[:, col0:col0 + LANES] = st_ref[s, pl.ds(r, tm // dil, stride=dil), :].astype(_BF16)

    def project(col0):
        return _dot(hb, w_ref[:, col0:col0 + gw])

    outs = (o0_ref, o1_ref, o2_ref)
    dils = [dl for _, dl in DIL_PATTERNS]
    normed = [(g, part, part * DIL_WIDTH + g * gw) for g in range(N_DIL_GROUPS) for part in (0, 1)]
    for n, (_, _, col0) in enumerate(normed):
        acc_ref[n] = project(col0)
    plain = [("v", g) for g in range(N_DIL_GROUPS)] + [("sb", part) for part in range(3)]
    for n, (kind, j) in enumerate(plain):
        if kind == "v":
            store(outs[j], dils[j], 2, project(2 * DIL_WIDTH + j * gw))
        else:
            acc = project(3 * DIL_WIDTH + j * gw)
            osb_ref[:, j * gw:(j + 1) * gw] = (acc * ATTN_SCALE if j == 0 else acc).astype(_BF16)
        g, part, _ = normed[n]
        gain = qg_ref[...] if part == 0 else kg_ref[...]
        store(outs[g], dils[g], part, normed_rotated(acc_ref[n], gain))


def _qkv(x2, mod3, g_mix, w_in, qg, kg, trig, bd, seq):
    t, d = x2.shape
    tm = min(QKV_TILE, seq)
    per_b = seq // tm
    row = lambda i: (i, 0)
    const = lambda i: (0, 0)
    width = 3 * GROUP_WIDTH
    dils = [dl for _, dl in DIL_PATTERNS] + [1]
    return pl.pallas_call(
        _qkv_body,
        out_shape=[jax.ShapeDtypeStruct((t // dl, dl * width), _BF16) for dl in dils],
        grid=(t // tm,),
        in_specs=[pl.BlockSpec((tm, d), row),
                  pl.BlockSpec((None, 1, mod3.shape[2]), lambda i: (i // per_b, 0, 0)),
                  pl.BlockSpec((1, d), const),
                  pl.BlockSpec(w_in.shape, const),
                  pl.BlockSpec((1, GROUP_WIDTH), const),
                  pl.BlockSpec((1, GROUP_WIDTH), const),
                  pl.BlockSpec((tm, ROPE_DIM), row),
                  pl.BlockSpec(bd.shape, const)],
        out_specs=[pl.BlockSpec((tm // dl, dl * width), row) for dl in dils],
        scratch_shapes=[pltpu.VMEM((GROUP_WIDTH // LANES, tm, LANES), _F32),
                        pltpu.VMEM((2 * N_DIL_GROUPS, tm, GROUP_WIDTH), _F32)],
        compiler_params=_params("arbitrary"),
        name="qkv",
    )(x2, mod3, g_mix, w_in, qg, kg, trig, bd)


def _dil_body(q_ref, kp_ref, kc_ref, vp_ref, vc_ref, o_ref, lse_ref, kf_ref, vf_ref):
    tq = q_ref.shape[0]
    first = pl.program_id(2) == 0
    kf_ref[0:QBLK, :] = kp_ref[...]
    kf_ref[QBLK:, :] = kc_ref[...]
    vf_ref[0:QBLK, :] = vp_ref[...]
    vf_ref[QBLK:, :] = vc_ref[...]
    nh = HEADS_PER_GROUP
    row = lax.broadcasted_iota(jnp.int32, (nh * QBLK, 2 * QBLK), 0) & (QBLK - 1)
    col = lax.broadcasted_iota(jnp.int32, (nh * QBLK, 2 * QBLK), 1)
    band = (col >= row) & (col <= row + WINDOW_KEYS)
    lane = lax.broadcasted_iota(jnp.int32, (1, GROUP_WIDTH), 1)
    slane = lax.broadcasted_iota(jnp.int32, (1, LANES), 1)
    head_masks = [(lane >= h * HEAD_DIM) & (lane < (h + 1) * HEAD_DIM) for h in range(nh)]
    subs = range(tq // QBLK)
    rows = [slice(j * QBLK, (j + 1) * QBLK) for j in subs]
    window = [slice(j * QBLK, (j + 2) * QBLK) for j in subs]

    def stack(qj):
        return jnp.concatenate([jnp.where(hm, qj, jnp.zeros_like(qj)) for hm in head_masks], axis=0)

    valid = [band & ((col >= QBLK) | jnp.logical_not(first)) if j == 0 else band for j in subs]
    s = [jnp.where(valid[j], _dot_nt(stack(q_ref[rows[j], :]), kf_ref[window[j], :]), NEG_BIG) for j in subs]
    m = [jnp.max(s[j], axis=1, keepdims=True) for j in subs]
    p = [jnp.exp(s[j] - m[j]) for j in subs]
    l = [jnp.sum(p[j], axis=1, keepdims=True) for j in subs]
    o_all = [_dot(p[j].astype(_BF16), vf_ref[window[j], :]) / l[j] for j in subs]
    for j in subs:
        lse_all = m[j] + jnp.log(l[j])
        o_acc = jnp.zeros((QBLK, GROUP_WIDTH), _F32)
        lse_t = jnp.zeros((QBLK, LANES), _F32)
        for h, hm in enumerate(head_masks):
            o_acc = jnp.where(hm, o_all[j][h * QBLK:(h + 1) * QBLK, :], o_acc)
            sm = (slane >= h * LSE_SEG) & (slane < (h + 1) * LSE_SEG)
            lse_t = jnp.where(sm, lse_all[h * QBLK:(h + 1) * QBLK, :], lse_t)
        o_ref[rows[j], :] = o_acc.astype(_BF16)
        lse_ref[rows[j], :] = lse_t


def _dilated_group(view2, dil, batch, seq):
    sd = seq // dil
    tq = min(DIL_QTILE, sd)
    per = tq // QBLK
    gw = GROUP_WIDTH
    view = view2.reshape(batch, sd, dil * 3 * gw)
    cur = lambda part: pl.BlockSpec((None, tq, gw), lambda b, r, i: (b, i, 3 * r + part))
    prev = lambda part: pl.BlockSpec((None, QBLK, gw),
                                     lambda b, r, i: (b, jnp.maximum(i * per - 1, 0), 3 * r + part))
    o, lse = pl.pallas_call(
        _dil_body,
        out_shape=[jax.ShapeDtypeStruct((batch, sd, dil * gw), _BF16),
                   jax.ShapeDtypeStruct((batch, sd, dil * LANES), _F32)],
        grid=(batch, dil, sd // tq),
        in_specs=[cur(0), prev(1), cur(1), prev(2), cur(2)],
        out_specs=[pl.BlockSpec((None, tq, gw), lambda b, r, i: (b, i, r)),
                   pl.BlockSpec((None, tq, LANES), lambda b, r, i: (b, i, r))],
        scratch_shapes=[pltpu.VMEM((tq + QBLK, gw), _BF16), pltpu.VMEM((tq + QBLK, gw), _BF16)],
        compiler_params=_params("arbitrary", "arbitrary", "arbitrary"),
        name=f"dil{dil}",
    )(view, view, view, view, view)
    return o.reshape(batch * sd, dil * gw), lse.reshape(batch * sd, dil * LANES)


def _sb_body(q_ref, k_ref, v_ref, o_ref, carry_ref, acc_ref, qs_ref):
    step_id = pl.program_id(1)
    chains = range(SB_CHAINS)
    blk = [step_id * SB_CHAINS + c for c in chains]
    nh = HEADS_PER_GROUP
    row = lax.broadcasted_iota(jnp.int32, (nh * QBLK, QBLK), 0) & (QBLK - 1)
    col = lax.broadcasted_iota(jnp.int32, (nh * QBLK, QBLK), 1)
    strict = col < row
    ur = lax.broadcasted_iota(jnp.int32, (2 * QBLK, QBLK), 0) & (QBLK - 1)
    uc = lax.broadcasted_iota(jnp.int32, (2 * QBLK, QBLK), 1)
    u = jnp.where(ur > uc, 1.0, 0.0).astype(_BF16)
    lane = lax.broadcasted_iota(jnp.int32, (1, GROUP_WIDTH), 1)
    head_masks = [(lane >= h * HEAD_DIM) & (lane < (h + 1) * HEAD_DIM) for h in range(nh)]
    for c in chains:
        q = q_ref[c * QBLK:(c + 1) * QBLK, :]
        qs_ref[c] = jnp.concatenate([jnp.where(hm, q, jnp.zeros_like(q)) for hm in head_masks], axis=0)

    def softplus(z):
        return jnp.maximum(z, 0.0) + jnp.log(1.0 + jnp.exp(-jnp.abs(z)))

    def later_keys(log_1m):
        hi, lo = _split(log_1m)
        return _dot(jnp.concatenate([hi, lo], axis=1), u)

    nb = SB_HEAD_BLOCKS
    cols = [slice(j * QBLK, (j + 1) * QBLK) for j in range(nb)]
    kbs = [[blk[c] - (nb - 1) + j for j in range(nb)] for c in chains]
    starts = [[pl.multiple_of(jnp.maximum(kb, 0) * QBLK, QBLK) for kb in kbs[c]] for c in chains]
    keep = [[strict if j == nb - 1 else (kbs[c][j] >= 0) for j in range(nb)] for c in chains]
    z = [_dot_nt(qs_ref[c], jnp.concatenate([k_ref[pl.ds(s, QBLK), :] for s in starts[c]], axis=0)) for c in chains]
    sp = [softplus(z[c]) for c in chains]
    log_1m = [[jnp.where(keep[c][j], -sp[c][:, cols[j]], 0.0) for j in range(nb)] for c in chains]
    totals = [[jnp.sum(l, axis=1, keepdims=True) for l in log_1m[c]] for c in chains]
    later = [[later_keys(log_1m[c][j]) for j in range(nb)] for c in chains]
    for c in chains:
        a_blocks = []
        after = jnp.zeros_like(totals[c][0])
        for j in reversed(range(nb)):
            a = jnp.exp((z[c][:, cols[j]] - sp[c][:, cols[j]]) + later[c][j] + after)
            a_blocks.insert(0, jnp.where(keep[c][j], a, 0.0).astype(_BF16))
            after = after + totals[c][j]
        acc_ref[c] = _dot(jnp.concatenate(a_blocks, axis=1),
                          jnp.concatenate([v_ref[pl.ds(s, QBLK), :] for s in starts[c]], axis=0))
        carry_ref[c] = after

    for c in chains:
        def tile(kb, c=c):
            start = pl.multiple_of(kb * QBLK, QBLK)
            z = _dot_nt(qs_ref[c], k_ref[pl.ds(start, QBLK), :])
            sp = softplus(z)
            log_1m = -sp
            a = jnp.exp((z - sp) + later_keys(log_1m) + carry_ref[c])
            acc_ref[c] += _dot(a.astype(_BF16), v_ref[pl.ds(start, QBLK), :])
            carry = carry_ref[c] + jnp.sum(log_1m, axis=1, keepdims=True)
            carry_ref[c] = carry
            return jnp.max(carry)

        def cond(st):
            return (st[0] >= 0) & (st[1] > SB_DEAD_LOG)

        def step(st, tile=tile):
            return st[0] - 1, tile(st[0])

        lax.while_loop(cond, step, (blk[c] - nb, jnp.max(carry_ref[c])))
        out = jnp.zeros((QBLK, GROUP_WIDTH), _F32)
        for h, hm in enumerate(head_masks):
            out = jnp.where(hm, acc_ref[c, h * QBLK:(h + 1) * QBLK, :], out)
        o_ref[c * QBLK:(c + 1) * QBLK, :] = out.astype(_BF16)


def _stick_breaking(arr, batch, seq):
    gw = GROUP_WIDTH
    qt = SB_CHAINS * QBLK
    view = arr.reshape(batch, seq, 3 * gw)
    o = pl.pallas_call(
        _sb_body,
        out_shape=jax.ShapeDtypeStruct((batch, seq, gw), _BF16),
        grid=(batch, seq // qt),
        in_specs=[pl.BlockSpec((None, qt, gw), lambda b, i: (b, i, 0)),
                  pl.BlockSpec((None, seq, gw), lambda b, i: (b, 0, 1)),
                  pl.BlockSpec((None, seq, gw), lambda b, i: (b, 0, 2))],
        out_specs=pl.BlockSpec((None, qt, gw), lambda b, i: (b, i, 0)),
        scratch_shapes=[pltpu.VMEM((SB_CHAINS, HEADS_PER_GROUP * QBLK, 1), _F32),
                        pltpu.VMEM((SB_CHAINS, HEADS_PER_GROUP * QBLK, gw), _F32),
                        pltpu.VMEM((SB_CHAINS, HEADS_PER_GROUP * QBLK, gw), _BF16)],
        compiler_params=_params("arbitrary", "arbitrary"),
        name="sb",
    )(view, view, view)
    return o.reshape(batch * seq, gw)


def _merge_body(x_ref, mod_ref, g1_ref, g2_ref, o0_ref, o1_ref, o2_ref, l0_ref, l1_ref, l2_ref, osb_ref,
                wbg_ref, wbd_ref, wbs_ref, wout_ref, wrh_ref, wrl_ref, br_ref, ex_ref, tri_ref,
                x1_ref, h2p_ref, rinfo_ref, er_ref, cnt_out_ref, cnt_ref, os1_ref, os2_ref, ls1_ref, ls2_ref):
    d = D_MODEL
    tm = x_ref.shape[0]
    x = x_ref[...]
    hb = _rms_mod(x, g1_ref[...], mod_ref[:, d:2 * d], mod_ref[:, 0:d]).astype(_BF16)

    def natural(ref, st_ref, dil):
        if dil == 1:
            return ref[...].astype(_F32)
        slabs = st_ref.shape[0]
        for r in range(dil):
            for s in range(slabs):
                col0 = (r * slabs + s) * LANES
                st_ref[s, pl.ds(r, tm // dil, stride=dil), :] = ref[:, col0:col0 + LANES].astype(_F32)
        return jnp.concatenate([st_ref[s] for s in range(slabs)], axis=1)

    dils = [dl for _, dl in DIL_PATTERNS]
    o_nat = [natural(r, s, dl) for r, s, dl in zip((o0_ref, o1_ref, o2_ref), (None, os1_ref, os2_ref), dils)]
    l0, l1, l2 = [natural(r, s, dl) for r, s, dl in zip((l0_ref, l1_ref, l2_ref), (None, ls1_ref, ls2_ref), dils)]

    lmax = jnp.maximum(jnp.maximum(l0, l1), l2)
    e0, e1, e2 = jnp.exp(l0 - lmax), jnp.exp(l1 - lmax), jnp.exp(l2 - lmax)
    inv = 1.0 / (e0 + e1 + e2)
    ex = ex_ref[...]

    def widen(w):
        hi, lo = _split(w)
        return _dot(jnp.concatenate([hi, lo], axis=1), ex)

    w_groups = [widen(e * inv) for e in (e0, e1, e2)]
    gate_dil = jax.nn.sigmoid(_dot(hb, wbg_ref[:, :d]))
    o_dil = w_groups[0] * o_nat[0] + w_groups[1] * o_nat[1] + w_groups[2] * o_nat[2]
    branch_dil = _dot(o_dil.astype(_BF16), wbd_ref[...])
    branch_sb = _dot(osb_ref[...], wbs_ref[...])
    gate_sb = jax.nn.sigmoid(_dot(hb, wbg_ref[:, d:]))
    merged = gate_dil * branch_dil + gate_sb * branch_sb
    x1 = x + mod_ref[:, 2 * d:3 * d] * _dot(merged.astype(_BF16), wout_ref[...])
    x1_ref[...] = x1

    h2 = _rms_mod(x1, g2_ref[...], mod_ref[:, 4 * d:5 * d], mod_ref[:, 3 * d:4 * d])
    h2p_ref[...] = _pack_halves(h2)

    hh, hl = _split(h2)
    logits = _dot(hh, wrh_ref[...]) + (_dot(hl, wrh_ref[...]) + _dot(hh, wrl_ref[...])) + br_ref[...]
    lane = lax.broadcasted_iota(jnp.int32, (tm, ROUTER_LANES), 1).astype(_F32)
    far = float(ROUTER_LANES)

    def top(vals):
        m = jnp.max(vals, axis=1, keepdims=True)
        return m, jnp.min(jnp.where(vals == m, lane, far), axis=1, keepdims=True)

    is_group = lane < N_GROUPS
    mg, gsel = top(jnp.where(is_group, logits, NEG_BIG))
    pg_top = 1.0 / jnp.sum(jnp.where(is_group, jnp.exp(logits - mg), 0.0), axis=1, keepdims=True)
    lane0 = EXPERT_LANE0 + EXPERTS_PER_GROUP * gsel
    le = jnp.where((lane >= lane0) & (lane < lane0 + EXPERTS_PER_GROUP), logits, NEG_BIG)
    m1, i1 = top(le)
    m2, i2 = top(jnp.where(lane == i1, NEG_BIG, le))
    t2 = jnp.exp(m2 - m1)
    w0 = pg_top / (1.0 + t2)
    w1 = pg_top * t2 / (1.0 + t2)

    @pl.when(pl.program_id(0) == 0)
    def _():
        cnt_ref[...] = jnp.zeros_like(cnt_ref)

    sel0, sel1 = lane == i1, lane == i2
    onehot = jnp.where(sel0 | sel1, 1.0, 0.0)
    before = _dot(tri_ref[...], onehot.astype(_BF16)) + cnt_ref[0:1, :]
    r0 = jnp.sum(jnp.where(sel0, before, 0.0), axis=1, keepdims=True)
    r1 = jnp.sum(jnp.where(sel1, before, 0.0), axis=1, keepdims=True)
    cnt_ref[...] += jnp.sum(onehot, axis=0, keepdims=True)
    cnt_out_ref[...] = cnt_ref[...]
    cols = (i1 - EXPERT_LANE0, i2 - EXPERT_LANE0, r0, r1, w0, w1)
    rinfo = jnp.zeros((tm, ROUTER_LANES), _F32)
    for c, v in enumerate(cols):
        rinfo = jnp.where(lane == float(c), v, rinfo)
    rinfo_ref[...] = rinfo
    er_ref[...] = jnp.transpose(rinfo)[0:8, :].astype(jnp.int32)


def _merge(x2, mod3, g1, g2, outs, lses, osb, wbg, wbd, wbs, wout, wrh, wrl, br, ex, tri, seq, part):
    d = x2.shape[1]
    tm = ROW_TILE
    per_b = seq // tm
    t = x2.shape[0] // MOE_PARTS
    first = part * (t // tm)
    src = lambda i: (i + first, 0)
    row = lambda i: (i, 0)
    const = lambda i: (0, 0)
    full = lambda a: pl.BlockSpec(a.shape, const)
    gw = GROUP_WIDTH
    dils = [dl for _, dl in DIL_PATTERNS]
    return pl.pallas_call(
        _merge_body,
        out_shape=[jax.ShapeDtypeStruct((t, d), _F32),
                   jax.ShapeDtypeStruct((t, d // 2), jnp.uint32),
                   jax.ShapeDtypeStruct((t, ROUTER_LANES), _F32),
                   jax.ShapeDtypeStruct((8, t), jnp.int32),
                   jax.ShapeDtypeStruct((8, ROUTER_LANES), _F32)],
        grid=(t // tm,),
        in_specs=[pl.BlockSpec((tm, d), src),
                  pl.BlockSpec((None, 1, mod3.shape[2]), lambda i: ((i + first) // per_b, 0, 0)),
                  full(g1), full(g2)]
                 + [pl.BlockSpec((tm // dl, dl * gw), src) for dl in dils]
                 + [pl.BlockSpec((tm // dl, dl * LANES), src) for dl in dils]
                 + [pl.BlockSpec((tm, gw), src)]
                 + [full(a) for a in (wbg, wbd, wbs, wout, wrh, wrl, br, ex, tri)],
        out_specs=[pl.BlockSpec((tm, d), row),
                   pl.BlockSpec((tm, d // 2), row),
                   pl.BlockSpec((tm, ROUTER_LANES), row),
                   pl.BlockSpec((8, tm), lambda i: (0, i)),
                   pl.BlockSpec((8, ROUTER_LANES), const)],
        scratch_shapes=[pltpu.VMEM((8, ROUTER_LANES), _F32),
                        pltpu.VMEM((gw // LANES, tm, LANES), _F32), pltpu.VMEM((gw // LANES, tm, LANES), _F32),
                        pltpu.VMEM((1, tm, LANES), _F32), pltpu.VMEM((1, tm, LANES), _F32)],
        compiler_params=_params("arbitrary"),
        name="merge",
    )(x2, mod3, g1, g2, *outs, *lses, osb, wbg, wbd, wbs, wout, wrh, wrl, br, ex, tri)


def _dest_body(ps_ref, er_ref, d_ref):
    e = er_ref[0:2, :]
    start = jnp.zeros_like(e)
    for x in range(N_EXPERTS):
        start = jnp.where(e == x, ps_ref[x], start)
    d_ref[...] = start + er_ref[2:4, :]


def _dest(pstart, er):
    t = er.shape[1]
    tw = min(DEST_TILE, t)
    return pl.pallas_call(
        _dest_body,
        out_shape=jax.ShapeDtypeStruct((2, t), jnp.int32),
        grid_spec=pltpu.PrefetchScalarGridSpec(
            num_scalar_prefetch=1,
            grid=(t // tw,),
            in_specs=[pl.BlockSpec((8, tw), lambda i, ps: (0, i))],
            out_specs=pl.BlockSpec((2, tw), lambda i, ps: (0, i))),
        compiler_params=_params("arbitrary"),
        name="dest",
    )(pstart, er)


def _sc_mesh():
    return plsc.VectorSubcoreMesh(core_axis_name="core", subcore_axis_name="subcore",
                                  num_cores=SC_CORES, num_subcores=SC_SUBCORES)


def _sc_worker():
    return lax.axis_index("subcore") * SC_CORES + lax.axis_index("core")


def _sc_scatter(x, idx0, idx1, n_slots):
    chunks = idx0.shape[0]
    per = chunks // SC_WORKERS
    win = idx0.shape[1]

    @functools.partial(
        pl.kernel, mesh=_sc_mesh(), out_type=jax.ShapeDtypeStruct((n_slots, x.shape[1]), x.dtype),
        scratch_types=[pltpu.VMEM((1, win), jnp.int32), pltpu.VMEM((1, win), jnp.int32),
                       pltpu.VMEM((win, x.shape[1]), x.dtype), pltpu.SemaphoreType.DMA],
        name="sc_scatter")
    def run(x_hbm, i0_hbm, i1_hbm, o_hbm, i0_v, i1_v, rows_v, sem):
        wid = _sc_worker()

        @pl.loop(0, per)
        def _(j):
            c = wid * per + j
            pltpu.sync_copy(i0_hbm.at[pl.ds(c, 1)], i0_v)
            pltpu.sync_copy(i1_hbm.at[pl.ds(c, 1)], i1_v)
            pltpu.sync_copy(x_hbm.at[pl.ds(c * win, win)], rows_v)
            first = pltpu.async_copy(rows_v, o_hbm.at[i0_v.at[0]], sem)
            second = pltpu.async_copy(rows_v, o_hbm.at[i1_v.at[0]], sem)
            first.wait()
            second.wait()

    return run(x, idx0, idx1)


def _sc_gather(table, idx):
    chunks, win = idx.shape
    per = chunks // SC_WORKERS

    @functools.partial(
        pl.kernel, mesh=_sc_mesh(), out_type=jax.ShapeDtypeStruct((chunks * win, table.shape[1]), table.dtype),
        scratch_types=[pltpu.VMEM((1, win), jnp.int32), pltpu.VMEM((win, table.shape[1]), table.dtype),
                       pltpu.SemaphoreType.DMA],
        name="sc_gather")
    def run(t_hbm, i_hbm, o_hbm, i_v, rows_v, sem):
        wid = _sc_worker()

        @pl.loop(0, per)
        def _(j):
            c = wid * per + j
            pltpu.sync_copy(i_hbm.at[pl.ds(c, 1)], i_v)
            pltpu.async_copy(t_hbm.at[i_v.at[0]], rows_v, sem).wait()
            pltpu.sync_copy(rows_v, o_hbm.at[pl.ds(c * win, win)])

    return run(table, idx)


def _pack_halves(a):
    h = a.shape[1] // 2
    lo = lax.bitcast_convert_type(a[:, :h].astype(_BF16).astype(_F32), jnp.uint32) >> 16
    hi = lax.bitcast_convert_type(a[:, h:].astype(_BF16).astype(_F32), jnp.uint32) & jnp.uint32(0xFFFF0000)
    return lo | hi


def _unpack_halves(w):
    return jnp.concatenate(
        [lax.bitcast_convert_type(w << 16, _F32), lax.bitcast_convert_type(w & jnp.uint32(0xFFFF0000), _F32)], axis=1)


def _experts_body(ce_ref, nv_ref, nu_ref, seg_ref, nxt_ref, xs_ref, wg_ref, wu_ref, wd_ref, ys_ref,
                  wgb_ref, wub_ref, wdb_ref, wgf_ref, wuf_ref, wdf_ref, sem):
    c = pl.program_id(0)

    def fetch(expert, slot):
        return [pltpu.make_async_copy(src.at[expert], dst.at[slot], sem.at[slot])
                for src, dst in ((wg_ref, wgf_ref), (wu_ref, wuf_ref), (wd_ref, wdf_ref))]

    @pl.when(c < nu_ref[0])
    def _():
        @pl.when(seg_ref[c] >= 0)
        def _():
            slot = seg_ref[c] & 1

            @pl.when(c == 0)
            def _():
                for copy in fetch(ce_ref[0], 0):
                    copy.start()

            for copy in fetch(ce_ref[c], slot):
                copy.wait()
            wgb_ref[...] = wgf_ref[slot].astype(_BF16)
            wub_ref[...] = wuf_ref[slot].astype(_BF16)
            wdb_ref[...] = wdf_ref[slot].astype(_BF16)

            @pl.when(nxt_ref[c] >= 0)
            def _():
                for copy in fetch(nxt_ref[c], 1 - slot):
                    copy.start()

        row = lax.broadcasted_iota(jnp.int32, xs_ref.shape, 0)
        x = _unpack_halves(jnp.where(row < nv_ref[c], xs_ref[...], jnp.uint32(0))).astype(_BF16)
        half = x.shape[0] // 2
        ups = [(_dot(x[r:r + half], wgb_ref[...]), _dot(x[r:r + half], wub_ref[...])) for r in (0, half)]
        for (g, u), r in zip(ups, (0, half)):
            hmid = (g * jax.nn.sigmoid(g)) * u
            ys_ref[r:r + half, :] = _pack_halves(_dot(hmid.astype(_BF16), wdb_ref[...]))


def _experts(chunk_e, n_valid, n_used, seg, nxt, xs, wg, wu, wd):
    n_slots, w = xs.shape
    ch = EXPERT_CHUNK
    d, de = wg.shape[1], wg.shape[2]
    slot = lambda c, ce, nv, nu, sg, nx: (jnp.minimum(c, nu[0] - 1), 0)
    hbm = pl.BlockSpec(memory_space=pl.ANY)
    return pl.pallas_call(
        _experts_body,
        out_shape=jax.ShapeDtypeStruct((n_slots, d // 2), jnp.uint32),
        grid_spec=pltpu.PrefetchScalarGridSpec(
            num_scalar_prefetch=5,
            grid=(n_slots // ch,),
            in_specs=[pl.BlockSpec((ch, w), slot), hbm, hbm, hbm],
            out_specs=pl.BlockSpec((ch, d // 2), slot),
            scratch_shapes=[pltpu.VMEM((d, de), _BF16), pltpu.VMEM((d, de), _BF16), pltpu.VMEM((de, d), _BF16),
                            pltpu.VMEM((2, d, de), _F32), pltpu.VMEM((2, d, de), _F32), pltpu.VMEM((2, de, d), _F32),
                            pltpu.SemaphoreType.DMA((2,))]),
        compiler_params=_params("arbitrary"),
        name="experts",
    )(chunk_e, n_valid, n_used, seg, nxt, xs, wg, wu, wd)


def _combine_body(x1_ref, rinfo_ref, mod_ref, y0_ref, y1_ref, *rest):
    o_ref = rest[-1]
    y = rinfo_ref[:, 4:5] * _unpack_halves(y0_ref[...]) + rinfo_ref[:, 5:6] * _unpack_halves(y1_ref[...])
    o_ref[...] = x1_ref[...] + mod_ref[:, 5 * D_MODEL:6 * D_MODEL] * y


def _combine(x1, rinfo, mod3, gathered, seq, part, out_so_far):
    t, d = x1.shape
    tf = min(COMBINE_TILE, seq)
    per_b = seq // tf
    nt = t // tf
    first = part * nt
    in_specs = [pl.BlockSpec((tf, d), lambda i: (i, 0)),
                pl.BlockSpec((tf, ROUTER_LANES), lambda i: (i, 0)),
                pl.BlockSpec((None, 1, mod3.shape[2]), lambda i: ((i + first) // per_b, 0, 0)),
                pl.BlockSpec((tf, d // 2), lambda i: (i, 0)),
                pl.BlockSpec((tf, d // 2), lambda i: (i + nt, 0))]
    args = [x1, rinfo, mod3, gathered, gathered]
    aliases = {}
    if out_so_far is not None:
        in_specs.append(pl.BlockSpec(memory_space=pl.ANY))
        args.append(out_so_far)
        aliases = {len(args) - 1: 0}
    return pl.pallas_call(
        _combine_body,
        out_shape=jax.ShapeDtypeStruct((t * MOE_PARTS, d), _F32),
        grid=(nt,),
        in_specs=in_specs,
        out_specs=pl.BlockSpec((tf, d), lambda i: (i + first, 0)),
        input_output_aliases=aliases,
        compiler_params=_params("arbitrary"),
        name="combine",
    )(*args)


def _rope_trig(positions):
    inv_freq = ROPE_THETA ** (-jnp.arange(0, ROPE_DIM, 2, dtype=_F32) / ROPE_DIM)
    ang = positions.reshape(-1).astype(_F32)[:, None] * inv_freq
    return jnp.concatenate([jnp.cos(ang), jnp.sin(ang)], axis=1)


def _layer(x, mod, positions, g_mix, g_ffn, w_in, w_bg, qg, kg, w_bd, w_bs, w_out, w_rg, b_rg, w_re, b_re,
           w_eg, w_eu, w_ed):
    batch, seq, d = x.shape
    t = batch * seq
    x2 = x.reshape(t, d)
    mod3 = mod.reshape(batch, 1, mod.shape[1])
    gw = GROUP_WIDTH

    lane = jnp.arange(gw)
    bd = jnp.where(lane[:, None] // HEAD_DIM == lane[None, :] // HEAD_DIM, 1.0 / HEAD_DIM, 0.0).astype(_BF16)
    ex = (jnp.arange(LANES)[:, None] == (lane[None, :] // HEAD_DIM) * LSE_SEG).astype(_BF16)
    ex = jnp.concatenate([ex, ex], axis=0)
    tri = (jnp.arange(ROW_TILE)[:, None] > jnp.arange(ROW_TILE)[None, :]).astype(_BF16)
    tile4 = lambda g: jnp.tile(g.astype(_F32), HEADS_PER_GROUP).reshape(1, gw)
    wr = jnp.zeros((d, ROUTER_LANES), _F32).at[:, :N_GROUPS].set(w_rg).at[:, N_GROUPS:N_GROUPS + N_EXPERTS].set(w_re)
    wrh = wr.astype(_BF16)
    wrl = (wr - wrh.astype(_F32)).astype(_BF16)
    br = jnp.zeros((1, ROUTER_LANES), _F32).at[0, :N_GROUPS].set(b_rg).at[0, N_GROUPS:N_GROUPS + N_EXPERTS].set(b_re)

    d0, d1, d2, sbp = _qkv(x2, mod3, g_mix.reshape(1, d), w_in.astype(_BF16), tile4(qg) * ATTN_SCALE, tile4(kg),
                           _rope_trig(positions), bd, seq)
    dil = [_dilated_group(a, dl, batch, seq) for a, (_, dl) in zip((d0, d1, d2), DIL_PATTERNS)]
    osb = _stick_breaking(sbp, batch, seq)

    merge_weights = (w_bg.astype(_BF16), w_bd.astype(_BF16), w_bs.astype(_BF16), w_out.astype(_BF16))
    tp = t // MOE_PARTS
    ch = EXPERT_CHUNK
    win = SC_INDEX_WINDOW
    n_chunks = -(-2 * tp // ch) + N_EXPERTS
    chunk_start = jnp.arange(n_chunks, dtype=jnp.int32) * ch
    expert_ids = jnp.arange(N_EXPERTS, dtype=jnp.int32)
    out = None
    for part in range(MOE_PARTS):
        x1, h2p, rinfo, er, cnt = _merge(
            x2, mod3, g_mix.reshape(1, d), g_ffn.reshape(1, d), [o for o, _ in dil], [l for _, l in dil], osb,
            *merge_weights, wrh, wrl, br, ex, tri, seq, part)

        counts = cnt[0, EXPERT_LANE0:EXPERT_LANE0 + N_EXPERTS].astype(jnp.int32)
        padded = (counts + ch - 1) // ch * ch
        pend = jnp.cumsum(padded)
        pstart = pend - padded
        chunk_e = jnp.minimum(jnp.sum((pend[None, :] <= chunk_start[:, None]).astype(jnp.int32), axis=1),
                              N_EXPERTS - 1)
        n_used = (pend[-1:] // ch).astype(jnp.int32)
        begin = chunk_start[:, None]
        inside = (pstart[None, :] <= begin) & (begin < pend[None, :])
        n_valid = jnp.sum(jnp.where(inside, jnp.clip(counts[None, :] - (begin - pstart[None, :]), 0, ch), 0), axis=1)

        dest = _dest(pstart, er)
        xs = _sc_scatter(h2p, dest[0].reshape(tp // win, win), dest[1].reshape(tp // win, win), n_chunks * ch)
        first = (chunk_e != jnp.concatenate([jnp.full((1,), -1, jnp.int32), chunk_e[:-1]])) & (chunk_start < pend[-1])
        seg_no = jnp.cumsum(first.astype(jnp.int32)) - 1
        seg = jnp.where(first, seg_no, -1 - seg_no)
        later = (expert_ids[None, :] > expert_ids[:, None]) & (padded > 0)[None, :]
        next_expert = jnp.min(jnp.where(later, expert_ids[None, :], N_EXPERTS), axis=1)
        next_expert = jnp.where(next_expert == N_EXPERTS, -1, next_expert)
        nxt = jnp.sum(jnp.where(chunk_e[:, None] == expert_ids[None, :], next_expert[None, :], 0), axis=1)
        ys = _experts(chunk_e, n_valid, n_used, seg.astype(jnp.int32), nxt.astype(jnp.int32), xs, w_eg, w_eu, w_ed)
        gathered = _sc_gather(ys, dest.reshape(2 * tp // win, win))
        out = _combine(x1, rinfo, mod3, gathered, seq, part, out)
    return out.reshape(batch, seq, d)


def kernel(x, c, positions, w_ada, b_ada, g_norm_mix, g_norm_ffn, w_in, w_branch_gate, q_norm_g, k_norm_g,
           w_branch_dil, w_branch_sb, w_out, w_router_group, b_router_group, w_router_expert, b_router_expert,
           w_expert_gate, w_expert_up, w_expert_down):
    for l in range(w_ada.shape[0]):
        mod = _ada(c, w_ada[l], b_ada[l])
        x = _layer(x, mod, positions, g_norm_mix[l], g_norm_ffn[l], w_in[l], w_branch_gate[l], q_norm_g[l],
                   k_norm_g[l], w_branch_dil[l], w_branch_sb[l], w_out[l], w_router_group[l], b_router_group[l],
                   w_router_expert[l], b_router_expert[l], w_expert_gate[l], w_expert_up[l], w_expert_down[l])
    return x
```

```python
import functools

import jax
import jax.numpy as jnp
from jax import lax
from jax.experimental import pallas as pl
from jax.experimental.pallas import tpu as pltpu
from jax.experimental.pallas import tpu_sc as plsc

D_MODEL = 1024
HEAD_DIM = 64
DIL_PATTERNS = ((128, 1), (512, 4), (2048, 16))
HEADS_PER_GROUP = 4
GROUP_WIDTH = HEADS_PER_GROUP * HEAD_DIM
N_DIL_GROUPS = len(DIL_PATTERNS)
DIL_WIDTH = N_DIL_GROUPS * GROUP_WIDTH
QKV_WIDTH = 3 * DIL_WIDTH + 3 * GROUP_WIDTH
WINDOW_KEYS = 128
ROPE_THETA = 500000.0
ROPE_DIM = HEAD_DIM // 4
N_GROUPS = 4
EXPERTS_PER_GROUP = 8
N_EXPERTS = N_GROUPS * EXPERTS_PER_GROUP
D_EXPERT = 512
RMS_EPS = 1e-6
ATTN_SCALE = HEAD_DIM ** -0.5

LANES = 128
ROUTER_LANES = LANES
EXPERT_LANE0 = N_GROUPS
LSE_SEG = LANES // HEADS_PER_GROUP
NEG_BIG = -1e30
SB_DEAD_LOG = -120.0
SB_HEAD_BLOCKS = 3
SB_CHAINS = 8

ROW_TILE = 512
QKV_TILE = 1024
QBLK = 128
DIL_QTILE = 1024
EXPERT_CHUNK = 512
COMBINE_TILE = 1024
DEST_TILE = 8192
SC_CORES = 2
SC_SUBCORES = 16
SC_WORKERS = SC_CORES * SC_SUBCORES
SC_INDEX_WINDOW = 128
MOE_PARTS = 2
VMEM_LIMIT = 48 * 1024 * 1024

_BF16 = jnp.bfloat16
_F32 = jnp.float32
_NT = (((1,), (1,)), ((), ()))


def _dot(a, b):
    return jnp.dot(a, b, preferred_element_type=_F32)


def _dot_nt(a, b):
    return lax.dot_general(a, b, _NT, preferred_element_type=_F32)


def _split(a):
    hi = a.astype(_BF16)
    lo = (a - hi.astype(_F32)).astype(_BF16)
    return hi, lo


def _dot3(a, b):
    ah, al = _split(a)
    bh, bl = _split(b)
    return _dot(ah, bh) + (_dot(ah, bl) + _dot(al, bh))


def _rms_mod(x, g, scale, shift):
    y = x * lax.rsqrt(jnp.mean(x * x, axis=-1, keepdims=True) + RMS_EPS)
    return y * g * (1.0 + scale) + shift


def _params(*sem):
    return pltpu.CompilerParams(dimension_semantics=sem, vmem_limit_bytes=VMEM_LIMIT)


def _ada_body(c_ref, w_ref, b_ref, o_ref):
    c = c_ref[...]
    o_ref[...] = _dot3(c * jax.nn.sigmoid(c), w_ref[...]) + b_ref[...]


def _ada(c, w_ada, b_ada):
    b, d = c.shape
    n = w_ada.shape[1]
    rows = -(-b // 16) * 16
    cp = jnp.zeros((rows, d), _F32).at[:b].set(c)
    nt = 1536
    out = pl.pallas_call(
        _ada_body,
        out_shape=jax.ShapeDtypeStruct((rows, n), _F32),
        grid=(n // nt,),
        in_specs=[pl.BlockSpec((rows, d), lambda j: (0, 0)),
                  pl.BlockSpec((d, nt), lambda j: (0, j)),
                  pl.BlockSpec((1, nt), lambda j: (0, j))],
        out_specs=pl.BlockSpec((rows, nt), lambda j: (0, j)),
        compiler_params=_params("arbitrary"),
        name="ada",
    )(cp, w_ada, b_ada.reshape(1, n))
    return out[:b]


def _qkv_body(x_ref, mod_ref, g_ref, w_ref, qg_ref, kg_ref, trig_ref, bd_ref,
              o0_ref, o1_ref, o2_ref, osb_ref, st_ref, acc_ref):
    d = D_MODEL
    tm = x_ref.shape[0]
    h = _rms_mod(x_ref[...], g_ref[...], mod_ref[:, d:2 * d], mod_ref[:, 0:d])
    hb = h.astype(_BF16)
    half = ROPE_DIM // 2
    in_head = lax.broadcasted_iota(jnp.int32, (tm, LANES), 1) & (HEAD_DIM - 1)
    freq = in_head & (half - 1)
    trig = jnp.concatenate([trig_ref[...], jnp.zeros((tm, LANES - ROPE_DIM), _F32)], axis=1)
    cos = jnp.take_along_axis(trig, freq, axis=1)
    sin = jnp.take_along_axis(trig, freq + half, axis=1)
    cc = jnp.where(in_head < ROPE_DIM, cos, 1.0)
    s1 = jnp.where(in_head < half, -sin, 0.0)
    s2 = jnp.where((in_head >= half) & (in_head < ROPE_DIM), sin, 0.0)
    cc, s1, s2 = [jnp.concatenate([a, a], axis=1) for a in (cc, s1, s2)]
    bd = bd_ref[...]
    gw = GROUP_WIDTH

    def normed_rotated(acc, gain):
        ms = _dot((acc * acc).astype(_BF16), bd)
        y = acc * lax.rsqrt(ms + RMS_EPS) * gain
        return y * cc + pltpu.roll(y, gw - ROPE_DIM // 2, 1) * s1 + pltpu.roll(y, ROPE_DIM // 2, 1) * s2

    def store(o_ref, dil, part, y):
        if dil == 1:
            o_ref[:, part * gw:(part + 1) * gw] = y.astype(_BF16)
            return
        for s in range(gw // LANES):
            st_ref[s] = y[:, s * LANES:(s + 1) * LANES]
        for r in range(dil):
            for s in range(gw // LANES):
                col0 = (3 * r + part) * gw + s * LANES
                o_ref[:, col0:col0 + LANES] = st_ref[s, pl.ds(r, tm // dil, stride=dil), :].astype(_BF16)

    def project(col0):
        return _dot(hb, w_ref[:, col0:col0 + gw])

    outs = (o0_ref, o1_ref, o2_ref)
    dils = [dl for _, dl in DIL_PATTERNS]
    normed = [(g, part, part * DIL_WIDTH + g * gw) for g in range(N_DIL_GROUPS) for part in (0, 1)]
    for n, (_, _, col0) in enumerate(normed):
        acc_ref[n] = project(col0)
    plain = [("v", g) for g in range(N_DIL_GROUPS)] + [("sb", part) for part in range(3)]
    for n, (kind, j) in enumerate(plain):
        if kind == "v":
            store(outs[j], dils[j], 2, project(2 * DIL_WIDTH + j * gw))
        else:
            acc = project(3 * DIL_WIDTH + j * gw)
            osb_ref[:, j * gw:(j + 1) * gw] = (acc * ATTN_SCALE if j == 0 else acc).astype(_BF16)
        g, part, _ = normed[n]
        gain = qg_ref[...] if part == 0 else kg_ref[...]
        store(outs[g], dils[g], part, normed_rotated(acc_ref[n], gain))


def _qkv(x2, mod3, g_mix, w_in, qg, kg, trig, bd, seq):
    t, d = x2.shape
    tm = min(QKV_TILE, seq)
    per_b = seq // tm
    row = lambda i: (i, 0)
    const = lambda i: (0, 0)
    width = 3 * GROUP_WIDTH
    dils = [dl for _, dl in DIL_PATTERNS] + [1]
    return pl.pallas_call(
        _qkv_body,
        out_shape=[jax.ShapeDtypeStruct((t // dl, dl * width), _BF16) for dl in dils],
        grid=(t // tm,),
        in_specs=[pl.BlockSpec((tm, d), row),
                  pl.BlockSpec((None, 1, mod3.shape[2]), lambda i: (i // per_b, 0, 0)),
                  pl.BlockSpec((1, d), const),
                  pl.BlockSpec(w_in.shape, const),
                  pl.BlockSpec((1, GROUP_WIDTH), const),
                  pl.BlockSpec((1, GROUP_WIDTH), const),
                  pl.BlockSpec((tm, ROPE_DIM), row),
                  pl.BlockSpec(bd.shape, const)],
        out_specs=[pl.BlockSpec((tm // dl, dl * width), row) for dl in dils],
        scratch_shapes=[pltpu.VMEM((GROUP_WIDTH // LANES, tm, LANES), _F32),
                        pltpu.VMEM((2 * N_DIL_GROUPS, tm, GROUP_WIDTH), _F32)],
        compiler_params=_params("arbitrary"),
        name="qkv",
    )(x2, mod3, g_mix, w_in, qg, kg, trig, bd)


def _dil_body(q_ref, kp_ref, kc_ref, vp_ref, vc_ref, o_ref, lse_ref, kf_ref, vf_ref):
    tq = q_ref.shape[0]
    first = pl.program_id(2) == 0
    kf_ref[0:QBLK, :] = kp_ref[...]
    kf_ref[QBLK:, :] = kc_ref[...]
    vf_ref[0:QBLK, :] = vp_ref[...]
    vf_ref[QBLK:, :] = vc_ref[...]
    nh = HEADS_PER_GROUP
    row = lax.broadcasted_iota(jnp.int32, (nh * QBLK, 2 * QBLK), 0) & (QBLK - 1)
    col = lax.broadcasted_iota(jnp.int32, (nh * QBLK, 2 * QBLK), 1)
    band = (col >= row) & (col <= row + WINDOW_KEYS)
    lane = lax.broadcasted_iota(jnp.int32, (1, GROUP_WIDTH), 1)
    slane = lax.broadcasted_iota(jnp.int32, (1, LANES), 1)
    head_masks = [(lane >= h * HEAD_DIM) & (lane < (h + 1) * HEAD_DIM) for h in range(nh)]
    subs = range(tq // QBLK)
    rows = [slice(j * QBLK, (j + 1) * QBLK) for j in subs]
    window = [slice(j * QBLK, (j + 2) * QBLK) for j in subs]

    def stack(qj):
        return jnp.concatenate([jnp.where(hm, qj, jnp.zeros_like(qj)) for hm in head_masks], axis=0)

    valid = [band & ((col >= QBLK) | jnp.logical_not(first)) if j == 0 else band for j in subs]
    s = [jnp.where(valid[j], _dot_nt(stack(q_ref[rows[j], :]), kf_ref[window[j], :]), NEG_BIG) for j in subs]
    m = [jnp.max(s[j], axis=1, keepdims=True) for j in subs]
    p = [jnp.exp(s[j] - m[j]) for j in subs]
    l = [jnp.sum(p[j], axis=1, keepdims=True) for j in subs]
    o_all = [_dot(p[j].astype(_BF16), vf_ref[window[j], :]) / l[j] for j in subs]
    for j in subs:
        lse_all = m[j] + jnp.log(l[j])
        o_acc = jnp.zeros((QBLK, GROUP_WIDTH), _F32)
        lse_t = jnp.zeros((QBLK, LANES), _F32)
        for h, hm in enumerate(head_masks):
            o_acc = jnp.where(hm, o_all[j][h * QBLK:(h + 1) * QBLK, :], o_acc)
            sm = (slane >= h * LSE_SEG) & (slane < (h + 1) * LSE_SEG)
            lse_t = jnp.where(sm, lse_all[h * QBLK:(h + 1) * QBLK, :], lse_t)
        o_ref[rows[j], :] = o_acc.astype(_BF16)
        lse_ref[rows[j], :] = lse_t


def _dilated_group(view2, dil, batch, seq):
    sd = seq // dil
    tq = min(DIL_QTILE, sd)
    per = tq // QBLK
    gw = GROUP_WIDTH
    view = view2.reshape(batch, sd, dil * 3 * gw)
    cur = lambda part: pl.BlockSpec((None, tq, gw), lambda b, r, i: (b, i, 3 * r + part))
    prev = lambda part: pl.BlockSpec((None, QBLK, gw),
                                     lambda b, r, i: (b, jnp.maximum(i * per - 1, 0), 3 * r + part))
    o, lse = pl.pallas_call(
        _dil_body,
        out_shape=[jax.ShapeDtypeStruct((batch, sd, dil * gw), _BF16),
                   jax.ShapeDtypeStruct((batch, sd, dil * LANES), _F32)],
        grid=(batch, dil, sd // tq),
        in_specs=[cur(0), prev(1), cur(1), prev(2), cur(2)],
        out_specs=[pl.BlockSpec((None, tq, gw), lambda b, r, i: (b, i, r)),
                   pl.BlockSpec((None, tq, LANES), lambda b, r, i: (b, i, r))],
        scratch_shapes=[pltpu.VMEM((tq + QBLK, gw), _BF16), pltpu.VMEM((tq + QBLK, gw), _BF16)],
        compiler_params=_params("arbitrary", "arbitrary", "arbitrary"),
        name=f"dil{dil}",
    )(view, view, view, view, view)
    return o.reshape(batch * sd, dil * gw), lse.reshape(batch * sd, dil * LANES)


def _sb_body(q_ref, k_ref, v_ref, o_ref, carry_ref, acc_ref, qs_ref):
    step_id = pl.program_id(1)
    chains = range(SB_CHAINS)
    blk = [step_id * SB_CHAINS + c for c in chains]
    nh = HEADS_PER_GROUP
    row = lax.broadcasted_iota(jnp.int32, (nh * QBLK, QBLK), 0) & (QBLK - 1)
    col = lax.broadcasted_iota(jnp.int32, (nh * QBLK, QBLK), 1)
    strict = col < row
    ur = lax.broadcasted_iota(jnp.int32, (2 * QBLK, QBLK), 0) & (QBLK - 1)
    uc = lax.broadcasted_iota(jnp.int32, (2 * QBLK, QBLK), 1)
    u = jnp.where(ur > uc, 1.0, 0.0).astype(_BF16)
    lane = lax.broadcasted_iota(jnp.int32, (1, GROUP_WIDTH), 1)
    head_masks = [(lane >= h * HEAD_DIM) & (lane < (h + 1) * HEAD_DIM) for h in range(nh)]
    for c in chains:
        q = q_ref[c * QBLK:(c + 1) * QBLK, :]
        qs_ref[c] = jnp.concatenate([jnp.where(hm, q, jnp.zeros_like(q)) for hm in head_masks], axis=0)

    def softplus(z):
        return jnp.maximum(z, 0.0) + jnp.log(1.0 + jnp.exp(-jnp.abs(z)))

    def later_keys(log_1m):
        hi, lo = _split(log_1m)
        return _dot(jnp.concatenate([hi, lo], axis=1), u)

    nb = SB_HEAD_BLOCKS
    cols = [slice(j * QBLK, (j + 1) * QBLK) for j in range(nb)]
    kbs = [[blk[c] - (nb - 1) + j for j in range(nb)] for c in chains]
    starts = [[pl.multiple_of(jnp.maximum(kb, 0) * QBLK, QBLK) for kb in kbs[c]] for c in chains]
    keep = [[strict if j == nb - 1 else (kbs[c][j] >= 0) for j in range(nb)] for c in chains]
    z = [_dot_nt(qs_ref[c], jnp.concatenate([k_ref[pl.ds(s, QBLK), :] for s in starts[c]], axis=0)) for c in chains]
    sp = [softplus(z[c]) for c in chains]
    log_1m = [[jnp.where(keep[c][j], -sp[c][:, cols[j]], 0.0) for j in range(nb)] for c in chains]
    totals = [[jnp.sum(l, axis=1, keepdims=True) for l in log_1m[c]] for c in chains]
    later = [[later_keys(log_1m[c][j]) for j in range(nb)] for c in chains]
    for c in chains:
        a_blocks = []
        after = jnp.zeros_like(totals[c][0])
        for j in reversed(range(nb)):
            a = jnp.exp((z[c][:, cols[j]] - sp[c][:, cols[j]]) + later[c][j] + after)
            a_blocks.insert(0, jnp.where(keep[c][j], a, 0.0).astype(_BF16))
            after = after + totals[c][j]
        acc_ref[c] = _dot(jnp.concatenate(a_blocks, axis=1),
                          jnp.concatenate([v_ref[pl.ds(s, QBLK), :] for s in starts[c]], axis=0))
        carry_ref[c] = after

    for c in chains:
        def tile(kb, c=c):
            start = pl.multiple_of(kb * QBLK, QBLK)
            z = _dot_nt(qs_ref[c], k_ref[pl.ds(start, QBLK), :])
            sp = softplus(z)
            log_1m = -sp
            a = jnp.exp((z - sp) + later_keys(log_1m) + carry_ref[c])
            acc_ref[c] += _dot(a.astype(_BF16), v_ref[pl.ds(start, QBLK), :])
            carry = carry_ref[c] + jnp.sum(log_1m, axis=1, keepdims=True)
            carry_ref[c] = carry
            return jnp.max(carry)

        def cond(st):
            return (st[0] >= 0) & (st[1] > SB_DEAD_LOG)

        def step(st, tile=tile):
            return st[0] - 1, tile(st[0])

        lax.while_loop(cond, step, (blk[c] - nb, jnp.max(carry_ref[c])))
        out = jnp.zeros((QBLK, GROUP_WIDTH), _F32)
        for h, hm in enumerate(head_masks):
            out = jnp.where(hm, acc_ref[c, h * QBLK:(h + 1) * QBLK, :], out)
        o_ref[c * QBLK:(c + 1) * QBLK, :] = out.astype(_BF16)


def _stick_breaking(arr, batch, seq):
    gw = GROUP_WIDTH
    qt = SB_CHAINS * QBLK
    view = arr.reshape(batch, seq, 3 * gw)
    o = pl.pallas_call(
        _sb_body,
        out_shape=jax.ShapeDtypeStruct((batch, seq, gw), _BF16),
        grid=(batch, seq // qt),
        in_specs=[pl.BlockSpec((None, qt, gw), lambda b, i: (b, i, 0)),
                  pl.BlockSpec((None, seq, gw), lambda b, i: (b, 0, 1)),
                  pl.BlockSpec((None, seq, gw), lambda b, i: (b, 0, 2))],
        out_specs=pl.BlockSpec((None, qt, gw), lambda b, i: (b, i, 0)),
        scratch_shapes=[pltpu.VMEM((SB_CHAINS, HEADS_PER_GROUP * QBLK, 1), _F32),
                        pltpu.VMEM((SB_CHAINS, HEADS_PER_GROUP * QBLK, gw), _F32),
                        pltpu.VMEM((SB_CHAINS, HEADS_PER_GROUP * QBLK, gw), _BF16)],
        compiler_params=_params("arbitrary", "arbitrary"),
        name="sb",
    )(view, view, view)
    return o.reshape(batch * seq, gw)


def _merge_body(x_ref, mod_ref, g1_ref, g2_ref, o0_ref, o1_ref, o2_ref, l0_ref, l1_ref, l2_ref, osb_ref,
                wbg_ref, wbd_ref, wbs_ref, wout_ref, wrh_ref, wrl_ref, br_ref, ex_ref, tri_ref,
                x1_ref, h2p_ref, rinfo_ref, er_ref, cnt_out_ref, cnt_ref, os1_ref, os2_ref, ls1_ref, ls2_ref):
    d = D_MODEL
    tm = x_ref.shape[0]
    x = x_ref[...]
    hb = _rms_mod(x, g1_ref[...], mod_ref[:, d:2 * d], mod_ref[:, 0:d]).astype(_BF16)

    def natural(ref, st_ref, dil):
        if dil == 1:
            return ref[...].astype(_F32)
        slabs = st_ref.shape[0]
        for r in range(dil):
            for s in range(slabs):
                col0 = (r * slabs + s) * LANES
                st_ref[s, pl.ds(r, tm // dil, stride=dil), :] = ref[:, col0:col0 + LANES].astype(_F32)
        return jnp.concatenate([st_ref[s] for s in range(slabs)], axis=1)

    dils = [dl for _, dl in DIL_PATTERNS]
    o_nat = [natural(r, s, dl) for r, s, dl in zip((o0_ref, o1_ref, o2_ref), (None, os1_ref, os2_ref), dils)]
    l0, l1, l2 = [natural(r, s, dl) for r, s, dl in zip((l0_ref, l1_ref, l2_ref), (None, ls1_ref, ls2_ref), dils)]

    lmax = jnp.maximum(jnp.maximum(l0, l1), l2)
    e0, e1, e2 = jnp.exp(l0 - lmax), jnp.exp(l1 - lmax), jnp.exp(l2 - lmax)
    inv = 1.0 / (e0 + e1 + e2)
    ex = ex_ref[...]

    def widen(w):
        hi, lo = _split(w)
        return _dot(jnp.concatenate([hi, lo], axis=1), ex)

    w_groups = [widen(e * inv) for e in (e0, e1, e2)]
    gate_dil = jax.nn.sigmoid(_dot(hb, wbg_ref[:, :d]))
    o_dil = w_groups[0] * o_nat[0] + w_groups[1] * o_nat[1] + w_groups[2] * o_nat[2]
    branch_dil = _dot(o_dil.astype(_BF16), wbd_ref[...])
    branch_sb = _dot(osb_ref[...], wbs_ref[...])
    gate_sb = jax.nn.sigmoid(_dot(hb, wbg_ref[:, d:]))
    merged = gate_dil * branch_dil + gate_sb * branch_sb
    x1 = x + mod_ref[:, 2 * d:3 * d] * _dot(merged.astype(_BF16), wout_ref[...])
    x1_ref[...] = x1

    h2 = _rms_mod(x1, g2_ref[...], mod_ref[:, 4 * d:5 * d], mod_ref[:, 3 * d:4 * d])
    h2p_ref[...] = _pack_halves(h2)

    hh, hl = _split(h2)
    logits = _dot(hh, wrh_ref[...]) + (_dot(hl, wrh_ref[...]) + _dot(hh, wrl_ref[...])) + br_ref[...]
    lane = lax.broadcasted_iota(jnp.int32, (tm, ROUTER_LANES), 1).astype(_F32)
    far = float(ROUTER_LANES)

    def top(vals):
        m = jnp.max(vals, axis=1, keepdims=True)
        return m, jnp.min(jnp.where(vals == m, lane, far), axis=1, keepdims=True)

    is_group = lane < N_GROUPS
    mg, gsel = top(jnp.where(is_group, logits, NEG_BIG))
    pg_top = 1.0 / jnp.sum(jnp.where(is_group, jnp.exp(logits - mg), 0.0), axis=1, keepdims=True)
    lane0 = EXPERT_LANE0 + EXPERTS_PER_GROUP * gsel
    le = jnp.where((lane >= lane0) & (lane < lane0 + EXPERTS_PER_GROUP), logits, NEG_BIG)
    m1, i1 = top(le)
    m2, i2 = top(jnp.where(lane == i1, NEG_BIG, le))
    t2 = jnp.exp(m2 - m1)
    w0 = pg_top / (1.0 + t2)
    w1 = pg_top * t2 / (1.0 + t2)

    @pl.when(pl.program_id(0) == 0)
    def _():
        cnt_ref[...] = jnp.zeros_like(cnt_ref)

    sel0, sel1 = lane == i1, lane == i2
    onehot = jnp.where(sel0 | sel1, 1.0, 0.0)
    before = _dot(tri_ref[...], onehot.astype(_BF16)) + cnt_ref[0:1, :]
    r0 = jnp.sum(jnp.where(sel0, before, 0.0), axis=1, keepdims=True)
    r1 = jnp.sum(jnp.where(sel1, before, 0.0), axis=1, keepdims=True)
    cnt_ref[...] += jnp.sum(onehot, axis=0, keepdims=True)
    cnt_out_ref[...] = cnt_ref[...]
    cols = (i1 - EXPERT_LANE0, i2 - EXPERT_LANE0, r0, r1, w0, w1)
    rinfo = jnp.zeros((tm, ROUTER_LANES), _F32)
    for c, v in enumerate(cols):
        rinfo = jnp.where(lane == float(c), v, rinfo)
    rinfo_ref[...] = rinfo
    er_ref[...] = jnp.transpose(rinfo)[0:8, :].astype(jnp.int32)


def _merge(x2, mod3, g1, g2, outs, lses, osb, wbg, wbd, wbs, wout, wrh, wrl, br, ex, tri, seq, part):
    d = x2.shape[1]
    tm = ROW_TILE
    per_b = seq // tm
    t = x2.shape[0] // MOE_PARTS
    first = part * (t // tm)
    src = lambda i: (i + first, 0)
    row = lambda i: (i, 0)
    const = lambda i: (0, 0)
    full = lambda a: pl.BlockSpec(a.shape, const)
    gw = GROUP_WIDTH
    dils = [dl for _, dl in DIL_PATTERNS]
    return pl.pallas_call(
        _merge_body,
        out_shape=[jax.ShapeDtypeStruct((t, d), _F32),
                   jax.ShapeDtypeStruct((t, d // 2), jnp.uint32),
                   jax.ShapeDtypeStruct((t, ROUTER_LANES), _F32),
                   jax.ShapeDtypeStruct((8, t), jnp.int32),
                   jax.ShapeDtypeStruct((8, ROUTER_LANES), _F32)],
        grid=(t // tm,),
        in_specs=[pl.BlockSpec((tm, d), src),
                  pl.BlockSpec((None, 1, mod3.shape[2]), lambda i: ((i + first) // per_b, 0, 0)),
                  full(g1), full(g2)]
                 + [pl.BlockSpec((tm // dl, dl * gw), src) for dl in dils]
                 + [pl.BlockSpec((tm // dl, dl * LANES), src) for dl in dils]
                 + [pl.BlockSpec((tm, gw), src)]
                 + [full(a) for a in (wbg, wbd, wbs, wout, wrh, wrl, br, ex, tri)],
        out_specs=[pl.BlockSpec((tm, d), row),
                   pl.BlockSpec((tm, d // 2), row),
                   pl.BlockSpec((tm, ROUTER_LANES), row),
                   pl.BlockSpec((8, tm), lambda i: (0, i)),
                   pl.BlockSpec((8, ROUTER_LANES), const)],
        scratch_shapes=[pltpu.VMEM((8, ROUTER_LANES), _F32),
                        pltpu.VMEM((gw // LANES, tm, LANES), _F32), pltpu.VMEM((gw // LANES, tm, LANES), _F32),
                        pltpu.VMEM((1, tm, LANES), _F32), pltpu.VMEM((1, tm, LANES), _F32)],
        compiler_params=_params("arbitrary"),
        name="merge",
    )(x2, mod3, g1, g2, *outs, *lses, osb, wbg, wbd, wbs, wout, wrh, wrl, br, ex, tri)


def _dest_body(ps_ref, er_ref, d_ref):
    e = er_ref[0:2, :]
    start = jnp.zeros_like(e)
    for x in range(N_EXPERTS):
        start = jnp.where(e == x, ps_ref[x], start)
    d_ref[...] = start + er_ref[2:4, :]


def _dest(pstart, er):
    t = er.shape[1]
    tw = min(DEST_TILE, t)
    return pl.pallas_call(
        _dest_body,
        out_shape=jax.ShapeDtypeStruct((2, t), jnp.int32),
        grid_spec=pltpu.PrefetchScalarGridSpec(
            num_scalar_prefetch=1,
            grid=(t // tw,),
            in_specs=[pl.BlockSpec((8, tw), lambda i, ps: (0, i))],
            out_specs=pl.BlockSpec((2, tw), lambda i, ps: (0, i))),
        compiler_params=_params("arbitrary"),
        name="dest",
    )(pstart, er)


def _sc_mesh():
    return plsc.VectorSubcoreMesh(core_axis_name="core", subcore_axis_name="subcore",
                                  num_cores=SC_CORES, num_subcores=SC_SUBCORES)


def _sc_worker():
    return lax.axis_index("subcore") * SC_CORES + lax.axis_index("core")


def _sc_scatter(x, idx0, idx1, n_slots):
    chunks = idx0.shape[0]
    per = chunks // SC_WORKERS
    win = idx0.shape[1]

    @functools.partial(
        pl.kernel, mesh=_sc_mesh(), out_type=jax.ShapeDtypeStruct((n_slots, x.shape[1]), x.dtype),
        scratch_types=[pltpu.VMEM((1, win), jnp.int32), pltpu.VMEM((1, win), jnp.int32),
                       pltpu.VMEM((win, x.shape[1]), x.dtype), pltpu.SemaphoreType.DMA],
        name="sc_scatter")
    def run(x_hbm, i0_hbm, i1_hbm, o_hbm, i0_v, i1_v, rows_v, sem):
        wid = _sc_worker()

        @pl.loop(0, per)
        def _(j):
            c = wid * per + j
            pltpu.sync_copy(i0_hbm.at[pl.ds(c, 1)], i0_v)
            pltpu.sync_copy(i1_hbm.at[pl.ds(c, 1)], i1_v)
            pltpu.sync_copy(x_hbm.at[pl.ds(c * win, win)], rows_v)
            first = pltpu.async_copy(rows_v, o_hbm.at[i0_v.at[0]], sem)
            second = pltpu.async_copy(rows_v, o_hbm.at[i1_v.at[0]], sem)
            first.wait()
            second.wait()

    return run(x, idx0, idx1)


def _sc_gather(table, idx):
    chunks, win = idx.shape
    per = chunks // SC_WORKERS

    @functools.partial(
        pl.kernel, mesh=_sc_mesh(), out_type=jax.ShapeDtypeStruct((chunks * win, table.shape[1]), table.dtype),
        scratch_types=[pltpu.VMEM((1, win), jnp.int32), pltpu.VMEM((win, table.shape[1]), table.dtype),
                       pltpu.SemaphoreType.DMA],
        name="sc_gather")
    def run(t_hbm, i_hbm, o_hbm, i_v, rows_v, sem):
        wid = _sc_worker()

        @pl.loop(0, per)
        def _(j):
            c = wid * per + j
            pltpu.sync_copy(i_hbm.at[pl.ds(c, 1)], i_v)
            pltpu.async_copy(t_hbm.at[i_v.at[0]], rows_v, sem).wait()
            pltpu.sync_copy(rows_v, o_hbm.at[pl.ds(c * win, win)])

    return run(table, idx)


def _pack_halves(a):
    h = a.shape[1] // 2
    lo = lax.bitcast_convert_type(a[:, :h].astype(_BF16).astype(_F32), jnp.uint32) >> 16
    hi = lax.bitcast_convert_type(a[:, h:].astype(_BF16).astype(_F32), jnp.uint32) & jnp.uint32(0xFFFF0000)
    return lo | hi


def _unpack_halves(w):
    return jnp.concatenate(
        [lax.bitcast_convert_type(w << 16, _F32), lax.bitcast_convert_type(w & jnp.uint32(0xFFFF0000), _F32)], axis=1)


def _experts_body(ce_ref, nv_ref, nu_ref, seg_ref, nxt_ref, xs_ref, wg_ref, wu_ref, wd_ref, ys_ref,
                  wgb_ref, wub_ref, wdb_ref, wgf_ref, wuf_ref, wdf_ref, sem):
    c = pl.program_id(0)

    def fetch(expert, slot):
        return [pltpu.make_async_copy(src.at[expert], dst.at[slot], sem.at[slot])
                for src, dst in ((wg_ref, wgf_ref), (wu_ref, wuf_ref), (wd_ref, wdf_ref))]

    @pl.when(c < nu_ref[0])
    def _():
        @pl.when(seg_ref[c] >= 0)
        def _():
            slot = seg_ref[c] & 1

            @pl.when(c == 0)
            def _():
                for copy in fetch(ce_ref[0], 0):
                    copy.start()

            for copy in fetch(ce_ref[c], slot):
                copy.wait()
            wgb_ref[...] = wgf_ref[slot].astype(_BF16)
            wub_ref[...] = wuf_ref[slot].astype(_BF16)
            wdb_ref[...] = wdf_ref[slot].astype(_BF16)

            @pl.when(nxt_ref[c] >= 0)
            def _():
                for copy in fetch(nxt_ref[c], 1 - slot):
                    copy.start()

        row = lax.broadcasted_iota(jnp.int32, xs_ref.shape, 0)
        x = _unpack_halves(jnp.where(row < nv_ref[c], xs_ref[...], jnp.uint32(0))).astype(_BF16)
        half = x.shape[0] // 2

        def ffn(r):
            xr = x[r:r + half]
            g = _dot(xr, wgb_ref[...])
            hmid = (g * jax.nn.sigmoid(g)) * _dot(xr, wub_ref[...])
            ys_ref[r:r + half, :] = _pack_halves(_dot(hmid.astype(_BF16), wdb_ref[...]))

        ffn(0)

        @pl.when(nv_ref[c] > half)
        def _():
            ffn(half)

        @pl.when(nv_ref[c] <= half)
        def _():
            ys_ref[half:, :] = jnp.zeros((half, ys_ref.shape[1]), ys_ref.dtype)


def _experts(chunk_e, n_valid, n_used, seg, nxt, xs, wg, wu, wd):
    n_slots, w = xs.shape
    ch = EXPERT_CHUNK
    d, de = wg.shape[1], wg.shape[2]
    slot = lambda c, ce, nv, nu, sg, nx: (jnp.minimum(c, nu[0] - 1), 0)
    hbm = pl.BlockSpec(memory_space=pl.ANY)
    return pl.pallas_call(
        _experts_body,
        out_shape=jax.ShapeDtypeStruct((n_slots, d // 2), jnp.uint32),
        grid_spec=pltpu.PrefetchScalarGridSpec(
            num_scalar_prefetch=5,
            grid=(n_slots // ch,),
            in_specs=[pl.BlockSpec((ch, w), slot), hbm, hbm, hbm],
            out_specs=pl.BlockSpec((ch, d // 2), slot),
            scratch_shapes=[pltpu.VMEM((d, de), _BF16), pltpu.VMEM((d, de), _BF16), pltpu.VMEM((de, d), _BF16),
                            pltpu.VMEM((2, d, de), _F32), pltpu.VMEM((2, d, de), _F32), pltpu.VMEM((2, de, d), _F32),
                            pltpu.SemaphoreType.DMA((2,))]),
        compiler_params=_params("arbitrary"),
        name="experts",
    )(chunk_e, n_valid, n_used, seg, nxt, xs, wg, wu, wd)


def _combine_body(x1_ref, rinfo_ref, mod_ref, y0_ref, y1_ref, *rest):
    o_ref = rest[-1]
    y = rinfo_ref[:, 4:5] * _unpack_halves(y0_ref[...]) + rinfo_ref[:, 5:6] * _unpack_halves(y1_ref[...])
    o_ref[...] = x1_ref[...] + mod_ref[:, 5 * D_MODEL:6 * D_MODEL] * y


def _combine(x1, rinfo, mod3, gathered, seq, part, out_so_far):
    t, d = x1.shape
    tf = min(COMBINE_TILE, seq)
    per_b = seq // tf
    nt = t // tf
    first = part * nt
    in_specs = [pl.BlockSpec((tf, d), lambda i: (i, 0)),
                pl.BlockSpec((tf, ROUTER_LANES), lambda i: (i, 0)),
                pl.BlockSpec((None, 1, mod3.shape[2]), lambda i: ((i + first) // per_b, 0, 0)),
                pl.BlockSpec((tf, d // 2), lambda i: (i, 0)),
                pl.BlockSpec((tf, d // 2), lambda i: (i + nt, 0))]
    args = [x1, rinfo, mod3, gathered, gathered]
    aliases = {}
    if out_so_far is not None:
        in_specs.append(pl.BlockSpec(memory_space=pl.ANY))
        args.append(out_so_far)
        aliases = {len(args) - 1: 0}
    return pl.pallas_call(
        _combine_body,
        out_shape=jax.ShapeDtypeStruct((t * MOE_PARTS, d), _F32),
        grid=(nt,),
        in_specs=in_specs,
        out_specs=pl.BlockSpec((tf, d), lambda i: (i + first, 0)),
        input_output_aliases=aliases,
        compiler_params=_params("arbitrary"),
        name="combine",
    )(*args)


def _rope_trig(positions):
    inv_freq = ROPE_THETA ** (-jnp.arange(0, ROPE_DIM, 2, dtype=_F32) / ROPE_DIM)
    ang = positions.reshape(-1).astype(_F32)[:, None] * inv_freq
    return jnp.concatenate([jnp.cos(ang), jnp.sin(ang)], axis=1)


def _layer(x, mod, positions, g_mix, g_ffn, w_in, w_bg, qg, kg, w_bd, w_bs, w_out, w_rg, b_rg, w_re, b_re,
           w_eg, w_eu, w_ed):
    batch, seq, d = x.shape
    t = batch * seq
    x2 = x.reshape(t, d)
    mod3 = mod.reshape(batch, 1, mod.shape[1])
    gw = GROUP_WIDTH

    lane = jnp.arange(gw)
    bd = jnp.where(lane[:, None] // HEAD_DIM == lane[None, :] // HEAD_DIM, 1.0 / HEAD_DIM, 0.0).astype(_BF16)
    ex = (jnp.arange(LANES)[:, None] == (lane[None, :] // HEAD_DIM) * LSE_SEG).astype(_BF16)
    ex = jnp.concatenate([ex, ex], axis=0)
    tri = (jnp.arange(ROW_TILE)[:, None] > jnp.arange(ROW_TILE)[None, :]).astype(_BF16)
    tile4 = lambda g: jnp.tile(g.astype(_F32), HEADS_PER_GROUP).reshape(1, gw)
    wr = jnp.zeros((d, ROUTER_LANES), _F32).at[:, :N_GROUPS].set(w_rg).at[:, N_GROUPS:N_GROUPS + N_EXPERTS].set(w_re)
    wrh = wr.astype(_BF16)
    wrl = (wr - wrh.astype(_F32)).astype(_BF16)
    br = jnp.zeros((1, ROUTER_LANES), _F32).at[0, :N_GROUPS].set(b_rg).at[0, N_GROUPS:N_GROUPS + N_EXPERTS].set(b_re)

    d0, d1, d2, sbp = _qkv(x2, mod3, g_mix.reshape(1, d), w_in.astype(_BF16), tile4(qg) * ATTN_SCALE, tile4(kg),
                           _rope_trig(positions), bd, seq)
    dil = [_dilated_group(a, dl, batch, seq) for a, (_, dl) in zip((d0, d1, d2), DIL_PATTERNS)]
    osb = _stick_breaking(sbp, batch, seq)

    merge_weights = (w_bg.astype(_BF16), w_bd.astype(_BF16), w_bs.astype(_BF16), w_out.astype(_BF16))
    tp = t // MOE_PARTS
    ch = EXPERT_CHUNK
    win = SC_INDEX_WINDOW
    n_chunks = -(-2 * tp // ch) + N_EXPERTS
    chunk_start = jnp.arange(n_chunks, dtype=jnp.int32) * ch
    expert_ids = jnp.arange(N_EXPERTS, dtype=jnp.int32)
    out = None
    for part in range(MOE_PARTS):
        x1, h2p, rinfo, er, cnt = _merge(
            x2, mod3, g_mix.reshape(1, d), g_ffn.reshape(1, d), [o for o, _ in dil], [l for _, l in dil], osb,
            *merge_weights, wrh, wrl, br, ex, tri, seq, part)

        counts = cnt[0, EXPERT_LANE0:EXPERT_LANE0 + N_EXPERTS].astype(jnp.int32)
        padded = (counts + ch - 1) // ch * ch
        pend = jnp.cumsum(padded)
        pstart = pend - padded
        chunk_e = jnp.minimum(jnp.sum((pend[None, :] <= chunk_start[:, None]).astype(jnp.int32), axis=1),
                              N_EXPERTS - 1)
        n_used = (pend[-1:] // ch).astype(jnp.int32)
        begin = chunk_start[:, None]
        inside = (pstart[None, :] <= begin) & (begin < pend[None, :])
        n_valid = jnp.sum(jnp.where(inside, jnp.clip(counts[None, :] - (begin - pstart[None, :]), 0, ch), 0), axis=1)

        dest = _dest(pstart, er)
        xs = _sc_scatter(h2p, dest[0].reshape(tp // win, win), dest[1].reshape(tp // win, win), n_chunks * ch)
        first = (chunk_e != jnp.concatenate([jnp.full((1,), -1, jnp.int32), chunk_e[:-1]])) & (chunk_start < pend[-1])
        seg_no = jnp.cumsum(first.astype(jnp.int32)) - 1
        seg = jnp.where(first, seg_no, -1 - seg_no)
        later = (expert_ids[None, :] > expert_ids[:, None]) & (padded > 0)[None, :]
        next_expert = jnp.min(jnp.where(later, expert_ids[None, :], N_EXPERTS), axis=1)
        next_expert = jnp.where(next_expert == N_EXPERTS, -1, next_expert)
        nxt = jnp.sum(jnp.where(chunk_e[:, None] == expert_ids[None, :], next_expert[None, :], 0), axis=1)
        ys = _experts(chunk_e, n_valid, n_used, seg.astype(jnp.int32), nxt.astype(jnp.int32), xs, w_eg, w_eu, w_ed)
        gathered = _sc_gather(ys, dest.reshape(2 * tp // win, win))
        out = _combine(x1, rinfo, mod3, gathered, seq, part, out)
    return out.reshape(batch, seq, d)


def kernel(x, c, positions, w_ada, b_ada, g_norm_mix, g_norm_ffn, w_in, w_branch_gate, q_norm_g, k_norm_g,
           w_branch_dil, w_branch_sb, w_out, w_router_group, b_router_group, w_router_expert, b_router_expert,
           w_expert_gate, w_expert_up, w_expert_down):
    for l in range(w_ada.shape[0]):
        mod = _ada(c, w_ada[l], b_ada[l])
        x = _layer(x, mod, positions, g_norm_mix[l], g_norm_ffn[l], w_in[l], w_branch_gate[l], q_norm_g[l],
                   k_norm_g[l], w_branch_dil[l], w_branch_sb[l], w_out[l], w_router_group[l], b_router_group[l],
                   w_router_expert[l], b_router_expert[l], w_expert_gate[l], w_expert_up[l], w_expert_down[l])
    return x
```

```python
import functools

import jax
import jax.numpy as jnp
from jax import lax
from jax.experimental import pallas as pl
from jax.experimental.pallas import tpu as pltpu
from jax.experimental.pallas import tpu_sc as plsc

D_MODEL = 1024
HEAD_DIM = 64
DIL_PATTERNS = ((128, 1), (512, 4), (2048, 16))
HEADS_PER_GROUP = 4
GROUP_WIDTH = HEADS_PER_GROUP * HEAD_DIM
N_DIL_GROUPS = len(DIL_PATTERNS)
DIL_WIDTH = N_DIL_GROUPS * GROUP_WIDTH
QKV_WIDTH = 3 * DIL_WIDTH + 3 * GROUP_WIDTH
WINDOW_KEYS = 128
ROPE_THETA = 500000.0
ROPE_DIM = HEAD_DIM // 4
N_GROUPS = 4
EXPERTS_PER_GROUP = 8
N_EXPERTS = N_GROUPS * EXPERTS_PER_GROUP
D_EXPERT = 512
RMS_EPS = 1e-6
ATTN_SCALE = HEAD_DIM ** -0.5

LANES = 128
ROUTER_LANES = LANES
EXPERT_LANE0 = N_GROUPS
LSE_SEG = LANES // HEADS_PER_GROUP
NEG_BIG = -1e30
SB_DEAD_LOG = -120.0
SB_HEAD_BLOCKS = 3
SB_CHAINS = 8

ROW_TILE = 512
QKV_TILE = 1024
QBLK = 128
DIL_QTILE = 1024
EXPERT_CHUNK = 512
COMBINE_TILE = 1024
DEST_TILE = 8192
SC_CORES = 2
SC_SUBCORES = 16
SC_WORKERS = SC_CORES * SC_SUBCORES
SC_INDEX_WINDOW = 128
MOE_PARTS = 2
VMEM_LIMIT = 48 * 1024 * 1024

_BF16 = jnp.bfloat16
_F32 = jnp.float32
_NT = (((1,), (1,)), ((), ()))


def _dot(a, b):
    return jnp.dot(a, b, preferred_element_type=_F32)


def _dot_nt(a, b):
    return lax.dot_general(a, b, _NT, preferred_element_type=_F32)


def _split(a):
    hi = a.astype(_BF16)
    lo = (a - hi.astype(_F32)).astype(_BF16)
    return hi, lo


def _dot3(a, b):
    ah, al = _split(a)
    bh, bl = _split(b)
    return _dot(ah, bh) + (_dot(ah, bl) + _dot(al, bh))


def _rms_mod(x, g, scale, shift):
    y = x * lax.rsqrt(jnp.mean(x * x, axis=-1, keepdims=True) + RMS_EPS)
    return y * g * (1.0 + scale) + shift


def _params(*sem):
    return pltpu.CompilerParams(dimension_semantics=sem, vmem_limit_bytes=VMEM_LIMIT)


def _ada_body(c_ref, w_ref, b_ref, o_ref):
    c = c_ref[...]
    o_ref[...] = _dot3(c * jax.nn.sigmoid(c), w_ref[...]) + b_ref[...]


def _ada(c, w_ada, b_ada):
    b, d = c.shape
    n = w_ada.shape[1]
    rows = -(-b // 16) * 16
    cp = jnp.zeros((rows, d), _F32).at[:b].set(c)
    nt = 1536
    out = pl.pallas_call(
        _ada_body,
        out_shape=jax.ShapeDtypeStruct((rows, n), _F32),
        grid=(n // nt,),
        in_specs=[pl.BlockSpec((rows, d), lambda j: (0, 0)),
                  pl.BlockSpec((d, nt), lambda j: (0, j)),
                  pl.BlockSpec((1, nt), lambda j: (0, j))],
        out_specs=pl.BlockSpec((rows, nt), lambda j: (0, j)),
        compiler_params=_params("arbitrary"),
        name="ada",
    )(cp, w_ada, b_ada.reshape(1, n))
    return out[:b]


def _qkv_body(x_ref, mod_ref, g_ref, w_ref, qg_ref, kg_ref, trig_ref, bd_ref,
              o0_ref, o1_ref, o2_ref, osb_ref, st_ref, acc_ref):
    d = D_MODEL
    tm = x_ref.shape[0]
    h = _rms_mod(x_ref[...], g_ref[...], mod_ref[:, d:2 * d], mod_ref[:, 0:d])
    hb = h.astype(_BF16)
    half = ROPE_DIM // 2
    in_head = lax.broadcasted_iota(jnp.int32, (tm, LANES), 1) & (HEAD_DIM - 1)
    freq = in_head & (half - 1)
    trig = jnp.concatenate([trig_ref[...], jnp.zeros((tm, LANES - ROPE_DIM), _F32)], axis=1)
    cos = jnp.take_along_axis(trig, freq, axis=1)
    sin = jnp.take_along_axis(trig, freq + half, axis=1)
    cc = jnp.where(in_head < ROPE_DIM, cos, 1.0)
    s1 = jnp.where(in_head < half, -sin, 0.0)
    s2 = jnp.where((in_head >= half) & (in_head < ROPE_DIM), sin, 0.0)
    cc, s1, s2 = [jnp.concatenate([a, a], axis=1) for a in (cc, s1, s2)]
    bd = bd_ref[...]
    gw = GROUP_WIDTH

    def normed_rotated(acc, gain):
        ms = _dot((acc * acc).astype(_BF16), bd)
        y = acc * lax.rsqrt(ms + RMS_EPS) * gain
        return y * cc + pltpu.roll(y, gw - ROPE_DIM // 2, 1) * s1 + pltpu.roll(y, ROPE_DIM // 2, 1) * s2

    def store(o_ref, dil, part, y):
        if dil == 1:
            o_ref[:, part * gw:(part + 1) * gw] = y.astype(_BF16)
            return
        for s in range(gw // LANES):
            st_ref[s] = y[:, s * LANES:(s + 1) * LANES]
        for r in range(dil):
            for s in range(gw // LANES):
                col0 = (3 * r + part) * gw + s * LANES
                o_ref[:, col0:col0 + LANES] = st_ref[s, pl.ds(r, tm // dil, stride=dil), :].astype(_BF16)

    def project(col0):
        return _dot(hb, w_ref[:, col0:col0 + gw])

    outs = (o0_ref, o1_ref, o2_ref)
    dils = [dl for _, dl in DIL_PATTERNS]
    normed = [(g, part, part * DIL_WIDTH + g * gw) for g in range(N_DIL_GROUPS) for part in (0, 1)]
    for n, (_, _, col0) in enumerate(normed):
        acc_ref[n] = project(col0)
    plain = [("v", g) for g in range(N_DIL_GROUPS)] + [("sb", part) for part in range(3)]
    for n, (kind, j) in enumerate(plain):
        if kind == "v":
            store(outs[j], dils[j], 2, project(2 * DIL_WIDTH + j * gw))
        else:
            acc = project(3 * DIL_WIDTH + j * gw)
            osb_ref[:, j * gw:(j + 1) * gw] = (acc * ATTN_SCALE if j == 0 else acc).astype(_BF16)
        g, part, _ = normed[n]
        gain = qg_ref[...] if part == 0 else kg_ref[...]
        store(outs[g], dils[g], part, normed_rotated(acc_ref[n], gain))


def _qkv(x2, mod3, g_mix, w_in, qg, kg, trig, bd, seq):
    t, d = x2.shape
    tm = min(QKV_TILE, seq)
    per_b = seq // tm
    row = lambda i: (i, 0)
    const = lambda i: (0, 0)
    width = 3 * GROUP_WIDTH
    dils = [dl for _, dl in DIL_PATTERNS] + [1]
    return pl.pallas_call(
        _qkv_body,
        out_shape=[jax.ShapeDtypeStruct((t // dl, dl * width), _BF16) for dl in dils],
        grid=(t // tm,),
        in_specs=[pl.BlockSpec((tm, d), row),
                  pl.BlockSpec((None, 1, mod3.shape[2]), lambda i: (i // per_b, 0, 0)),
                  pl.BlockSpec((1, d), const),
                  pl.BlockSpec(w_in.shape, const),
                  pl.BlockSpec((1, GROUP_WIDTH), const),
                  pl.BlockSpec((1, GROUP_WIDTH), const),
                  pl.BlockSpec((tm, ROPE_DIM), row),
                  pl.BlockSpec(bd.shape, const)],
        out_specs=[pl.BlockSpec((tm // dl, dl * width), row) for dl in dils],
        scratch_shapes=[pltpu.VMEM((GROUP_WIDTH // LANES, tm, LANES), _F32),
                        pltpu.VMEM((2 * N_DIL_GROUPS, tm, GROUP_WIDTH), _F32)],
        compiler_params=_params("arbitrary"),
        name="qkv",
    )(x2, mod3, g_mix, w_in, qg, kg, trig, bd)


def _dil_body(prev_ref, cur_ref, o_ref, lse_ref, kf_ref, vf_ref):
    tq = cur_ref.shape[0]
    gw = GROUP_WIDTH
    n_res = cur_ref.shape[1] // (3 * gw)
    first = pl.program_id(2) == 0
    for r in range(n_res):
        for part, full_ref in ((1, kf_ref), (2, vf_ref)):
            lanes = slice((3 * r + part) * gw, (3 * r + part + 1) * gw)
            full_ref[r, 0:QBLK, :] = prev_ref[:, lanes]
            full_ref[r, QBLK:, :] = cur_ref[:, lanes]
    nh = HEADS_PER_GROUP
    row = lax.broadcasted_iota(jnp.int32, (nh * QBLK, 2 * QBLK), 0) & (QBLK - 1)
    col = lax.broadcasted_iota(jnp.int32, (nh * QBLK, 2 * QBLK), 1)
    band = (col >= row) & (col <= row + WINDOW_KEYS)
    lane = lax.broadcasted_iota(jnp.int32, (1, GROUP_WIDTH), 1)
    slane = lax.broadcasted_iota(jnp.int32, (1, LANES), 1)
    head_masks = [(lane >= h * HEAD_DIM) & (lane < (h + 1) * HEAD_DIM) for h in range(nh)]
    subs = [(r, j) for r in range(n_res) for j in range(tq // QBLK)]
    rows = [slice(j * QBLK, (j + 1) * QBLK) for _, j in subs]
    window = [slice(j * QBLK, (j + 2) * QBLK) for _, j in subs]
    n = range(len(subs))

    def stack(qj):
        return jnp.concatenate([jnp.where(hm, qj, jnp.zeros_like(qj)) for hm in head_masks], axis=0)

    valid = [band & ((col >= QBLK) | jnp.logical_not(first)) if j == 0 else band for _, j in subs]
    q = [cur_ref[rows[i], 3 * r * gw:(3 * r + 1) * gw] for i, (r, _) in enumerate(subs)]
    s = [jnp.where(valid[i], _dot_nt(stack(q[i]), kf_ref[subs[i][0], window[i], :]), NEG_BIG) for i in n]
    m = [jnp.max(s[i], axis=1, keepdims=True) for i in n]
    p = [jnp.exp(s[i] - m[i]) for i in n]
    l = [jnp.sum(p[i], axis=1, keepdims=True) for i in n]
    o_all = [_dot(p[i].astype(_BF16), vf_ref[subs[i][0], window[i], :]) / l[i] for i in n]
    for i, (r, _) in enumerate(subs):
        lse_all = m[i] + jnp.log(l[i])
        o_acc = jnp.zeros((QBLK, gw), _F32)
        lse_t = jnp.zeros((QBLK, LANES), _F32)
        for h, hm in enumerate(head_masks):
            o_acc = jnp.where(hm, o_all[i][h * QBLK:(h + 1) * QBLK, :], o_acc)
            sm = (slane >= h * LSE_SEG) & (slane < (h + 1) * LSE_SEG)
            lse_t = jnp.where(sm, lse_all[h * QBLK:(h + 1) * QBLK, :], lse_t)
        o_ref[rows[i], r * gw:(r + 1) * gw] = o_acc.astype(_BF16)
        lse_ref[rows[i], r * LANES:(r + 1) * LANES] = lse_t


def _dilated_group(view2, dil, batch, seq):
    sd = seq // dil
    tq = min(DIL_QTILE, sd)
    n_res = min(dil, DIL_QTILE // tq)
    per = tq // QBLK
    gw = GROUP_WIDTH
    view = view2.reshape(batch, sd, dil * 3 * gw)
    o, lse = pl.pallas_call(
        _dil_body,
        out_shape=[jax.ShapeDtypeStruct((batch, sd, dil * gw), _BF16),
                   jax.ShapeDtypeStruct((batch, sd, dil * LANES), _F32)],
        grid=(batch, dil // n_res, sd // tq),
        in_specs=[pl.BlockSpec((None, QBLK, n_res * 3 * gw), lambda b, r, i: (b, jnp.maximum(i * per - 1, 0), r)),
                  pl.BlockSpec((None, tq, n_res * 3 * gw), lambda b, r, i: (b, i, r))],
        out_specs=[pl.BlockSpec((None, tq, n_res * gw), lambda b, r, i: (b, i, r)),
                   pl.BlockSpec((None, tq, n_res * LANES), lambda b, r, i: (b, i, r))],
        scratch_shapes=[pltpu.VMEM((n_res, tq + QBLK, gw), _BF16), pltpu.VMEM((n_res, tq + QBLK, gw), _BF16)],
        compiler_params=_params("arbitrary", "arbitrary", "arbitrary"),
        name=f"dil{dil}",
    )(view, view)
    return o.reshape(batch * sd, dil * gw), lse.reshape(batch * sd, dil * LANES)


def _sb_body(q_ref, k_ref, v_ref, o_ref, carry_ref, acc_ref, qs_ref):
    step_id = pl.program_id(1)
    chains = range(SB_CHAINS)
    blk = [step_id * SB_CHAINS + c for c in chains]
    nh = HEADS_PER_GROUP
    row = lax.broadcasted_iota(jnp.int32, (nh * QBLK, QBLK), 0) & (QBLK - 1)
    col = lax.broadcasted_iota(jnp.int32, (nh * QBLK, QBLK), 1)
    strict = col < row
    ur = lax.broadcasted_iota(jnp.int32, (2 * QBLK, QBLK), 0) & (QBLK - 1)
    uc = lax.broadcasted_iota(jnp.int32, (2 * QBLK, QBLK), 1)
    u = jnp.where(ur > uc, 1.0, 0.0).astype(_BF16)
    lane = lax.broadcasted_iota(jnp.int32, (1, GROUP_WIDTH), 1)
    head_masks = [(lane >= h * HEAD_DIM) & (lane < (h + 1) * HEAD_DIM) for h in range(nh)]
    for c in chains:
        q = q_ref[c * QBLK:(c + 1) * QBLK, :]
        qs_ref[c] = jnp.concatenate([jnp.where(hm, q, jnp.zeros_like(q)) for hm in head_masks], axis=0)

    def softplus(z):
        return jnp.maximum(z, 0.0) + jnp.log(1.0 + jnp.exp(-jnp.abs(z)))

    def later_keys(log_1m):
        hi, lo = _split(log_1m)
        return _dot(jnp.concatenate([hi, lo], axis=1), u)

    nb = SB_HEAD_BLOCKS
    cols = [slice(j * QBLK, (j + 1) * QBLK) for j in range(nb)]
    kbs = [[blk[c] - (nb - 1) + j for j in range(nb)] for c in chains]
    starts = [[pl.multiple_of(jnp.maximum(kb, 0) * QBLK, QBLK) for kb in kbs[c]] for c in chains]
    keep = [[strict if j == nb - 1 else (kbs[c][j] >= 0) for j in range(nb)] for c in chains]
    z = [_dot_nt(qs_ref[c], jnp.concatenate([k_ref[pl.ds(s, QBLK), :] for s in starts[c]], axis=0)) for c in chains]
    sp = [softplus(z[c]) for c in chains]
    log_1m = [[jnp.where(keep[c][j], -sp[c][:, cols[j]], 0.0) for j in range(nb)] for c in chains]
    totals = [[jnp.sum(l, axis=1, keepdims=True) for l in log_1m[c]] for c in chains]
    later = [[later_keys(log_1m[c][j]) for j in range(nb)] for c in chains]
    for c in chains:
        a_blocks = []
        after = jnp.zeros_like(totals[c][0])
        for j in reversed(range(nb)):
            a = jnp.exp((z[c][:, cols[j]] - sp[c][:, cols[j]]) + later[c][j] + after)
            a_blocks.insert(0, jnp.where(keep[c][j], a, 0.0).astype(_BF16))
            after = after + totals[c][j]
        acc_ref[c] = _dot(jnp.concatenate(a_blocks, axis=1),
                          jnp.concatenate([v_ref[pl.ds(s, QBLK), :] for s in starts[c]], axis=0))
        carry_ref[c] = after

    for c in chains:
        def tile(kb, c=c):
            start = pl.multiple_of(kb * QBLK, QBLK)
            z = _dot_nt(qs_ref[c], k_ref[pl.ds(start, QBLK), :])
            sp = softplus(z)
            log_1m = -sp
            a = jnp.exp((z - sp) + later_keys(log_1m) + carry_ref[c])
            acc_ref[c] += _dot(a.astype(_BF16), v_ref[pl.ds(start, QBLK), :])
            carry = carry_ref[c] + jnp.sum(log_1m, axis=1, keepdims=True)
            carry_ref[c] = carry
            return jnp.max(carry)

        def cond(st):
            return (st[0] >= 0) & (st[1] > SB_DEAD_LOG)

        def step(st, tile=tile):
            return st[0] - 1, tile(st[0])

        lax.while_loop(cond, step, (blk[c] - nb, jnp.max(carry_ref[c])))
        out = jnp.zeros((QBLK, GROUP_WIDTH), _F32)
        for h, hm in enumerate(head_masks):
            out = jnp.where(hm, acc_ref[c, h * QBLK:(h + 1) * QBLK, :], out)
        o_ref[c * QBLK:(c + 1) * QBLK, :] = out.astype(_BF16)


def _stick_breaking(arr, batch, seq):
    gw = GROUP_WIDTH
    qt = SB_CHAINS * QBLK
    view = arr.reshape(batch, seq, 3 * gw)
    o = pl.pallas_call(
        _sb_body,
        out_shape=jax.ShapeDtypeStruct((batch, seq, gw), _BF16),
        grid=(batch, seq // qt),
        in_specs=[pl.BlockSpec((None, qt, gw), lambda b, i: (b, i, 0)),
                  pl.BlockSpec((None, seq, gw), lambda b, i: (b, 0, 1)),
                  pl.BlockSpec((None, seq, gw), lambda b, i: (b, 0, 2))],
        out_specs=pl.BlockSpec((None, qt, gw), lambda b, i: (b, i, 0)),
        scratch_shapes=[pltpu.VMEM((SB_CHAINS, HEADS_PER_GROUP * QBLK, 1), _F32),
                        pltpu.VMEM((SB_CHAINS, HEADS_PER_GROUP * QBLK, gw), _F32),
                        pltpu.VMEM((SB_CHAINS, HEADS_PER_GROUP * QBLK, gw), _BF16)],
        compiler_params=_params("arbitrary", "arbitrary"),
        name="sb",
    )(view, view, view)
    return o.reshape(batch * seq, gw)


def _merge_body(x_ref, mod_ref, g1_ref, g2_ref, o0_ref, o1_ref, o2_ref, l0_ref, l1_ref, l2_ref, osb_ref,
                wbg_ref, wbd_ref, wbs_ref, wout_ref, wrh_ref, wrl_ref, br_ref, ex_ref, tri_ref,
                x1_ref, h2p_ref, rinfo_ref, er_ref, cnt_out_ref, cnt_ref, os1_ref, os2_ref, ls1_ref, ls2_ref):
    d = D_MODEL
    tm = x_ref.shape[0]
    x = x_ref[...]
    hb = _rms_mod(x, g1_ref[...], mod_ref[:, d:2 * d], mod_ref[:, 0:d]).astype(_BF16)

    def natural(ref, st_ref, dil):
        if dil == 1:
            return ref[...].astype(_F32)
        slabs = st_ref.shape[0]
        for r in range(dil):
            for s in range(slabs):
                col0 = (r * slabs + s) * LANES
                st_ref[s, pl.ds(r, tm // dil, stride=dil), :] = ref[:, col0:col0 + LANES].astype(_F32)
        return jnp.concatenate([st_ref[s] for s in range(slabs)], axis=1)

    dils = [dl for _, dl in DIL_PATTERNS]
    o_nat = [natural(r, s, dl) for r, s, dl in zip((o0_ref, o1_ref, o2_ref), (None, os1_ref, os2_ref), dils)]
    l0, l1, l2 = [natural(r, s, dl) for r, s, dl in zip((l0_ref, l1_ref, l2_ref), (None, ls1_ref, ls2_ref), dils)]

    lmax = jnp.maximum(jnp.maximum(l0, l1), l2)
    e0, e1, e2 = jnp.exp(l0 - lmax), jnp.exp(l1 - lmax), jnp.exp(l2 - lmax)
    inv = 1.0 / (e0 + e1 + e2)
    ex = ex_ref[...]

    def widen(w):
        hi, lo = _split(w)
        return _dot(jnp.concatenate([hi, lo], axis=1), ex)

    w_groups = [widen(e * inv) for e in (e0, e1, e2)]
    gate_dil = jax.nn.sigmoid(_dot(hb, wbg_ref[:, :d]))
    o_dil = w_groups[0] * o_nat[0] + w_groups[1] * o_nat[1] + w_groups[2] * o_nat[2]
    branch_dil = _dot(o_dil.astype(_BF16), wbd_ref[...])
    branch_sb = _dot(osb_ref[...], wbs_ref[...])
    gate_sb = jax.nn.sigmoid(_dot(hb, wbg_ref[:, d:]))
    merged = gate_dil * branch_dil + gate_sb * branch_sb
    x1 = x + mod_ref[:, 2 * d:3 * d] * _dot(merged.astype(_BF16), wout_ref[...])
    x1_ref[...] = x1

    h2 = _rms_mod(x1, g2_ref[...], mod_ref[:, 4 * d:5 * d], mod_ref[:, 3 * d:4 * d])
    h2p_ref[...] = _pack_halves(h2)

    hh, hl = _split(h2)
    logits = _dot(hh, wrh_ref[...]) + (_dot(hl, wrh_ref[...]) + _dot(hh, wrl_ref[...])) + br_ref[...]
    lane = lax.broadcasted_iota(jnp.int32, (tm, ROUTER_LANES), 1).astype(_F32)
    far = float(ROUTER_LANES)

    def top(vals):
        m = jnp.max(vals, axis=1, keepdims=True)
        return m, jnp.min(jnp.where(vals == m, lane, far), axis=1, keepdims=True)

    is_group = lane < N_GROUPS
    mg, gsel = top(jnp.where(is_group, logits, NEG_BIG))
    pg_top = 1.0 / jnp.sum(jnp.where(is_group, jnp.exp(logits - mg), 0.0), axis=1, keepdims=True)
    lane0 = EXPERT_LANE0 + EXPERTS_PER_GROUP * gsel
    le = jnp.where((lane >= lane0) & (lane < lane0 + EXPERTS_PER_GROUP), logits, NEG_BIG)
    m1, i1 = top(le)
    m2, i2 = top(jnp.where(lane == i1, NEG_BIG, le))
    t2 = jnp.exp(m2 - m1)
    w0 = pg_top / (1.0 + t2)
    w1 = pg_top * t2 / (1.0 + t2)

    @pl.when(pl.program_id(0) == 0)
    def _():
        cnt_ref[...] = jnp.zeros_like(cnt_ref)

    sel0, sel1 = lane == i1, lane == i2
    onehot = jnp.where(sel0 | sel1, 1.0, 0.0)
    before = _dot(tri_ref[...], onehot.astype(_BF16)) + cnt_ref[0:1, :]
    r0 = jnp.sum(jnp.where(sel0, before, 0.0), axis=1, keepdims=True)
    r1 = jnp.sum(jnp.where(sel1, before, 0.0), axis=1, keepdims=True)
    cnt_ref[...] += jnp.sum(onehot, axis=0, keepdims=True)
    cnt_out_ref[...] = cnt_ref[...]
    cols = (i1 - EXPERT_LANE0, i2 - EXPERT_LANE0, r0, r1, w0, w1)
    rinfo = jnp.zeros((tm, ROUTER_LANES), _F32)
    for c, v in enumerate(cols):
        rinfo = jnp.where(lane == float(c), v, rinfo)
    rinfo_ref[...] = rinfo
    er_ref[...] = jnp.transpose(rinfo)[0:8, :].astype(jnp.int32)


def _merge(x2, mod3, g1, g2, outs, lses, osb, wbg, wbd, wbs, wout, wrh, wrl, br, ex, tri, seq, part):
    d = x2.shape[1]
    tm = ROW_TILE
    per_b = seq // tm
    t = x2.shape[0] // MOE_PARTS
    first = part * (t // tm)
    src = lambda i: (i + first, 0)
    row = lambda i: (i, 0)
    const = lambda i: (0, 0)
    full = lambda a: pl.BlockSpec(a.shape, const)
    gw = GROUP_WIDTH
    dils = [dl for _, dl in DIL_PATTERNS]
    return pl.pallas_call(
        _merge_body,
        out_shape=[jax.ShapeDtypeStruct((t, d), _F32),
                   jax.ShapeDtypeStruct((t, d // 2), jnp.uint32),
                   jax.ShapeDtypeStruct((t, ROUTER_LANES), _F32),
                   jax.ShapeDtypeStruct((8, t), jnp.int32),
                   jax.ShapeDtypeStruct((8, ROUTER_LANES), _F32)],
        grid=(t // tm,),
        in_specs=[pl.BlockSpec((tm, d), src),
                  pl.BlockSpec((None, 1, mod3.shape[2]), lambda i: ((i + first) // per_b, 0, 0)),
                  full(g1), full(g2)]
                 + [pl.BlockSpec((tm // dl, dl * gw), src) for dl in dils]
                 + [pl.BlockSpec((tm // dl, dl * LANES), src) for dl in dils]
                 + [pl.BlockSpec((tm, gw), src)]
                 + [full(a) for a in (wbg, wbd, wbs, wout, wrh, wrl, br, ex, tri)],
        out_specs=[pl.BlockSpec((tm, d), row),
                   pl.BlockSpec((tm, d // 2), row),
                   pl.BlockSpec((tm, ROUTER_LANES), row),
                   pl.BlockSpec((8, tm), lambda i: (0, i)),
                   pl.BlockSpec((8, ROUTER_LANES), const)],
        scratch_shapes=[pltpu.VMEM((8, ROUTER_LANES), _F32),
                        pltpu.VMEM((gw // LANES, tm, LANES), _F32), pltpu.VMEM((gw // LANES, tm, LANES), _F32),
                        pltpu.VMEM((1, tm, LANES), _F32), pltpu.VMEM((1, tm, LANES), _F32)],
        compiler_params=_params("arbitrary"),
        name="merge",
    )(x2, mod3, g1, g2, *outs, *lses, osb, wbg, wbd, wbs, wout, wrh, wrl, br, ex, tri)


def _dest_body(ps_ref, er_ref, d_ref):
    e = er_ref[0:2, :]
    start = jnp.zeros_like(e)
    for x in range(N_EXPERTS):
        start = jnp.where(e == x, ps_ref[x], start)
    d_ref[...] = start + er_ref[2:4, :]


def _dest(pstart, er):
    t = er.shape[1]
    tw = min(DEST_TILE, t)
    return pl.pallas_call(
        _dest_body,
        out_shape=jax.ShapeDtypeStruct((2, t), jnp.int32),
        grid_spec=pltpu.PrefetchScalarGridSpec(
            num_scalar_prefetch=1,
            grid=(t // tw,),
            in_specs=[pl.BlockSpec((8, tw), lambda i, ps: (0, i))],
            out_specs=pl.BlockSpec((2, tw), lambda i, ps: (0, i))),
        compiler_params=_params("arbitrary"),
        name="dest",
    )(pstart, er)


def _sc_mesh():
    return plsc.VectorSubcoreMesh(core_axis_name="core", subcore_axis_name="subcore",
                                  num_cores=SC_CORES, num_subcores=SC_SUBCORES)


def _sc_worker():
    return lax.axis_index("subcore") * SC_CORES + lax.axis_index("core")


def _sc_scatter(x, idx0, idx1, n_slots):
    chunks = idx0.shape[0]
    per = chunks // SC_WORKERS
    win = idx0.shape[1]

    @functools.partial(
        pl.kernel, mesh=_sc_mesh(), out_type=jax.ShapeDtypeStruct((n_slots, x.shape[1]), x.dtype),
        scratch_types=[pltpu.VMEM((1, win), jnp.int32), pltpu.VMEM((1, win), jnp.int32),
                       pltpu.VMEM((win, x.shape[1]), x.dtype), pltpu.SemaphoreType.DMA],
        name="sc_scatter")
    def run(x_hbm, i0_hbm, i1_hbm, o_hbm, i0_v, i1_v, rows_v, sem):
        wid = _sc_worker()

        @pl.loop(0, per)
        def _(j):
            c = wid * per + j
            pltpu.sync_copy(i0_hbm.at[pl.ds(c, 1)], i0_v)
            pltpu.sync_copy(i1_hbm.at[pl.ds(c, 1)], i1_v)
            pltpu.sync_copy(x_hbm.at[pl.ds(c * win, win)], rows_v)
            first = pltpu.async_copy(rows_v, o_hbm.at[i0_v.at[0]], sem)
            second = pltpu.async_copy(rows_v, o_hbm.at[i1_v.at[0]], sem)
            first.wait()
            second.wait()

    return run(x, idx0, idx1)


def _sc_gather(table, idx):
    chunks, win = idx.shape
    per = chunks // SC_WORKERS

    @functools.partial(
        pl.kernel, mesh=_sc_mesh(), out_type=jax.ShapeDtypeStruct((chunks * win, table.shape[1]), table.dtype),
        scratch_types=[pltpu.VMEM((1, win), jnp.int32), pltpu.VMEM((win, table.shape[1]), table.dtype),
                       pltpu.SemaphoreType.DMA],
        name="sc_gather")
    def run(t_hbm, i_hbm, o_hbm, i_v, rows_v, sem):
        wid = _sc_worker()

        @pl.loop(0, per)
        def _(j):
            c = wid * per + j
            pltpu.sync_copy(i_hbm.at[pl.ds(c, 1)], i_v)
            pltpu.async_copy(t_hbm.at[i_v.at[0]], rows_v, sem).wait()
            pltpu.sync_copy(rows_v, o_hbm.at[pl.ds(c * win, win)])

    return run(table, idx)


def _pack_halves(a):
    h = a.shape[1] // 2
    lo = lax.bitcast_convert_type(a[:, :h].astype(_BF16).astype(_F32), jnp.uint32) >> 16
    hi = lax.bitcast_convert_type(a[:, h:].astype(_BF16).astype(_F32), jnp.uint32) & jnp.uint32(0xFFFF0000)
    return lo | hi


def _unpack_halves(w):
    return jnp.concatenate(
        [lax.bitcast_convert_type(w << 16, _F32), lax.bitcast_convert_type(w & jnp.uint32(0xFFFF0000), _F32)], axis=1)


def _experts_body(ce_ref, nv_ref, nu_ref, seg_ref, nxt_ref, xs_ref, wg_ref, wu_ref, wd_ref, ys_ref,
                  wgb_ref, wub_ref, wdb_ref, wgf_ref, wuf_ref, wdf_ref, sem):
    c = pl.program_id(0)

    def fetch(expert, slot):
        return [pltpu.make_async_copy(src.at[expert], dst.at[slot], sem.at[slot])
                for src, dst in ((wg_ref, wgf_ref), (wu_ref, wuf_ref), (wd_ref, wdf_ref))]

    @pl.when(c < nu_ref[0])
    def _():
        @pl.when(seg_ref[c] >= 0)
        def _():
            slot = seg_ref[c] & 1

            @pl.when(c == 0)
            def _():
                for copy in fetch(ce_ref[0], 0):
                    copy.start()

            for copy in fetch(ce_ref[c], slot):
                copy.wait()
            wgb_ref[...] = wgf_ref[slot].astype(_BF16)
            wub_ref[...] = wuf_ref[slot].astype(_BF16)
            wdb_ref[...] = wdf_ref[slot].astype(_BF16)

            @pl.when(nxt_ref[c] >= 0)
            def _():
                for copy in fetch(nxt_ref[c], 1 - slot):
                    copy.start()

        row = lax.broadcasted_iota(jnp.int32, xs_ref.shape, 0)
        x = _unpack_halves(jnp.where(row < nv_ref[c], xs_ref[...], jnp.uint32(0))).astype(_BF16)
        half = x.shape[0] // 2
        ups = [(_dot(x[r:r + half], wgb_ref[...]), _dot(x[r:r + half], wub_ref[...])) for r in (0, half)]
        for (g, u), r in zip(ups, (0, half)):
            hmid = (g * jax.nn.sigmoid(g)) * u
            ys_ref[r:r + half, :] = _pack_halves(_dot(hmid.astype(_BF16), wdb_ref[...]))


def _experts(chunk_e, n_valid, n_used, seg, nxt, xs, wg, wu, wd):
    n_slots, w = xs.shape
    ch = EXPERT_CHUNK
    d, de = wg.shape[1], wg.shape[2]
    slot = lambda c, ce, nv, nu, sg, nx: (jnp.minimum(c, nu[0] - 1), 0)
    hbm = pl.BlockSpec(memory_space=pl.ANY)
    return pl.pallas_call(
        _experts_body,
        out_shape=jax.ShapeDtypeStruct((n_slots, d // 2), jnp.uint32),
        grid_spec=pltpu.PrefetchScalarGridSpec(
            num_scalar_prefetch=5,
            grid=(n_slots // ch,),
            in_specs=[pl.BlockSpec((ch, w), slot), hbm, hbm, hbm],
            out_specs=pl.BlockSpec((ch, d // 2), slot),
            scratch_shapes=[pltpu.VMEM((d, de), _BF16), pltpu.VMEM((d, de), _BF16), pltpu.VMEM((de, d), _BF16),
                            pltpu.VMEM((2, d, de), _F32), pltpu.VMEM((2, d, de), _F32), pltpu.VMEM((2, de, d), _F32),
                            pltpu.SemaphoreType.DMA((2,))]),
        compiler_params=_params("arbitrary"),
        name="experts",
    )(chunk_e, n_valid, n_used, seg, nxt, xs, wg, wu, wd)


def _combine_body(x1_ref, rinfo_ref, mod_ref, y0_ref, y1_ref, *rest):
    o_ref = rest[-1]
    y = rinfo_ref[:, 4:5] * _unpack_halves(y0_ref[...]) + rinfo_ref[:, 5:6] * _unpack_halves(y1_ref[...])
    o_ref[...] = x1_ref[...] + mod_ref[:, 5 * D_MODEL:6 * D_MODEL] * y


def _combine(x1, rinfo, mod3, gathered, seq, part, out_so_far):
    t, d = x1.shape
    tf = min(COMBINE_TILE, seq)
    per_b = seq // tf
    nt = t // tf
    first = part * nt
    in_specs = [pl.BlockSpec((tf, d), lambda i: (i, 0)),
                pl.BlockSpec((tf, ROUTER_LANES), lambda i: (i, 0)),
                pl.BlockSpec((None, 1, mod3.shape[2]), lambda i: ((i + first) // per_b, 0, 0)),
                pl.BlockSpec((tf, d // 2), lambda i: (i, 0)),
                pl.BlockSpec((tf, d // 2), lambda i: (i + nt, 0))]
    args = [x1, rinfo, mod3, gathered, gathered]
    aliases = {}
    if out_so_far is not None:
        in_specs.append(pl.BlockSpec(memory_space=pl.ANY))
        args.append(out_so_far)
        aliases = {len(args) - 1: 0}
    return pl.pallas_call(
        _combine_body,
        out_shape=jax.ShapeDtypeStruct((t * MOE_PARTS, d), _F32),
        grid=(nt,),
        in_specs=in_specs,
        out_specs=pl.BlockSpec((tf, d), lambda i: (i + first, 0)),
        input_output_aliases=aliases,
        compiler_params=_params("arbitrary"),
        name="combine",
    )(*args)


def _rope_trig(positions):
    inv_freq = ROPE_THETA ** (-jnp.arange(0, ROPE_DIM, 2, dtype=_F32) / ROPE_DIM)
    ang = positions.reshape(-1).astype(_F32)[:, None] * inv_freq
    return jnp.concatenate([jnp.cos(ang), jnp.sin(ang)], axis=1)


def _layer(x, mod, positions, g_mix, g_ffn, w_in, w_bg, qg, kg, w_bd, w_bs, w_out, w_rg, b_rg, w_re, b_re,
           w_eg, w_eu, w_ed):
    batch, seq, d = x.shape
    t = batch * seq
    x2 = x.reshape(t, d)
    mod3 = mod.reshape(batch, 1, mod.shape[1])
    gw = GROUP_WIDTH

    lane = jnp.arange(gw)
    bd = jnp.where(lane[:, None] // HEAD_DIM == lane[None, :] // HEAD_DIM, 1.0 / HEAD_DIM, 0.0).astype(_BF16)
    ex = (jnp.arange(LANES)[:, None] == (lane[None, :] // HEAD_DIM) * LSE_SEG).astype(_BF16)
    ex = jnp.concatenate([ex, ex], axis=0)
    tri = (jnp.arange(ROW_TILE)[:, None] > jnp.arange(ROW_TILE)[None, :]).astype(_BF16)
    tile4 = lambda g: jnp.tile(g.astype(_F32), HEADS_PER_GROUP).reshape(1, gw)
    wr = jnp.zeros((d, ROUTER_LANES), _F32).at[:, :N_GROUPS].set(w_rg).at[:, N_GROUPS:N_GROUPS + N_EXPERTS].set(w_re)
    wrh = wr.astype(_BF16)
    wrl = (wr - wrh.astype(_F32)).astype(_BF16)
    br = jnp.zeros((1, ROUTER_LANES), _F32).at[0, :N_GROUPS].set(b_rg).at[0, N_GROUPS:N_GROUPS + N_EXPERTS].set(b_re)

    d0, d1, d2, sbp = _qkv(x2, mod3, g_mix.reshape(1, d), w_in.astype(_BF16), tile4(qg) * ATTN_SCALE, tile4(kg),
                           _rope_trig(positions), bd, seq)
    dil = [_dilated_group(a, dl, batch, seq) for a, (_, dl) in zip((d0, d1, d2), DIL_PATTERNS)]
    osb = _stick_breaking(sbp, batch, seq)

    merge_weights = (w_bg.astype(_BF16), w_bd.astype(_BF16), w_bs.astype(_BF16), w_out.astype(_BF16))
    tp = t // MOE_PARTS
    ch = EXPERT_CHUNK
    win = SC_INDEX_WINDOW
    n_chunks = -(-2 * tp // ch) + N_EXPERTS
    chunk_start = jnp.arange(n_chunks, dtype=jnp.int32) * ch
    expert_ids = jnp.arange(N_EXPERTS, dtype=jnp.int32)
    out = None
    for part in range(MOE_PARTS):
        x1, h2p, rinfo, er, cnt = _merge(
            x2, mod3, g_mix.reshape(1, d), g_ffn.reshape(1, d), [o for o, _ in dil], [l for _, l in dil], osb,
            *merge_weights, wrh, wrl, br, ex, tri, seq, part)

        counts = cnt[0, EXPERT_LANE0:EXPERT_LANE0 + N_EXPERTS].astype(jnp.int32)
        padded = (counts + ch - 1) // ch * ch
        pend = jnp.cumsum(padded)
        pstart = pend - padded
        chunk_e = jnp.minimum(jnp.sum((pend[None, :] <= chunk_start[:, None]).astype(jnp.int32), axis=1),
                              N_EXPERTS - 1)
        n_used = (pend[-1:] // ch).astype(jnp.int32)
        begin = chunk_start[:, None]
        inside = (pstart[None, :] <= begin) & (begin < pend[None, :])
        n_valid = jnp.sum(jnp.where(inside, jnp.clip(counts[None, :] - (begin - pstart[None, :]), 0, ch), 0), axis=1)

        dest = _dest(pstart, er)
        xs = _sc_scatter(h2p, dest[0].reshape(tp // win, win), dest[1].reshape(tp // win, win), n_chunks * ch)
        first = (chunk_e != jnp.concatenate([jnp.full((1,), -1, jnp.int32), chunk_e[:-1]])) & (chunk_start < pend[-1])
        seg_no = jnp.cumsum(first.astype(jnp.int32)) - 1
        seg = jnp.where(first, seg_no, -1 - seg_no)
        later = (expert_ids[None, :] > expert_ids[:, None]) & (padded > 0)[None, :]
        next_expert = jnp.min(jnp.where(later, expert_ids[None, :], N_EXPERTS), axis=1)
        next_expert = jnp.where(next_expert == N_EXPERTS, -1, next_expert)
        nxt = jnp.sum(jnp.where(chunk_e[:, None] == expert_ids[None, :], next_expert[None, :], 0), axis=1)
        ys = _experts(chunk_e, n_valid, n_used, seg.astype(jnp.int32), nxt.astype(jnp.int32), xs, w_eg, w_eu, w_ed)
        gathered = _sc_gather(ys, dest.reshape(2 * tp // win, win))
        out = _combine(x1, rinfo, mod3, gathered, seq, part, out)
    return out.reshape(batch, seq, d)


def kernel(x, c, positions, w_ada, b_ada, g_norm_mix, g_norm_ffn, w_in, w_branch_gate, q_norm_g, k_norm_g,
           w_branch_dil, w_branch_sb, w_out, w_router_group, b_router_group, w_router_expert, b_router_expert,
           w_expert_gate, w_expert_up, w_expert_down):
    for l in range(w_ada.shape[0]):
        mod = _ada(c, w_ada[l], b_ada[l])
        x = _layer(x, mod, positions, g_norm_mix[l], g_norm_ffn[l], w_in[l], w_branch_gate[l], q_norm_g[l],
                   k_norm_g[l], w_branch_dil[l], w_branch_sb[l], w_out[l], w_router_group[l], b_router_group[l],
                   w_router_expert[l], b_router_expert[l], w_expert_gate[l], w_expert_up[l], w_expert_down[l])
    return x
```

```python
import functools

import jax
import jax.numpy as jnp
from jax import lax
from jax.experimental import pallas as pl
from jax.experimental.pallas import tpu as pltpu
from jax.experimental.pallas import tpu_sc as plsc

D_MODEL = 1024
HEAD_DIM = 64
DIL_PATTERNS = ((128, 1), (512, 4), (2048, 16))
HEADS_PER_GROUP = 4
GROUP_WIDTH = HEADS_PER_GROUP * HEAD_DIM
N_DIL_GROUPS = len(DIL_PATTERNS)
DIL_WIDTH = N_DIL_GROUPS * GROUP_WIDTH
QKV_WIDTH = 3 * DIL_WIDTH + 3 * GROUP_WIDTH
WINDOW_KEYS = 128
ROPE_THETA = 500000.0
ROPE_DIM = HEAD_DIM // 4
N_GROUPS = 4
EXPERTS_PER_GROUP = 8
N_EXPERTS = N_GROUPS * EXPERTS_PER_GROUP
D_EXPERT = 512
RMS_EPS = 1e-6
ATTN_SCALE = HEAD_DIM ** -0.5

LANES = 128
ROUTER_LANES = LANES
EXPERT_LANE0 = N_GROUPS
LSE_SEG = LANES // HEADS_PER_GROUP
NEG_BIG = -1e30
SB_DEAD_LOG = -120.0
SB_HEAD_BLOCKS = 3
SB_CHAINS = 8

ROW_TILE = 512
QKV_TILE = 1024
QBLK = 128
DIL_QTILE = 2048
EXPERT_CHUNK = 512
COMBINE_TILE = 1024
DEST_TILE = 8192
SC_CORES = 2
SC_SUBCORES = 16
SC_WORKERS = SC_CORES * SC_SUBCORES
SC_INDEX_WINDOW = 128
MOE_PARTS = 2
VMEM_LIMIT = 48 * 1024 * 1024

_BF16 = jnp.bfloat16
_F32 = jnp.float32
_NT = (((1,), (1,)), ((), ()))


def _dot(a, b):
    return jnp.dot(a, b, preferred_element_type=_F32)


def _dot_nt(a, b):
    return lax.dot_general(a, b, _NT, preferred_element_type=_F32)


def _split(a):
    hi = a.astype(_BF16)
    lo = (a - hi.astype(_F32)).astype(_BF16)
    return hi, lo


def _dot3(a, b):
    ah, al = _split(a)
    bh, bl = _split(b)
    return _dot(ah, bh) + (_dot(ah, bl) + _dot(al, bh))


def _rms_mod(x, g, scale, shift):
    y = x * lax.rsqrt(jnp.mean(x * x, axis=-1, keepdims=True) + RMS_EPS)
    return y * g * (1.0 + scale) + shift


def _params(*sem):
    return pltpu.CompilerParams(dimension_semantics=sem, vmem_limit_bytes=VMEM_LIMIT)


def _ada_body(c_ref, w_ref, b_ref, o_ref):
    c = c_ref[...]
    o_ref[...] = _dot3(c * jax.nn.sigmoid(c), w_ref[...]) + b_ref[...]


def _ada(c, w_ada, b_ada):
    b, d = c.shape
    n = w_ada.shape[1]
    rows = -(-b // 16) * 16
    cp = jnp.zeros((rows, d), _F32).at[:b].set(c)
    nt = 1536
    out = pl.pallas_call(
        _ada_body,
        out_shape=jax.ShapeDtypeStruct((rows, n), _F32),
        grid=(n // nt,),
        in_specs=[pl.BlockSpec((rows, d), lambda j: (0, 0)),
                  pl.BlockSpec((d, nt), lambda j: (0, j)),
                  pl.BlockSpec((1, nt), lambda j: (0, j))],
        out_specs=pl.BlockSpec((rows, nt), lambda j: (0, j)),
        compiler_params=_params("arbitrary"),
        name="ada",
    )(cp, w_ada, b_ada.reshape(1, n))
    return out[:b]


def _qkv_body(x_ref, mod_ref, g_ref, w_ref, qg_ref, kg_ref, trig_ref, bd_ref,
              o0_ref, o1_ref, o2_ref, osb_ref, st_ref, acc_ref):
    d = D_MODEL
    tm = x_ref.shape[0]
    h = _rms_mod(x_ref[...], g_ref[...], mod_ref[:, d:2 * d], mod_ref[:, 0:d])
    hb = h.astype(_BF16)
    half = ROPE_DIM // 2
    in_head = lax.broadcasted_iota(jnp.int32, (tm, LANES), 1) & (HEAD_DIM - 1)
    freq = in_head & (half - 1)
    trig = jnp.concatenate([trig_ref[...], jnp.zeros((tm, LANES - ROPE_DIM), _F32)], axis=1)
    cos = jnp.take_along_axis(trig, freq, axis=1)
    sin = jnp.take_along_axis(trig, freq + half, axis=1)
    cc = jnp.where(in_head < ROPE_DIM, cos, 1.0)
    s1 = jnp.where(in_head < half, -sin, 0.0)
    s2 = jnp.where((in_head >= half) & (in_head < ROPE_DIM), sin, 0.0)
    cc, s1, s2 = [jnp.concatenate([a, a], axis=1) for a in (cc, s1, s2)]
    bd = bd_ref[...]
    gw = GROUP_WIDTH

    def normed_rotated(acc, gain):
        ms = _dot((acc * acc).astype(_BF16), bd)
        y = acc * lax.rsqrt(ms + RMS_EPS) * gain
        return y * cc + pltpu.roll(y, gw - ROPE_DIM // 2, 1) * s1 + pltpu.roll(y, ROPE_DIM // 2, 1) * s2

    def store(o_ref, dil, part, y):
        if dil == 1:
            o_ref[:, part * gw:(part + 1) * gw] = y.astype(_BF16)
            return
        for s in range(gw // LANES):
            st_ref[s] = y[:, s * LANES:(s + 1) * LANES]
        for r in range(dil):
            for s in range(gw // LANES):
                col0 = (3 * r + part) * gw + s * LANES
                o_ref[:, col0:col0 + LANES] = st_ref[s, pl.ds(r, tm // dil, stride=dil), :].astype(_BF16)

    def project(col0):
        return _dot(hb, w_ref[:, col0:col0 + gw])

    outs = (o0_ref, o1_ref, o2_ref)
    dils = [dl for _, dl in DIL_PATTERNS]
    normed = [(g, part, part * DIL_WIDTH + g * gw) for g in range(N_DIL_GROUPS) for part in (0, 1)]
    for n, (_, _, col0) in enumerate(normed):
        acc_ref[n] = project(col0)
    plain = [("v", g) for g in range(N_DIL_GROUPS)] + [("sb", part) for part in range(3)]
    for n, (kind, j) in enumerate(plain):
        if kind == "v":
            store(outs[j], dils[j], 2, project(2 * DIL_WIDTH + j * gw))
        else:
            acc = project(3 * DIL_WIDTH + j * gw)
            osb_ref[:, j * gw:(j + 1) * gw] = (acc * ATTN_SCALE if j == 0 else acc).astype(_BF16)
        g, part, _ = normed[n]
        gain = qg_ref[...] if part == 0 else kg_ref[...]
        store(outs[g], dils[g], part, normed_rotated(acc_ref[n], gain))


def _qkv(x2, mod3, g_mix, w_in, qg, kg, trig, bd, seq):
    t, d = x2.shape
    tm = min(QKV_TILE, seq)
    per_b = seq // tm
    row = lambda i: (i, 0)
    const = lambda i: (0, 0)
    width = 3 * GROUP_WIDTH
    dils = [dl for _, dl in DIL_PATTERNS] + [1]
    return pl.pallas_call(
        _qkv_body,
        out_shape=[jax.ShapeDtypeStruct((t // dl, dl * width), _BF16) for dl in dils],
        grid=(t // tm,),
        in_specs=[pl.BlockSpec((tm, d), row),
                  pl.BlockSpec((None, 1, mod3.shape[2]), lambda i: (i // per_b, 0, 0)),
                  pl.BlockSpec((1, d), const),
                  pl.BlockSpec(w_in.shape, const),
                  pl.BlockSpec((1, GROUP_WIDTH), const),
                  pl.BlockSpec((1, GROUP_WIDTH), const),
                  pl.BlockSpec((tm, ROPE_DIM), row),
                  pl.BlockSpec(bd.shape, const)],
        out_specs=[pl.BlockSpec((tm // dl, dl * width), row) for dl in dils],
        scratch_shapes=[pltpu.VMEM((GROUP_WIDTH // LANES, tm, LANES), _F32),
                        pltpu.VMEM((2 * N_DIL_GROUPS, tm, GROUP_WIDTH), _F32)],
        compiler_params=_params("arbitrary"),
        name="qkv",
    )(x2, mod3, g_mix, w_in, qg, kg, trig, bd)


def _dil_body(prev_ref, cur_ref, o_ref, lse_ref, kf_ref, vf_ref):
    tq = cur_ref.shape[0]
    gw = GROUP_WIDTH
    n_res = cur_ref.shape[1] // (3 * gw)
    first = pl.program_id(2) == 0
    for r in range(n_res):
        for part, full_ref in ((1, kf_ref), (2, vf_ref)):
            lanes = slice((3 * r + part) * gw, (3 * r + part + 1) * gw)
            full_ref[r, 0:QBLK, :] = prev_ref[:, lanes]
            full_ref[r, QBLK:, :] = cur_ref[:, lanes]
    nh = HEADS_PER_GROUP
    row = lax.broadcasted_iota(jnp.int32, (nh * QBLK, 2 * QBLK), 0) & (QBLK - 1)
    col = lax.broadcasted_iota(jnp.int32, (nh * QBLK, 2 * QBLK), 1)
    band = (col >= row) & (col <= row + WINDOW_KEYS)
    lane = lax.broadcasted_iota(jnp.int32, (1, GROUP_WIDTH), 1)
    slane = lax.broadcasted_iota(jnp.int32, (1, LANES), 1)
    head_masks = [(lane >= h * HEAD_DIM) & (lane < (h + 1) * HEAD_DIM) for h in range(nh)]
    subs = [(r, j) for r in range(n_res) for j in range(tq // QBLK)]
    rows = [slice(j * QBLK, (j + 1) * QBLK) for _, j in subs]
    window = [slice(j * QBLK, (j + 2) * QBLK) for _, j in subs]
    n = range(len(subs))

    def stack(qj):
        return jnp.concatenate([jnp.where(hm, qj, jnp.zeros_like(qj)) for hm in head_masks], axis=0)

    valid = [band & ((col >= QBLK) | jnp.logical_not(first)) if j == 0 else band for _, j in subs]
    q = [cur_ref[rows[i], 3 * r * gw:(3 * r + 1) * gw] for i, (r, _) in enumerate(subs)]
    s = [jnp.where(valid[i], _dot_nt(stack(q[i]), kf_ref[subs[i][0], window[i], :]), NEG_BIG) for i in n]
    m = [jnp.max(s[i], axis=1, keepdims=True) for i in n]
    p = [jnp.exp(s[i] - m[i]) for i in n]
    l = [jnp.sum(p[i], axis=1, keepdims=True) for i in n]
    o_all = [_dot(p[i].astype(_BF16), vf_ref[subs[i][0], window[i], :]) / l[i] for i in n]
    for i, (r, _) in enumerate(subs):
        lse_all = m[i] + jnp.log(l[i])
        o_acc = jnp.zeros((QBLK, gw), _F32)
        lse_t = jnp.zeros((QBLK, LANES), _F32)
        for h, hm in enumerate(head_masks):
            o_acc = jnp.where(hm, o_all[i][h * QBLK:(h + 1) * QBLK, :], o_acc)
            sm = (slane >= h * LSE_SEG) & (slane < (h + 1) * LSE_SEG)
            lse_t = jnp.where(sm, lse_all[h * QBLK:(h + 1) * QBLK, :], lse_t)
        o_ref[rows[i], r * gw:(r + 1) * gw] = o_acc.astype(_BF16)
        lse_ref[rows[i], r * LANES:(r + 1) * LANES] = lse_t


def _dilated_group(view2, dil, batch, seq):
    sd = seq // dil
    tq = min(DIL_QTILE, sd)
    n_res = min(dil, DIL_QTILE // tq)
    per = tq // QBLK
    gw = GROUP_WIDTH
    view = view2.reshape(batch, sd, dil * 3 * gw)
    o, lse = pl.pallas_call(
        _dil_body,
        out_shape=[jax.ShapeDtypeStruct((batch, sd, dil * gw), _BF16),
                   jax.ShapeDtypeStruct((batch, sd, dil * LANES), _F32)],
        grid=(batch, dil // n_res, sd // tq),
        in_specs=[pl.BlockSpec((None, QBLK, n_res * 3 * gw), lambda b, r, i: (b, jnp.maximum(i * per - 1, 0), r)),
                  pl.BlockSpec((None, tq, n_res * 3 * gw), lambda b, r, i: (b, i, r))],
        out_specs=[pl.BlockSpec((None, tq, n_res * gw), lambda b, r, i: (b, i, r)),
                   pl.BlockSpec((None, tq, n_res * LANES), lambda b, r, i: (b, i, r))],
        scratch_shapes=[pltpu.VMEM((n_res, tq + QBLK, gw), _BF16), pltpu.VMEM((n_res, tq + QBLK, gw), _BF16)],
        compiler_params=_params("arbitrary", "arbitrary", "arbitrary"),
        name=f"dil{dil}",
    )(view, view)
    return o.reshape(batch * sd, dil * gw), lse.reshape(batch * sd, dil * LANES)


def _sb_body(q_ref, k_ref, v_ref, o_ref, carry_ref, acc_ref, qs_ref):
    step_id = pl.program_id(1)
    chains = range(SB_CHAINS)
    blk = [step_id * SB_CHAINS + c for c in chains]
    nh = HEADS_PER_GROUP
    row = lax.broadcasted_iota(jnp.int32, (nh * QBLK, QBLK), 0) & (QBLK - 1)
    col = lax.broadcasted_iota(jnp.int32, (nh * QBLK, QBLK), 1)
    strict = col < row
    ur = lax.broadcasted_iota(jnp.int32, (2 * QBLK, QBLK), 0) & (QBLK - 1)
    uc = lax.broadcasted_iota(jnp.int32, (2 * QBLK, QBLK), 1)
    u = jnp.where(ur > uc, 1.0, 0.0).astype(_BF16)
    lane = lax.broadcasted_iota(jnp.int32, (1, GROUP_WIDTH), 1)
    head_masks = [(lane >= h * HEAD_DIM) & (lane < (h + 1) * HEAD_DIM) for h in range(nh)]
    for c in chains:
        q = q_ref[c * QBLK:(c + 1) * QBLK, :]
        qs_ref[c] = jnp.concatenate([jnp.where(hm, q, jnp.zeros_like(q)) for hm in head_masks], axis=0)

    def softplus(z):
        return jnp.maximum(z, 0.0) + jnp.log(1.0 + jnp.exp(-jnp.abs(z)))

    def later_keys(log_1m):
        hi, lo = _split(log_1m)
        return _dot(jnp.concatenate([hi, lo], axis=1), u)

    nb = SB_HEAD_BLOCKS
    cols = [slice(j * QBLK, (j + 1) * QBLK) for j in range(nb)]
    kbs = [[blk[c] - (nb - 1) + j for j in range(nb)] for c in chains]
    starts = [[pl.multiple_of(jnp.maximum(kb, 0) * QBLK, QBLK) for kb in kbs[c]] for c in chains]
    keep = [[strict if j == nb - 1 else (kbs[c][j] >= 0) for j in range(nb)] for c in chains]
    z = [_dot_nt(qs_ref[c], jnp.concatenate([k_ref[pl.ds(s, QBLK), :] for s in starts[c]], axis=0)) for c in chains]
    sp = [softplus(z[c]) for c in chains]
    log_1m = [[jnp.where(keep[c][j], -sp[c][:, cols[j]], 0.0) for j in range(nb)] for c in chains]
    totals = [[jnp.sum(l, axis=1, keepdims=True) for l in log_1m[c]] for c in chains]
    later = [[later_keys(log_1m[c][j]) for j in range(nb)] for c in chains]
    for c in chains:
        a_blocks = []
        after = jnp.zeros_like(totals[c][0])
        for j in reversed(range(nb)):
            a = jnp.exp((z[c][:, cols[j]] - sp[c][:, cols[j]]) + later[c][j] + after)
            a_blocks.insert(0, jnp.where(keep[c][j], a, 0.0).astype(_BF16))
            after = after + totals[c][j]
        acc_ref[c] = _dot(jnp.concatenate(a_blocks, axis=1),
                          jnp.concatenate([v_ref[pl.ds(s, QBLK), :] for s in starts[c]], axis=0))
        carry_ref[c] = after

    for c in chains:
        def tile(kb, c=c):
            start = pl.multiple_of(kb * QBLK, QBLK)
            z = _dot_nt(qs_ref[c], k_ref[pl.ds(start, QBLK), :])
            sp = softplus(z)
            log_1m = -sp
            a = jnp.exp((z - sp) + later_keys(log_1m) + carry_ref[c])
            acc_ref[c] += _dot(a.astype(_BF16), v_ref[pl.ds(start, QBLK), :])
            carry = carry_ref[c] + jnp.sum(log_1m, axis=1, keepdims=True)
            carry_ref[c] = carry
            return jnp.max(carry)

        def cond(st):
            return (st[0] >= 0) & (st[1] > SB_DEAD_LOG)

        def step(st, tile=tile):
            return st[0] - 1, tile(st[0])

        lax.while_loop(cond, step, (blk[c] - nb, jnp.max(carry_ref[c])))
        out = jnp.zeros((QBLK, GROUP_WIDTH), _F32)
        for h, hm in enumerate(head_masks):
            out = jnp.where(hm, acc_ref[c, h * QBLK:(h + 1) * QBLK, :], out)
        o_ref[c * QBLK:(c + 1) * QBLK, :] = out.astype(_BF16)


def _stick_breaking(arr, batch, seq):
    gw = GROUP_WIDTH
    qt = SB_CHAINS * QBLK
    view = arr.reshape(batch, seq, 3 * gw)
    o = pl.pallas_call(
        _sb_body,
        out_shape=jax.ShapeDtypeStruct((batch, seq, gw), _BF16),
        grid=(batch, seq // qt),
        in_specs=[pl.BlockSpec((None, qt, gw), lambda b, i: (b, i, 0)),
                  pl.BlockSpec((None, seq, gw), lambda b, i: (b, 0, 1)),
                  pl.BlockSpec((None, seq, gw), lambda b, i: (b, 0, 2))],
        out_specs=pl.BlockSpec((None, qt, gw), lambda b, i: (b, i, 0)),
        scratch_shapes=[pltpu.VMEM((SB_CHAINS, HEADS_PER_GROUP * QBLK, 1), _F32),
                        pltpu.VMEM((SB_CHAINS, HEADS_PER_GROUP * QBLK, gw), _F32),
                        pltpu.VMEM((SB_CHAINS, HEADS_PER_GROUP * QBLK, gw), _BF16)],
        compiler_params=_params("arbitrary", "arbitrary"),
        name="sb",
    )(view, view, view)
    return o.reshape(batch * seq, gw)


def _merge_body(x_ref, mod_ref, g1_ref, g2_ref, o0_ref, o1_ref, o2_ref, l0_ref, l1_ref, l2_ref, osb_ref,
                wbg_ref, wbd_ref, wbs_ref, wout_ref, wrh_ref, wrl_ref, br_ref, ex_ref, tri_ref,
                x1_ref, h2p_ref, rinfo_ref, er_ref, cnt_out_ref, cnt_ref, os1_ref, os2_ref, ls1_ref, ls2_ref):
    d = D_MODEL
    tm = x_ref.shape[0]
    x = x_ref[...]
    hb = _rms_mod(x, g1_ref[...], mod_ref[:, d:2 * d], mod_ref[:, 0:d]).astype(_BF16)

    def natural(ref, st_ref, dil):
        if dil == 1:
            return ref[...].astype(_F32)
        slabs = st_ref.shape[0]
        for r in range(dil):
            for s in range(slabs):
                col0 = (r * slabs + s) * LANES
                st_ref[s, pl.ds(r, tm // dil, stride=dil), :] = ref[:, col0:col0 + LANES].astype(_F32)
        return jnp.concatenate([st_ref[s] for s in range(slabs)], axis=1)

    dils = [dl for _, dl in DIL_PATTERNS]
    o_nat = [natural(r, s, dl) for r, s, dl in zip((o0_ref, o1_ref, o2_ref), (None, os1_ref, os2_ref), dils)]
    l0, l1, l2 = [natural(r, s, dl) for r, s, dl in zip((l0_ref, l1_ref, l2_ref), (None, ls1_ref, ls2_ref), dils)]

    lmax = jnp.maximum(jnp.maximum(l0, l1), l2)
    e0, e1, e2 = jnp.exp(l0 - lmax), jnp.exp(l1 - lmax), jnp.exp(l2 - lmax)
    inv = 1.0 / (e0 + e1 + e2)
    ex = ex_ref[...]

    def widen(w):
        hi, lo = _split(w)
        return _dot(jnp.concatenate([hi, lo], axis=1), ex)

    w_groups = [widen(e * inv) for e in (e0, e1, e2)]
    gate_dil = jax.nn.sigmoid(_dot(hb, wbg_ref[:, :d]))
    o_dil = w_groups[0] * o_nat[0] + w_groups[1] * o_nat[1] + w_groups[2] * o_nat[2]
    branch_dil = _dot(o_dil.astype(_BF16), wbd_ref[...])
    branch_sb = _dot(osb_ref[...], wbs_ref[...])
    gate_sb = jax.nn.sigmoid(_dot(hb, wbg_ref[:, d:]))
    merged = gate_dil * branch_dil + gate_sb * branch_sb
    x1 = x + mod_ref[:, 2 * d:3 * d] * _dot(merged.astype(_BF16), wout_ref[...])
    x1_ref[...] = x1

    h2 = _rms_mod(x1, g2_ref[...], mod_ref[:, 4 * d:5 * d], mod_ref[:, 3 * d:4 * d])
    h2p_ref[...] = _pack_halves(h2)

    hh, hl = _split(h2)
    logits = _dot(hh, wrh_ref[...]) + (_dot(hl, wrh_ref[...]) + _dot(hh, wrl_ref[...])) + br_ref[...]
    lane = lax.broadcasted_iota(jnp.int32, (tm, ROUTER_LANES), 1).astype(_F32)
    far = float(ROUTER_LANES)

    def top(vals):
        m = jnp.max(vals, axis=1, keepdims=True)
        return m, jnp.min(jnp.where(vals == m, lane, far), axis=1, keepdims=True)

    is_group = lane < N_GROUPS
    mg, gsel = top(jnp.where(is_group, logits, NEG_BIG))
    pg_top = 1.0 / jnp.sum(jnp.where(is_group, jnp.exp(logits - mg), 0.0), axis=1, keepdims=True)
    lane0 = EXPERT_LANE0 + EXPERTS_PER_GROUP * gsel
    le = jnp.where((lane >= lane0) & (lane < lane0 + EXPERTS_PER_GROUP), logits, NEG_BIG)
    m1, i1 = top(le)
    m2, i2 = top(jnp.where(lane == i1, NEG_BIG, le))
    t2 = jnp.exp(m2 - m1)
    w0 = pg_top / (1.0 + t2)
    w1 = pg_top * t2 / (1.0 + t2)

    @pl.when(pl.program_id(0) == 0)
    def _():
        cnt_ref[...] = jnp.zeros_like(cnt_ref)

    sel0, sel1 = lane == i1, lane == i2
    onehot = jnp.where(sel0 | sel1, 1.0, 0.0)
    before = _dot(tri_ref[...], onehot.astype(_BF16)) + cnt_ref[0:1, :]
    r0 = jnp.sum(jnp.where(sel0, before, 0.0), axis=1, keepdims=True)
    r1 = jnp.sum(jnp.where(sel1, before, 0.0), axis=1, keepdims=True)
    cnt_ref[...] += jnp.sum(onehot, axis=0, keepdims=True)
    cnt_out_ref[...] = cnt_ref[...]
    cols = (i1 - EXPERT_LANE0, i2 - EXPERT_LANE0, r0, r1, w0, w1)
    rinfo = jnp.zeros((tm, ROUTER_LANES), _F32)
    for c, v in enumerate(cols):
        rinfo = jnp.where(lane == float(c), v, rinfo)
    rinfo_ref[...] = rinfo
    er_ref[...] = jnp.transpose(rinfo)[0:8, :].astype(jnp.int32)


def _merge(x2, mod3, g1, g2, outs, lses, osb, wbg, wbd, wbs, wout, wrh, wrl, br, ex, tri, seq, part):
    d = x2.shape[1]
    tm = ROW_TILE
    per_b = seq // tm
    t = x2.shape[0] // MOE_PARTS
    first = part * (t // tm)
    src = lambda i: (i + first, 0)
    row = lambda i: (i, 0)
    const = lambda i: (0, 0)
    full = lambda a: pl.BlockSpec(a.shape, const)
    gw = GROUP_WIDTH
    dils = [dl for _, dl in DIL_PATTERNS]
    return pl.pallas_call(
        _merge_body,
        out_shape=[jax.ShapeDtypeStruct((t, d), _F32),
                   jax.ShapeDtypeStruct((t, d // 2), jnp.uint32),
                   jax.ShapeDtypeStruct((t, ROUTER_LANES), _F32),
                   jax.ShapeDtypeStruct((8, t), jnp.int32),
                   jax.ShapeDtypeStruct((8, ROUTER_LANES), _F32)],
        grid=(t // tm,),
        in_specs=[pl.BlockSpec((tm, d), src),
                  pl.BlockSpec((None, 1, mod3.shape[2]), lambda i: ((i + first) // per_b, 0, 0)),
                  full(g1), full(g2)]
                 + [pl.BlockSpec((tm // dl, dl * gw), src) for dl in dils]
                 + [pl.BlockSpec((tm // dl, dl * LANES), src) for dl in dils]
                 + [pl.BlockSpec((tm, gw), src)]
                 + [full(a) for a in (wbg, wbd, wbs, wout, wrh, wrl, br, ex, tri)],
        out_specs=[pl.BlockSpec((tm, d), row),
                   pl.BlockSpec((tm, d // 2), row),
                   pl.BlockSpec((tm, ROUTER_LANES), row),
                   pl.BlockSpec((8, tm), lambda i: (0, i)),
                   pl.BlockSpec((8, ROUTER_LANES), const)],
        scratch_shapes=[pltpu.VMEM((8, ROUTER_LANES), _F32),
                        pltpu.VMEM((gw // LANES, tm, LANES), _F32), pltpu.VMEM((gw // LANES, tm, LANES), _F32),
                        pltpu.VMEM((1, tm, LANES), _F32), pltpu.VMEM((1, tm, LANES), _F32)],
        compiler_params=_params("arbitrary"),
        name="merge",
    )(x2, mod3, g1, g2, *outs, *lses, osb, wbg, wbd, wbs, wout, wrh, wrl, br, ex, tri)


def _dest_body(ps_ref, er_ref, d_ref):
    e = er_ref[0:2, :]
    start = jnp.zeros_like(e)
    for x in range(N_EXPERTS):
        start = jnp.where(e == x, ps_ref[x], start)
    d_ref[...] = start + er_ref[2:4, :]


def _dest(pstart, er):
    t = er.shape[1]
    tw = min(DEST_TILE, t)
    return pl.pallas_call(
        _dest_body,
        out_shape=jax.ShapeDtypeStruct((2, t), jnp.int32),
        grid_spec=pltpu.PrefetchScalarGridSpec(
            num_scalar_prefetch=1,
            grid=(t // tw,),
            in_specs=[pl.BlockSpec((8, tw), lambda i, ps: (0, i))],
            out_specs=pl.BlockSpec((2, tw), lambda i, ps: (0, i))),
        compiler_params=_params("arbitrary"),
        name="dest",
    )(pstart, er)


def _sc_mesh():
    return plsc.VectorSubcoreMesh(core_axis_name="core", subcore_axis_name="subcore",
                                  num_cores=SC_CORES, num_subcores=SC_SUBCORES)


def _sc_worker():
    return lax.axis_index("subcore") * SC_CORES + lax.axis_index("core")


def _sc_scatter(x, idx0, idx1, n_slots):
    chunks = idx0.shape[0]
    per = chunks // SC_WORKERS
    win = idx0.shape[1]

    @functools.partial(
        pl.kernel, mesh=_sc_mesh(), out_type=jax.ShapeDtypeStruct((n_slots, x.shape[1]), x.dtype),
        scratch_types=[pltpu.VMEM((1, win), jnp.int32), pltpu.VMEM((1, win), jnp.int32),
                       pltpu.VMEM((win, x.shape[1]), x.dtype), pltpu.SemaphoreType.DMA],
        name="sc_scatter")
    def run(x_hbm, i0_hbm, i1_hbm, o_hbm, i0_v, i1_v, rows_v, sem):
        wid = _sc_worker()

        @pl.loop(0, per)
        def _(j):
            c = wid * per + j
            pltpu.sync_copy(i0_hbm.at[pl.ds(c, 1)], i0_v)
            pltpu.sync_copy(i1_hbm.at[pl.ds(c, 1)], i1_v)
            pltpu.sync_copy(x_hbm.at[pl.ds(c * win, win)], rows_v)
            first = pltpu.async_copy(rows_v, o_hbm.at[i0_v.at[0]], sem)
            second = pltpu.async_copy(rows_v, o_hbm.at[i1_v.at[0]], sem)
            first.wait()
            second.wait()

    return run(x, idx0, idx1)


def _sc_gather(table, idx):
    chunks, win = idx.shape
    per = chunks // SC_WORKERS

    @functools.partial(
        pl.kernel, mesh=_sc_mesh(), out_type=jax.ShapeDtypeStruct((chunks * win, table.shape[1]), table.dtype),
        scratch_types=[pltpu.VMEM((1, win), jnp.int32), pltpu.VMEM((win, table.shape[1]), table.dtype),
                       pltpu.SemaphoreType.DMA],
        name="sc_gather")
    def run(t_hbm, i_hbm, o_hbm, i_v, rows_v, sem):
        wid = _sc_worker()

        @pl.loop(0, per)
        def _(j):
            c = wid * per + j
            pltpu.sync_copy(i_hbm.at[pl.ds(c, 1)], i_v)
            pltpu.async_copy(t_hbm.at[i_v.at[0]], rows_v, sem).wait()
            pltpu.sync_copy(rows_v, o_hbm.at[pl.ds(c * win, win)])

    return run(table, idx)


def _pack_halves(a):
    h = a.shape[1] // 2
    lo = lax.bitcast_convert_type(a[:, :h].astype(_BF16).astype(_F32), jnp.uint32) >> 16
    hi = lax.bitcast_convert_type(a[:, h:].astype(_BF16).astype(_F32), jnp.uint32) & jnp.uint32(0xFFFF0000)
    return lo | hi


def _unpack_halves(w):
    return jnp.concatenate(
        [lax.bitcast_convert_type(w << 16, _F32), lax.bitcast_convert_type(w & jnp.uint32(0xFFFF0000), _F32)], axis=1)


def _experts_body(ce_ref, nv_ref, nu_ref, seg_ref, nxt_ref, xs_ref, wg_ref, wu_ref, wd_ref, ys_ref,
                  wgb_ref, wub_ref, wdb_ref, wgf_ref, wuf_ref, wdf_ref, sem):
    c = pl.program_id(0)

    def fetch(expert, slot):
        return [pltpu.make_async_copy(src.at[expert], dst.at[slot], sem.at[slot])
                for src, dst in ((wg_ref, wgf_ref), (wu_ref, wuf_ref), (wd_ref, wdf_ref))]

    @pl.when(c < nu_ref[0])
    def _():
        @pl.when(seg_ref[c] >= 0)
        def _():
            slot = seg_ref[c] & 1

            @pl.when(c == 0)
            def _():
                for copy in fetch(ce_ref[0], 0):
                    copy.start()

            for copy in fetch(ce_ref[c], slot):
                copy.wait()
            wgb_ref[...] = wgf_ref[slot].astype(_BF16)
            wub_ref[...] = wuf_ref[slot].astype(_BF16)
            wdb_ref[...] = wdf_ref[slot].astype(_BF16)

            @pl.when(nxt_ref[c] >= 0)
            def _():
                for copy in fetch(nxt_ref[c], 1 - slot):
                    copy.start()

        row = lax.broadcasted_iota(jnp.int32, xs_ref.shape, 0)
        x = _unpack_halves(jnp.where(row < nv_ref[c], xs_ref[...], jnp.uint32(0))).astype(_BF16)
        half = x.shape[0] // 2
        ups = [(_dot(x[r:r + half], wgb_ref[...]), _dot(x[r:r + half], wub_ref[...])) for r in (0, half)]
        for (g, u), r in zip(ups, (0, half)):
            hmid = (g * jax.nn.sigmoid(g)) * u
            ys_ref[r:r + half, :] = _pack_halves(_dot(hmid.astype(_BF16), wdb_ref[...]))


def _experts(chunk_e, n_valid, n_used, seg, nxt, xs, wg, wu, wd):
    n_slots, w = xs.shape
    ch = EXPERT_CHUNK
    d, de = wg.shape[1], wg.shape[2]
    slot = lambda c, ce, nv, nu, sg, nx: (jnp.minimum(c, nu[0] - 1), 0)
    hbm = pl.BlockSpec(memory_space=pl.ANY)
    return pl.pallas_call(
        _experts_body,
        out_shape=jax.ShapeDtypeStruct((n_slots, d // 2), jnp.uint32),
        grid_spec=pltpu.PrefetchScalarGridSpec(
            num_scalar_prefetch=5,
            grid=(n_slots // ch,),
            in_specs=[pl.BlockSpec((ch, w), slot), hbm, hbm, hbm],
            out_specs=pl.BlockSpec((ch, d // 2), slot),
            scratch_shapes=[pltpu.VMEM((d, de), _BF16), pltpu.VMEM((d, de), _BF16), pltpu.VMEM((de, d), _BF16),
                            pltpu.VMEM((2, d, de), _F32), pltpu.VMEM((2, d, de), _F32), pltpu.VMEM((2, de, d), _F32),
                            pltpu.SemaphoreType.DMA((2,))]),
        compiler_params=_params("arbitrary"),
        name="experts",
    )(chunk_e, n_valid, n_used, seg, nxt, xs, wg, wu, wd)


def _combine_body(x1_ref, rinfo_ref, mod_ref, y0_ref, y1_ref, *rest):
    o_ref = rest[-1]
    y = rinfo_ref[:, 4:5] * _unpack_halves(y0_ref[...]) + rinfo_ref[:, 5:6] * _unpack_halves(y1_ref[...])
    o_ref[...] = x1_ref[...] + mod_ref[:, 5 * D_MODEL:6 * D_MODEL] * y


def _combine(x1, rinfo, mod3, gathered, seq, part, out_so_far):
    t, d = x1.shape
    tf = min(COMBINE_TILE, seq)
    per_b = seq // tf
    nt = t // tf
    first = part * nt
    in_specs = [pl.BlockSpec((tf, d), lambda i: (i, 0)),
                pl.BlockSpec((tf, ROUTER_LANES), lambda i: (i, 0)),
                pl.BlockSpec((None, 1, mod3.shape[2]), lambda i: ((i + first) // per_b, 0, 0)),
                pl.BlockSpec((tf, d // 2), lambda i: (i, 0)),
                pl.BlockSpec((tf, d // 2), lambda i: (i + nt, 0))]
    args = [x1, rinfo, mod3, gathered, gathered]
    aliases = {}
    if out_so_far is not None:
        in_specs.append(pl.BlockSpec(memory_space=pl.ANY))
        args.append(out_so_far)
        aliases = {len(args) - 1: 0}
    return pl.pallas_call(
        _combine_body,
        out_shape=jax.ShapeDtypeStruct((t * MOE_PARTS, d), _F32),
        grid=(nt,),
        in_specs=in_specs,
        out_specs=pl.BlockSpec((tf, d), lambda i: (i + first, 0)),
        input_output_aliases=aliases,
        compiler_params=_params("arbitrary"),
        name="combine",
    )(*args)


def _rope_trig(positions):
    inv_freq = ROPE_THETA ** (-jnp.arange(0, ROPE_DIM, 2, dtype=_F32) / ROPE_DIM)
    ang = positions.reshape(-1).astype(_F32)[:, None] * inv_freq
    return jnp.concatenate([jnp.cos(ang), jnp.sin(ang)], axis=1)


def _layer(x, mod, positions, g_mix, g_ffn, w_in, w_bg, qg, kg, w_bd, w_bs, w_out, w_rg, b_rg, w_re, b_re,
           w_eg, w_eu, w_ed):
    batch, seq, d = x.shape
    t = batch * seq
    x2 = x.reshape(t, d)
    mod3 = mod.reshape(batch, 1, mod.shape[1])
    gw = GROUP_WIDTH

    lane = jnp.arange(gw)
    bd = jnp.where(lane[:, None] // HEAD_DIM == lane[None, :] // HEAD_DIM, 1.0 / HEAD_DIM, 0.0).astype(_BF16)
    ex = (jnp.arange(LANES)[:, None] == (lane[None, :] // HEAD_DIM) * LSE_SEG).astype(_BF16)
    ex = jnp.concatenate([ex, ex], axis=0)
    tri = (jnp.arange(ROW_TILE)[:, None] > jnp.arange(ROW_TILE)[None, :]).astype(_BF16)
    tile4 = lambda g: jnp.tile(g.astype(_F32), HEADS_PER_GROUP).reshape(1, gw)
    wr = jnp.zeros((d, ROUTER_LANES), _F32).at[:, :N_GROUPS].set(w_rg).at[:, N_GROUPS:N_GROUPS + N_EXPERTS].set(w_re)
    wrh = wr.astype(_BF16)
    wrl = (wr - wrh.astype(_F32)).astype(_BF16)
    br = jnp.zeros((1, ROUTER_LANES), _F32).at[0, :N_GROUPS].set(b_rg).at[0, N_GROUPS:N_GROUPS + N_EXPERTS].set(b_re)

    d0, d1, d2, sbp = _qkv(x2, mod3, g_mix.reshape(1, d), w_in.astype(_BF16), tile4(qg) * ATTN_SCALE, tile4(kg),
                           _rope_trig(positions), bd, seq)
    dil = [_dilated_group(a, dl, batch, seq) for a, (_, dl) in zip((d0, d1, d2), DIL_PATTERNS)]
    osb = _stick_breaking(sbp, batch, seq)

    merge_weights = (w_bg.astype(_BF16), w_bd.astype(_BF16), w_bs.astype(_BF16), w_out.astype(_BF16))
    tp = t // MOE_PARTS
    ch = EXPERT_CHUNK
    win = SC_INDEX_WINDOW
    n_chunks = -(-2 * tp // ch) + N_EXPERTS
    chunk_start = jnp.arange(n_chunks, dtype=jnp.int32) * ch
    expert_ids = jnp.arange(N_EXPERTS, dtype=jnp.int32)
    out = None
    for part in range(MOE_PARTS):
        x1, h2p, rinfo, er, cnt = _merge(
            x2, mod3, g_mix.reshape(1, d), g_ffn.reshape(1, d), [o for o, _ in dil], [l for _, l in dil], osb,
            *merge_weights, wrh, wrl, br, ex, tri, seq, part)

        counts = cnt[0, EXPERT_LANE0:EXPERT_LANE0 + N_EXPERTS].astype(jnp.int32)
        padded = (counts + ch - 1) // ch * ch
        pend = jnp.cumsum(padded)
        pstart = pend - padded
        chunk_e = jnp.minimum(jnp.sum((pend[None, :] <= chunk_start[:, None]).astype(jnp.int32), axis=1),
                              N_EXPERTS - 1)
        n_used = (pend[-1:] // ch).astype(jnp.int32)
        begin = chunk_start[:, None]
        inside = (pstart[None, :] <= begin) & (begin < pend[None, :])
        n_valid = jnp.sum(jnp.where(inside, jnp.clip(counts[None, :] - (begin - pstart[None, :]), 0, ch), 0), axis=1)

        dest = _dest(pstart, er)
        xs = _sc_scatter(h2p, dest[0].reshape(tp // win, win), dest[1].reshape(tp // win, win), n_chunks * ch)
        first = (chunk_e != jnp.concatenate([jnp.full((1,), -1, jnp.int32), chunk_e[:-1]])) & (chunk_start < pend[-1])
        seg_no = jnp.cumsum(first.astype(jnp.int32)) - 1
        seg = jnp.where(first, seg_no, -1 - seg_no)
        later = (expert_ids[None, :] > expert_ids[:, None]) & (padded > 0)[None, :]
        next_expert = jnp.min(jnp.where(later, expert_ids[None, :], N_EXPERTS), axis=1)
        next_expert = jnp.where(next_expert == N_EXPERTS, -1, next_expert)
        nxt = jnp.sum(jnp.where(chunk_e[:, None] == expert_ids[None, :], next_expert[None, :], 0), axis=1)
        ys = _experts(chunk_e, n_valid, n_used, seg.astype(jnp.int32), nxt.astype(jnp.int32), xs, w_eg, w_eu, w_ed)
        gathered = _sc_gather(ys, dest.reshape(2 * tp // win, win))
        out = _combine(x1, rinfo, mod3, gathered, seq, part, out)
    return out.reshape(batch, seq, d)


def kernel(x, c, positions, w_ada, b_ada, g_norm_mix, g_norm_ffn, w_in, w_branch_gate, q_norm_g, k_norm_g,
           w_branch_dil, w_branch_sb, w_out, w_router_group, b_router_group, w_router_expert, b_router_expert,
           w_expert_gate, w_expert_up, w_expert_down):
    for l in range(w_ada.shape[0]):
        mod = _ada(c, w_ada[l], b_ada[l])
        x = _layer(x, mod, positions, g_norm_mix[l], g_norm_ffn[l], w_in[l], w_branch_gate[l], q_norm_g[l],
                   k_norm_g[l], w_branch_dil[l], w_branch_sb[l], w_out[l], w_router_group[l], b_router_group[l],
                   w_router_expert[l], b_router_expert[l], w_expert_gate[l], w_expert_up[l], w_expert_down[l])
    return x
```

```python
import functools

import jax
import jax.numpy as jnp
from jax import lax
from jax.experimental import pallas as pl
from jax.experimental.pallas import tpu as pltpu
from jax.experimental.pallas import tpu_sc as plsc

D_MODEL = 1024
HEAD_DIM = 64
DIL_PATTERNS = ((128, 1), (512, 4), (2048, 16))
HEADS_PER_GROUP = 4
GROUP_WIDTH = HEADS_PER_GROUP * HEAD_DIM
N_DIL_GROUPS = len(DIL_PATTERNS)
DIL_WIDTH = N_DIL_GROUPS * GROUP_WIDTH
QKV_WIDTH = 3 * DIL_WIDTH + 3 * GROUP_WIDTH
WINDOW_KEYS = 128
ROPE_THETA = 500000.0
ROPE_DIM = HEAD_DIM // 4
N_GROUPS = 4
EXPERTS_PER_GROUP = 8
N_EXPERTS = N_GROUPS * EXPERTS_PER_GROUP
D_EXPERT = 512
RMS_EPS = 1e-6
ATTN_SCALE = HEAD_DIM ** -0.5

LANES = 128
ROUTER_LANES = LANES
EXPERT_LANE0 = N_GROUPS
LSE_SEG = LANES // HEADS_PER_GROUP
NEG_BIG = -1e30
SB_DEAD_LOG = -120.0
SOFTPLUS_LINEAR_FROM = 30.0
SB_HEAD_BLOCKS = 3
SB_CHAINS = 8

ROW_TILE = 512
QKV_TILE = 1024
QBLK = 128
DIL_QTILE = 2048
EXPERT_CHUNK = 512
COMBINE_TILE = 1024
DEST_TILE = 8192
SC_CORES = 2
SC_SUBCORES = 16
SC_WORKERS = SC_CORES * SC_SUBCORES
SC_INDEX_WINDOW = 128
MOE_PARTS = 2
VMEM_LIMIT = 48 * 1024 * 1024

_BF16 = jnp.bfloat16
_F32 = jnp.float32
_NT = (((1,), (1,)), ((), ()))


def _dot(a, b):
    return jnp.dot(a, b, preferred_element_type=_F32)


def _dot_nt(a, b):
    return lax.dot_general(a, b, _NT, preferred_element_type=_F32)


def _split(a):
    hi = a.astype(_BF16)
    lo = (a - hi.astype(_F32)).astype(_BF16)
    return hi, lo


def _dot3(a, b):
    ah, al = _split(a)
    bh, bl = _split(b)
    return _dot(ah, bh) + (_dot(ah, bl) + _dot(al, bh))


def _rms_mod(x, g, scale, shift):
    y = x * lax.rsqrt(jnp.mean(x * x, axis=-1, keepdims=True) + RMS_EPS)
    return y * g * (1.0 + scale) + shift


def _params(*sem):
    return pltpu.CompilerParams(dimension_semantics=sem, vmem_limit_bytes=VMEM_LIMIT)


def _ada_body(c_ref, w_ref, b_ref, o_ref):
    c = c_ref[...]
    o_ref[...] = _dot3(c * jax.nn.sigmoid(c), w_ref[...]) + b_ref[...]


def _ada(c, w_ada, b_ada):
    b, d = c.shape
    n = w_ada.shape[1]
    rows = -(-b // 16) * 16
    cp = jnp.zeros((rows, d), _F32).at[:b].set(c)
    nt = 1536
    out = pl.pallas_call(
        _ada_body,
        out_shape=jax.ShapeDtypeStruct((rows, n), _F32),
        grid=(n // nt,),
        in_specs=[pl.BlockSpec((rows, d), lambda j: (0, 0)),
                  pl.BlockSpec((d, nt), lambda j: (0, j)),
                  pl.BlockSpec((1, nt), lambda j: (0, j))],
        out_specs=pl.BlockSpec((rows, nt), lambda j: (0, j)),
        compiler_params=_params("arbitrary"),
        name="ada",
    )(cp, w_ada, b_ada.reshape(1, n))
    return out[:b]


def _qkv_body(x_ref, mod_ref, g_ref, w_ref, qg_ref, kg_ref, trig_ref, bd_ref,
              o0_ref, o1_ref, o2_ref, osb_ref, st_ref, acc_ref):
    d = D_MODEL
    tm = x_ref.shape[0]
    h = _rms_mod(x_ref[...], g_ref[...], mod_ref[:, d:2 * d], mod_ref[:, 0:d])
    hb = h.astype(_BF16)
    half = ROPE_DIM // 2
    in_head = lax.broadcasted_iota(jnp.int32, (tm, LANES), 1) & (HEAD_DIM - 1)
    freq = in_head & (half - 1)
    trig = jnp.concatenate([trig_ref[...], jnp.zeros((tm, LANES - ROPE_DIM), _F32)], axis=1)
    cos = jnp.take_along_axis(trig, freq, axis=1)
    sin = jnp.take_along_axis(trig, freq + half, axis=1)
    cc = jnp.where(in_head < ROPE_DIM, cos, 1.0)
    s1 = jnp.where(in_head < half, -sin, 0.0)
    s2 = jnp.where((in_head >= half) & (in_head < ROPE_DIM), sin, 0.0)
    cc, s1, s2 = [jnp.concatenate([a, a], axis=1) for a in (cc, s1, s2)]
    bd = bd_ref[...]
    gw = GROUP_WIDTH

    def normed_rotated(acc, gain):
        ms = _dot((acc * acc).astype(_BF16), bd)
        y = acc * lax.rsqrt(ms + RMS_EPS) * gain
        return y * cc + pltpu.roll(y, gw - ROPE_DIM // 2, 1) * s1 + pltpu.roll(y, ROPE_DIM // 2, 1) * s2

    def store(o_ref, dil, part, y):
        if dil == 1:
            o_ref[:, part * gw:(part + 1) * gw] = y.astype(_BF16)
            return
        for s in range(gw // LANES):
            st_ref[s] = y[:, s * LANES:(s + 1) * LANES]
        for r in range(dil):
            for s in range(gw // LANES):
                col0 = (3 * r + part) * gw + s * LANES
                o_ref[:, col0:col0 + LANES] = st_ref[s, pl.ds(r, tm // dil, stride=dil), :].astype(_BF16)

    def project(col0):
        return _dot(hb, w_ref[:, col0:col0 + gw])

    outs = (o0_ref, o1_ref, o2_ref)
    dils = [dl for _, dl in DIL_PATTERNS]
    normed = [(g, part, part * DIL_WIDTH + g * gw) for g in range(N_DIL_GROUPS) for part in (0, 1)]
    for n, (_, _, col0) in enumerate(normed):
        acc_ref[n] = project(col0)
    plain = [("v", g) for g in range(N_DIL_GROUPS)] + [("sb", part) for part in range(3)]
    for n, (kind, j) in enumerate(plain):
        if kind == "v":
            store(outs[j], dils[j], 2, project(2 * DIL_WIDTH + j * gw))
        else:
            acc = project(3 * DIL_WIDTH + j * gw)
            osb_ref[:, j * gw:(j + 1) * gw] = (acc * ATTN_SCALE if j == 0 else acc).astype(_BF16)
        g, part, _ = normed[n]
        gain = qg_ref[...] if part == 0 else kg_ref[...]
        store(outs[g], dils[g], part, normed_rotated(acc_ref[n], gain))


def _qkv(x2, mod3, g_mix, w_in, qg, kg, trig, bd, seq):
    t, d = x2.shape
    tm = min(QKV_TILE, seq)
    per_b = seq // tm
    row = lambda i: (i, 0)
    const = lambda i: (0, 0)
    width = 3 * GROUP_WIDTH
    dils = [dl for _, dl in DIL_PATTERNS] + [1]
    return pl.pallas_call(
        _qkv_body,
        out_shape=[jax.ShapeDtypeStruct((t // dl, dl * width), _BF16) for dl in dils],
        grid=(t // tm,),
        in_specs=[pl.BlockSpec((tm, d), row),
                  pl.BlockSpec((None, 1, mod3.shape[2]), lambda i: (i // per_b, 0, 0)),
                  pl.BlockSpec((1, d), const),
                  pl.BlockSpec(w_in.shape, const),
                  pl.BlockSpec((1, GROUP_WIDTH), const),
                  pl.BlockSpec((1, GROUP_WIDTH), const),
                  pl.BlockSpec((tm, ROPE_DIM), row),
                  pl.BlockSpec(bd.shape, const)],
        out_specs=[pl.BlockSpec((tm // dl, dl * width), row) for dl in dils],
        scratch_shapes=[pltpu.VMEM((GROUP_WIDTH // LANES, tm, LANES), _F32),
                        pltpu.VMEM((2 * N_DIL_GROUPS, tm, GROUP_WIDTH), _F32)],
        compiler_params=_params("arbitrary"),
        name="qkv",
    )(x2, mod3, g_mix, w_in, qg, kg, trig, bd)


def _dil_body(prev_ref, cur_ref, o_ref, lse_ref, kf_ref, vf_ref):
    tq = cur_ref.shape[0]
    gw = GROUP_WIDTH
    n_res = cur_ref.shape[1] // (3 * gw)
    first = pl.program_id(2) == 0
    for r in range(n_res):
        for part, full_ref in ((1, kf_ref), (2, vf_ref)):
            lanes = slice((3 * r + part) * gw, (3 * r + part + 1) * gw)
            full_ref[r, 0:QBLK, :] = prev_ref[:, lanes]
            full_ref[r, QBLK:, :] = cur_ref[:, lanes]
    nh = HEADS_PER_GROUP
    row = lax.broadcasted_iota(jnp.int32, (nh * QBLK, 2 * QBLK), 0) & (QBLK - 1)
    col = lax.broadcasted_iota(jnp.int32, (nh * QBLK, 2 * QBLK), 1)
    band = (col >= row) & (col <= row + WINDOW_KEYS)
    lane = lax.broadcasted_iota(jnp.int32, (1, GROUP_WIDTH), 1)
    slane = lax.broadcasted_iota(jnp.int32, (1, LANES), 1)
    head_masks = [(lane >= h * HEAD_DIM) & (lane < (h + 1) * HEAD_DIM) for h in range(nh)]
    subs = [(r, j) for r in range(n_res) for j in range(tq // QBLK)]
    rows = [slice(j * QBLK, (j + 1) * QBLK) for _, j in subs]
    window = [slice(j * QBLK, (j + 2) * QBLK) for _, j in subs]
    n = range(len(subs))

    def stack(qj):
        return jnp.concatenate([jnp.where(hm, qj, jnp.zeros_like(qj)) for hm in head_masks], axis=0)

    valid = [band & ((col >= QBLK) | jnp.logical_not(first)) if j == 0 else band for _, j in subs]
    q = [cur_ref[rows[i], 3 * r * gw:(3 * r + 1) * gw] for i, (r, _) in enumerate(subs)]
    s = [jnp.where(valid[i], _dot_nt(stack(q[i]), kf_ref[subs[i][0], window[i], :]), NEG_BIG) for i in n]
    m = [jnp.max(s[i], axis=1, keepdims=True) for i in n]
    p = [jnp.exp(s[i] - m[i]) for i in n]
    l = [jnp.sum(p[i], axis=1, keepdims=True) for i in n]
    o_all = [_dot(p[i].astype(_BF16), vf_ref[subs[i][0], window[i], :]) / l[i] for i in n]
    for i, (r, _) in enumerate(subs):
        lse_all = m[i] + jnp.log(l[i])
        o_acc = jnp.zeros((QBLK, gw), _F32)
        lse_t = jnp.zeros((QBLK, LANES), _F32)
        for h, hm in enumerate(head_masks):
            o_acc = jnp.where(hm, o_all[i][h * QBLK:(h + 1) * QBLK, :], o_acc)
            sm = (slane >= h * LSE_SEG) & (slane < (h + 1) * LSE_SEG)
            lse_t = jnp.where(sm, lse_all[h * QBLK:(h + 1) * QBLK, :], lse_t)
        o_ref[rows[i], r * gw:(r + 1) * gw] = o_acc.astype(_BF16)
        lse_ref[rows[i], r * LANES:(r + 1) * LANES] = lse_t


def _dilated_group(view2, dil, batch, seq):
    sd = seq // dil
    tq = min(DIL_QTILE, sd)
    n_res = min(dil, DIL_QTILE // tq)
    per = tq // QBLK
    gw = GROUP_WIDTH
    view = view2.reshape(batch, sd, dil * 3 * gw)
    o, lse = pl.pallas_call(
        _dil_body,
        out_shape=[jax.ShapeDtypeStruct((batch, sd, dil * gw), _BF16),
                   jax.ShapeDtypeStruct((batch, sd, dil * LANES), _F32)],
        grid=(batch, dil // n_res, sd // tq),
        in_specs=[pl.BlockSpec((None, QBLK, n_res * 3 * gw), lambda b, r, i: (b, jnp.maximum(i * per - 1, 0), r)),
                  pl.BlockSpec((None, tq, n_res * 3 * gw), lambda b, r, i: (b, i, r))],
        out_specs=[pl.BlockSpec((None, tq, n_res * gw), lambda b, r, i: (b, i, r)),
                   pl.BlockSpec((None, tq, n_res * LANES), lambda b, r, i: (b, i, r))],
        scratch_shapes=[pltpu.VMEM((n_res, tq + QBLK, gw), _BF16), pltpu.VMEM((n_res, tq + QBLK, gw), _BF16)],
        compiler_params=_params("arbitrary", "arbitrary", "arbitrary"),
        name=f"dil{dil}",
    )(view, view)
    return o.reshape(batch * sd, dil * gw), lse.reshape(batch * sd, dil * LANES)


def _sb_body(q_ref, k_ref, v_ref, o_ref, carry_ref, acc_ref, qs_ref):
    step_id = pl.program_id(1)
    chains = range(SB_CHAINS)
    blk = [step_id * SB_CHAINS + c for c in chains]
    nh = HEADS_PER_GROUP
    row = lax.broadcasted_iota(jnp.int32, (nh * QBLK, QBLK), 0) & (QBLK - 1)
    col = lax.broadcasted_iota(jnp.int32, (nh * QBLK, QBLK), 1)
    strict = col < row
    ur = lax.broadcasted_iota(jnp.int32, (2 * QBLK, QBLK), 0) & (QBLK - 1)
    uc = lax.broadcasted_iota(jnp.int32, (2 * QBLK, QBLK), 1)
    u = jnp.where(ur > uc, 1.0, 0.0).astype(_BF16)
    lane = lax.broadcasted_iota(jnp.int32, (1, GROUP_WIDTH), 1)
    head_masks = [(lane >= h * HEAD_DIM) & (lane < (h + 1) * HEAD_DIM) for h in range(nh)]
    for c in chains:
        q = q_ref[c * QBLK:(c + 1) * QBLK, :]
        qs_ref[c] = jnp.concatenate([jnp.where(hm, q, jnp.zeros_like(q)) for hm in head_masks], axis=0)

    def softplus(z):
        return jnp.where(z > SOFTPLUS_LINEAR_FROM, z, jnp.log(1.0 + jnp.exp(z)))

    def later_keys(log_1m):
        hi, lo = _split(log_1m)
        return _dot(jnp.concatenate([hi, lo], axis=1), u)

    nb = SB_HEAD_BLOCKS
    cols = [slice(j * QBLK, (j + 1) * QBLK) for j in range(nb)]
    kbs = [[blk[c] - (nb - 1) + j for j in range(nb)] for c in chains]
    starts = [[pl.multiple_of(jnp.maximum(kb, 0) * QBLK, QBLK) for kb in kbs[c]] for c in chains]
    keep = [[strict if j == nb - 1 else (kbs[c][j] >= 0) for j in range(nb)] for c in chains]
    z = [_dot_nt(qs_ref[c], jnp.concatenate([k_ref[pl.ds(s, QBLK), :] for s in starts[c]], axis=0)) for c in chains]
    sp = [softplus(z[c]) for c in chains]
    log_1m = [[jnp.where(keep[c][j], -sp[c][:, cols[j]], 0.0) for j in range(nb)] for c in chains]
    totals = [[jnp.sum(l, axis=1, keepdims=True) for l in log_1m[c]] for c in chains]
    later = [[later_keys(log_1m[c][j]) for j in range(nb)] for c in chains]
    for c in chains:
        a_blocks = []
        after = jnp.zeros_like(totals[c][0])
        for j in reversed(range(nb)):
            a = jnp.exp((z[c][:, cols[j]] - sp[c][:, cols[j]]) + later[c][j] + after)
            a_blocks.insert(0, jnp.where(keep[c][j], a, 0.0).astype(_BF16))
            after = after + totals[c][j]
        acc_ref[c] = _dot(jnp.concatenate(a_blocks, axis=1),
                          jnp.concatenate([v_ref[pl.ds(s, QBLK), :] for s in starts[c]], axis=0))
        carry_ref[c] = after

    for c in chains:
        def tile(kb, c=c):
            start = pl.multiple_of(kb * QBLK, QBLK)
            z = _dot_nt(qs_ref[c], k_ref[pl.ds(start, QBLK), :])
            sp = softplus(z)
            log_1m = -sp
            a = jnp.exp((z - sp) + later_keys(log_1m) + carry_ref[c])
            acc_ref[c] += _dot(a.astype(_BF16), v_ref[pl.ds(start, QBLK), :])
            carry = carry_ref[c] + jnp.sum(log_1m, axis=1, keepdims=True)
            carry_ref[c] = carry
            return jnp.max(carry)

        def cond(st):
            return (st[0] >= 0) & (st[1] > SB_DEAD_LOG)

        def step(st, tile=tile):
            return st[0] - 1, tile(st[0])

        lax.while_loop(cond, step, (blk[c] - nb, jnp.max(carry_ref[c])))
        out = jnp.zeros((QBLK, GROUP_WIDTH), _F32)
        for h, hm in enumerate(head_masks):
            out = jnp.where(hm, acc_ref[c, h * QBLK:(h + 1) * QBLK, :], out)
        o_ref[c * QBLK:(c + 1) * QBLK, :] = out.astype(_BF16)


def _stick_breaking(arr, batch, seq):
    gw = GROUP_WIDTH
    qt = SB_CHAINS * QBLK
    view = arr.reshape(batch, seq, 3 * gw)
    o = pl.pallas_call(
        _sb_body,
        out_shape=jax.ShapeDtypeStruct((batch, seq, gw), _BF16),
        grid=(batch, seq // qt),
        in_specs=[pl.BlockSpec((None, qt, gw), lambda b, i: (b, i, 0)),
                  pl.BlockSpec((None, seq, gw), lambda b, i: (b, 0, 1)),
                  pl.BlockSpec((None, seq, gw), lambda b, i: (b, 0, 2))],
        out_specs=pl.BlockSpec((None, qt, gw), lambda b, i: (b, i, 0)),
        scratch_shapes=[pltpu.VMEM((SB_CHAINS, HEADS_PER_GROUP * QBLK, 1), _F32),
                        pltpu.VMEM((SB_CHAINS, HEADS_PER_GROUP * QBLK, gw), _F32),
                        pltpu.VMEM((SB_CHAINS, HEADS_PER_GROUP * QBLK, gw), _BF16)],
        compiler_params=_params("arbitrary", "arbitrary"),
        name="sb",
    )(view, view, view)
    return o.reshape(batch * seq, gw)


def _merge_body(x_ref, mod_ref, g1_ref, g2_ref, o0_ref, o1_ref, o2_ref, l0_ref, l1_ref, l2_ref, osb_ref,
                wbg_ref, wbd_ref, wbs_ref, wout_ref, wrh_ref, wrl_ref, br_ref, ex_ref, tri_ref,
                x1_ref, h2p_ref, rinfo_ref, er_ref, cnt_out_ref, cnt_ref, os1_ref, os2_ref, ls1_ref, ls2_ref):
    d = D_MODEL
    tm = x_ref.shape[0]
    x = x_ref[...]
    hb = _rms_mod(x, g1_ref[...], mod_ref[:, d:2 * d], mod_ref[:, 0:d]).astype(_BF16)

    def natural(ref, st_ref, dil):
        if dil == 1:
            return ref[...].astype(_F32)
        slabs = st_ref.shape[0]
        for r in range(dil):
            for s in range(slabs):
                col0 = (r * slabs + s) * LANES
                st_ref[s, pl.ds(r, tm // dil, stride=dil), :] = ref[:, col0:col0 + LANES].astype(_F32)
        return jnp.concatenate([st_ref[s] for s in range(slabs)], axis=1)

    dils = [dl for _, dl in DIL_PATTERNS]
    o_nat = [natural(r, s, dl) for r, s, dl in zip((o0_ref, o1_ref, o2_ref), (None, os1_ref, os2_ref), dils)]
    l0, l1, l2 = [natural(r, s, dl) for r, s, dl in zip((l0_ref, l1_ref, l2_ref), (None, ls1_ref, ls2_ref), dils)]

    lmax = jnp.maximum(jnp.maximum(l0, l1), l2)
    e0, e1, e2 = jnp.exp(l0 - lmax), jnp.exp(l1 - lmax), jnp.exp(l2 - lmax)
    inv = 1.0 / (e0 + e1 + e2)
    ex = ex_ref[...]

    def widen(w):
        hi, lo = _split(w)
        return _dot(jnp.concatenate([hi, lo], axis=1), ex)

    w_groups = [widen(e * inv) for e in (e0, e1, e2)]
    gate_dil = jax.nn.sigmoid(_dot(hb, wbg_ref[:, :d]))
    o_dil = w_groups[0] * o_nat[0] + w_groups[1] * o_nat[1] + w_groups[2] * o_nat[2]
    branch_dil = _dot(o_dil.astype(_BF16), wbd_ref[...])
    branch_sb = _dot(osb_ref[...], wbs_ref[...])
    gate_sb = jax.nn.sigmoid(_dot(hb, wbg_ref[:, d:]))
    merged = gate_dil * branch_dil + gate_sb * branch_sb
    x1 = x + mod_ref[:, 2 * d:3 * d] * _dot(merged.astype(_BF16), wout_ref[...])
    x1_ref[...] = x1

    h2 = _rms_mod(x1, g2_ref[...], mod_ref[:, 4 * d:5 * d], mod_ref[:, 3 * d:4 * d])
    h2p_ref[...] = _pack_halves(h2)

    hh, hl = _split(h2)
    logits = _dot(hh, wrh_ref[...]) + (_dot(hl, wrh_ref[...]) + _dot(hh, wrl_ref[...])) + br_ref[...]
    lane = lax.broadcasted_iota(jnp.int32, (tm, ROUTER_LANES), 1).astype(_F32)
    far = float(ROUTER_LANES)

    def top(vals):
        m = jnp.max(vals, axis=1, keepdims=True)
        return m, jnp.min(jnp.where(vals == m, lane, far), axis=1, keepdims=True)

    is_group = lane < N_GROUPS
    mg, gsel = top(jnp.where(is_group, logits, NEG_BIG))
    pg_top = 1.0 / jnp.sum(jnp.where(is_group, jnp.exp(logits - mg), 0.0), axis=1, keepdims=True)
    lane0 = EXPERT_LANE0 + EXPERTS_PER_GROUP * gsel
    le = jnp.where((lane >= lane0) & (lane < lane0 + EXPERTS_PER_GROUP), logits, NEG_BIG)
    m1, i1 = top(le)
    m2, i2 = top(jnp.where(lane == i1, NEG_BIG, le))
    t2 = jnp.exp(m2 - m1)
    w0 = pg_top / (1.0 + t2)
    w1 = pg_top * t2 / (1.0 + t2)

    @pl.when(pl.program_id(0) == 0)
    def _():
        cnt_ref[...] = jnp.zeros_like(cnt_ref)

    sel0, sel1 = lane == i1, lane == i2
    onehot = jnp.where(sel0 | sel1, 1.0, 0.0)
    before = _dot(tri_ref[...], onehot.astype(_BF16)) + cnt_ref[0:1, :]
    r0 = jnp.sum(jnp.where(sel0, before, 0.0), axis=1, keepdims=True)
    r1 = jnp.sum(jnp.where(sel1, before, 0.0), axis=1, keepdims=True)
    cnt_ref[...] += jnp.sum(onehot, axis=0, keepdims=True)
    cnt_out_ref[...] = cnt_ref[...]
    cols = (i1 - EXPERT_LANE0, i2 - EXPERT_LANE0, r0, r1, w0, w1)
    rinfo = jnp.zeros((tm, ROUTER_LANES), _F32)
    for c, v in enumerate(cols):
        rinfo = jnp.where(lane == float(c), v, rinfo)
    rinfo_ref[...] = rinfo
    er_ref[...] = jnp.transpose(rinfo)[0:8, :].astype(jnp.int32)


def _merge(x2, mod3, g1, g2, outs, lses, osb, wbg, wbd, wbs, wout, wrh, wrl, br, ex, tri, seq, part):
    d = x2.shape[1]
    tm = ROW_TILE
    per_b = seq // tm
    t = x2.shape[0] // MOE_PARTS
    first = part * (t // tm)
    src = lambda i: (i + first, 0)
    row = lambda i: (i, 0)
    const = lambda i: (0, 0)
    full = lambda a: pl.BlockSpec(a.shape, const)
    gw = GROUP_WIDTH
    dils = [dl for _, dl in DIL_PATTERNS]
    return pl.pallas_call(
        _merge_body,
        out_shape=[jax.ShapeDtypeStruct((t, d), _F32),
                   jax.ShapeDtypeStruct((t, d // 2), jnp.uint32),
                   jax.ShapeDtypeStruct((t, ROUTER_LANES), _F32),
                   jax.ShapeDtypeStruct((8, t), jnp.int32),
                   jax.ShapeDtypeStruct((8, ROUTER_LANES), _F32)],
        grid=(t // tm,),
        in_specs=[pl.BlockSpec((tm, d), src),
                  pl.BlockSpec((None, 1, mod3.shape[2]), lambda i: ((i + first) // per_b, 0, 0)),
                  full(g1), full(g2)]
                 + [pl.BlockSpec((tm // dl, dl * gw), src) for dl in dils]
                 + [pl.BlockSpec((tm // dl, dl * LANES), src) for dl in dils]
                 + [pl.BlockSpec((tm, gw), src)]
                 + [full(a) for a in (wbg, wbd, wbs, wout, wrh, wrl, br, ex, tri)],
        out_specs=[pl.BlockSpec((tm, d), row),
                   pl.BlockSpec((tm, d // 2), row),
                   pl.BlockSpec((tm, ROUTER_LANES), row),
                   pl.BlockSpec((8, tm), lambda i: (0, i)),
                   pl.BlockSpec((8, ROUTER_LANES), const)],
        scratch_shapes=[pltpu.VMEM((8, ROUTER_LANES), _F32),
                        pltpu.VMEM((gw // LANES, tm, LANES), _F32), pltpu.VMEM((gw // LANES, tm, LANES), _F32),
                        pltpu.VMEM((1, tm, LANES), _F32), pltpu.VMEM((1, tm, LANES), _F32)],
        compiler_params=_params("arbitrary"),
        name="merge",
    )(x2, mod3, g1, g2, *outs, *lses, osb, wbg, wbd, wbs, wout, wrh, wrl, br, ex, tri)


def _dest_body(ps_ref, er_ref, d_ref):
    e = er_ref[0:2, :]
    start = jnp.zeros_like(e)
    for x in range(N_EXPERTS):
        start = jnp.where(e == x, ps_ref[x], start)
    d_ref[...] = start + er_ref[2:4, :]


def _dest(pstart, er):
    t = er.shape[1]
    tw = min(DEST_TILE, t)
    return pl.pallas_call(
        _dest_body,
        out_shape=jax.ShapeDtypeStruct((2, t), jnp.int32),
        grid_spec=pltpu.PrefetchScalarGridSpec(
            num_scalar_prefetch=1,
            grid=(t // tw,),
            in_specs=[pl.BlockSpec((8, tw), lambda i, ps: (0, i))],
            out_specs=pl.BlockSpec((2, tw), lambda i, ps: (0, i))),
        compiler_params=_params("arbitrary"),
        name="dest",
    )(pstart, er)


def _sc_mesh():
    return plsc.VectorSubcoreMesh(core_axis_name="core", subcore_axis_name="subcore",
                                  num_cores=SC_CORES, num_subcores=SC_SUBCORES)


def _sc_worker():
    return lax.axis_index("subcore") * SC_CORES + lax.axis_index("core")


def _sc_scatter(x, idx0, idx1, n_slots):
    chunks = idx0.shape[0]
    per = chunks // SC_WORKERS
    win = idx0.shape[1]

    @functools.partial(
        pl.kernel, mesh=_sc_mesh(), out_type=jax.ShapeDtypeStruct((n_slots, x.shape[1]), x.dtype),
        scratch_types=[pltpu.VMEM((1, win), jnp.int32), pltpu.VMEM((1, win), jnp.int32),
                       pltpu.VMEM((win, x.shape[1]), x.dtype), pltpu.SemaphoreType.DMA],
        name="sc_scatter")
    def run(x_hbm, i0_hbm, i1_hbm, o_hbm, i0_v, i1_v, rows_v, sem):
        wid = _sc_worker()

        @pl.loop(0, per)
        def _(j):
            c = wid * per + j
            pltpu.sync_copy(i0_hbm.at[pl.ds(c, 1)], i0_v)
            pltpu.sync_copy(i1_hbm.at[pl.ds(c, 1)], i1_v)
            pltpu.sync_copy(x_hbm.at[pl.ds(c * win, win)], rows_v)
            first = pltpu.async_copy(rows_v, o_hbm.at[i0_v.at[0]], sem)
            second = pltpu.async_copy(rows_v, o_hbm.at[i1_v.at[0]], sem)
            first.wait()
            second.wait()

    return run(x, idx0, idx1)


def _sc_gather(table, idx):
    chunks, win = idx.shape
    per = chunks // SC_WORKERS

    @functools.partial(
        pl.kernel, mesh=_sc_mesh(), out_type=jax.ShapeDtypeStruct((chunks * win, table.shape[1]), table.dtype),
        scratch_types=[pltpu.VMEM((1, win), jnp.int32), pltpu.VMEM((win, table.shape[1]), table.dtype),
                       pltpu.SemaphoreType.DMA],
        name="sc_gather")
    def run(t_hbm, i_hbm, o_hbm, i_v, rows_v, sem):
        wid = _sc_worker()

        @pl.loop(0, per)
        def _(j):
            c = wid * per + j
            pltpu.sync_copy(i_hbm.at[pl.ds(c, 1)], i_v)
            pltpu.async_copy(t_hbm.at[i_v.at[0]], rows_v, sem).wait()
            pltpu.sync_copy(rows_v, o_hbm.at[pl.ds(c * win, win)])

    return run(table, idx)


def _pack_halves(a):
    h = a.shape[1] // 2
    lo = lax.bitcast_convert_type(a[:, :h].astype(_BF16).astype(_F32), jnp.uint32) >> 16
    hi = lax.bitcast_convert_type(a[:, h:].astype(_BF16).astype(_F32), jnp.uint32) & jnp.uint32(0xFFFF0000)
    return lo | hi


def _unpack_halves(w):
    return jnp.concatenate(
        [lax.bitcast_convert_type(w << 16, _F32), lax.bitcast_convert_type(w & jnp.uint32(0xFFFF0000), _F32)], axis=1)


def _experts_body(ce_ref, nv_ref, nu_ref, seg_ref, nxt_ref, xs_ref, wg_ref, wu_ref, wd_ref, ys_ref,
                  wgb_ref, wub_ref, wdb_ref, wgf_ref, wuf_ref, wdf_ref, sem):
    c = pl.program_id(0)

    def fetch(expert, slot):
        return [pltpu.make_async_copy(src.at[expert], dst.at[slot], sem.at[slot])
                for src, dst in ((wg_ref, wgf_ref), (wu_ref, wuf_ref), (wd_ref, wdf_ref))]

    @pl.when(c < nu_ref[0])
    def _():
        @pl.when(seg_ref[c] >= 0)
        def _():
            slot = seg_ref[c] & 1

            @pl.when(c == 0)
            def _():
                for copy in fetch(ce_ref[0], 0):
                    copy.start()

            for copy in fetch(ce_ref[c], slot):
                copy.wait()
            wgb_ref[...] = wgf_ref[slot].astype(_BF16)
            wub_ref[...] = wuf_ref[slot].astype(_BF16)
            wdb_ref[...] = wdf_ref[slot].astype(_BF16)

            @pl.when(nxt_ref[c] >= 0)
            def _():
                for copy in fetch(nxt_ref[c], 1 - slot):
                    copy.start()

        row = lax.broadcasted_iota(jnp.int32, xs_ref.shape, 0)
        x = _unpack_halves(jnp.where(row < nv_ref[c], xs_ref[...], jnp.uint32(0))).astype(_BF16)
        half = x.shape[0] // 2
        ups = [(_dot(x[r:r + half], wgb_ref[...]), _dot(x[r:r + half], wub_ref[...])) for r in (0, half)]
        for (g, u), r in zip(ups, (0, half)):
            hmid = (g * jax.nn.sigmoid(g)) * u
            ys_ref[r:r + half, :] = _pack_halves(_dot(hmid.astype(_BF16), wdb_ref[...]))


def _experts(chunk_e, n_valid, n_used, seg, nxt, xs, wg, wu, wd):
    n_slots, w = xs.shape
    ch = EXPERT_CHUNK
    d, de = wg.shape[1], wg.shape[2]
    slot = lambda c, ce, nv, nu, sg, nx: (jnp.minimum(c, nu[0] - 1), 0)
    hbm = pl.BlockSpec(memory_space=pl.ANY)
    return pl.pallas_call(
        _experts_body,
        out_shape=jax.ShapeDtypeStruct((n_slots, d // 2), jnp.uint32),
        grid_spec=pltpu.PrefetchScalarGridSpec(
            num_scalar_prefetch=5,
            grid=(n_slots // ch,),
            in_specs=[pl.BlockSpec((ch, w), slot), hbm, hbm, hbm],
            out_specs=pl.BlockSpec((ch, d // 2), slot),
            scratch_shapes=[pltpu.VMEM((d, de), _BF16), pltpu.VMEM((d, de), _BF16), pltpu.VMEM((de, d), _BF16),
                            pltpu.VMEM((2, d, de), _F32), pltpu.VMEM((2, d, de), _F32), pltpu.VMEM((2, de, d), _F32),
                            pltpu.SemaphoreType.DMA((2,))]),
        compiler_params=_params("arbitrary"),
        name="experts",
    )(chunk_e, n_valid, n_used, seg, nxt, xs, wg, wu, wd)


def _combine_body(x1_ref, rinfo_ref, mod_ref, y0_ref, y1_ref, *rest):
    o_ref = rest[-1]
    y = rinfo_ref[:, 4:5] * _unpack_halves(y0_ref[...]) + rinfo_ref[:, 5:6] * _unpack_halves(y1_ref[...])
    o_ref[...] = x1_ref[...] + mod_ref[:, 5 * D_MODEL:6 * D_MODEL] * y


def _combine(x1, rinfo, mod3, gathered, seq, part, out_so_far):
    t, d = x1.shape
    tf = min(COMBINE_TILE, seq)
    per_b = seq // tf
    nt = t // tf
    first = part * nt
    in_specs = [pl.BlockSpec((tf, d), lambda i: (i, 0)),
                pl.BlockSpec((tf, ROUTER_LANES), lambda i: (i, 0)),
                pl.BlockSpec((None, 1, mod3.shape[2]), lambda i: ((i + first) // per_b, 0, 0)),
                pl.BlockSpec((tf, d // 2), lambda i: (i, 0)),
                pl.BlockSpec((tf, d // 2), lambda i: (i + nt, 0))]
    args = [x1, rinfo, mod3, gathered, gathered]
    aliases = {}
    if out_so_far is not None:
        in_specs.append(pl.BlockSpec(memory_space=pl.ANY))
        args.append(out_so_far)
        aliases = {len(args) - 1: 0}
    return pl.pallas_call(
        _combine_body,
        out_shape=jax.ShapeDtypeStruct((t * MOE_PARTS, d), _F32),
        grid=(nt,),
        in_specs=in_specs,
        out_specs=pl.BlockSpec((tf, d), lambda i: (i + first, 0)),
        input_output_aliases=aliases,
        compiler_params=_params("arbitrary"),
        name="combine",
    )(*args)


def _rope_trig(positions):
    inv_freq = ROPE_THETA ** (-jnp.arange(0, ROPE_DIM, 2, dtype=_F32) / ROPE_DIM)
    ang = positions.reshape(-1).astype(_F32)[:, None] * inv_freq
    return jnp.concatenate([jnp.cos(ang), jnp.sin(ang)], axis=1)


def _layer(x, mod, positions, g_mix, g_ffn, w_in, w_bg, qg, kg, w_bd, w_bs, w_out, w_rg, b_rg, w_re, b_re,
           w_eg, w_eu, w_ed):
    batch, seq, d = x.shape
    t = batch * seq
    x2 = x.reshape(t, d)
    mod3 = mod.reshape(batch, 1, mod.shape[1])
    gw = GROUP_WIDTH

    lane = jnp.arange(gw)
    bd = jnp.where(lane[:, None] // HEAD_DIM == lane[None, :] // HEAD_DIM, 1.0 / HEAD_DIM, 0.0).astype(_BF16)
    ex = (jnp.arange(LANES)[:, None] == (lane[None, :] // HEAD_DIM) * LSE_SEG).astype(_BF16)
    ex = jnp.concatenate([ex, ex], axis=0)
    tri = (jnp.arange(ROW_TILE)[:, None] > jnp.arange(ROW_TILE)[None, :]).astype(_BF16)
    tile4 = lambda g: jnp.tile(g.astype(_F32), HEADS_PER_GROUP).reshape(1, gw)
    wr = jnp.zeros((d, ROUTER_LANES), _F32).at[:, :N_GROUPS].set(w_rg).at[:, N_GROUPS:N_GROUPS + N_EXPERTS].set(w_re)
    wrh = wr.astype(_BF16)
    wrl = (wr - wrh.astype(_F32)).astype(_BF16)
    br = jnp.zeros((1, ROUTER_LANES), _F32).at[0, :N_GROUPS].set(b_rg).at[0, N_GROUPS:N_GROUPS + N_EXPERTS].set(b_re)

    d0, d1, d2, sbp = _qkv(x2, mod3, g_mix.reshape(1, d), w_in.astype(_BF16), tile4(qg) * ATTN_SCALE, tile4(kg),
                           _rope_trig(positions), bd, seq)
    dil = [_dilated_group(a, dl, batch, seq) for a, (_, dl) in zip((d0, d1, d2), DIL_PATTERNS)]
    osb = _stick_breaking(sbp, batch, seq)

    merge_weights = (w_bg.astype(_BF16), w_bd.astype(_BF16), w_bs.astype(_BF16), w_out.astype(_BF16))
    tp = t // MOE_PARTS
    ch = EXPERT_CHUNK
    win = SC_INDEX_WINDOW
    n_chunks = -(-2 * tp // ch) + N_EXPERTS
    chunk_start = jnp.arange(n_chunks, dtype=jnp.int32) * ch
    expert_ids = jnp.arange(N_EXPERTS, dtype=jnp.int32)
    out = None
    for part in range(MOE_PARTS):
        x1, h2p, rinfo, er, cnt = _merge(
            x2, mod3, g_mix.reshape(1, d), g_ffn.reshape(1, d), [o for o, _ in dil], [l for _, l in dil], osb,
            *merge_weights, wrh, wrl, br, ex, tri, seq, part)

        counts = cnt[0, EXPERT_LANE0:EXPERT_LANE0 + N_EXPERTS].astype(jnp.int32)
        padded = (counts + ch - 1) // ch * ch
        pend = jnp.cumsum(padded)
        pstart = pend - padded
        chunk_e = jnp.minimum(jnp.sum((pend[None, :] <= chunk_start[:, None]).astype(jnp.int32), axis=1),
                              N_EXPERTS - 1)
        n_used = (pend[-1:] // ch).astype(jnp.int32)
        begin = chunk_start[:, None]
        inside = (pstart[None, :] <= begin) & (begin < pend[None, :])
        n_valid = jnp.sum(jnp.where(inside, jnp.clip(counts[None, :] - (begin - pstart[None, :]), 0, ch), 0), axis=1)

        dest = _dest(pstart, er)
        xs = _sc_scatter(h2p, dest[0].reshape(tp // win, win), dest[1].reshape(tp // win, win), n_chunks * ch)
        first = (chunk_e != jnp.concatenate([jnp.full((1,), -1, jnp.int32), chunk_e[:-1]])) & (chunk_start < pend[-1])
        seg_no = jnp.cumsum(first.astype(jnp.int32)) - 1
        seg = jnp.where(first, seg_no, -1 - seg_no)
        later = (expert_ids[None, :] > expert_ids[:, None]) & (padded > 0)[None, :]
        next_expert = jnp.min(jnp.where(later, expert_ids[None, :], N_EXPERTS), axis=1)
        next_expert = jnp.where(next_expert == N_EXPERTS, -1, next_expert)
        nxt = jnp.sum(jnp.where(chunk_e[:, None] == expert_ids[None, :], next_expert[None, :], 0), axis=1)
        ys = _experts(chunk_e, n_valid, n_used, seg.astype(jnp.int32), nxt.astype(jnp.int32), xs, w_eg, w_eu, w_ed)
        gathered = _sc_gather(ys, dest.reshape(2 * tp // win, win))
        out = _combine(x1, rinfo, mod3, gathered, seq, part, out)
    return out.reshape(batch, seq, d)


def kernel(x, c, positions, w_ada, b_ada, g_norm_mix, g_norm_ffn, w_in, w_branch_gate, q_norm_g, k_norm_g,
           w_branch_dil, w_branch_sb, w_out, w_router_group, b_router_group, w_router_expert, b_router_expert,
           w_expert_gate, w_expert_up, w_expert_down):
    for l in range(w_ada.shape[0]):
        mod = _ada(c, w_ada[l], b_ada[l])
        x = _layer(x, mod, positions, g_norm_mix[l], g_norm_ffn[l], w_in[l], w_branch_gate[l], q_norm_g[l],
                   k_norm_g[l], w_branch_dil[l], w_branch_sb[l], w_out[l], w_router_group[l], b_router_group[l],
                   w_router_expert[l], b_router_expert[l], w_expert_gate[l], w_expert_up[l], w_expert_down[l])
    return x
```

```python
import functools

import jax
import jax.numpy as jnp
from jax import lax
from jax.experimental import pallas as pl
from jax.experimental.pallas import tpu as pltpu
from jax.experimental.pallas import tpu_sc as plsc

D_MODEL = 1024
HEAD_DIM = 64
DIL_PATTERNS = ((128, 1), (512, 4), (2048, 16))
HEADS_PER_GROUP = 4
GROUP_WIDTH = HEADS_PER_GROUP * HEAD_DIM
N_DIL_GROUPS = len(DIL_PATTERNS)
DIL_WIDTH = N_DIL_GROUPS * GROUP_WIDTH
QKV_WIDTH = 3 * DIL_WIDTH + 3 * GROUP_WIDTH
WINDOW_KEYS = 128
ROPE_THETA = 500000.0
ROPE_DIM = HEAD_DIM // 4
N_GROUPS = 4
EXPERTS_PER_GROUP = 8
N_EXPERTS = N_GROUPS * EXPERTS_PER_GROUP
D_EXPERT = 512
RMS_EPS = 1e-6
ATTN_SCALE = HEAD_DIM ** -0.5

LANES = 128
ROUTER_LANES = LANES
EXPERT_LANE0 = N_GROUPS
LSE_SEG = LANES // HEADS_PER_GROUP
NEG_BIG = -1e30
SB_DEAD_LOG = -120.0
SB_HEAD_BLOCKS = 3
SB_CHAINS = 8

ROW_TILE = 512
QKV_TILE = 1024
QBLK = 128
DIL_QTILE = 2048
EXPERT_CHUNK = 512
EXPERT_IN_BUFFERS = 3
COMBINE_TILE = 1024
DEST_TILE = 8192
SC_CORES = 2
SC_SUBCORES = 16
SC_WORKERS = SC_CORES * SC_SUBCORES
SC_INDEX_WINDOW = 128
MOE_PARTS = 2
VMEM_LIMIT = 48 * 1024 * 1024

_BF16 = jnp.bfloat16
_F32 = jnp.float32
_NT = (((1,), (1,)), ((), ()))


def _dot(a, b):
    return jnp.dot(a, b, preferred_element_type=_F32)


def _dot_nt(a, b):
    return lax.dot_general(a, b, _NT, preferred_element_type=_F32)


def _split(a):
    hi = a.astype(_BF16)
    lo = (a - hi.astype(_F32)).astype(_BF16)
    return hi, lo


def _dot3(a, b):
    ah, al = _split(a)
    bh, bl = _split(b)
    return _dot(ah, bh) + (_dot(ah, bl) + _dot(al, bh))


def _rms_mod(x, g, scale, shift):
    y = x * lax.rsqrt(jnp.mean(x * x, axis=-1, keepdims=True) + RMS_EPS)
    return y * g * (1.0 + scale) + shift


def _params(*sem):
    return pltpu.CompilerParams(dimension_semantics=sem, vmem_limit_bytes=VMEM_LIMIT)


def _ada_body(c_ref, w_ref, b_ref, o_ref):
    c = c_ref[...]
    o_ref[...] = _dot3(c * jax.nn.sigmoid(c), w_ref[...]) + b_ref[...]


def _ada(c, w_ada, b_ada):
    b, d = c.shape
    n = w_ada.shape[1]
    rows = -(-b // 16) * 16
    cp = jnp.zeros((rows, d), _F32).at[:b].set(c)
    nt = 1536
    out = pl.pallas_call(
        _ada_body,
        out_shape=jax.ShapeDtypeStruct((rows, n), _F32),
        grid=(n // nt,),
        in_specs=[pl.BlockSpec((rows, d), lambda j: (0, 0)),
                  pl.BlockSpec((d, nt), lambda j: (0, j)),
                  pl.BlockSpec((1, nt), lambda j: (0, j))],
        out_specs=pl.BlockSpec((rows, nt), lambda j: (0, j)),
        compiler_params=_params("arbitrary"),
        name="ada",
    )(cp, w_ada, b_ada.reshape(1, n))
    return out[:b]


def _qkv_body(x_ref, mod_ref, g_ref, w_ref, qg_ref, kg_ref, trig_ref, bd_ref,
              o0_ref, o1_ref, o2_ref, osb_ref, st_ref, acc_ref):
    d = D_MODEL
    tm = x_ref.shape[0]
    h = _rms_mod(x_ref[...], g_ref[...], mod_ref[:, d:2 * d], mod_ref[:, 0:d])
    hb = h.astype(_BF16)
    half = ROPE_DIM // 2
    in_head = lax.broadcasted_iota(jnp.int32, (tm, LANES), 1) & (HEAD_DIM - 1)
    freq = in_head & (half - 1)
    trig = jnp.concatenate([trig_ref[...], jnp.zeros((tm, LANES - ROPE_DIM), _F32)], axis=1)
    cos = jnp.take_along_axis(trig, freq, axis=1)
    sin = jnp.take_along_axis(trig, freq + half, axis=1)
    cc = jnp.where(in_head < ROPE_DIM, cos, 1.0)
    s1 = jnp.where(in_head < half, -sin, 0.0)
    s2 = jnp.where((in_head >= half) & (in_head < ROPE_DIM), sin, 0.0)
    cc, s1, s2 = [jnp.concatenate([a, a], axis=1) for a in (cc, s1, s2)]
    bd = bd_ref[...]
    gw = GROUP_WIDTH

    def normed_rotated(acc, gain):
        ms = _dot((acc * acc).astype(_BF16), bd)
        y = acc * lax.rsqrt(ms + RMS_EPS) * gain
        return y * cc + pltpu.roll(y, gw - ROPE_DIM // 2, 1) * s1 + pltpu.roll(y, ROPE_DIM // 2, 1) * s2

    def store(o_ref, dil, part, y):
        if dil == 1:
            o_ref[:, part * gw:(part + 1) * gw] = y.astype(_BF16)
            return
        for s in range(gw // LANES):
            st_ref[s] = y[:, s * LANES:(s + 1) * LANES]
        for r in range(dil):
            for s in range(gw // LANES):
                col0 = (3 * r + part) * gw + s * LANES
                o_ref[:, col0:col0 + LANES] = st_ref[s, pl.ds(r, tm // dil, stride=dil), :].astype(_BF16)

    def project(col0):
        return _dot(hb, w_ref[:, col0:col0 + gw])

    outs = (o0_ref, o1_ref, o2_ref)
    dils = [dl for _, dl in DIL_PATTERNS]
    normed = [(g, part, part * DIL_WIDTH + g * gw) for g in range(N_DIL_GROUPS) for part in (0, 1)]
    for n, (_, _, col0) in enumerate(normed):
        acc_ref[n] = project(col0)
    plain = [("v", g) for g in range(N_DIL_GROUPS)] + [("sb", part) for part in range(3)]
    for n, (kind, j) in enumerate(plain):
        if kind == "v":
            store(outs[j], dils[j], 2, project(2 * DIL_WIDTH + j * gw))
        else:
            acc = project(3 * DIL_WIDTH + j * gw)
            osb_ref[:, j * gw:(j + 1) * gw] = (acc * ATTN_SCALE if j == 0 else acc).astype(_BF16)
        g, part, _ = normed[n]
        gain = qg_ref[...] if part == 0 else kg_ref[...]
        store(outs[g], dils[g], part, normed_rotated(acc_ref[n], gain))


def _qkv(x2, mod3, g_mix, w_in, qg, kg, trig, bd, seq):
    t, d = x2.shape
    tm = min(QKV_TILE, seq)
    per_b = seq // tm
    row = lambda i: (i, 0)
    const = lambda i: (0, 0)
    width = 3 * GROUP_WIDTH
    dils = [dl for _, dl in DIL_PATTERNS] + [1]
    return pl.pallas_call(
        _qkv_body,
        out_shape=[jax.ShapeDtypeStruct((t // dl, dl * width), _BF16) for dl in dils],
        grid=(t // tm,),
        in_specs=[pl.BlockSpec((tm, d), row),
                  pl.BlockSpec((None, 1, mod3.shape[2]), lambda i: (i // per_b, 0, 0)),
                  pl.BlockSpec((1, d), const),
                  pl.BlockSpec(w_in.shape, const),
                  pl.BlockSpec((1, GROUP_WIDTH), const),
                  pl.BlockSpec((1, GROUP_WIDTH), const),
                  pl.BlockSpec((tm, ROPE_DIM), row),
                  pl.BlockSpec(bd.shape, const)],
        out_specs=[pl.BlockSpec((tm // dl, dl * width), row) for dl in dils],
        scratch_shapes=[pltpu.VMEM((GROUP_WIDTH // LANES, tm, LANES), _F32),
                        pltpu.VMEM((2 * N_DIL_GROUPS, tm, GROUP_WIDTH), _F32)],
        compiler_params=_params("arbitrary"),
        name="qkv",
    )(x2, mod3, g_mix, w_in, qg, kg, trig, bd)


def _dil_body(prev_ref, cur_ref, o_ref, lse_ref, kf_ref, vf_ref):
    tq = cur_ref.shape[0]
    gw = GROUP_WIDTH
    n_res = cur_ref.shape[1] // (3 * gw)
    first = pl.program_id(2) == 0
    for r in range(n_res):
        for part, full_ref in ((1, kf_ref), (2, vf_ref)):
            lanes = slice((3 * r + part) * gw, (3 * r + part + 1) * gw)
            full_ref[r, 0:QBLK, :] = prev_ref[:, lanes]
            full_ref[r, QBLK:, :] = cur_ref[:, lanes]
    nh = HEADS_PER_GROUP
    row = lax.broadcasted_iota(jnp.int32, (nh * QBLK, 2 * QBLK), 0) & (QBLK - 1)
    col = lax.broadcasted_iota(jnp.int32, (nh * QBLK, 2 * QBLK), 1)
    band = (col >= row) & (col <= row + WINDOW_KEYS)
    lane = lax.broadcasted_iota(jnp.int32, (1, GROUP_WIDTH), 1)
    slane = lax.broadcasted_iota(jnp.int32, (1, LANES), 1)
    head_masks = [(lane >= h * HEAD_DIM) & (lane < (h + 1) * HEAD_DIM) for h in range(nh)]
    subs = [(r, j) for r in range(n_res) for j in range(tq // QBLK)]
    rows = [slice(j * QBLK, (j + 1) * QBLK) for _, j in subs]
    window = [slice(j * QBLK, (j + 2) * QBLK) for _, j in subs]
    n = range(len(subs))

    def stack(qj):
        return jnp.concatenate([jnp.where(hm, qj, jnp.zeros_like(qj)) for hm in head_masks], axis=0)

    valid = [band & ((col >= QBLK) | jnp.logical_not(first)) if j == 0 else band for _, j in subs]
    q = [cur_ref[rows[i], 3 * r * gw:(3 * r + 1) * gw] for i, (r, _) in enumerate(subs)]
    s = [jnp.where(valid[i], _dot_nt(stack(q[i]), kf_ref[subs[i][0], window[i], :]), NEG_BIG) for i in n]
    m = [jnp.max(s[i], axis=1, keepdims=True) for i in n]
    p = [jnp.exp(s[i] - m[i]) for i in n]
    l = [jnp.sum(p[i], axis=1, keepdims=True) for i in n]
    o_all = [_dot(p[i].astype(_BF16), vf_ref[subs[i][0], window[i], :]) / l[i] for i in n]
    for i, (r, _) in enumerate(subs):
        lse_all = m[i] + jnp.log(l[i])
        o_acc = jnp.zeros((QBLK, gw), _F32)
        lse_t = jnp.zeros((QBLK, LANES), _F32)
        for h, hm in enumerate(head_masks):
            o_acc = jnp.where(hm, o_all[i][h * QBLK:(h + 1) * QBLK, :], o_acc)
            sm = (slane >= h * LSE_SEG) & (slane < (h + 1) * LSE_SEG)
            lse_t = jnp.where(sm, lse_all[h * QBLK:(h + 1) * QBLK, :], lse_t)
        o_ref[rows[i], r * gw:(r + 1) * gw] = o_acc.astype(_BF16)
        lse_ref[rows[i], r * LANES:(r + 1) * LANES] = lse_t


def _dilated_group(view2, dil, batch, seq):
    sd = seq // dil
    tq = min(DIL_QTILE, sd)
    n_res = min(dil, DIL_QTILE // tq)
    per = tq // QBLK
    gw = GROUP_WIDTH
    view = view2.reshape(batch, sd, dil * 3 * gw)
    o, lse = pl.pallas_call(
        _dil_body,
        out_shape=[jax.ShapeDtypeStruct((batch, sd, dil * gw), _BF16),
                   jax.ShapeDtypeStruct((batch, sd, dil * LANES), _F32)],
        grid=(batch, dil // n_res, sd // tq),
        in_specs=[pl.BlockSpec((None, QBLK, n_res * 3 * gw), lambda b, r, i: (b, jnp.maximum(i * per - 1, 0), r)),
                  pl.BlockSpec((None, tq, n_res * 3 * gw), lambda b, r, i: (b, i, r))],
        out_specs=[pl.BlockSpec((None, tq, n_res * gw), lambda b, r, i: (b, i, r)),
                   pl.BlockSpec((None, tq, n_res * LANES), lambda b, r, i: (b, i, r))],
        scratch_shapes=[pltpu.VMEM((n_res, tq + QBLK, gw), _BF16), pltpu.VMEM((n_res, tq + QBLK, gw), _BF16)],
        compiler_params=_params("arbitrary", "arbitrary", "arbitrary"),
        name=f"dil{dil}",
    )(view, view)
    return o.reshape(batch * sd, dil * gw), lse.reshape(batch * sd, dil * LANES)


def _sb_body(q_ref, k_ref, v_ref, o_ref, carry_ref, acc_ref, qs_ref):
    step_id = pl.program_id(1)
    chains = range(SB_CHAINS)
    blk = [step_id * SB_CHAINS + c for c in chains]
    nh = HEADS_PER_GROUP
    row = lax.broadcasted_iota(jnp.int32, (nh * QBLK, QBLK), 0) & (QBLK - 1)
    col = lax.broadcasted_iota(jnp.int32, (nh * QBLK, QBLK), 1)
    strict = col < row
    ur = lax.broadcasted_iota(jnp.int32, (2 * QBLK, QBLK), 0) & (QBLK - 1)
    uc = lax.broadcasted_iota(jnp.int32, (2 * QBLK, QBLK), 1)
    u = jnp.where(ur > uc, 1.0, 0.0).astype(_BF16)
    lane = lax.broadcasted_iota(jnp.int32, (1, GROUP_WIDTH), 1)
    head_masks = [(lane >= h * HEAD_DIM) & (lane < (h + 1) * HEAD_DIM) for h in range(nh)]
    for c in chains:
        q = q_ref[c * QBLK:(c + 1) * QBLK, :]
        qs_ref[c] = jnp.concatenate([jnp.where(hm, q, jnp.zeros_like(q)) for hm in head_masks], axis=0)

    def softplus(z):
        return jnp.maximum(z, 0.0) + jnp.log(1.0 + jnp.exp(-jnp.abs(z)))

    def later_keys(log_1m):
        hi, lo = _split(log_1m)
        return _dot(jnp.concatenate([hi, lo], axis=1), u)

    nb = SB_HEAD_BLOCKS
    cols = [slice(j * QBLK, (j + 1) * QBLK) for j in range(nb)]
    kbs = [[blk[c] - (nb - 1) + j for j in range(nb)] for c in chains]
    starts = [[pl.multiple_of(jnp.maximum(kb, 0) * QBLK, QBLK) for kb in kbs[c]] for c in chains]
    keep = [[strict if j == nb - 1 else (kbs[c][j] >= 0) for j in range(nb)] for c in chains]
    z = [_dot_nt(qs_ref[c], jnp.concatenate([k_ref[pl.ds(s, QBLK), :] for s in starts[c]], axis=0)) for c in chains]
    sp = [softplus(z[c]) for c in chains]
    log_1m = [[jnp.where(keep[c][j], -sp[c][:, cols[j]], 0.0) for j in range(nb)] for c in chains]
    totals = [[jnp.sum(l, axis=1, keepdims=True) for l in log_1m[c]] for c in chains]
    later = [[later_keys(log_1m[c][j]) for j in range(nb)] for c in chains]
    for c in chains:
        a_blocks = []
        after = jnp.zeros_like(totals[c][0])
        for j in reversed(range(nb)):
            a = jnp.exp((z[c][:, cols[j]] - sp[c][:, cols[j]]) + later[c][j] + after)
            a_blocks.insert(0, jnp.where(keep[c][j], a, 0.0).astype(_BF16))
            after = after + totals[c][j]
        acc_ref[c] = _dot(jnp.concatenate(a_blocks, axis=1),
                          jnp.concatenate([v_ref[pl.ds(s, QBLK), :] for s in starts[c]], axis=0))
        carry_ref[c] = after

    for c in chains:
        def tile(kb, c=c):
            start = pl.multiple_of(kb * QBLK, QBLK)
            z = _dot_nt(qs_ref[c], k_ref[pl.ds(start, QBLK), :])
            sp = softplus(z)
            log_1m = -sp
            a = jnp.exp((z - sp) + later_keys(log_1m) + carry_ref[c])
            acc_ref[c] += _dot(a.astype(_BF16), v_ref[pl.ds(start, QBLK), :])
            carry = carry_ref[c] + jnp.sum(log_1m, axis=1, keepdims=True)
            carry_ref[c] = carry
            return jnp.max(carry)

        def cond(st):
            return (st[0] >= 0) & (st[1] > SB_DEAD_LOG)

        def step(st, tile=tile):
            return st[0] - 1, tile(st[0])

        lax.while_loop(cond, step, (blk[c] - nb, jnp.max(carry_ref[c])))
        out = jnp.zeros((QBLK, GROUP_WIDTH), _F32)
        for h, hm in enumerate(head_masks):
            out = jnp.where(hm, acc_ref[c, h * QBLK:(h + 1) * QBLK, :], out)
        o_ref[c * QBLK:(c + 1) * QBLK, :] = out.astype(_BF16)


def _stick_breaking(arr, batch, seq):
    gw = GROUP_WIDTH
    qt = SB_CHAINS * QBLK
    view = arr.reshape(batch, seq, 3 * gw)
    o = pl.pallas_call(
        _sb_body,
        out_shape=jax.ShapeDtypeStruct((batch, seq, gw), _BF16),
        grid=(batch, seq // qt),
        in_specs=[pl.BlockSpec((None, qt, gw), lambda b, i: (b, i, 0)),
                  pl.BlockSpec((None, seq, gw), lambda b, i: (b, 0, 1)),
                  pl.BlockSpec((None, seq, gw), lambda b, i: (b, 0, 2))],
        out_specs=pl.BlockSpec((None, qt, gw), lambda b, i: (b, i, 0)),
        scratch_shapes=[pltpu.VMEM((SB_CHAINS, HEADS_PER_GROUP * QBLK, 1), _F32),
                        pltpu.VMEM((SB_CHAINS, HEADS_PER_GROUP * QBLK, gw), _F32),
                        pltpu.VMEM((SB_CHAINS, HEADS_PER_GROUP * QBLK, gw), _BF16)],
        compiler_params=_params("arbitrary", "arbitrary"),
        name="sb",
    )(view, view, view)
    return o.reshape(batch * seq, gw)


def _merge_body(x_ref, mod_ref, g1_ref, g2_ref, o0_ref, o1_ref, o2_ref, l0_ref, l1_ref, l2_ref, osb_ref,
                wbg_ref, wbd_ref, wbs_ref, wout_ref, wrh_ref, wrl_ref, br_ref, ex_ref, tri_ref,
                x1_ref, h2p_ref, rinfo_ref, er_ref, cnt_out_ref, cnt_ref, os1_ref, os2_ref, ls1_ref, ls2_ref):
    d = D_MODEL
    tm = x_ref.shape[0]
    x = x_ref[...]
    hb = _rms_mod(x, g1_ref[...], mod_ref[:, d:2 * d], mod_ref[:, 0:d]).astype(_BF16)

    def natural(ref, st_ref, dil):
        if dil == 1:
            return ref[...].astype(_F32)
        slabs = st_ref.shape[0]
        for r in range(dil):
            for s in range(slabs):
                col0 = (r * slabs + s) * LANES
                st_ref[s, pl.ds(r, tm // dil, stride=dil), :] = ref[:, col0:col0 + LANES].astype(_F32)
        return jnp.concatenate([st_ref[s] for s in range(slabs)], axis=1)

    dils = [dl for _, dl in DIL_PATTERNS]
    o_nat = [natural(r, s, dl) for r, s, dl in zip((o0_ref, o1_ref, o2_ref), (None, os1_ref, os2_ref), dils)]
    l0, l1, l2 = [natural(r, s, dl) for r, s, dl in zip((l0_ref, l1_ref, l2_ref), (None, ls1_ref, ls2_ref), dils)]

    lmax = jnp.maximum(jnp.maximum(l0, l1), l2)
    e0, e1, e2 = jnp.exp(l0 - lmax), jnp.exp(l1 - lmax), jnp.exp(l2 - lmax)
    inv = 1.0 / (e0 + e1 + e2)
    ex = ex_ref[...]

    def widen(w):
        hi, lo = _split(w)
        return _dot(jnp.concatenate([hi, lo], axis=1), ex)

    w_groups = [widen(e * inv) for e in (e0, e1, e2)]
    gate_dil = jax.nn.sigmoid(_dot(hb, wbg_ref[:, :d]))
    o_dil = w_groups[0] * o_nat[0] + w_groups[1] * o_nat[1] + w_groups[2] * o_nat[2]
    branch_dil = _dot(o_dil.astype(_BF16), wbd_ref[...])
    branch_sb = _dot(osb_ref[...], wbs_ref[...])
    gate_sb = jax.nn.sigmoid(_dot(hb, wbg_ref[:, d:]))
    merged = gate_dil * branch_dil + gate_sb * branch_sb
    x1 = x + mod_ref[:, 2 * d:3 * d] * _dot(merged.astype(_BF16), wout_ref[...])
    x1_ref[...] = x1

    h2 = _rms_mod(x1, g2_ref[...], mod_ref[:, 4 * d:5 * d], mod_ref[:, 3 * d:4 * d])
    h2p_ref[...] = _pack_halves(h2)

    hh, hl = _split(h2)
    logits = _dot(hh, wrh_ref[...]) + (_dot(hl, wrh_ref[...]) + _dot(hh, wrl_ref[...])) + br_ref[...]
    lane = lax.broadcasted_iota(jnp.int32, (tm, ROUTER_LANES), 1).astype(_F32)
    far = float(ROUTER_LANES)

    def top(vals):
        m = jnp.max(vals, axis=1, keepdims=True)
        return m, jnp.min(jnp.where(vals == m, lane, far), axis=1, keepdims=True)

    is_group = lane < N_GROUPS
    mg, gsel = top(jnp.where(is_group, logits, NEG_BIG))
    pg_top = 1.0 / jnp.sum(jnp.where(is_group, jnp.exp(logits - mg), 0.0), axis=1, keepdims=True)
    lane0 = EXPERT_LANE0 + EXPERTS_PER_GROUP * gsel
    le = jnp.where((lane >= lane0) & (lane < lane0 + EXPERTS_PER_GROUP), logits, NEG_BIG)
    m1, i1 = top(le)
    m2, i2 = top(jnp.where(lane == i1, NEG_BIG, le))
    t2 = jnp.exp(m2 - m1)
    w0 = pg_top / (1.0 + t2)
    w1 = pg_top * t2 / (1.0 + t2)

    @pl.when(pl.program_id(0) == 0)
    def _():
        cnt_ref[...] = jnp.zeros_like(cnt_ref)

    sel0, sel1 = lane == i1, lane == i2
    onehot = jnp.where(sel0 | sel1, 1.0, 0.0)
    before = _dot(tri_ref[...], onehot.astype(_BF16)) + cnt_ref[0:1, :]
    r0 = jnp.sum(jnp.where(sel0, before, 0.0), axis=1, keepdims=True)
    r1 = jnp.sum(jnp.where(sel1, before, 0.0), axis=1, keepdims=True)
    cnt_ref[...] += jnp.sum(onehot, axis=0, keepdims=True)
    cnt_out_ref[...] = cnt_ref[...]
    cols = (i1 - EXPERT_LANE0, i2 - EXPERT_LANE0, r0, r1, w0, w1)
    rinfo = jnp.zeros((tm, ROUTER_LANES), _F32)
    for c, v in enumerate(cols):
        rinfo = jnp.where(lane == float(c), v, rinfo)
    rinfo_ref[...] = rinfo
    er_ref[...] = jnp.transpose(rinfo)[0:8, :].astype(jnp.int32)


def _merge(x2, mod3, g1, g2, outs, lses, osb, wbg, wbd, wbs, wout, wrh, wrl, br, ex, tri, seq, part):
    d = x2.shape[1]
    tm = ROW_TILE
    per_b = seq // tm
    t = x2.shape[0] // MOE_PARTS
    first = part * (t // tm)
    src = lambda i: (i + first, 0)
    row = lambda i: (i, 0)
    const = lambda i: (0, 0)
    full = lambda a: pl.BlockSpec(a.shape, const)
    gw = GROUP_WIDTH
    dils = [dl for _, dl in DIL_PATTERNS]
    return pl.pallas_call(
        _merge_body,
        out_shape=[jax.ShapeDtypeStruct((t, d), _F32),
                   jax.ShapeDtypeStruct((t, d // 2), jnp.uint32),
                   jax.ShapeDtypeStruct((t, ROUTER_LANES), _F32),
                   jax.ShapeDtypeStruct((8, t), jnp.int32),
                   jax.ShapeDtypeStruct((8, ROUTER_LANES), _F32)],
        grid=(t // tm,),
        in_specs=[pl.BlockSpec((tm, d), src),
                  pl.BlockSpec((None, 1, mod3.shape[2]), lambda i: ((i + first) // per_b, 0, 0)),
                  full(g1), full(g2)]
                 + [pl.BlockSpec((tm // dl, dl * gw), src) for dl in dils]
                 + [pl.BlockSpec((tm // dl, dl * LANES), src) for dl in dils]
                 + [pl.BlockSpec((tm, gw), src)]
                 + [full(a) for a in (wbg, wbd, wbs, wout, wrh, wrl, br, ex, tri)],
        out_specs=[pl.BlockSpec((tm, d), row),
                   pl.BlockSpec((tm, d // 2), row),
                   pl.BlockSpec((tm, ROUTER_LANES), row),
                   pl.BlockSpec((8, tm), lambda i: (0, i)),
                   pl.BlockSpec((8, ROUTER_LANES), const)],
        scratch_shapes=[pltpu.VMEM((8, ROUTER_LANES), _F32),
                        pltpu.VMEM((gw // LANES, tm, LANES), _F32), pltpu.VMEM((gw // LANES, tm, LANES), _F32),
                        pltpu.VMEM((1, tm, LANES), _F32), pltpu.VMEM((1, tm, LANES), _F32)],
        compiler_params=_params("arbitrary"),
        name="merge",
    )(x2, mod3, g1, g2, *outs, *lses, osb, wbg, wbd, wbs, wout, wrh, wrl, br, ex, tri)


def _dest_body(ps_ref, er_ref, d_ref):
    e = er_ref[0:2, :]
    start = jnp.zeros_like(e)
    for x in range(N_EXPERTS):
        start = jnp.where(e == x, ps_ref[x], start)
    d_ref[...] = start + er_ref[2:4, :]


def _dest(pstart, er):
    t = er.shape[1]
    tw = min(DEST_TILE, t)
    return pl.pallas_call(
        _dest_body,
        out_shape=jax.ShapeDtypeStruct((2, t), jnp.int32),
        grid_spec=pltpu.PrefetchScalarGridSpec(
            num_scalar_prefetch=1,
            grid=(t // tw,),
            in_specs=[pl.BlockSpec((8, tw), lambda i, ps: (0, i))],
            out_specs=pl.BlockSpec((2, tw), lambda i, ps: (0, i))),
        compiler_params=_params("arbitrary"),
        name="dest",
    )(pstart, er)


def _sc_mesh():
    return plsc.VectorSubcoreMesh(core_axis_name="core", subcore_axis_name="subcore",
                                  num_cores=SC_CORES, num_subcores=SC_SUBCORES)


def _sc_worker():
    return lax.axis_index("subcore") * SC_CORES + lax.axis_index("core")


def _sc_scatter(x, idx0, idx1, n_slots):
    chunks = idx0.shape[0]
    per = chunks // SC_WORKERS
    win = idx0.shape[1]

    @functools.partial(
        pl.kernel, mesh=_sc_mesh(), out_type=jax.ShapeDtypeStruct((n_slots, x.shape[1]), x.dtype),
        scratch_types=[pltpu.VMEM((1, win), jnp.int32), pltpu.VMEM((1, win), jnp.int32),
                       pltpu.VMEM((win, x.shape[1]), x.dtype), pltpu.SemaphoreType.DMA],
        name="sc_scatter")
    def run(x_hbm, i0_hbm, i1_hbm, o_hbm, i0_v, i1_v, rows_v, sem):
        wid = _sc_worker()

        @pl.loop(0, per)
        def _(j):
            c = wid * per + j
            pltpu.sync_copy(i0_hbm.at[pl.ds(c, 1)], i0_v)
            pltpu.sync_copy(i1_hbm.at[pl.ds(c, 1)], i1_v)
            pltpu.sync_copy(x_hbm.at[pl.ds(c * win, win)], rows_v)
            first = pltpu.async_copy(rows_v, o_hbm.at[i0_v.at[0]], sem)
            second = pltpu.async_copy(rows_v, o_hbm.at[i1_v.at[0]], sem)
            first.wait()
            second.wait()

    return run(x, idx0, idx1)


def _sc_gather(table, idx):
    chunks, win = idx.shape
    per = chunks // SC_WORKERS

    @functools.partial(
        pl.kernel, mesh=_sc_mesh(), out_type=jax.ShapeDtypeStruct((chunks * win, table.shape[1]), table.dtype),
        scratch_types=[pltpu.VMEM((1, win), jnp.int32), pltpu.VMEM((win, table.shape[1]), table.dtype),
                       pltpu.SemaphoreType.DMA],
        name="sc_gather")
    def run(t_hbm, i_hbm, o_hbm, i_v, rows_v, sem):
        wid = _sc_worker()

        @pl.loop(0, per)
        def _(j):
            c = wid * per + j
            pltpu.sync_copy(i_hbm.at[pl.ds(c, 1)], i_v)
            pltpu.async_copy(t_hbm.at[i_v.at[0]], rows_v, sem).wait()
            pltpu.sync_copy(rows_v, o_hbm.at[pl.ds(c * win, win)])

    return run(table, idx)


def _pack_halves(a):
    h = a.shape[1] // 2
    lo = lax.bitcast_convert_type(a[:, :h].astype(_BF16).astype(_F32), jnp.uint32) >> 16
    hi = lax.bitcast_convert_type(a[:, h:].astype(_BF16).astype(_F32), jnp.uint32) & jnp.uint32(0xFFFF0000)
    return lo | hi


def _unpack_halves(w):
    return jnp.concatenate(
        [lax.bitcast_convert_type(w << 16, _F32), lax.bitcast_convert_type(w & jnp.uint32(0xFFFF0000), _F32)], axis=1)


def _experts_body(ce_ref, nv_ref, nu_ref, seg_ref, nxt_ref, xs_hbm, wg_ref, wu_ref, wd_ref, ys_hbm,
                  wgb_ref, wub_ref, wdb_ref, wgf_ref, wuf_ref, wdf_ref, sem):
    def fetch(expert, slot):
        return [pltpu.make_async_copy(src.at[expert], dst.at[slot], sem.at[slot])
                for src, dst in ((wg_ref, wgf_ref), (wu_ref, wuf_ref), (wd_ref, wdf_ref))]

    def chunk(xs_ref, ys_ref):
        c = pl.program_id(0)

        @pl.when(seg_ref[c] >= 0)
        def _():
            slot = seg_ref[c] & 1

            @pl.when(c == 0)
            def _():
                for copy in fetch(ce_ref[0], 0):
                    copy.start()

            for copy in fetch(ce_ref[c], slot):
                copy.wait()
            wgb_ref[...] = wgf_ref[slot].astype(_BF16)
            wub_ref[...] = wuf_ref[slot].astype(_BF16)
            wdb_ref[...] = wdf_ref[slot].astype(_BF16)

            @pl.when(nxt_ref[c] >= 0)
            def _():
                for copy in fetch(nxt_ref[c], 1 - slot):
                    copy.start()

        row = lax.broadcasted_iota(jnp.int32, xs_ref.shape, 0)
        x = _unpack_halves(jnp.where(row < nv_ref[c], xs_ref[...], jnp.uint32(0))).astype(_BF16)
        half = x.shape[0] // 2
        ups = [(_dot(x[r:r + half], wgb_ref[...]), _dot(x[r:r + half], wub_ref[...])) for r in (0, half)]
        for (g, u), r in zip(ups, (0, half)):
            hmid = (g * jax.nn.sigmoid(g)) * u
            ys_ref[r:r + half, :] = _pack_halves(_dot(hmid.astype(_BF16), wdb_ref[...]))

    ch = EXPERT_CHUNK
    pltpu.emit_pipeline(
        chunk,
        grid=(nu_ref[0],),
        in_specs=[pl.BlockSpec((ch, xs_hbm.shape[1]), lambda c: (c, 0), pipeline_mode=pl.Buffered(EXPERT_IN_BUFFERS))],
        out_specs=[pl.BlockSpec((ch, ys_hbm.shape[1]), lambda c: (c, 0))],
    )(xs_hbm, ys_hbm)


def _experts(chunk_e, n_valid, n_used, seg, nxt, xs, wg, wu, wd):
    n_slots, w = xs.shape
    d, de = wg.shape[1], wg.shape[2]
    hbm = pl.BlockSpec(memory_space=pl.ANY)
    return pl.pallas_call(
        _experts_body,
        out_shape=jax.ShapeDtypeStruct((n_slots, d // 2), jnp.uint32),
        grid_spec=pltpu.PrefetchScalarGridSpec(
            num_scalar_prefetch=5,
            grid=(1,),
            in_specs=[hbm, hbm, hbm, hbm],
            out_specs=hbm,
            scratch_shapes=[pltpu.VMEM((d, de), _BF16), pltpu.VMEM((d, de), _BF16), pltpu.VMEM((de, d), _BF16),
                            pltpu.VMEM((2, d, de), _F32), pltpu.VMEM((2, d, de), _F32), pltpu.VMEM((2, de, d), _F32),
                            pltpu.SemaphoreType.DMA((2,))]),
        compiler_params=_params("arbitrary"),
        name="experts",
    )(chunk_e, n_valid, n_used, seg, nxt, xs, wg, wu, wd)


def _combine_body(x1_ref, rinfo_ref, mod_ref, y0_ref, y1_ref, *rest):
    o_ref = rest[-1]
    y = rinfo_ref[:, 4:5] * _unpack_halves(y0_ref[...]) + rinfo_ref[:, 5:6] * _unpack_halves(y1_ref[...])
    o_ref[...] = x1_ref[...] + mod_ref[:, 5 * D_MODEL:6 * D_MODEL] * y


def _combine(x1, rinfo, mod3, gathered, seq, part, out_so_far):
    t, d = x1.shape
    tf = min(COMBINE_TILE, seq)
    per_b = seq // tf
    nt = t // tf
    first = part * nt
    in_specs = [pl.BlockSpec((tf, d), lambda i: (i, 0)),
                pl.BlockSpec((tf, ROUTER_LANES), lambda i: (i, 0)),
                pl.BlockSpec((None, 1, mod3.shape[2]), lambda i: ((i + first) // per_b, 0, 0)),
                pl.BlockSpec((tf, d // 2), lambda i: (i, 0)),
                pl.BlockSpec((tf, d // 2), lambda i: (i + nt, 0))]
    args = [x1, rinfo, mod3, gathered, gathered]
    aliases = {}
    if out_so_far is not None:
        in_specs.append(pl.BlockSpec(memory_space=pl.ANY))
        args.append(out_so_far)
        aliases = {len(args) - 1: 0}
    return pl.pallas_call(
        _combine_body,
        out_shape=jax.ShapeDtypeStruct((t * MOE_PARTS, d), _F32),
        grid=(nt,),
        in_specs=in_specs,
        out_specs=pl.BlockSpec((tf, d), lambda i: (i + first, 0)),
        input_output_aliases=aliases,
        compiler_params=_params("arbitrary"),
        name="combine",
    )(*args)


def _rope_trig(positions):
    inv_freq = ROPE_THETA ** (-jnp.arange(0, ROPE_DIM, 2, dtype=_F32) / ROPE_DIM)
    ang = positions.reshape(-1).astype(_F32)[:, None] * inv_freq
    return jnp.concatenate([jnp.cos(ang), jnp.sin(ang)], axis=1)


def _layer(x, mod, positions, g_mix, g_ffn, w_in, w_bg, qg, kg, w_bd, w_bs, w_out, w_rg, b_rg, w_re, b_re,
           w_eg, w_eu, w_ed):
    batch, seq, d = x.shape
    t = batch * seq
    x2 = x.reshape(t, d)
    mod3 = mod.reshape(batch, 1, mod.shape[1])
    gw = GROUP_WIDTH

    lane = jnp.arange(gw)
    bd = jnp.where(lane[:, None] // HEAD_DIM == lane[None, :] // HEAD_DIM, 1.0 / HEAD_DIM, 0.0).astype(_BF16)
    ex = (jnp.arange(LANES)[:, None] == (lane[None, :] // HEAD_DIM) * LSE_SEG).astype(_BF16)
    ex = jnp.concatenate([ex, ex], axis=0)
    tri = (jnp.arange(ROW_TILE)[:, None] > jnp.arange(ROW_TILE)[None, :]).astype(_BF16)
    tile4 = lambda g: jnp.tile(g.astype(_F32), HEADS_PER_GROUP).reshape(1, gw)
    wr = jnp.zeros((d, ROUTER_LANES), _F32).at[:, :N_GROUPS].set(w_rg).at[:, N_GROUPS:N_GROUPS + N_EXPERTS].set(w_re)
    wrh = wr.astype(_BF16)
    wrl = (wr - wrh.astype(_F32)).astype(_BF16)
    br = jnp.zeros((1, ROUTER_LANES), _F32).at[0, :N_GROUPS].set(b_rg).at[0, N_GROUPS:N_GROUPS + N_EXPERTS].set(b_re)

    d0, d1, d2, sbp = _qkv(x2, mod3, g_mix.reshape(1, d), w_in.astype(_BF16), tile4(qg) * ATTN_SCALE, tile4(kg),
                           _rope_trig(positions), bd, seq)
    dil = [_dilated_group(a, dl, batch, seq) for a, (_, dl) in zip((d0, d1, d2), DIL_PATTERNS)]
    osb = _stick_breaking(sbp, batch, seq)

    merge_weights = (w_bg.astype(_BF16), w_bd.astype(_BF16), w_bs.astype(_BF16), w_out.astype(_BF16))
    tp = t // MOE_PARTS
    ch = EXPERT_CHUNK
    win = SC_INDEX_WINDOW
    n_chunks = -(-2 * tp // ch) + N_EXPERTS
    chunk_start = jnp.arange(n_chunks, dtype=jnp.int32) * ch
    expert_ids = jnp.arange(N_EXPERTS, dtype=jnp.int32)
    out = None
    for part in range(MOE_PARTS):
        x1, h2p, rinfo, er, cnt = _merge(
            x2, mod3, g_mix.reshape(1, d), g_ffn.reshape(1, d), [o for o, _ in dil], [l for _, l in dil], osb,
            *merge_weights, wrh, wrl, br, ex, tri, seq, part)

        counts = cnt[0, EXPERT_LANE0:EXPERT_LANE0 + N_EXPERTS].astype(jnp.int32)
        padded = (counts + ch - 1) // ch * ch
        pend = jnp.cumsum(padded)
        pstart = pend - padded
        chunk_e = jnp.minimum(jnp.sum((pend[None, :] <= chunk_start[:, None]).astype(jnp.int32), axis=1),
                              N_EXPERTS - 1)
        n_used = (pend[-1:] // ch).astype(jnp.int32)
        begin = chunk_start[:, None]
        inside = (pstart[None, :] <= begin) & (begin < pend[None, :])
        n_valid = jnp.sum(jnp.where(inside, jnp.clip(counts[None, :] - (begin - pstart[None, :]), 0, ch), 0), axis=1)

        dest = _dest(pstart, er)
        xs = _sc_scatter(h2p, dest[0].reshape(tp // win, win), dest[1].reshape(tp // win, win), n_chunks * ch)
        first = (chunk_e != jnp.concatenate([jnp.full((1,), -1, jnp.int32), chunk_e[:-1]])) & (chunk_start < pend[-1])
        seg_no = jnp.cumsum(first.astype(jnp.int32)) - 1
        seg = jnp.where(first, seg_no, -1 - seg_no)
        later = (expert_ids[None, :] > expert_ids[:, None]) & (padded > 0)[None, :]
        next_expert = jnp.min(jnp.where(later, expert_ids[None, :], N_EXPERTS), axis=1)
        next_expert = jnp.where(next_expert == N_EXPERTS, -1, next_expert)
        nxt = jnp.sum(jnp.where(chunk_e[:, None] == expert_ids[None, :], next_expert[None, :], 0), axis=1)
        ys = _experts(chunk_e, n_valid, n_used, seg.astype(jnp.int32), nxt.astype(jnp.int32), xs, w_eg, w_eu, w_ed)
        gathered = _sc_gather(ys, dest.reshape(2 * tp // win, win))
        out = _combine(x1, rinfo, mod3, gathered, seq, part, out)
    return out.reshape(batch, seq, d)


def kernel(x, c, positions, w_ada, b_ada, g_norm_mix, g_norm_ffn, w_in, w_branch_gate, q_norm_g, k_norm_g,
           w_branch_dil, w_branch_sb, w_out, w_router_group, b_router_group, w_router_expert, b_router_expert,
           w_expert_gate, w_expert_up, w_expert_down):
    for l in range(w_ada.shape[0]):
        mod = _ada(c, w_ada[l], b_ada[l])
        x = _layer(x, mod, positions, g_norm_mix[l], g_norm_ffn[l], w_in[l], w_branch_gate[l], q_norm_g[l],
                   k_norm_g[l], w_branch_dil[l], w_branch_sb[l], w_out[l], w_router_group[l], b_router_group[l],
                   w_router_expert[l], b_router_expert[l], w_expert_gate[l], w_expert_up[l], w_expert_down[l])
    return x
```

```python
import functools

import jax
import jax.numpy as jnp
from jax import lax
from jax.experimental import pallas as pl
from jax.experimental.pallas import tpu as pltpu
from jax.experimental.pallas import tpu_sc as plsc

D_MODEL = 1024
HEAD_DIM = 64
DIL_PATTERNS = ((128, 1), (512, 4), (2048, 16))
HEADS_PER_GROUP = 4
GROUP_WIDTH = HEADS_PER_GROUP * HEAD_DIM
N_DIL_GROUPS = len(DIL_PATTERNS)
DIL_WIDTH = N_DIL_GROUPS * GROUP_WIDTH
QKV_WIDTH = 3 * DIL_WIDTH + 3 * GROUP_WIDTH
WINDOW_KEYS = 128
ROPE_THETA = 500000.0
ROPE_DIM = HEAD_DIM // 4
N_GROUPS = 4
EXPERTS_PER_GROUP = 8
N_EXPERTS = N_GROUPS * EXPERTS_PER_GROUP
D_EXPERT = 512
RMS_EPS = 1e-6
ATTN_SCALE = HEAD_DIM ** -0.5

LANES = 128
ROUTER_LANES = LANES
EXPERT_LANE0 = N_GROUPS
LSE_SEG = LANES // HEADS_PER_GROUP
NEG_BIG = -1e30
SB_DEAD_LOG = -120.0
SB_HEAD_BLOCKS = 3
SB_CHAINS = 8

ROW_TILE = 512
QKV_TILE = 1024
QBLK = 128
DIL_QTILE = 2048
EXPERT_CHUNK = 512
EXPERT_IN_BUFFERS = 3
COMBINE_TILE = 1024
COMBINE_IN_BUFFERS = 3
DEST_TILE = 8192
SC_CORES = 2
SC_SUBCORES = 16
SC_WORKERS = SC_CORES * SC_SUBCORES
SC_INDEX_WINDOW = 128
MOE_PARTS = 2
VMEM_LIMIT = 48 * 1024 * 1024

_BF16 = jnp.bfloat16
_F32 = jnp.float32
_NT = (((1,), (1,)), ((), ()))


def _dot(a, b):
    return jnp.dot(a, b, preferred_element_type=_F32)


def _dot_nt(a, b):
    return lax.dot_general(a, b, _NT, preferred_element_type=_F32)


def _split(a):
    hi = a.astype(_BF16)
    lo = (a - hi.astype(_F32)).astype(_BF16)
    return hi, lo


def _dot3(a, b):
    ah, al = _split(a)
    bh, bl = _split(b)
    return _dot(ah, bh) + (_dot(ah, bl) + _dot(al, bh))


def _rms_mod(x, g, scale, shift):
    y = x * lax.rsqrt(jnp.mean(x * x, axis=-1, keepdims=True) + RMS_EPS)
    return y * g * (1.0 + scale) + shift


def _params(*sem):
    return pltpu.CompilerParams(dimension_semantics=sem, vmem_limit_bytes=VMEM_LIMIT)


def _ada_body(c_ref, w_ref, b_ref, o_ref):
    c = c_ref[...]
    o_ref[...] = _dot3(c * jax.nn.sigmoid(c), w_ref[...]) + b_ref[...]


def _ada(c, w_ada, b_ada):
    b, d = c.shape
    n = w_ada.shape[1]
    rows = -(-b // 16) * 16
    cp = jnp.zeros((rows, d), _F32).at[:b].set(c)
    nt = 1536
    out = pl.pallas_call(
        _ada_body,
        out_shape=jax.ShapeDtypeStruct((rows, n), _F32),
        grid=(n // nt,),
        in_specs=[pl.BlockSpec((rows, d), lambda j: (0, 0)),
                  pl.BlockSpec((d, nt), lambda j: (0, j)),
                  pl.BlockSpec((1, nt), lambda j: (0, j))],
        out_specs=pl.BlockSpec((rows, nt), lambda j: (0, j)),
        compiler_params=_params("arbitrary"),
        name="ada",
    )(cp, w_ada, b_ada.reshape(1, n))
    return out[:b]


def _qkv_body(x_ref, mod_ref, g_ref, w_ref, qg_ref, kg_ref, trig_ref, bd_ref,
              o0_ref, o1_ref, o2_ref, osb_ref, st_ref, acc_ref):
    d = D_MODEL
    tm = x_ref.shape[0]
    h = _rms_mod(x_ref[...], g_ref[...], mod_ref[:, d:2 * d], mod_ref[:, 0:d])
    hb = h.astype(_BF16)
    half = ROPE_DIM // 2
    in_head = lax.broadcasted_iota(jnp.int32, (tm, LANES), 1) & (HEAD_DIM - 1)
    freq = in_head & (half - 1)
    trig = jnp.concatenate([trig_ref[...], jnp.zeros((tm, LANES - ROPE_DIM), _F32)], axis=1)
    cos = jnp.take_along_axis(trig, freq, axis=1)
    sin = jnp.take_along_axis(trig, freq + half, axis=1)
    cc = jnp.where(in_head < ROPE_DIM, cos, 1.0)
    s1 = jnp.where(in_head < half, -sin, 0.0)
    s2 = jnp.where((in_head >= half) & (in_head < ROPE_DIM), sin, 0.0)
    cc, s1, s2 = [jnp.concatenate([a, a], axis=1) for a in (cc, s1, s2)]
    bd = bd_ref[...]
    gw = GROUP_WIDTH

    def normed_rotated(acc, gain):
        ms = _dot((acc * acc).astype(_BF16), bd)
        y = acc * lax.rsqrt(ms + RMS_EPS) * gain
        return y * cc + pltpu.roll(y, gw - ROPE_DIM // 2, 1) * s1 + pltpu.roll(y, ROPE_DIM // 2, 1) * s2

    def store(o_ref, dil, part, y):
        if dil == 1:
            o_ref[:, part * gw:(part + 1) * gw] = y.astype(_BF16)
            return
        for s in range(gw // LANES):
            st_ref[s] = y[:, s * LANES:(s + 1) * LANES]
        for r in range(dil):
            for s in range(gw // LANES):
                col0 = (3 * r + part) * gw + s * LANES
                o_ref[:, col0:col0 + LANES] = st_ref[s, pl.ds(r, tm // dil, stride=dil), :].astype(_BF16)

    def project(col0):
        return _dot(hb, w_ref[:, col0:col0 + gw])

    outs = (o0_ref, o1_ref, o2_ref)
    dils = [dl for _, dl in DIL_PATTERNS]
    normed = [(g, part, part * DIL_WIDTH + g * gw) for g in range(N_DIL_GROUPS) for part in (0, 1)]
    for n, (_, _, col0) in enumerate(normed):
        acc_ref[n] = project(col0)
    plain = [("v", g) for g in range(N_DIL_GROUPS)] + [("sb", part) for part in range(3)]
    for n, (kind, j) in enumerate(plain):
        if kind == "v":
            store(outs[j], dils[j], 2, project(2 * DIL_WIDTH + j * gw))
        else:
            acc = project(3 * DIL_WIDTH + j * gw)
            osb_ref[:, j * gw:(j + 1) * gw] = (acc * ATTN_SCALE if j == 0 else acc).astype(_BF16)
        g, part, _ = normed[n]
        gain = qg_ref[...] if part == 0 else kg_ref[...]
        store(outs[g], dils[g], part, normed_rotated(acc_ref[n], gain))


def _qkv(x2, mod3, g_mix, w_in, qg, kg, trig, bd, seq):
    t, d = x2.shape
    tm = min(QKV_TILE, seq)
    per_b = seq // tm
    row = lambda i: (i, 0)
    const = lambda i: (0, 0)
    width = 3 * GROUP_WIDTH
    dils = [dl for _, dl in DIL_PATTERNS] + [1]
    return pl.pallas_call(
        _qkv_body,
        out_shape=[jax.ShapeDtypeStruct((t // dl, dl * width), _BF16) for dl in dils],
        grid=(t // tm,),
        in_specs=[pl.BlockSpec((tm, d), row),
                  pl.BlockSpec((None, 1, mod3.shape[2]), lambda i: (i // per_b, 0, 0)),
                  pl.BlockSpec((1, d), const),
                  pl.BlockSpec(w_in.shape, const),
                  pl.BlockSpec((1, GROUP_WIDTH), const),
                  pl.BlockSpec((1, GROUP_WIDTH), const),
                  pl.BlockSpec((tm, ROPE_DIM), row),
                  pl.BlockSpec(bd.shape, const)],
        out_specs=[pl.BlockSpec((tm // dl, dl * width), row) for dl in dils],
        scratch_shapes=[pltpu.VMEM((GROUP_WIDTH // LANES, tm, LANES), _F32),
                        pltpu.VMEM((2 * N_DIL_GROUPS, tm, GROUP_WIDTH), _F32)],
        compiler_params=_params("arbitrary"),
        name="qkv",
    )(x2, mod3, g_mix, w_in, qg, kg, trig, bd)


def _dil_body(prev_ref, cur_ref, o_ref, lse_ref, kf_ref, vf_ref):
    tq = cur_ref.shape[0]
    gw = GROUP_WIDTH
    n_res = cur_ref.shape[1] // (3 * gw)
    first = pl.program_id(2) == 0
    for r in range(n_res):
        for part, full_ref in ((1, kf_ref), (2, vf_ref)):
            lanes = slice((3 * r + part) * gw, (3 * r + part + 1) * gw)
            full_ref[r, 0:QBLK, :] = prev_ref[:, lanes]
            full_ref[r, QBLK:, :] = cur_ref[:, lanes]
    nh = HEADS_PER_GROUP
    row = lax.broadcasted_iota(jnp.int32, (nh * QBLK, 2 * QBLK), 0) & (QBLK - 1)
    col = lax.broadcasted_iota(jnp.int32, (nh * QBLK, 2 * QBLK), 1)
    band = (col >= row) & (col <= row + WINDOW_KEYS)
    lane = lax.broadcasted_iota(jnp.int32, (1, GROUP_WIDTH), 1)
    slane = lax.broadcasted_iota(jnp.int32, (1, LANES), 1)
    head_masks = [(lane >= h * HEAD_DIM) & (lane < (h + 1) * HEAD_DIM) for h in range(nh)]
    subs = [(r, j) for r in range(n_res) for j in range(tq // QBLK)]
    rows = [slice(j * QBLK, (j + 1) * QBLK) for _, j in subs]
    window = [slice(j * QBLK, (j + 2) * QBLK) for _, j in subs]
    n = range(len(subs))

    def stack(qj):
        return jnp.concatenate([jnp.where(hm, qj, jnp.zeros_like(qj)) for hm in head_masks], axis=0)

    valid = [band & ((col >= QBLK) | jnp.logical_not(first)) if j == 0 else band for _, j in subs]
    q = [cur_ref[rows[i], 3 * r * gw:(3 * r + 1) * gw] for i, (r, _) in enumerate(subs)]
    s = [jnp.where(valid[i], _dot_nt(stack(q[i]), kf_ref[subs[i][0], window[i], :]), NEG_BIG) for i in n]
    m = [jnp.max(s[i], axis=1, keepdims=True) for i in n]
    p = [jnp.exp(s[i] - m[i]) for i in n]
    l = [jnp.sum(p[i], axis=1, keepdims=True) for i in n]
    o_all = [_dot(p[i].astype(_BF16), vf_ref[subs[i][0], window[i], :]) / l[i] for i in n]
    for i, (r, _) in enumerate(subs):
        lse_all = m[i] + jnp.log(l[i])
        o_acc = jnp.zeros((QBLK, gw), _F32)
        lse_t = jnp.zeros((QBLK, LANES), _F32)
        for h, hm in enumerate(head_masks):
            o_acc = jnp.where(hm, o_all[i][h * QBLK:(h + 1) * QBLK, :], o_acc)
            sm = (slane >= h * LSE_SEG) & (slane < (h + 1) * LSE_SEG)
            lse_t = jnp.where(sm, lse_all[h * QBLK:(h + 1) * QBLK, :], lse_t)
        o_ref[rows[i], r * gw:(r + 1) * gw] = o_acc.astype(_BF16)
        lse_ref[rows[i], r * LANES:(r + 1) * LANES] = lse_t


def _dilated_group(view2, dil, batch, seq):
    sd = seq // dil
    tq = min(DIL_QTILE, sd)
    n_res = min(dil, DIL_QTILE // tq)
    per = tq // QBLK
    gw = GROUP_WIDTH
    view = view2.reshape(batch, sd, dil * 3 * gw)
    o, lse = pl.pallas_call(
        _dil_body,
        out_shape=[jax.ShapeDtypeStruct((batch, sd, dil * gw), _BF16),
                   jax.ShapeDtypeStruct((batch, sd, dil * LANES), _F32)],
        grid=(batch, dil // n_res, sd // tq),
        in_specs=[pl.BlockSpec((None, QBLK, n_res * 3 * gw), lambda b, r, i: (b, jnp.maximum(i * per - 1, 0), r)),
                  pl.BlockSpec((None, tq, n_res * 3 * gw), lambda b, r, i: (b, i, r))],
        out_specs=[pl.BlockSpec((None, tq, n_res * gw), lambda b, r, i: (b, i, r)),
                   pl.BlockSpec((None, tq, n_res * LANES), lambda b, r, i: (b, i, r))],
        scratch_shapes=[pltpu.VMEM((n_res, tq + QBLK, gw), _BF16), pltpu.VMEM((n_res, tq + QBLK, gw), _BF16)],
        compiler_params=_params("arbitrary", "arbitrary", "arbitrary"),
        name=f"dil{dil}",
    )(view, view)
    return o.reshape(batch * sd, dil * gw), lse.reshape(batch * sd, dil * LANES)


def _sb_body(q_ref, k_ref, v_ref, o_ref, carry_ref, acc_ref, qs_ref):
    step_id = pl.program_id(1)
    chains = range(SB_CHAINS)
    blk = [step_id * SB_CHAINS + c for c in chains]
    nh = HEADS_PER_GROUP
    row = lax.broadcasted_iota(jnp.int32, (nh * QBLK, QBLK), 0) & (QBLK - 1)
    col = lax.broadcasted_iota(jnp.int32, (nh * QBLK, QBLK), 1)
    strict = col < row
    ur = lax.broadcasted_iota(jnp.int32, (2 * QBLK, QBLK), 0) & (QBLK - 1)
    uc = lax.broadcasted_iota(jnp.int32, (2 * QBLK, QBLK), 1)
    u = jnp.where(ur > uc, 1.0, 0.0).astype(_BF16)
    lane = lax.broadcasted_iota(jnp.int32, (1, GROUP_WIDTH), 1)
    head_masks = [(lane >= h * HEAD_DIM) & (lane < (h + 1) * HEAD_DIM) for h in range(nh)]
    for c in chains:
        q = q_ref[c * QBLK:(c + 1) * QBLK, :]
        qs_ref[c] = jnp.concatenate([jnp.where(hm, q, jnp.zeros_like(q)) for hm in head_masks], axis=0)

    def softplus(z):
        return jnp.maximum(z, 0.0) + jnp.log(1.0 + jnp.exp(-jnp.abs(z)))

    def later_keys(log_1m):
        hi, lo = _split(log_1m)
        return _dot(jnp.concatenate([hi, lo], axis=1), u)

    nb = SB_HEAD_BLOCKS
    cols = [slice(j * QBLK, (j + 1) * QBLK) for j in range(nb)]
    kbs = [[blk[c] - (nb - 1) + j for j in range(nb)] for c in chains]
    starts = [[pl.multiple_of(jnp.maximum(kb, 0) * QBLK, QBLK) for kb in kbs[c]] for c in chains]
    keep = [[strict if j == nb - 1 else (kbs[c][j] >= 0) for j in range(nb)] for c in chains]
    z = [_dot_nt(qs_ref[c], jnp.concatenate([k_ref[pl.ds(s, QBLK), :] for s in starts[c]], axis=0)) for c in chains]
    sp = [softplus(z[c]) for c in chains]
    log_1m = [[jnp.where(keep[c][j], -sp[c][:, cols[j]], 0.0) for j in range(nb)] for c in chains]
    totals = [[jnp.sum(l, axis=1, keepdims=True) for l in log_1m[c]] for c in chains]
    later = [[later_keys(log_1m[c][j]) for j in range(nb)] for c in chains]
    for c in chains:
        a_blocks = []
        after = jnp.zeros_like(totals[c][0])
        for j in reversed(range(nb)):
            a = jnp.exp((z[c][:, cols[j]] - sp[c][:, cols[j]]) + later[c][j] + after)
            a_blocks.insert(0, jnp.where(keep[c][j], a, 0.0).astype(_BF16))
            after = after + totals[c][j]
        acc_ref[c] = _dot(jnp.concatenate(a_blocks, axis=1),
                          jnp.concatenate([v_ref[pl.ds(s, QBLK), :] for s in starts[c]], axis=0))
        carry_ref[c] = after

    for c in chains:
        def tile(kb, c=c):
            start = pl.multiple_of(kb * QBLK, QBLK)
            z = _dot_nt(qs_ref[c], k_ref[pl.ds(start, QBLK), :])
            sp = softplus(z)
            log_1m = -sp
            a = jnp.exp((z - sp) + later_keys(log_1m) + carry_ref[c])
            acc_ref[c] += _dot(a.astype(_BF16), v_ref[pl.ds(start, QBLK), :])
            carry = carry_ref[c] + jnp.sum(log_1m, axis=1, keepdims=True)
            carry_ref[c] = carry
            return jnp.max(carry)

        def cond(st):
            return (st[0] >= 0) & (st[1] > SB_DEAD_LOG)

        def step(st, tile=tile):
            return st[0] - 1, tile(st[0])

        lax.while_loop(cond, step, (blk[c] - nb, jnp.max(carry_ref[c])))
        out = jnp.zeros((QBLK, GROUP_WIDTH), _F32)
        for h, hm in enumerate(head_masks):
            out = jnp.where(hm, acc_ref[c, h * QBLK:(h + 1) * QBLK, :], out)
        o_ref[c * QBLK:(c + 1) * QBLK, :] = out.astype(_BF16)


def _stick_breaking(arr, batch, seq):
    gw = GROUP_WIDTH
    qt = SB_CHAINS * QBLK
    view = arr.reshape(batch, seq, 3 * gw)
    o = pl.pallas_call(
        _sb_body,
        out_shape=jax.ShapeDtypeStruct((batch, seq, gw), _BF16),
        grid=(batch, seq // qt),
        in_specs=[pl.BlockSpec((None, qt, gw), lambda b, i: (b, i, 0)),
                  pl.BlockSpec((None, seq, gw), lambda b, i: (b, 0, 1)),
                  pl.BlockSpec((None, seq, gw), lambda b, i: (b, 0, 2))],
        out_specs=pl.BlockSpec((None, qt, gw), lambda b, i: (b, i, 0)),
        scratch_shapes=[pltpu.VMEM((SB_CHAINS, HEADS_PER_GROUP * QBLK, 1), _F32),
                        pltpu.VMEM((SB_CHAINS, HEADS_PER_GROUP * QBLK, gw), _F32),
                        pltpu.VMEM((SB_CHAINS, HEADS_PER_GROUP * QBLK, gw), _BF16)],
        compiler_params=_params("arbitrary", "arbitrary"),
        name="sb",
    )(view, view, view)
    return o.reshape(batch * seq, gw)


def _merge_body(x_ref, mod_ref, g1_ref, g2_ref, o0_ref, o1_ref, o2_ref, l0_ref, l1_ref, l2_ref, osb_ref,
                wbg_ref, wbd_ref, wbs_ref, wout_ref, wrh_ref, wrl_ref, br_ref, ex_ref, tri_ref,
                x1_ref, h2p_ref, rinfo_ref, er_ref, cnt_out_ref, cnt_ref, os1_ref, os2_ref, ls1_ref, ls2_ref):
    d = D_MODEL
    tm = x_ref.shape[0]
    x = x_ref[...]
    hb = _rms_mod(x, g1_ref[...], mod_ref[:, d:2 * d], mod_ref[:, 0:d]).astype(_BF16)

    def natural(ref, st_ref, dil):
        if dil == 1:
            return ref[...].astype(_F32)
        slabs = st_ref.shape[0]
        for r in range(dil):
            for s in range(slabs):
                col0 = (r * slabs + s) * LANES
                st_ref[s, pl.ds(r, tm // dil, stride=dil), :] = ref[:, col0:col0 + LANES].astype(_F32)
        return jnp.concatenate([st_ref[s] for s in range(slabs)], axis=1)

    dils = [dl for _, dl in DIL_PATTERNS]
    o_nat = [natural(r, s, dl) for r, s, dl in zip((o0_ref, o1_ref, o2_ref), (None, os1_ref, os2_ref), dils)]
    l0, l1, l2 = [natural(r, s, dl) for r, s, dl in zip((l0_ref, l1_ref, l2_ref), (None, ls1_ref, ls2_ref), dils)]

    lmax = jnp.maximum(jnp.maximum(l0, l1), l2)
    e0, e1, e2 = jnp.exp(l0 - lmax), jnp.exp(l1 - lmax), jnp.exp(l2 - lmax)
    inv = 1.0 / (e0 + e1 + e2)
    ex = ex_ref[...]

    def widen(w):
        hi, lo = _split(w)
        return _dot(jnp.concatenate([hi, lo], axis=1), ex)

    w_groups = [widen(e * inv) for e in (e0, e1, e2)]
    gate_dil = jax.nn.sigmoid(_dot(hb, wbg_ref[:, :d]))
    o_dil = w_groups[0] * o_nat[0] + w_groups[1] * o_nat[1] + w_groups[2] * o_nat[2]
    branch_dil = _dot(o_dil.astype(_BF16), wbd_ref[...])
    branch_sb = _dot(osb_ref[...], wbs_ref[...])
    gate_sb = jax.nn.sigmoid(_dot(hb, wbg_ref[:, d:]))
    merged = gate_dil * branch_dil + gate_sb * branch_sb
    x1 = x + mod_ref[:, 2 * d:3 * d] * _dot(merged.astype(_BF16), wout_ref[...])
    x1_ref[...] = x1

    h2 = _rms_mod(x1, g2_ref[...], mod_ref[:, 4 * d:5 * d], mod_ref[:, 3 * d:4 * d])
    h2p_ref[...] = _pack_halves(h2)

    hh, hl = _split(h2)
    logits = _dot(hh, wrh_ref[...]) + (_dot(hl, wrh_ref[...]) + _dot(hh, wrl_ref[...])) + br_ref[...]
    lane = lax.broadcasted_iota(jnp.int32, (tm, ROUTER_LANES), 1).astype(_F32)
    far = float(ROUTER_LANES)

    def top(vals):
        m = jnp.max(vals, axis=1, keepdims=True)
        return m, jnp.min(jnp.where(vals == m, lane, far), axis=1, keepdims=True)

    is_group = lane < N_GROUPS
    mg, gsel = top(jnp.where(is_group, logits, NEG_BIG))
    pg_top = 1.0 / jnp.sum(jnp.where(is_group, jnp.exp(logits - mg), 0.0), axis=1, keepdims=True)
    lane0 = EXPERT_LANE0 + EXPERTS_PER_GROUP * gsel
    le = jnp.where((lane >= lane0) & (lane < lane0 + EXPERTS_PER_GROUP), logits, NEG_BIG)
    m1, i1 = top(le)
    m2, i2 = top(jnp.where(lane == i1, NEG_BIG, le))
    t2 = jnp.exp(m2 - m1)
    w0 = pg_top / (1.0 + t2)
    w1 = pg_top * t2 / (1.0 + t2)

    @pl.when(pl.program_id(0) == 0)
    def _():
        cnt_ref[...] = jnp.zeros_like(cnt_ref)

    sel0, sel1 = lane == i1, lane == i2
    onehot = jnp.where(sel0 | sel1, 1.0, 0.0)
    before = _dot(tri_ref[...], onehot.astype(_BF16)) + cnt_ref[0:1, :]
    r0 = jnp.sum(jnp.where(sel0, before, 0.0), axis=1, keepdims=True)
    r1 = jnp.sum(jnp.where(sel1, before, 0.0), axis=1, keepdims=True)
    cnt_ref[...] += jnp.sum(onehot, axis=0, keepdims=True)
    cnt_out_ref[...] = cnt_ref[...]
    cols = (i1 - EXPERT_LANE0, i2 - EXPERT_LANE0, r0, r1, w0, w1)
    rinfo = jnp.zeros((tm, ROUTER_LANES), _F32)
    for c, v in enumerate(cols):
        rinfo = jnp.where(lane == float(c), v, rinfo)
    rinfo_ref[...] = rinfo
    er_ref[...] = jnp.transpose(rinfo)[0:8, :].astype(jnp.int32)


def _merge(x2, mod3, g1, g2, outs, lses, osb, wbg, wbd, wbs, wout, wrh, wrl, br, ex, tri, seq, part):
    d = x2.shape[1]
    tm = ROW_TILE
    per_b = seq // tm
    t = x2.shape[0] // MOE_PARTS
    first = part * (t // tm)
    src = lambda i: (i + first, 0)
    row = lambda i: (i, 0)
    const = lambda i: (0, 0)
    full = lambda a: pl.BlockSpec(a.shape, const)
    gw = GROUP_WIDTH
    dils = [dl for _, dl in DIL_PATTERNS]
    return pl.pallas_call(
        _merge_body,
        out_shape=[jax.ShapeDtypeStruct((t, d), _F32),
                   jax.ShapeDtypeStruct((t, d // 2), jnp.uint32),
                   jax.ShapeDtypeStruct((t, ROUTER_LANES), _F32),
                   jax.ShapeDtypeStruct((8, t), jnp.int32),
                   jax.ShapeDtypeStruct((8, ROUTER_LANES), _F32)],
        grid=(t // tm,),
        in_specs=[pl.BlockSpec((tm, d), src),
                  pl.BlockSpec((None, 1, mod3.shape[2]), lambda i: ((i + first) // per_b, 0, 0)),
                  full(g1), full(g2)]
                 + [pl.BlockSpec((tm // dl, dl * gw), src) for dl in dils]
                 + [pl.BlockSpec((tm // dl, dl * LANES), src) for dl in dils]
                 + [pl.BlockSpec((tm, gw), src)]
                 + [full(a) for a in (wbg, wbd, wbs, wout, wrh, wrl, br, ex, tri)],
        out_specs=[pl.BlockSpec((tm, d), row),
                   pl.BlockSpec((tm, d // 2), row),
                   pl.BlockSpec((tm, ROUTER_LANES), row),
                   pl.BlockSpec((8, tm), lambda i: (0, i)),
                   pl.BlockSpec((8, ROUTER_LANES), const)],
        scratch_shapes=[pltpu.VMEM((8, ROUTER_LANES), _F32),
                        pltpu.VMEM((gw // LANES, tm, LANES), _F32), pltpu.VMEM((gw // LANES, tm, LANES), _F32),
                        pltpu.VMEM((1, tm, LANES), _F32), pltpu.VMEM((1, tm, LANES), _F32)],
        compiler_params=_params("arbitrary"),
        name="merge",
    )(x2, mod3, g1, g2, *outs, *lses, osb, wbg, wbd, wbs, wout, wrh, wrl, br, ex, tri)


def _dest_body(ps_ref, er_ref, d_ref):
    e = er_ref[0:2, :]
    start = jnp.zeros_like(e)
    for x in range(N_EXPERTS):
        start = jnp.where(e == x, ps_ref[x], start)
    d_ref[...] = start + er_ref[2:4, :]


def _dest(pstart, er):
    t = er.shape[1]
    tw = min(DEST_TILE, t)
    return pl.pallas_call(
        _dest_body,
        out_shape=jax.ShapeDtypeStruct((2, t), jnp.int32),
        grid_spec=pltpu.PrefetchScalarGridSpec(
            num_scalar_prefetch=1,
            grid=(t // tw,),
            in_specs=[pl.BlockSpec((8, tw), lambda i, ps: (0, i))],
            out_specs=pl.BlockSpec((2, tw), lambda i, ps: (0, i))),
        compiler_params=_params("arbitrary"),
        name="dest",
    )(pstart, er)


def _sc_mesh():
    return plsc.VectorSubcoreMesh(core_axis_name="core", subcore_axis_name="subcore",
                                  num_cores=SC_CORES, num_subcores=SC_SUBCORES)


def _sc_worker():
    return lax.axis_index("subcore") * SC_CORES + lax.axis_index("core")


def _sc_scatter(x, idx0, idx1, n_slots):
    chunks = idx0.shape[0]
    per = chunks // SC_WORKERS
    win = idx0.shape[1]

    @functools.partial(
        pl.kernel, mesh=_sc_mesh(), out_type=jax.ShapeDtypeStruct((n_slots, x.shape[1]), x.dtype),
        scratch_types=[pltpu.VMEM((1, win), jnp.int32), pltpu.VMEM((1, win), jnp.int32),
                       pltpu.VMEM((win, x.shape[1]), x.dtype), pltpu.SemaphoreType.DMA],
        name="sc_scatter")
    def run(x_hbm, i0_hbm, i1_hbm, o_hbm, i0_v, i1_v, rows_v, sem):
        wid = _sc_worker()

        @pl.loop(0, per)
        def _(j):
            c = wid * per + j
            pltpu.sync_copy(i0_hbm.at[pl.ds(c, 1)], i0_v)
            pltpu.sync_copy(i1_hbm.at[pl.ds(c, 1)], i1_v)
            pltpu.sync_copy(x_hbm.at[pl.ds(c * win, win)], rows_v)
            first = pltpu.async_copy(rows_v, o_hbm.at[i0_v.at[0]], sem)
            second = pltpu.async_copy(rows_v, o_hbm.at[i1_v.at[0]], sem)
            first.wait()
            second.wait()

    return run(x, idx0, idx1)


def _sc_gather(table, idx):
    chunks, win = idx.shape
    per = chunks // SC_WORKERS

    @functools.partial(
        pl.kernel, mesh=_sc_mesh(), out_type=jax.ShapeDtypeStruct((chunks * win, table.shape[1]), table.dtype),
        scratch_types=[pltpu.VMEM((1, win), jnp.int32), pltpu.VMEM((win, table.shape[1]), table.dtype),
                       pltpu.SemaphoreType.DMA],
        name="sc_gather")
    def run(t_hbm, i_hbm, o_hbm, i_v, rows_v, sem):
        wid = _sc_worker()

        @pl.loop(0, per)
        def _(j):
            c = wid * per + j
            pltpu.sync_copy(i_hbm.at[pl.ds(c, 1)], i_v)
            pltpu.async_copy(t_hbm.at[i_v.at[0]], rows_v, sem).wait()
            pltpu.sync_copy(rows_v, o_hbm.at[pl.ds(c * win, win)])

    return run(table, idx)


def _pack_halves(a):
    h = a.shape[1] // 2
    lo = lax.bitcast_convert_type(a[:, :h].astype(_BF16).astype(_F32), jnp.uint32) >> 16
    hi = lax.bitcast_convert_type(a[:, h:].astype(_BF16).astype(_F32), jnp.uint32) & jnp.uint32(0xFFFF0000)
    return lo | hi


def _unpack_halves(w):
    return jnp.concatenate(
        [lax.bitcast_convert_type(w << 16, _F32), lax.bitcast_convert_type(w & jnp.uint32(0xFFFF0000), _F32)], axis=1)


def _experts_body(ce_ref, nv_ref, nu_ref, seg_ref, nxt_ref, xs_hbm, wg_ref, wu_ref, wd_ref, ys_hbm,
                  wgb_ref, wub_ref, wdb_ref, wgf_ref, wuf_ref, wdf_ref, sem):
    def fetch(expert, slot):
        return [pltpu.make_async_copy(src.at[expert], dst.at[slot], sem.at[slot])
                for src, dst in ((wg_ref, wgf_ref), (wu_ref, wuf_ref), (wd_ref, wdf_ref))]

    def chunk(xs_ref, ys_ref):
        c = pl.program_id(0)

        @pl.when(seg_ref[c] >= 0)
        def _():
            slot = seg_ref[c] & 1

            @pl.when(c == 0)
            def _():
                for copy in fetch(ce_ref[0], 0):
                    copy.start()

            for copy in fetch(ce_ref[c], slot):
                copy.wait()
            wgb_ref[...] = wgf_ref[slot].astype(_BF16)
            wub_ref[...] = wuf_ref[slot].astype(_BF16)
            wdb_ref[...] = wdf_ref[slot].astype(_BF16)

            @pl.when(nxt_ref[c] >= 0)
            def _():
                for copy in fetch(nxt_ref[c], 1 - slot):
                    copy.start()

        row = lax.broadcasted_iota(jnp.int32, xs_ref.shape, 0)
        x = _unpack_halves(jnp.where(row < nv_ref[c], xs_ref[...], jnp.uint32(0))).astype(_BF16)
        half = x.shape[0] // 2
        ups = [(_dot(x[r:r + half], wgb_ref[...]), _dot(x[r:r + half], wub_ref[...])) for r in (0, half)]
        for (g, u), r in zip(ups, (0, half)):
            hmid = (g * jax.nn.sigmoid(g)) * u
            ys_ref[r:r + half, :] = _pack_halves(_dot(hmid.astype(_BF16), wdb_ref[...]))

    ch = EXPERT_CHUNK
    pltpu.emit_pipeline(
        chunk,
        grid=(nu_ref[0],),
        in_specs=[pl.BlockSpec((ch, xs_hbm.shape[1]), lambda c: (c, 0), pipeline_mode=pl.Buffered(EXPERT_IN_BUFFERS))],
        out_specs=[pl.BlockSpec((ch, ys_hbm.shape[1]), lambda c: (c, 0))],
    )(xs_hbm, ys_hbm)


def _experts(chunk_e, n_valid, n_used, seg, nxt, xs, wg, wu, wd):
    n_slots, w = xs.shape
    d, de = wg.shape[1], wg.shape[2]
    hbm = pl.BlockSpec(memory_space=pl.ANY)
    return pl.pallas_call(
        _experts_body,
        out_shape=jax.ShapeDtypeStruct((n_slots, d // 2), jnp.uint32),
        grid_spec=pltpu.PrefetchScalarGridSpec(
            num_scalar_prefetch=5,
            grid=(1,),
            in_specs=[hbm, hbm, hbm, hbm],
            out_specs=hbm,
            scratch_shapes=[pltpu.VMEM((d, de), _BF16), pltpu.VMEM((d, de), _BF16), pltpu.VMEM((de, d), _BF16),
                            pltpu.VMEM((2, d, de), _F32), pltpu.VMEM((2, d, de), _F32), pltpu.VMEM((2, de, d), _F32),
                            pltpu.SemaphoreType.DMA((2,))]),
        compiler_params=_params("arbitrary"),
        name="experts",
    )(chunk_e, n_valid, n_used, seg, nxt, xs, wg, wu, wd)


def _combine(x1, rinfo, mod3, gathered, seq, part, out_so_far):
    t, d = x1.shape
    tf = min(COMBINE_TILE, seq)
    per_b = seq // tf
    nt = t // tf
    first = part * nt

    def tile(x1_ref, rinfo_ref, mod_ref, y0_ref, y1_ref, o_ref):
        y = rinfo_ref[:, 4:5] * _unpack_halves(y0_ref[...]) + rinfo_ref[:, 5:6] * _unpack_halves(y1_ref[...])
        o_ref[...] = x1_ref[...] + mod_ref[0, :, 5 * D_MODEL:6 * D_MODEL] * y

    def body(x1_hbm, rinfo_hbm, mod_hbm, g_hbm, *rest):
        o_hbm = rest[-1]
        deep = pl.Buffered(COMBINE_IN_BUFFERS)
        pltpu.emit_pipeline(
            tile,
            grid=(nt,),
            in_specs=[pl.BlockSpec((tf, d), lambda i: (i, 0), pipeline_mode=deep),
                      pl.BlockSpec((tf, ROUTER_LANES), lambda i: (i, 0), pipeline_mode=deep),
                      pl.BlockSpec((1, 1, mod3.shape[2]), lambda i: ((i + first) // per_b, 0, 0)),
                      pl.BlockSpec((tf, d // 2), lambda i: (i, 0), pipeline_mode=deep),
                      pl.BlockSpec((tf, d // 2), lambda i: (i + nt, 0), pipeline_mode=deep)],
            out_specs=[pl.BlockSpec((tf, d), lambda i: (i + first, 0))],
        )(x1_hbm, rinfo_hbm, mod_hbm, g_hbm, g_hbm, o_hbm)

    hbm = pl.BlockSpec(memory_space=pl.ANY)
    args = [x1, rinfo, mod3, gathered]
    aliases = {}
    if out_so_far is not None:
        args.append(out_so_far)
        aliases = {len(args) - 1: 0}
    return pl.pallas_call(
        body,
        out_shape=jax.ShapeDtypeStruct((t * MOE_PARTS, d), _F32),
        grid=(1,),
        in_specs=[hbm] * len(args),
        out_specs=hbm,
        input_output_aliases=aliases,
        compiler_params=_params("arbitrary"),
        name="combine",
    )(*args)


def _rope_trig(positions):
    inv_freq = ROPE_THETA ** (-jnp.arange(0, ROPE_DIM, 2, dtype=_F32) / ROPE_DIM)
    ang = positions.reshape(-1).astype(_F32)[:, None] * inv_freq
    return jnp.concatenate([jnp.cos(ang), jnp.sin(ang)], axis=1)


def _layer(x, mod, positions, g_mix, g_ffn, w_in, w_bg, qg, kg, w_bd, w_bs, w_out, w_rg, b_rg, w_re, b_re,
           w_eg, w_eu, w_ed):
    batch, seq, d = x.shape
    t = batch * seq
    x2 = x.reshape(t, d)
    mod3 = mod.reshape(batch, 1, mod.shape[1])
    gw = GROUP_WIDTH

    lane = jnp.arange(gw)
    bd = jnp.where(lane[:, None] // HEAD_DIM == lane[None, :] // HEAD_DIM, 1.0 / HEAD_DIM, 0.0).astype(_BF16)
    ex = (jnp.arange(LANES)[:, None] == (lane[None, :] // HEAD_DIM) * LSE_SEG).astype(_BF16)
    ex = jnp.concatenate([ex, ex], axis=0)
    tri = (jnp.arange(ROW_TILE)[:, None] > jnp.arange(ROW_TILE)[None, :]).astype(_BF16)
    tile4 = lambda g: jnp.tile(g.astype(_F32), HEADS_PER_GROUP).reshape(1, gw)
    wr = jnp.zeros((d, ROUTER_LANES), _F32).at[:, :N_GROUPS].set(w_rg).at[:, N_GROUPS:N_GROUPS + N_EXPERTS].set(w_re)
    wrh = wr.astype(_BF16)
    wrl = (wr - wrh.astype(_F32)).astype(_BF16)
    br = jnp.zeros((1, ROUTER_LANES), _F32).at[0, :N_GROUPS].set(b_rg).at[0, N_GROUPS:N_GROUPS + N_EXPERTS].set(b_re)

    d0, d1, d2, sbp = _qkv(x2, mod3, g_mix.reshape(1, d), w_in.astype(_BF16), tile4(qg) * ATTN_SCALE, tile4(kg),
                           _rope_trig(positions), bd, seq)
    dil = [_dilated_group(a, dl, batch, seq) for a, (_, dl) in zip((d0, d1, d2), DIL_PATTERNS)]
    osb = _stick_breaking(sbp, batch, seq)

    merge_weights = (w_bg.astype(_BF16), w_bd.astype(_BF16), w_bs.astype(_BF16), w_out.astype(_BF16))
    tp = t // MOE_PARTS
    ch = EXPERT_CHUNK
    win = SC_INDEX_WINDOW
    n_chunks = -(-2 * tp // ch) + N_EXPERTS
    chunk_start = jnp.arange(n_chunks, dtype=jnp.int32) * ch
    expert_ids = jnp.arange(N_EXPERTS, dtype=jnp.int32)
    out = None
    for part in range(MOE_PARTS):
        x1, h2p, rinfo, er, cnt = _merge(
            x2, mod3, g_mix.reshape(1, d), g_ffn.reshape(1, d), [o for o, _ in dil], [l for _, l in dil], osb,
            *merge_weights, wrh, wrl, br, ex, tri, seq, part)

        counts = cnt[0, EXPERT_LANE0:EXPERT_LANE0 + N_EXPERTS].astype(jnp.int32)
        padded = (counts + ch - 1) // ch * ch
        pend = jnp.cumsum(padded)
        pstart = pend - padded
        chunk_e = jnp.minimum(jnp.sum((pend[None, :] <= chunk_start[:, None]).astype(jnp.int32), axis=1),
                              N_EXPERTS - 1)
        n_used = (pend[-1:] // ch).astype(jnp.int32)
        begin = chunk_start[:, None]
        inside = (pstart[None, :] <= begin) & (begin < pend[None, :])
        n_valid = jnp.sum(jnp.where(inside, jnp.clip(counts[None, :] - (begin - pstart[None, :]), 0, ch), 0), axis=1)

        dest = _dest(pstart, er)
        xs = _sc_scatter(h2p, dest[0].reshape(tp // win, win), dest[1].reshape(tp // win, win), n_chunks * ch)
        first = (chunk_e != jnp.concatenate([jnp.full((1,), -1, jnp.int32), chunk_e[:-1]])) & (chunk_start < pend[-1])
        seg_no = jnp.cumsum(first.astype(jnp.int32)) - 1
        seg = jnp.where(first, seg_no, -1 - seg_no)
        later = (expert_ids[None, :] > expert_ids[:, None]) & (padded > 0)[None, :]
        next_expert = jnp.min(jnp.where(later, expert_ids[None, :], N_EXPERTS), axis=1)
        next_expert = jnp.where(next_expert == N_EXPERTS, -1, next_expert)
        nxt = jnp.sum(jnp.where(chunk_e[:, None] == expert_ids[None, :], next_expert[None, :], 0), axis=1)
        ys = _experts(chunk_e, n_valid, n_used, seg.astype(jnp.int32), nxt.astype(jnp.int32), xs, w_eg, w_eu, w_ed)
        gathered = _sc_gather(ys, dest.reshape(2 * tp // win, win))
        out = _combine(x1, rinfo, mod3, gathered, seq, part, out)
    return out.reshape(batch, seq, d)


def kernel(x, c, positions, w_ada, b_ada, g_norm_mix, g_norm_ffn, w_in, w_branch_gate, q_norm_g, k_norm_g,
           w_branch_dil, w_branch_sb, w_out, w_router_group, b_router_group, w_router_expert, b_router_expert,
           w_expert_gate, w_expert_up, w_expert_down):
    for l in range(w_ada.shape[0]):
        mod = _ada(c, w_ada[l], b_ada[l])
        x = _layer(x, mod, positions, g_norm_mix[l], g_norm_ffn[l], w_in[l], w_branch_gate[l], q_norm_g[l],
                   k_norm_g[l], w_branch_dil[l], w_branch_sb[l], w_out[l], w_router_group[l], b_router_group[l],
                   w_router_expert[l], b_router_expert[l], w_expert_gate[l], w_expert_up[l], w_expert_down[l])
    return x
```

```python
import functools

import jax
import jax.numpy as jnp
from jax import lax
from jax.experimental import pallas as pl
from jax.experimental.pallas import tpu as pltpu
from jax.experimental.pallas import tpu_sc as plsc

D_MODEL = 1024
HEAD_DIM = 64
DIL_PATTERNS = ((128, 1), (512, 4), (2048, 16))
HEADS_PER_GROUP = 4
GROUP_WIDTH = HEADS_PER_GROUP * HEAD_DIM
N_DIL_GROUPS = len(DIL_PATTERNS)
DIL_WIDTH = N_DIL_GROUPS * GROUP_WIDTH
QKV_WIDTH = 3 * DIL_WIDTH + 3 * GROUP_WIDTH
WINDOW_KEYS = 128
ROPE_THETA = 500000.0
ROPE_DIM = HEAD_DIM // 4
N_GROUPS = 4
EXPERTS_PER_GROUP = 8
N_EXPERTS = N_GROUPS * EXPERTS_PER_GROUP
D_EXPERT = 512
RMS_EPS = 1e-6
ATTN_SCALE = HEAD_DIM ** -0.5

LANES = 128
ROUTER_LANES = LANES
EXPERT_LANE0 = N_GROUPS
LSE_SEG = LANES // HEADS_PER_GROUP
NEG_BIG = -1e30
SB_DEAD_LOG = -120.0
SB_HEAD_BLOCKS = 3
SB_CHAINS = 8

ROW_TILE = 512
QKV_TILE = 1024
QKV_STAGE_RING = 3
QBLK = 128
DIL_QTILE = 2048
EXPERT_CHUNK = 512
EXPERT_IN_BUFFERS = 3
COMBINE_TILE = 1024
DEST_TILE = 8192
SC_CORES = 2
SC_SUBCORES = 16
SC_WORKERS = SC_CORES * SC_SUBCORES
SC_INDEX_WINDOW = 128
MOE_PARTS = 2
VMEM_LIMIT = 48 * 1024 * 1024

_BF16 = jnp.bfloat16
_F32 = jnp.float32
_NT = (((1,), (1,)), ((), ()))


def _dot(a, b):
    return jnp.dot(a, b, preferred_element_type=_F32)


def _dot_nt(a, b):
    return lax.dot_general(a, b, _NT, preferred_element_type=_F32)


def _split(a):
    hi = a.astype(_BF16)
    lo = (a - hi.astype(_F32)).astype(_BF16)
    return hi, lo


def _dot3(a, b):
    ah, al = _split(a)
    bh, bl = _split(b)
    return _dot(ah, bh) + (_dot(ah, bl) + _dot(al, bh))


def _rms_mod(x, g, scale, shift):
    y = x * lax.rsqrt(jnp.mean(x * x, axis=-1, keepdims=True) + RMS_EPS)
    return y * g * (1.0 + scale) + shift


def _params(*sem):
    return pltpu.CompilerParams(dimension_semantics=sem, vmem_limit_bytes=VMEM_LIMIT)


def _ada_body(c_ref, w_ref, b_ref, o_ref):
    c = c_ref[...]
    o_ref[...] = _dot3(c * jax.nn.sigmoid(c), w_ref[...]) + b_ref[...]


def _ada(c, w_ada, b_ada):
    b, d = c.shape
    n = w_ada.shape[1]
    rows = -(-b // 16) * 16
    cp = jnp.zeros((rows, d), _F32).at[:b].set(c)
    nt = 1536
    out = pl.pallas_call(
        _ada_body,
        out_shape=jax.ShapeDtypeStruct((rows, n), _F32),
        grid=(n // nt,),
        in_specs=[pl.BlockSpec((rows, d), lambda j: (0, 0)),
                  pl.BlockSpec((d, nt), lambda j: (0, j)),
                  pl.BlockSpec((1, nt), lambda j: (0, j))],
        out_specs=pl.BlockSpec((rows, nt), lambda j: (0, j)),
        compiler_params=_params("arbitrary"),
        name="ada",
    )(cp, w_ada, b_ada.reshape(1, n))
    return out[:b]


def _qkv_body(x_ref, mod_ref, g_ref, w_ref, qg_ref, kg_ref, trig_ref, bd_ref,
              o0_ref, o1_ref, o2_ref, osb_ref, st_ref, acc_ref):
    d = D_MODEL
    tm = x_ref.shape[0]
    h = _rms_mod(x_ref[...], g_ref[...], mod_ref[:, d:2 * d], mod_ref[:, 0:d])
    hb = h.astype(_BF16)
    half = ROPE_DIM // 2
    in_head = lax.broadcasted_iota(jnp.int32, (tm, LANES), 1) & (HEAD_DIM - 1)
    freq = in_head & (half - 1)
    trig = jnp.concatenate([trig_ref[...], jnp.zeros((tm, LANES - ROPE_DIM), _F32)], axis=1)
    cos = jnp.take_along_axis(trig, freq, axis=1)
    sin = jnp.take_along_axis(trig, freq + half, axis=1)
    cc = jnp.where(in_head < ROPE_DIM, cos, 1.0)
    s1 = jnp.where(in_head < half, -sin, 0.0)
    s2 = jnp.where((in_head >= half) & (in_head < ROPE_DIM), sin, 0.0)
    cc, s1, s2 = [jnp.concatenate([a, a], axis=1) for a in (cc, s1, s2)]
    bd = bd_ref[...]
    gw = GROUP_WIDTH

    def normed_rotated(acc, gain):
        ms = _dot((acc * acc).astype(_BF16), bd)
        y = acc * lax.rsqrt(ms + RMS_EPS) * gain
        return y * cc + pltpu.roll(y, gw - ROPE_DIM // 2, 1) * s1 + pltpu.roll(y, ROPE_DIM // 2, 1) * s2

    def store(o_ref, dil, part, y):
        if dil == 1:
            o_ref[:, part * gw:(part + 1) * gw] = y.astype(_BF16)
            return
        for s in range(gw // LANES):
            st_ref[s] = y[:, s * LANES:(s + 1) * LANES]
        for r in range(dil):
            for s in range(gw // LANES):
                col0 = (3 * r + part) * gw + s * LANES
                o_ref[:, col0:col0 + LANES] = st_ref[s, pl.ds(r, tm // dil, stride=dil), :].astype(_BF16)

    def project(col0):
        return _dot(hb, w_ref[:, col0:col0 + gw])

    outs = (o0_ref, o1_ref, o2_ref)
    dils = [dl for _, dl in DIL_PATTERNS]
    normed = [(g, part, part * DIL_WIDTH + g * gw) for g in range(N_DIL_GROUPS) for part in (0, 1)]
    ring = acc_ref.shape[0]

    def stage(n):
        acc_ref[n % ring] = project(normed[n][2])

    for n in range(ring - 1):
        stage(n)
    plain = [("v", g) for g in range(N_DIL_GROUPS)] + [("sb", part) for part in range(3)]
    for n, (kind, j) in enumerate(plain):
        if kind == "v":
            store(outs[j], dils[j], 2, project(2 * DIL_WIDTH + j * gw))
        else:
            acc = project(3 * DIL_WIDTH + j * gw)
            osb_ref[:, j * gw:(j + 1) * gw] = (acc * ATTN_SCALE if j == 0 else acc).astype(_BF16)
        if n + ring - 1 < len(normed):
            stage(n + ring - 1)
        g, part, _ = normed[n]
        gain = qg_ref[...] if part == 0 else kg_ref[...]
        store(outs[g], dils[g], part, normed_rotated(acc_ref[n % ring], gain))


def _qkv(x2, mod3, g_mix, w_in, qg, kg, trig, bd, seq):
    t, d = x2.shape
    tm = min(QKV_TILE, seq)
    per_b = seq // tm
    row = lambda i: (i, 0)
    const = lambda i: (0, 0)
    width = 3 * GROUP_WIDTH
    dils = [dl for _, dl in DIL_PATTERNS] + [1]
    return pl.pallas_call(
        _qkv_body,
        out_shape=[jax.ShapeDtypeStruct((t // dl, dl * width), _BF16) for dl in dils],
        grid=(t // tm,),
        in_specs=[pl.BlockSpec((tm, d), row),
                  pl.BlockSpec((None, 1, mod3.shape[2]), lambda i: (i // per_b, 0, 0)),
                  pl.BlockSpec((1, d), const),
                  pl.BlockSpec(w_in.shape, const),
                  pl.BlockSpec((1, GROUP_WIDTH), const),
                  pl.BlockSpec((1, GROUP_WIDTH), const),
                  pl.BlockSpec((tm, ROPE_DIM), row),
                  pl.BlockSpec(bd.shape, const)],
        out_specs=[pl.BlockSpec((tm // dl, dl * width), row) for dl in dils],
        scratch_shapes=[pltpu.VMEM((GROUP_WIDTH // LANES, tm, LANES), _F32),
                        pltpu.VMEM((QKV_STAGE_RING, tm, GROUP_WIDTH), _F32)],
        compiler_params=_params("arbitrary"),
        name="qkv",
    )(x2, mod3, g_mix, w_in, qg, kg, trig, bd)


def _dil_body(prev_ref, cur_ref, o_ref, lse_ref, kf_ref, vf_ref):
    tq = cur_ref.shape[0]
    gw = GROUP_WIDTH
    n_res = cur_ref.shape[1] // (3 * gw)
    first = pl.program_id(2) == 0
    for r in range(n_res):
        for part, full_ref in ((1, kf_ref), (2, vf_ref)):
            lanes = slice((3 * r + part) * gw, (3 * r + part + 1) * gw)
            full_ref[r, 0:QBLK, :] = prev_ref[:, lanes]
            full_ref[r, QBLK:, :] = cur_ref[:, lanes]
    nh = HEADS_PER_GROUP
    row = lax.broadcasted_iota(jnp.int32, (nh * QBLK, 2 * QBLK), 0) & (QBLK - 1)
    col = lax.broadcasted_iota(jnp.int32, (nh * QBLK, 2 * QBLK), 1)
    band = (col >= row) & (col <= row + WINDOW_KEYS)
    lane = lax.broadcasted_iota(jnp.int32, (1, GROUP_WIDTH), 1)
    slane = lax.broadcasted_iota(jnp.int32, (1, LANES), 1)
    head_masks = [(lane >= h * HEAD_DIM) & (lane < (h + 1) * HEAD_DIM) for h in range(nh)]
    subs = [(r, j) for r in range(n_res) for j in range(tq // QBLK)]
    rows = [slice(j * QBLK, (j + 1) * QBLK) for _, j in subs]
    window = [slice(j * QBLK, (j + 2) * QBLK) for _, j in subs]
    n = range(len(subs))

    def stack(qj):
        return jnp.concatenate([jnp.where(hm, qj, jnp.zeros_like(qj)) for hm in head_masks], axis=0)

    valid = [band & ((col >= QBLK) | jnp.logical_not(first)) if j == 0 else band for _, j in subs]
    q = [cur_ref[rows[i], 3 * r * gw:(3 * r + 1) * gw] for i, (r, _) in enumerate(subs)]
    s = [jnp.where(valid[i], _dot_nt(stack(q[i]), kf_ref[subs[i][0], window[i], :]), NEG_BIG) for i in n]
    m = [jnp.max(s[i], axis=1, keepdims=True) for i in n]
    p = [jnp.exp(s[i] - m[i]) for i in n]
    l = [jnp.sum(p[i], axis=1, keepdims=True) for i in n]
    o_all = [_dot(p[i].astype(_BF16), vf_ref[subs[i][0], window[i], :]) / l[i] for i in n]
    for i, (r, _) in enumerate(subs):
        lse_all = m[i] + jnp.log(l[i])
        o_acc = jnp.zeros((QBLK, gw), _F32)
        lse_t = jnp.zeros((QBLK, LANES), _F32)
        for h, hm in enumerate(head_masks):
            o_acc = jnp.where(hm, o_all[i][h * QBLK:(h + 1) * QBLK, :], o_acc)
            sm = (slane >= h * LSE_SEG) & (slane < (h + 1) * LSE_SEG)
            lse_t = jnp.where(sm, lse_all[h * QBLK:(h + 1) * QBLK, :], lse_t)
        o_ref[rows[i], r * gw:(r + 1) * gw] = o_acc.astype(_BF16)
        lse_ref[rows[i], r * LANES:(r + 1) * LANES] = lse_t


def _dilated_group(view2, dil, batch, seq):
    sd = seq // dil
    tq = min(DIL_QTILE, sd)
    n_res = min(dil, DIL_QTILE // tq)
    per = tq // QBLK
    gw = GROUP_WIDTH
    view = view2.reshape(batch, sd, dil * 3 * gw)
    o, lse = pl.pallas_call(
        _dil_body,
        out_shape=[jax.ShapeDtypeStruct((batch, sd, dil * gw), _BF16),
                   jax.ShapeDtypeStruct((batch, sd, dil * LANES), _F32)],
        grid=(batch, dil // n_res, sd // tq),
        in_specs=[pl.BlockSpec((None, QBLK, n_res * 3 * gw), lambda b, r, i: (b, jnp.maximum(i * per - 1, 0), r)),
                  pl.BlockSpec((None, tq, n_res * 3 * gw), lambda b, r, i: (b, i, r))],
        out_specs=[pl.BlockSpec((None, tq, n_res * gw), lambda b, r, i: (b, i, r)),
                   pl.BlockSpec((None, tq, n_res * LANES), lambda b, r, i: (b, i, r))],
        scratch_shapes=[pltpu.VMEM((n_res, tq + QBLK, gw), _BF16), pltpu.VMEM((n_res, tq + QBLK, gw), _BF16)],
        compiler_params=_params("arbitrary", "arbitrary", "arbitrary"),
        name=f"dil{dil}",
    )(view, view)
    return o.reshape(batch * sd, dil * gw), lse.reshape(batch * sd, dil * LANES)


def _sb_body(q_ref, k_ref, v_ref, o_ref, carry_ref, acc_ref, qs_ref):
    step_id = pl.program_id(1)
    chains = range(SB_CHAINS)
    blk = [step_id * SB_CHAINS + c for c in chains]
    nh = HEADS_PER_GROUP
    row = lax.broadcasted_iota(jnp.int32, (nh * QBLK, QBLK), 0) & (QBLK - 1)
    col = lax.broadcasted_iota(jnp.int32, (nh * QBLK, QBLK), 1)
    strict = col < row
    ur = lax.broadcasted_iota(jnp.int32, (2 * QBLK, QBLK), 0) & (QBLK - 1)
    uc = lax.broadcasted_iota(jnp.int32, (2 * QBLK, QBLK), 1)
    u = jnp.where(ur > uc, 1.0, 0.0).astype(_BF16)
    lane = lax.broadcasted_iota(jnp.int32, (1, GROUP_WIDTH), 1)
    head_masks = [(lane >= h * HEAD_DIM) & (lane < (h + 1) * HEAD_DIM) for h in range(nh)]
    for c in chains:
        q = q_ref[c * QBLK:(c + 1) * QBLK, :]
        qs_ref[c] = jnp.concatenate([jnp.where(hm, q, jnp.zeros_like(q)) for hm in head_masks], axis=0)

    def softplus(z):
        return jnp.maximum(z, 0.0) + jnp.log(1.0 + jnp.exp(-jnp.abs(z)))

    def later_keys(log_1m):
        hi, lo = _split(log_1m)
        return _dot(jnp.concatenate([hi, lo], axis=1), u)

    nb = SB_HEAD_BLOCKS
    cols = [slice(j * QBLK, (j + 1) * QBLK) for j in range(nb)]
    kbs = [[blk[c] - (nb - 1) + j for j in range(nb)] for c in chains]
    starts = [[pl.multiple_of(jnp.maximum(kb, 0) * QBLK, QBLK) for kb in kbs[c]] for c in chains]
    keep = [[strict if j == nb - 1 else (kbs[c][j] >= 0) for j in range(nb)] for c in chains]
    z = [_dot_nt(qs_ref[c], jnp.concatenate([k_ref[pl.ds(s, QBLK), :] for s in starts[c]], axis=0)) for c in chains]
    sp = [softplus(z[c]) for c in chains]
    log_1m = [[jnp.where(keep[c][j], -sp[c][:, cols[j]], 0.0) for j in range(nb)] for c in chains]
    totals = [[jnp.sum(l, axis=1, keepdims=True) for l in log_1m[c]] for c in chains]
    later = [[later_keys(log_1m[c][j]) for j in range(nb)] for c in chains]
    for c in chains:
        a_blocks = []
        after = jnp.zeros_like(totals[c][0])
        for j in reversed(range(nb)):
            a = jnp.exp((z[c][:, cols[j]] - sp[c][:, cols[j]]) + later[c][j] + after)
            a_blocks.insert(0, jnp.where(keep[c][j], a, 0.0).astype(_BF16))
            after = after + totals[c][j]
        acc_ref[c] = _dot(jnp.concatenate(a_blocks, axis=1),
                          jnp.concatenate([v_ref[pl.ds(s, QBLK), :] for s in starts[c]], axis=0))
        carry_ref[c] = after

    for c in chains:
        def tile(kb, c=c):
            start = pl.multiple_of(kb * QBLK, QBLK)
            z = _dot_nt(qs_ref[c], k_ref[pl.ds(start, QBLK), :])
            sp = softplus(z)
            log_1m = -sp
            a = jnp.exp((z - sp) + later_keys(log_1m) + carry_ref[c])
            acc_ref[c] += _dot(a.astype(_BF16), v_ref[pl.ds(start, QBLK), :])
            carry = carry_ref[c] + jnp.sum(log_1m, axis=1, keepdims=True)
            carry_ref[c] = carry
            return jnp.max(carry)

        def cond(st):
            return (st[0] >= 0) & (st[1] > SB_DEAD_LOG)

        def step(st, tile=tile):
            return st[0] - 1, tile(st[0])

        lax.while_loop(cond, step, (blk[c] - nb, jnp.max(carry_ref[c])))
        out = jnp.zeros((QBLK, GROUP_WIDTH), _F32)
        for h, hm in enumerate(head_masks):
            out = jnp.where(hm, acc_ref[c, h * QBLK:(h + 1) * QBLK, :], out)
        o_ref[c * QBLK:(c + 1) * QBLK, :] = out.astype(_BF16)


def _stick_breaking(arr, batch, seq):
    gw = GROUP_WIDTH
    qt = SB_CHAINS * QBLK
    view = arr.reshape(batch, seq, 3 * gw)
    o = pl.pallas_call(
        _sb_body,
        out_shape=jax.ShapeDtypeStruct((batch, seq, gw), _BF16),
        grid=(batch, seq // qt),
        in_specs=[pl.BlockSpec((None, qt, gw), lambda b, i: (b, i, 0)),
                  pl.BlockSpec((None, seq, gw), lambda b, i: (b, 0, 1)),
                  pl.BlockSpec((None, seq, gw), lambda b, i: (b, 0, 2))],
        out_specs=pl.BlockSpec((None, qt, gw), lambda b, i: (b, i, 0)),
        scratch_shapes=[pltpu.VMEM((SB_CHAINS, HEADS_PER_GROUP * QBLK, 1), _F32),
                        pltpu.VMEM((SB_CHAINS, HEADS_PER_GROUP * QBLK, gw), _F32),
                        pltpu.VMEM((SB_CHAINS, HEADS_PER_GROUP * QBLK, gw), _BF16)],
        compiler_params=_params("arbitrary", "arbitrary"),
        name="sb",
    )(view, view, view)
    return o.reshape(batch * seq, gw)


def _merge_body(x_ref, mod_ref, g1_ref, g2_ref, o0_ref, o1_ref, o2_ref, l0_ref, l1_ref, l2_ref, osb_ref,
                wbg_ref, wbd_ref, wbs_ref, wout_ref, wrh_ref, wrl_ref, br_ref, ex_ref, tri_ref,
                x1_ref, h2p_ref, rinfo_ref, er_ref, cnt_out_ref, cnt_ref, os1_ref, os2_ref, ls1_ref, ls2_ref):
    d = D_MODEL
    tm = x_ref.shape[0]
    x = x_ref[...]
    hb = _rms_mod(x, g1_ref[...], mod_ref[:, d:2 * d], mod_ref[:, 0:d]).astype(_BF16)

    def natural(ref, st_ref, dil):
        if dil == 1:
            return ref[...].astype(_F32)
        slabs = st_ref.shape[0]
        for r in range(dil):
            for s in range(slabs):
                col0 = (r * slabs + s) * LANES
                st_ref[s, pl.ds(r, tm // dil, stride=dil), :] = ref[:, col0:col0 + LANES].astype(_F32)
        return jnp.concatenate([st_ref[s] for s in range(slabs)], axis=1)

    dils = [dl for _, dl in DIL_PATTERNS]
    o_nat = [natural(r, s, dl) for r, s, dl in zip((o0_ref, o1_ref, o2_ref), (None, os1_ref, os2_ref), dils)]
    l0, l1, l2 = [natural(r, s, dl) for r, s, dl in zip((l0_ref, l1_ref, l2_ref), (None, ls1_ref, ls2_ref), dils)]

    lmax = jnp.maximum(jnp.maximum(l0, l1), l2)
    e0, e1, e2 = jnp.exp(l0 - lmax), jnp.exp(l1 - lmax), jnp.exp(l2 - lmax)
    inv = 1.0 / (e0 + e1 + e2)
    ex = ex_ref[...]

    def widen(w):
        hi, lo = _split(w)
        return _dot(jnp.concatenate([hi, lo], axis=1), ex)

    w_groups = [widen(e * inv) for e in (e0, e1, e2)]
    gate_dil = jax.nn.sigmoid(_dot(hb, wbg_ref[:, :d]))
    o_dil = w_groups[0] * o_nat[0] + w_groups[1] * o_nat[1] + w_groups[2] * o_nat[2]
    branch_dil = _dot(o_dil.astype(_BF16), wbd_ref[...])
    branch_sb = _dot(osb_ref[...], wbs_ref[...])
    gate_sb = jax.nn.sigmoid(_dot(hb, wbg_ref[:, d:]))
    merged = gate_dil * branch_dil + gate_sb * branch_sb
    x1 = x + mod_ref[:, 2 * d:3 * d] * _dot(merged.astype(_BF16), wout_ref[...])
    x1_ref[...] = x1

    h2 = _rms_mod(x1, g2_ref[...], mod_ref[:, 4 * d:5 * d], mod_ref[:, 3 * d:4 * d])
    h2p_ref[...] = _pack_halves(h2)

    hh, hl = _split(h2)
    logits = _dot(hh, wrh_ref[...]) + (_dot(hl, wrh_ref[...]) + _dot(hh, wrl_ref[...])) + br_ref[...]
    lane = lax.broadcasted_iota(jnp.int32, (tm, ROUTER_LANES), 1).astype(_F32)
    far = float(ROUTER_LANES)

    def top(vals):
        m = jnp.max(vals, axis=1, keepdims=True)
        return m, jnp.min(jnp.where(vals == m, lane, far), axis=1, keepdims=True)

    is_group = lane < N_GROUPS
    mg, gsel = top(jnp.where(is_group, logits, NEG_BIG))
    pg_top = 1.0 / jnp.sum(jnp.where(is_group, jnp.exp(logits - mg), 0.0), axis=1, keepdims=True)
    lane0 = EXPERT_LANE0 + EXPERTS_PER_GROUP * gsel
    le = jnp.where((lane >= lane0) & (lane < lane0 + EXPERTS_PER_GROUP), logits, NEG_BIG)
    m1, i1 = top(le)
    m2, i2 = top(jnp.where(lane == i1, NEG_BIG, le))
    t2 = jnp.exp(m2 - m1)
    w0 = pg_top / (1.0 + t2)
    w1 = pg_top * t2 / (1.0 + t2)

    @pl.when(pl.program_id(0) == 0)
    def _():
        cnt_ref[...] = jnp.zeros_like(cnt_ref)

    sel0, sel1 = lane == i1, lane == i2
    onehot = jnp.where(sel0 | sel1, 1.0, 0.0)
    before = _dot(tri_ref[...], onehot.astype(_BF16)) + cnt_ref[0:1, :]
    r0 = jnp.sum(jnp.where(sel0, before, 0.0), axis=1, keepdims=True)
    r1 = jnp.sum(jnp.where(sel1, before, 0.0), axis=1, keepdims=True)
    cnt_ref[...] += jnp.sum(onehot, axis=0, keepdims=True)
    cnt_out_ref[...] = cnt_ref[...]
    cols = (i1 - EXPERT_LANE0, i2 - EXPERT_LANE0, r0, r1, w0, w1)
    rinfo = jnp.zeros((tm, ROUTER_LANES), _F32)
    for c, v in enumerate(cols):
        rinfo = jnp.where(lane == float(c), v, rinfo)
    rinfo_ref[...] = rinfo
    er_ref[...] = jnp.transpose(rinfo)[0:8, :].astype(jnp.int32)


def _merge(x2, mod3, g1, g2, outs, lses, osb, wbg, wbd, wbs, wout, wrh, wrl, br, ex, tri, seq, part):
    d = x2.shape[1]
    tm = ROW_TILE
    per_b = seq // tm
    t = x2.shape[0] // MOE_PARTS
    first = part * (t // tm)
    src = lambda i: (i + first, 0)
    row = lambda i: (i, 0)
    const = lambda i: (0, 0)
    full = lambda a: pl.BlockSpec(a.shape, const)
    gw = GROUP_WIDTH
    dils = [dl for _, dl in DIL_PATTERNS]
    return pl.pallas_call(
        _merge_body,
        out_shape=[jax.ShapeDtypeStruct((t, d), _F32),
                   jax.ShapeDtypeStruct((t, d // 2), jnp.uint32),
                   jax.ShapeDtypeStruct((t, ROUTER_LANES), _F32),
                   jax.ShapeDtypeStruct((8, t), jnp.int32),
                   jax.ShapeDtypeStruct((8, ROUTER_LANES), _F32)],
        grid=(t // tm,),
        in_specs=[pl.BlockSpec((tm, d), src),
                  pl.BlockSpec((None, 1, mod3.shape[2]), lambda i: ((i + first) // per_b, 0, 0)),
                  full(g1), full(g2)]
                 + [pl.BlockSpec((tm // dl, dl * gw), src) for dl in dils]
                 + [pl.BlockSpec((tm // dl, dl * LANES), src) for dl in dils]
                 + [pl.BlockSpec((tm, gw), src)]
                 + [full(a) for a in (wbg, wbd, wbs, wout, wrh, wrl, br, ex, tri)],
        out_specs=[pl.BlockSpec((tm, d), row),
                   pl.BlockSpec((tm, d // 2), row),
                   pl.BlockSpec((tm, ROUTER_LANES), row),
                   pl.BlockSpec((8, tm), lambda i: (0, i)),
                   pl.BlockSpec((8, ROUTER_LANES), const)],
        scratch_shapes=[pltpu.VMEM((8, ROUTER_LANES), _F32),
                        pltpu.VMEM((gw // LANES, tm, LANES), _F32), pltpu.VMEM((gw // LANES, tm, LANES), _F32),
                        pltpu.VMEM((1, tm, LANES), _F32), pltpu.VMEM((1, tm, LANES), _F32)],
        compiler_params=_params("arbitrary"),
        name="merge",
    )(x2, mod3, g1, g2, *outs, *lses, osb, wbg, wbd, wbs, wout, wrh, wrl, br, ex, tri)


def _dest_body(ps_ref, er_ref, d_ref):
    e = er_ref[0:2, :]
    start = jnp.zeros_like(e)
    for x in range(N_EXPERTS):
        start = jnp.where(e == x, ps_ref[x], start)
    d_ref[...] = start + er_ref[2:4, :]


def _dest(pstart, er):
    t = er.shape[1]
    tw = min(DEST_TILE, t)
    return pl.pallas_call(
        _dest_body,
        out_shape=jax.ShapeDtypeStruct((2, t), jnp.int32),
        grid_spec=pltpu.PrefetchScalarGridSpec(
            num_scalar_prefetch=1,
            grid=(t // tw,),
            in_specs=[pl.BlockSpec((8, tw), lambda i, ps: (0, i))],
            out_specs=pl.BlockSpec((2, tw), lambda i, ps: (0, i))),
        compiler_params=_params("arbitrary"),
        name="dest",
    )(pstart, er)


def _sc_mesh():
    return plsc.VectorSubcoreMesh(core_axis_name="core", subcore_axis_name="subcore",
                                  num_cores=SC_CORES, num_subcores=SC_SUBCORES)


def _sc_worker():
    return lax.axis_index("subcore") * SC_CORES + lax.axis_index("core")


def _sc_scatter(x, idx0, idx1, n_slots):
    chunks = idx0.shape[0]
    per = chunks // SC_WORKERS
    win = idx0.shape[1]

    @functools.partial(
        pl.kernel, mesh=_sc_mesh(), out_type=jax.ShapeDtypeStruct((n_slots, x.shape[1]), x.dtype),
        scratch_types=[pltpu.VMEM((1, win), jnp.int32), pltpu.VMEM((1, win), jnp.int32),
                       pltpu.VMEM((win, x.shape[1]), x.dtype), pltpu.SemaphoreType.DMA],
        name="sc_scatter")
    def run(x_hbm, i0_hbm, i1_hbm, o_hbm, i0_v, i1_v, rows_v, sem):
        wid = _sc_worker()

        @pl.loop(0, per)
        def _(j):
            c = wid * per + j
            pltpu.sync_copy(i0_hbm.at[pl.ds(c, 1)], i0_v)
            pltpu.sync_copy(i1_hbm.at[pl.ds(c, 1)], i1_v)
            pltpu.sync_copy(x_hbm.at[pl.ds(c * win, win)], rows_v)
            first = pltpu.async_copy(rows_v, o_hbm.at[i0_v.at[0]], sem)
            second = pltpu.async_copy(rows_v, o_hbm.at[i1_v.at[0]], sem)
            first.wait()
            second.wait()

    return run(x, idx0, idx1)


def _sc_gather(table, idx):
    chunks, win = idx.shape
    per = chunks // SC_WORKERS

    @functools.partial(
        pl.kernel, mesh=_sc_mesh(), out_type=jax.ShapeDtypeStruct((chunks * win, table.shape[1]), table.dtype),
        scratch_types=[pltpu.VMEM((1, win), jnp.int32), pltpu.VMEM((win, table.shape[1]), table.dtype),
                       pltpu.SemaphoreType.DMA],
        name="sc_gather")
    def run(t_hbm, i_hbm, o_hbm, i_v, rows_v, sem):
        wid = _sc_worker()

        @pl.loop(0, per)
        def _(j):
            c = wid * per + j
            pltpu.sync_copy(i_hbm.at[pl.ds(c, 1)], i_v)
            pltpu.async_copy(t_hbm.at[i_v.at[0]], rows_v, sem).wait()
            pltpu.sync_copy(rows_v, o_hbm.at[pl.ds(c * win, win)])

    return run(table, idx)


def _pack_halves(a):
    h = a.shape[1] // 2
    lo = lax.bitcast_convert_type(a[:, :h].astype(_BF16).astype(_F32), jnp.uint32) >> 16
    hi = lax.bitcast_convert_type(a[:, h:].astype(_BF16).astype(_F32), jnp.uint32) & jnp.uint32(0xFFFF0000)
    return lo | hi


def _unpack_halves(w):
    return jnp.concatenate(
        [lax.bitcast_convert_type(w << 16, _F32), lax.bitcast_convert_type(w & jnp.uint32(0xFFFF0000), _F32)], axis=1)


def _experts_body(ce_ref, nv_ref, nu_ref, seg_ref, nxt_ref, xs_hbm, wg_ref, wu_ref, wd_ref, ys_hbm,
                  wgb_ref, wub_ref, wdb_ref, wgf_ref, wuf_ref, wdf_ref, sem):
    def fetch(expert, slot):
        return [pltpu.make_async_copy(src.at[expert], dst.at[slot], sem.at[slot])
                for src, dst in ((wg_ref, wgf_ref), (wu_ref, wuf_ref), (wd_ref, wdf_ref))]

    def chunk(xs_ref, ys_ref):
        c = pl.program_id(0)

        @pl.when(seg_ref[c] >= 0)
        def _():
            slot = seg_ref[c] & 1

            @pl.when(c == 0)
            def _():
                for copy in fetch(ce_ref[0], 0):
                    copy.start()

            for copy in fetch(ce_ref[c], slot):
                copy.wait()
            wgb_ref[...] = wgf_ref[slot].astype(_BF16)
            wub_ref[...] = wuf_ref[slot].astype(_BF16)
            wdb_ref[...] = wdf_ref[slot].astype(_BF16)

            @pl.when(nxt_ref[c] >= 0)
            def _():
                for copy in fetch(nxt_ref[c], 1 - slot):
                    copy.start()

        row = lax.broadcasted_iota(jnp.int32, xs_ref.shape, 0)
        x = _unpack_halves(jnp.where(row < nv_ref[c], xs_ref[...], jnp.uint32(0))).astype(_BF16)
        half = x.shape[0] // 2
        ups = [(_dot(x[r:r + half], wgb_ref[...]), _dot(x[r:r + half], wub_ref[...])) for r in (0, half)]
        for (g, u), r in zip(ups, (0, half)):
            hmid = (g * jax.nn.sigmoid(g)) * u
            ys_ref[r:r + half, :] = _pack_halves(_dot(hmid.astype(_BF16), wdb_ref[...]))

    ch = EXPERT_CHUNK
    pltpu.emit_pipeline(
        chunk,
        grid=(nu_ref[0],),
        in_specs=[pl.BlockSpec((ch, xs_hbm.shape[1]), lambda c: (c, 0), pipeline_mode=pl.Buffered(EXPERT_IN_BUFFERS))],
        out_specs=[pl.BlockSpec((ch, ys_hbm.shape[1]), lambda c: (c, 0))],
    )(xs_hbm, ys_hbm)


def _experts(chunk_e, n_valid, n_used, seg, nxt, xs, wg, wu, wd):
    n_slots, w = xs.shape
    d, de = wg.shape[1], wg.shape[2]
    hbm = pl.BlockSpec(memory_space=pl.ANY)
    return pl.pallas_call(
        _experts_body,
        out_shape=jax.ShapeDtypeStruct((n_slots, d // 2), jnp.uint32),
        grid_spec=pltpu.PrefetchScalarGridSpec(
            num_scalar_prefetch=5,
            grid=(1,),
            in_specs=[hbm, hbm, hbm, hbm],
            out_specs=hbm,
            scratch_shapes=[pltpu.VMEM((d, de), _BF16), pltpu.VMEM((d, de), _BF16), pltpu.VMEM((de, d), _BF16),
                            pltpu.VMEM((2, d, de), _F32), pltpu.VMEM((2, d, de), _F32), pltpu.VMEM((2, de, d), _F32),
                            pltpu.SemaphoreType.DMA((2,))]),
        compiler_params=_params("arbitrary"),
        name="experts",
    )(chunk_e, n_valid, n_used, seg, nxt, xs, wg, wu, wd)


def _combine_body(x1_ref, rinfo_ref, mod_ref, y0_ref, y1_ref, *rest):
    o_ref = rest[-1]
    y = rinfo_ref[:, 4:5] * _unpack_halves(y0_ref[...]) + rinfo_ref[:, 5:6] * _unpack_halves(y1_ref[...])
    o_ref[...] = x1_ref[...] + mod_ref[:, 5 * D_MODEL:6 * D_MODEL] * y


def _combine(x1, rinfo, mod3, gathered, seq, part, out_so_far):
    t, d = x1.shape
    tf = min(COMBINE_TILE, seq)
    per_b = seq // tf
    nt = t // tf
    first = part * nt
    in_specs = [pl.BlockSpec((tf, d), lambda i: (i, 0)),
                pl.BlockSpec((tf, ROUTER_LANES), lambda i: (i, 0)),
                pl.BlockSpec((None, 1, mod3.shape[2]), lambda i: ((i + first) // per_b, 0, 0)),
                pl.BlockSpec((tf, d // 2), lambda i: (i, 0)),
                pl.BlockSpec((tf, d // 2), lambda i: (i + nt, 0))]
    args = [x1, rinfo, mod3, gathered, gathered]
    aliases = {}
    if out_so_far is not None:
        in_specs.append(pl.BlockSpec(memory_space=pl.ANY))
        args.append(out_so_far)
        aliases = {len(args) - 1: 0}
    return pl.pallas_call(
        _combine_body,
        out_shape=jax.ShapeDtypeStruct((t * MOE_PARTS, d), _F32),
        grid=(nt,),
        in_specs=in_specs,
        out_specs=pl.BlockSpec((tf, d), lambda i: (i + first, 0)),
        input_output_aliases=aliases,
        compiler_params=_params("arbitrary"),
        name="combine",
    )(*args)


def _rope_trig(positions):
    inv_freq = ROPE_THETA ** (-jnp.arange(0, ROPE_DIM, 2, dtype=_F32) / ROPE_DIM)
    ang = positions.reshape(-1).astype(_F32)[:, None] * inv_freq
    return jnp.concatenate([jnp.cos(ang), jnp.sin(ang)], axis=1)


def _layer(x, mod, positions, g_mix, g_ffn, w_in, w_bg, qg, kg, w_bd, w_bs, w_out, w_rg, b_rg, w_re, b_re,
           w_eg, w_eu, w_ed):
    batch, seq, d = x.shape
    t = batch * seq
    x2 = x.reshape(t, d)
    mod3 = mod.reshape(batch, 1, mod.shape[1])
    gw = GROUP_WIDTH

    lane = jnp.arange(gw)
    bd = jnp.where(lane[:, None] // HEAD_DIM == lane[None, :] // HEAD_DIM, 1.0 / HEAD_DIM, 0.0).astype(_BF16)
    ex = (jnp.arange(LANES)[:, None] == (lane[None, :] // HEAD_DIM) * LSE_SEG).astype(_BF16)
    ex = jnp.concatenate([ex, ex], axis=0)
    tri = (jnp.arange(ROW_TILE)[:, None] > jnp.arange(ROW_TILE)[None, :]).astype(_BF16)
    tile4 = lambda g: jnp.tile(g.astype(_F32), HEADS_PER_GROUP).reshape(1, gw)
    wr = jnp.zeros((d, ROUTER_LANES), _F32).at[:, :N_GROUPS].set(w_rg).at[:, N_GROUPS:N_GROUPS + N_EXPERTS].set(w_re)
    wrh = wr.astype(_BF16)
    wrl = (wr - wrh.astype(_F32)).astype(_BF16)
    br = jnp.zeros((1, ROUTER_LANES), _F32).at[0, :N_GROUPS].set(b_rg).at[0, N_GROUPS:N_GROUPS + N_EXPERTS].set(b_re)

    d0, d1, d2, sbp = _qkv(x2, mod3, g_mix.reshape(1, d), w_in.astype(_BF16), tile4(qg) * ATTN_SCALE, tile4(kg),
                           _rope_trig(positions), bd, seq)
    dil = [_dilated_group(a, dl, batch, seq) for a, (_, dl) in zip((d0, d1, d2), DIL_PATTERNS)]
    osb = _stick_breaking(sbp, batch, seq)

    merge_weights = (w_bg.astype(_BF16), w_bd.astype(_BF16), w_bs.astype(_BF16), w_out.astype(_BF16))
    tp = t // MOE_PARTS
    ch = EXPERT_CHUNK
    win = SC_INDEX_WINDOW
    n_chunks = -(-2 * tp // ch) + N_EXPERTS
    chunk_start = jnp.arange(n_chunks, dtype=jnp.int32) * ch
    expert_ids = jnp.arange(N_EXPERTS, dtype=jnp.int32)
    out = None
    for part in range(MOE_PARTS):
        x1, h2p, rinfo, er, cnt = _merge(
            x2, mod3, g_mix.reshape(1, d), g_ffn.reshape(1, d), [o for o, _ in dil], [l for _, l in dil], osb,
            *merge_weights, wrh, wrl, br, ex, tri, seq, part)

        counts = cnt[0, EXPERT_LANE0:EXPERT_LANE0 + N_EXPERTS].astype(jnp.int32)
        padded = (counts + ch - 1) // ch * ch
        pend = jnp.cumsum(padded)
        pstart = pend - padded
        chunk_e = jnp.minimum(jnp.sum((pend[None, :] <= chunk_start[:, None]).astype(jnp.int32), axis=1),
                              N_EXPERTS - 1)
        n_used = (pend[-1:] // ch).astype(jnp.int32)
        begin = chunk_start[:, None]
        inside = (pstart[None, :] <= begin) & (begin < pend[None, :])
        n_valid = jnp.sum(jnp.where(inside, jnp.clip(counts[None, :] - (begin - pstart[None, :]), 0, ch), 0), axis=1)

        dest = _dest(pstart, er)
        xs = _sc_scatter(h2p, dest[0].reshape(tp // win, win), dest[1].reshape(tp // win, win), n_chunks * ch)
        first = (chunk_e != jnp.concatenate([jnp.full((1,), -1, jnp.int32), chunk_e[:-1]])) & (chunk_start < pend[-1])
        seg_no = jnp.cumsum(first.astype(jnp.int32)) - 1
        seg = jnp.where(first, seg_no, -1 - seg_no)
        later = (expert_ids[None, :] > expert_ids[:, None]) & (padded > 0)[None, :]
        next_expert = jnp.min(jnp.where(later, expert_ids[None, :], N_EXPERTS), axis=1)
        next_expert = jnp.where(next_expert == N_EXPERTS, -1, next_expert)
        nxt = jnp.sum(jnp.where(chunk_e[:, None] == expert_ids[None, :], next_expert[None, :], 0), axis=1)
        ys = _experts(chunk_e, n_valid, n_used, seg.astype(jnp.int32), nxt.astype(jnp.int32), xs, w_eg, w_eu, w_ed)
        gathered = _sc_gather(ys, dest.reshape(2 * tp // win, win))
        out = _combine(x1, rinfo, mod3, gathered, seq, part, out)
    return out.reshape(batch, seq, d)


def kernel(x, c, positions, w_ada, b_ada, g_norm_mix, g_norm_ffn, w_in, w_branch_gate, q_norm_g, k_norm_g,
           w_branch_dil, w_branch_sb, w_out, w_router_group, b_router_group, w_router_expert, b_router_expert,
           w_expert_gate, w_expert_up, w_expert_down):
    for l in range(w_ada.shape[0]):
        mod = _ada(c, w_ada[l], b_ada[l])
        x = _layer(x, mod, positions, g_norm_mix[l], g_norm_ffn[l], w_in[l], w_branch_gate[l], q_norm_g[l],
                   k_norm_g[l], w_branch_dil[l], w_branch_sb[l], w_out[l], w_router_group[l], b_router_group[l],
                   w_router_expert[l], b_router_expert[l], w_expert_gate[l], w_expert_up[l], w_expert_down[l])
    return x
```
